```python
import math
import jax, jax.numpy as jnp
from jax import lax
import numpy as np

D_MODEL = 1024
BATCH = 4
SEQ = 4096
DEPTH = 2

MEM_LEN = 256
N_AB = (DEPTH + 1) // 2
N_CD = DEPTH // 2
MIX_W = D_MODEL // 2
NORM_EPS = 1e-6
N_NORMS = 7

GLA_HEADS = 4
GLA_DV = MIX_W // GLA_HEADS
GLA_DK = GLA_DV // 2
GLA_RANK = 16
GLA_TAU = 16.0
GLA_CHUNK = 64

S5_GROUP = 16
S5_GROUPS = MIX_W // S5_GROUP
S5_STATE = 64

RWKV_HEAD = 64
RWKV_HEADS = MIX_W // RWKV_HEAD
RWKV_DECAY_RANK = 64
RWKV_A_RANK = 64
RWKV_GATE_RANK = 128
RWKV_GN_EPS = 64e-5

LRU_BLOCKS = 8
LRU_BLOCK = MIX_W // LRU_BLOCKS
LRU_CONV = 4
LRU_C = 8.0

XA_HEADS = 4
XA_HEAD_DIM = D_MODEL // XA_HEADS
D_FF = 4 * D_MODEL

AB_SIZES = (GLA_HEADS * GLA_DK, GLA_HEADS * GLA_DK, MIX_W, MIX_W, GLA_RANK, MIX_W)
AB_COLS = sum(AB_SIZES)
RWKV_SIZES = (MIX_W, RWKV_DECAY_RANK, MIX_W, MIX_W, RWKV_A_RANK, RWKV_GATE_RANK)
RWKV_COLS = sum(RWKV_SIZES)
CD_SIZES = (RWKV_COLS, MIX_W, MIX_W)
CD_COLS = sum(CD_SIZES)

kernel_name = 'hybrid_gla_s5_rwkv7_rglru_trunk'


def _split(p, sizes):
    return jnp.split(p, [int(s) for s in np.cumsum(sizes)[:-1]], axis=-1)


def rmsnorm(x, gain):
    x32 = x.astype(jnp.float32)
    y = x32 * lax.rsqrt(jnp.mean(x32 * x32, axis=-1, keepdims=True) + NORM_EPS) * gain.astype(jnp.float32)
    return y.astype(x.dtype)


def _linear_scan(a, b, axis):
    def combine(e1, e2):
        a1, b1 = e1
        a2, b2 = e2
        return a1 * a2, a2 * b1 + b2
    _, h = lax.associative_scan(combine, (a, b), axis=axis)
    return h


def gla_mix(q, k, v, g, dlr, w_decay2, b_decay, norm_gain):
    f32 = jnp.float32
    bsz, seq, _ = q.shape
    n_c = seq // GLA_CHUNK
    q = q.astype(f32).reshape(bsz, n_c, GLA_CHUNK, GLA_HEADS, GLA_DK) * GLA_DK ** -0.5
    k = k.astype(f32).reshape(bsz, n_c, GLA_CHUNK, GLA_HEADS, GLA_DK)
    v = v.astype(f32).reshape(bsz, n_c, GLA_CHUNK, GLA_HEADS, GLA_DV)
    log_a = jax.nn.log_sigmoid(dlr.astype(f32) @ w_decay2.astype(f32) + b_decay.astype(f32)) / GLA_TAU
    log_a = log_a.reshape(bsz, n_c, GLA_CHUNK, GLA_HEADS, GLA_DK)
    b = jnp.cumsum(log_a, axis=2)
    b_last = b[:, :, -1:]
    q_in = q * jnp.exp(b)
    k_in = k * jnp.exp(-b)
    scores = jnp.einsum('bnthd,bnshd->bnhts', q_in, k_in)
    causal = jnp.tril(jnp.ones((GLA_CHUNK, GLA_CHUNK), dtype=bool))
    scores = jnp.where(causal, scores, 0.0)
    o_intra = jnp.einsum('bnhts,bnshv->bnthv', scores, v)
    k_state = k * jnp.exp(b_last - b)
    d_state = jnp.einsum('bnshd,bnshv->nbhdv', k_state, v)
    chunk_decay = jnp.transpose(jnp.exp(b_last[:, :, 0]), (1, 0, 2, 3))

    def step(state, inp):
        dec, ds = inp
        return dec[..., None] * state + ds, state

    s0 = jnp.zeros((bsz, GLA_HEADS, GLA_DK, GLA_DV), f32)
    _, s_prev = lax.scan(step, s0, (chunk_decay, d_state))
    o_inter = jnp.einsum('bnthd,nbhdv->bnthv', q_in, s_prev)
    o = (o_intra + o_inter).reshape(bsz, seq, GLA_HEADS, GLA_DV)
    o = rmsnorm(o, norm_gain).reshape(bsz, seq, MIX_W)
    return o * jax.nn.silu(g.astype(f32))


def s5_mix(u, lam_re, lam_im, log_step, b_re, b_im, c_re, c_im, d_skip, w_glu, b_glu):
    f32 = jnp.float32
    bsz, seq, _ = u.shape
    u32 = u.astype(f32)
    ug = u32.reshape(bsz, seq, S5_GROUPS, S5_GROUP)
    lam = lax.complex(jnp.minimum(lam_re.astype(f32), -1e-4), lam_im.astype(f32))
    delta = jnp.exp(log_step.astype(f32))[:, None]
    lam_bar = jnp.exp(lam * delta)
    b_mat = lax.complex(b_re.astype(f32), b_im.astype(f32))
    b_bar = ((lam_bar - 1.0) / lam)[..., None] * b_mat
    bu = jnp.einsum('blgc,gnc->blgn', ug.astype(jnp.complex64), b_bar)
    h = _linear_scan(jnp.broadcast_to(lam_bar, bu.shape), bu, axis=1)
    c_mat = lax.complex(c_re.astype(f32), c_im.astype(f32))
    y = jnp.real(jnp.einsum('blgn,gcn->blgc', h, c_mat)).reshape(bsz, seq, MIX_W)
    y = y + d_skip.astype(f32) * u32
    return jax.nn.gelu(y) * jax.nn.sigmoid(y @ w_glu.astype(f32) + b_glu.astype(f32))


def rwkv7_mix(p, mu, w0, w2, a0, a2, g2, k_k, k_a, r_k, ln_gain, ln_bias):
    f32 = jnp.float32
    bsz, seq, _ = p.shape
    p = p.astype(f32)
    prev = jnp.pad(p, ((0, 0), (1, 0), (0, 0)))[:, :-1]
    p = p + (prev - p) * mu.astype(f32)
    r, w1, k, v, a1, g1 = _split(p, RWKV_SIZES)
    w = -jax.nn.softplus(-(w0.astype(f32) + jnp.tanh(w1) @ w2.astype(f32))) - 0.5
    decay = jnp.exp(-jnp.exp(w))
    a = jax.nn.sigmoid(a0.astype(f32) + a1 @ a2.astype(f32))
    g = jax.nn.sigmoid(g1) @ g2.astype(f32)
    heads = lambda t: t.reshape(bsz, seq, RWKV_HEADS, RWKV_HEAD)
    kk = heads(k * k_k.astype(f32))
    kk = kk / jnp.maximum(jnp.sqrt(jnp.sum(kk * kk, axis=-1, keepdims=True)), 1e-12)
    k = k * (1.0 + (a - 1.0) * k_a.astype(f32))
    r_h, k_h, v_h, a_h, w_h = heads(r), heads(k), heads(v), heads(a), heads(decay)

    def step(state, inp):
        r_t, w_t, k_t, v_t, kk_t, a_t = inp
        sa = jnp.einsum('bhij,bhj->bhi', state, kk_t)
        state = (state * w_t[:, :, None, :]
                 - sa[..., None] * (kk_t * a_t)[:, :, None, :]
                 + v_t[..., None] * k_t[:, :, None, :])
        return state, jnp.einsum('bhij,bhj->bhi', state, r_t)

    tm = lambda t: jnp.moveaxis(t, 1, 0)
    s0 = jnp.zeros((bsz, RWKV_HEADS, RWKV_HEAD, RWKV_HEAD), f32)
    _, y = lax.scan(step, s0, (tm(r_h), tm(w_h), tm(k_h), tm(v_h), tm(kk), tm(a_h)))
    y = jnp.moveaxis(y, 0, 1)
    mean = jnp.mean(y, axis=-1, keepdims=True)
    var = jnp.mean(jnp.square(y - mean), axis=-1, keepdims=True)
    y = ((y - mean) * lax.rsqrt(var + RWKV_GN_EPS)).reshape(bsz, seq, MIX_W)
    y = y * ln_gain.astype(f32) + ln_bias.astype(f32)
    bonus = jnp.sum(r_h * k_h * r_k.astype(f32), axis=-1, keepdims=True) * v_h
    y = y + bonus.reshape(bsz, seq, MIX_W)
    return y * g


def rglru_mix(xb, gate, conv_w, conv_b, w_a, b_a, w_x, b_x, lam):
    f32 = jnp.float32
    bsz, seq, _ = xb.shape
    xc = lax.conv_general_dilated(
        xb.astype(f32), conv_w.astype(f32)[:, None, :], window_strides=(1,),
        padding=((LRU_CONV - 1, 0),), dimension_numbers=('NWC', 'WIO', 'NWC'),
        feature_group_count=MIX_W) + conv_b.astype(f32)
    xg = xc.reshape(bsz, seq, LRU_BLOCKS, LRU_BLOCK)
    r = jax.nn.sigmoid(jnp.einsum('blhi,hij->blhj', xg, w_a.astype(f32)).reshape(bsz, seq, MIX_W) + b_a.astype(f32))
    i = jax.nn.sigmoid(jnp.einsum('blhi,hij->blhj', xg, w_x.astype(f32)).reshape(bsz, seq, MIX_W) + b_x.astype(f32))
    log_a = -LRU_C * r * jax.nn.softplus(-lam.astype(f32))
    a = jnp.exp(log_a)
    mult = jnp.sqrt(-jnp.expm1(2.0 * log_a))
    h = _linear_scan(a, mult * (i * xc), axis=1)
    return h * jax.nn.gelu(gate.astype(f32))


def cross_attention(xn, memn, wq, wk, wv, wo):
    f32 = jnp.float32
    bsz, seq, _ = xn.shape
    q = (xn @ wq).astype(f32).reshape(bsz, seq, XA_HEADS, XA_HEAD_DIM)
    k = (memn @ wk).astype(f32).reshape(bsz, MEM_LEN, XA_HEADS, XA_HEAD_DIM)
    v = (memn @ wv).astype(f32).reshape(bsz, MEM_LEN, XA_HEADS, XA_HEAD_DIM)
    s = jnp.einsum('blhd,bmhd->bhlm', q, k) * XA_HEAD_DIM ** -0.5
    prob = jax.nn.softmax(s, axis=-1)
    o = jnp.einsum('bhlm,bmhd->blhd', prob, v).reshape(bsz, seq, D_MODEL)
    return (o @ wo.astype(f32)).astype(xn.dtype)


def squared_relu_mlp(xn, w1, w2):
    return jnp.square(jax.nn.relu(xn @ w1)) @ w2


def setup_inputs(seed: int = 0) -> dict:
    key = jax.random.key(seed)
    keys = jax.random.split(key, 64)
    counter = [0]

    def nk():
        kk = keys[counter[0]]
        counter[0] += 1
        return kk

    def nrm(shape, scale=1.0):
        return scale * jax.random.normal(nk(), shape, jnp.float32)

    def unif(shape, lo, hi):
        return jax.random.uniform(nk(), shape, jnp.float32, lo, hi)

    n_idx = jnp.arange(S5_STATE, dtype=jnp.float32)
    w0_base = jnp.tile(jnp.linspace(-6.0, -1.0, RWKV_HEAD, dtype=jnp.float32), RWKV_HEADS)
    x = nrm((BATCH, SEQ, D_MODEL))
    mem = nrm((BATCH, MEM_LEN, D_MODEL))
    norm_gain = 1.0 + nrm((DEPTH, N_NORMS, D_MODEL), 0.05)
    xa_wq = nrm((DEPTH, D_MODEL, D_MODEL), D_MODEL ** -0.5)
    xa_wk = nrm((DEPTH, D_MODEL, D_MODEL), D_MODEL ** -0.5)
    xa_wv = nrm((DEPTH, D_MODEL, D_MODEL), D_MODEL ** -0.5)
    xa_wo = nrm((DEPTH, D_MODEL, D_MODEL), D_MODEL ** -0.5)
    mlp_w1 = nrm((DEPTH, D_MODEL, D_FF), D_MODEL ** -0.5)
    mlp_w2 = nrm((DEPTH, D_FF, D_MODEL), D_FF ** -0.5)
    ab_w_in = nrm((N_AB, D_MODEL, AB_COLS), D_MODEL ** -0.5)
    gla_w_decay2 = nrm((N_AB, GLA_RANK, GLA_HEADS * GLA_DK), GLA_RANK ** -0.5)
    gla_b_decay = nrm((N_AB, GLA_HEADS * GLA_DK), 0.1)
    gla_norm_gain = 1.0 + nrm((N_AB, GLA_HEADS, GLA_DV), 0.05)
    s5_lambda_re = -0.5 + nrm((N_AB, S5_GROUPS, S5_STATE), 0.01)
    s5_lambda_im = math.pi * n_idx + nrm((N_AB, S5_GROUPS, S5_STATE), 0.01)
    s5_log_step = unif((N_AB, S5_GROUPS), math.log(1e-3), math.log(1e-1))
    s5_b_re = nrm((N_AB, S5_GROUPS, S5_STATE, S5_GROUP), (2.0 * S5_GROUP) ** -0.5)
    s5_b_im = nrm((N_AB, S5_GROUPS, S5_STATE, S5_GROUP), (2.0 * S5_GROUP) ** -0.5)
    s5_c_re = nrm((N_AB, S5_GROUPS, S5_GROUP, S5_STATE), (2.0 * S5_STATE) ** -0.5)
    s5_c_im = nrm((N_AB, S5_GROUPS, S5_GROUP, S5_STATE), (2.0 * S5_STATE) ** -0.5)
    s5_d = nrm((N_AB, MIX_W))
    s5_w_glu = nrm((N_AB, MIX_W, MIX_W), MIX_W ** -0.5)
    s5_b_glu = nrm((N_AB, MIX_W), 0.01)
    ab_w_out = nrm((N_AB, 2 * MIX_W, D_MODEL), (2 * MIX_W) ** -0.5)
    cd_w_in = nrm((N_CD, D_MODEL, CD_COLS), D_MODEL ** -0.5)
    rwkv_mu = unif((N_CD, RWKV_COLS), 0.0, 1.0)
    rwkv_w0 = w0_base + nrm((N_CD, MIX_W), 0.1)
    rwkv_w2 = nrm((N_CD, RWKV_DECAY_RANK, MIX_W), 0.5 * RWKV_DECAY_RANK ** -0.5)
    rwkv_a0 = nrm((N_CD, MIX_W), 0.1)
    rwkv_a2 = nrm((N_CD, RWKV_A_RANK, MIX_W), 0.5 * RWKV_A_RANK ** -0.5)
    rwkv_g2 = nrm((N_CD, RWKV_GATE_RANK, MIX_W), RWKV_GATE_RANK ** -0.5)
    rwkv_k_k = 0.85 + nrm((N_CD, MIX_W), 0.05)
    rwkv_k_a = 1.0 + nrm((N_CD, MIX_W), 0.05)
    rwkv_r_k = nrm((N_CD, RWKV_HEADS, RWKV_HEAD), 0.1)
    rwkv_ln_gain = 1.0 + nrm((N_CD, MIX_W), 0.05)
    rwkv_ln_bias = nrm((N_CD, MIX_W), 0.01)
    lru_conv_w = nrm((N_CD, LRU_CONV, MIX_W), LRU_CONV ** -0.5)
    lru_conv_b = nrm((N_CD, MIX_W), 0.01)
    lru_w_a = nrm((N_CD, LRU_BLOCKS, LRU_BLOCK, LRU_BLOCK), LRU_BLOCK ** -0.5)
    lru_b_a = nrm((N_CD, MIX_W), 0.01)
    lru_w_x = nrm((N_CD, LRU_BLOCKS, LRU_BLOCK, LRU_BLOCK), LRU_BLOCK ** -0.5)
    lru_b_x = nrm((N_CD, MIX_W), 0.01)
    lru_a = unif((N_CD, MIX_W), 0.9, 0.999) ** (1.0 / LRU_C)
    lru_lambda = jnp.log(lru_a) - jnp.log1p(-lru_a)
    cd_w_out = nrm((N_CD, 2 * MIX_W, D_MODEL), (2 * MIX_W) ** -0.5)
    return {
        'x': x, 'mem': mem, 'norm_gain': norm_gain,
        'xa_wq': xa_wq, 'xa_wk': xa_wk, 'xa_wv': xa_wv, 'xa_wo': xa_wo,
        'mlp_w1': mlp_w1, 'mlp_w2': mlp_w2,
        'ab_w_in': ab_w_in, 'gla_w_decay2': gla_w_decay2, 'gla_b_decay': gla_b_decay,
        'gla_norm_gain': gla_norm_gain,
        's5_lambda_re': s5_lambda_re, 's5_lambda_im': s5_lambda_im, 's5_log_step': s5_log_step,
        's5_b_re': s5_b_re, 's5_b_im': s5_b_im, 's5_c_re': s5_c_re, 's5_c_im': s5_c_im,
        's5_d': s5_d, 's5_w_glu': s5_w_glu, 's5_b_glu': s5_b_glu, 'ab_w_out': ab_w_out,
        'cd_w_in': cd_w_in, 'rwkv_mu': rwkv_mu, 'rwkv_w0': rwkv_w0, 'rwkv_w2': rwkv_w2,
        'rwkv_a0': rwkv_a0, 'rwkv_a2': rwkv_a2, 'rwkv_g2': rwkv_g2, 'rwkv_k_k': rwkv_k_k,
        'rwkv_k_a': rwkv_k_a, 'rwkv_r_k': rwkv_r_k, 'rwkv_ln_gain': rwkv_ln_gain,
        'rwkv_ln_bias': rwkv_ln_bias, 'lru_conv_w': lru_conv_w, 'lru_conv_b': lru_conv_b,
        'lru_w_a': lru_w_a, 'lru_b_a': lru_b_a, 'lru_w_x': lru_w_x, 'lru_b_x': lru_b_x,
        'lru_lambda': lru_lambda, 'cd_w_out': cd_w_out,
    }


def reference(x, mem, norm_gain, xa_wq, xa_wk, xa_wv, xa_wo, mlp_w1, mlp_w2,
              ab_w_in, gla_w_decay2, gla_b_decay, gla_norm_gain,
              s5_lambda_re, s5_lambda_im, s5_log_step, s5_b_re, s5_b_im, s5_c_re, s5_c_im,
              s5_d, s5_w_glu, s5_b_glu, ab_w_out,
              cd_w_in, rwkv_mu, rwkv_w0, rwkv_w2, rwkv_a0, rwkv_a2, rwkv_g2, rwkv_k_k,
              rwkv_k_a, rwkv_r_k, rwkv_ln_gain, rwkv_ln_bias,
              lru_conv_w, lru_conv_b, lru_w_a, lru_b_a, lru_w_x, lru_b_x, lru_lambda, cd_w_out):
    f32 = jnp.float32
    h = x
    for layer in range(DEPTH):
        g = norm_gain[layer]
        i = layer // 2
        hn = rmsnorm(h, g[0])
        if layer % 2 == 0:
            q, k, v, gate, dlr, u = _split(hn @ ab_w_in[i], AB_SIZES)
            o_a = gla_mix(q, k, v, gate, dlr, gla_w_decay2[i], gla_b_decay[i], gla_norm_gain[i])
            o_b = s5_mix(u, s5_lambda_re[i], s5_lambda_im[i], s5_log_step[i], s5_b_re[i], s5_b_im[i],
                         s5_c_re[i], s5_c_im[i], s5_d[i], s5_w_glu[i], s5_b_glu[i])
            mix = jnp.concatenate([o_a, o_b], axis=-1) @ ab_w_out[i].astype(f32)
        else:
            p_rwkv, xb, gate = _split(hn @ cd_w_in[i], CD_SIZES)
            o_c = rwkv7_mix(p_rwkv, rwkv_mu[i], rwkv_w0[i], rwkv_w2[i], rwkv_a0[i], rwkv_a2[i],
                            rwkv_g2[i], rwkv_k_k[i], rwkv_k_a[i], rwkv_r_k[i],
                            rwkv_ln_gain[i], rwkv_ln_bias[i])
            o_d = rglru_mix(xb, gate, lru_conv_w[i], lru_conv_b[i], lru_w_a[i], lru_b_a[i],
                            lru_w_x[i], lru_b_x[i], lru_lambda[i])
            mix = jnp.concatenate([o_c, o_d], axis=-1) @ cd_w_out[i].astype(f32)
        h = h + rmsnorm(mix, g[1]).astype(h.dtype)
        memn = rmsnorm(mem, g[6])
        xa = cross_attention(rmsnorm(h, g[2]), memn, xa_wq[layer], xa_wk[layer], xa_wv[layer], xa_wo[layer])
        h = h + rmsnorm(xa, g[3]).astype(h.dtype)
        ff = squared_relu_mlp(rmsnorm(h, g[4]), mlp_w1[layer], mlp_w2[layer])
        h = h + rmsnorm(ff, g[5]).astype(h.dtype)
    return h
```

```python
import functools
import math

import jax
import jax.numpy as jnp
from jax import lax
from jax.experimental import pallas as pl
from jax.experimental.pallas import tpu as pltpu

F32 = jnp.float32
BF16 = jnp.bfloat16
HIGHEST = lax.Precision.HIGHEST

NORM_EPS = 1e-6
GLA_HEADS = 4
GLA_TAU = 16.0
GLA_CHUNK = 64
S5_GROUP = 16
S5_STATE = 64
S5_CHUNK = 16
RWKV_HEAD = 64
RWKV_CHUNK = 64
RWKV_GN_EPS = 64e-5
LRU_CONV = 4
LRU_C = 8.0
XA_HEADS = 4

V7X_SUBLANES = 8
VMEM_LIMIT_BYTES = 48 * 1024 * 1024


def _cparams(*semantics):
    return pltpu.CompilerParams(dimension_semantics=semantics, vmem_limit_bytes=VMEM_LIMIT_BYTES)


def _rms(x, gain):
    return x * lax.rsqrt(jnp.mean(x * x, axis=-1, keepdims=True) + NORM_EPS) * gain


def _bdot(a, b):
    return jnp.dot(a.astype(BF16), b.astype(BF16), preferred_element_type=F32)


def _bdot_nt(a, b):
    return lax.dot_general(a.astype(BF16), b.astype(BF16), (((1,), (1,)), ((), ())),
                           preferred_element_type=F32)


def _bdot_tn(a, b):
    return lax.dot_general(a.astype(BF16), b.astype(BF16), (((0,), (0,)), ((), ())),
                           preferred_element_type=F32)


def _hdot(a, b):
    return jnp.dot(a, b, precision=HIGHEST, preferred_element_type=F32)


def _hdot_nt(a, b):
    return lax.dot_general(a, b, (((1,), (1,)), ((), ())), precision=HIGHEST,
                           preferred_element_type=F32)


def _hdot_tn(a, b):
    return lax.dot_general(a, b, (((0,), (0,)), ((), ())), precision=HIGHEST,
                           preferred_element_type=F32)


def _split_dot(x, w_bf16):
    hi = x.astype(BF16)
    lo = (x - hi.astype(F32)).astype(BF16)
    return (jnp.dot(hi, w_bf16, preferred_element_type=F32)
            + jnp.dot(lo, w_bf16, preferred_element_type=F32))


def _sigmoid(x):
    return 1.0 / (1.0 + jnp.exp(-x))


def _softplus(x):
    return jnp.maximum(x, 0.0) + jnp.log1p(jnp.exp(-jnp.abs(x)))


def _gelu_tanh(x):
    c = math.sqrt(2.0 / math.pi)
    return x * (0.5 * (1.0 + jnp.tanh(c * (x + 0.044715 * (x * x * x)))))


def _tril_mask(n, strict=False):
    row = lax.broadcasted_iota(jnp.int32, (n, n), 0)
    col = lax.broadcasted_iota(jnp.int32, (n, n), 1)
    return (col < row) if strict else (col <= row)


def _shift_rows(x, shift, carry):
    rolled = pltpu.roll(x, shift, axis=0)
    crolled = pltpu.roll(carry, shift, axis=0)
    rowi = lax.broadcasted_iota(jnp.int32, carry.shape, 0)
    first = jnp.where(rowi < shift, crolled, rolled[0:V7X_SUBLANES])
    return jnp.concatenate([first, rolled[V7X_SUBLANES:]], axis=0)


def _ab_in_kernel(h_ref, g_ref, w_ref, wd_ref, wd2_ref, bd_ref, out_ref, loga_ref):
    hn = _rms(h_ref[...], g_ref[...]).astype(BF16)
    out_ref[...] = jnp.dot(hn, w_ref[...], preferred_element_type=F32)
    dlr = jnp.dot(hn, wd_ref[...], preferred_element_type=F32)
    z = _hdot(dlr, wd2_ref[...]) + bd_ref[...]
    loga_ref[...] = (jnp.minimum(z, 0.0) - jnp.log1p(jnp.exp(-jnp.abs(z)))) * (1.0 / GLA_TAU)


def _ab_in(h, gain, w_main, w_dlr, w_decay2, b_decay, tm):
    t, d = h.shape
    n_main = w_main.shape[1]
    n_dk = w_decay2.shape[1]
    full = lambda a: pl.BlockSpec(a.shape, lambda i: (0, 0))
    return pl.pallas_call(
        _ab_in_kernel,
        grid=(t // tm,),
        in_specs=[pl.BlockSpec((tm, d), lambda i: (i, 0)), full(gain), full(w_main), full(w_dlr),
                  full(w_decay2), full(b_decay)],
        out_specs=[pl.BlockSpec((tm, n_main), lambda i: (i, 0)), pl.BlockSpec((tm, n_dk), lambda i: (i, 0))],
        out_shape=[jax.ShapeDtypeStruct((t, n_main), F32), jax.ShapeDtypeStruct((t, n_dk), F32)],
        compiler_params=_cparams("parallel"),
        name="ab_in",
    )(h, gain, w_main, w_dlr, w_decay2, b_decay)


def _gla_kernel(q_ref, k_ref, v_ref, gate_ref, la_ref, gain_ref, o_ref, state_ref, *, n_chunks, dk, dv):
    c = GLA_CHUNK

    @pl.when(pl.program_id(1) == 0)
    def _():
        state_ref[...] = jnp.zeros_like(state_ref)

    tril = _tril_mask(c)
    tril_f = tril.astype(F32)
    scale = dk ** -0.5
    for ci in range(n_chunks):
        rs = slice(ci * c, (ci + 1) * c)
        b = _hdot(tril_f, la_ref[rs, :])
        b_last = b[c - 1:c, :]
        q_in = q_ref[rs, :] * jnp.exp(b) * scale
        k = k_ref[rs, :]
        k_in = k * jnp.exp(-b)
        k_st = k * jnp.exp(b_last - b)
        dec = jnp.exp(b_last)
        for h in range(GLA_HEADS):
            ks = slice(h * dk, (h + 1) * dk)
            vs = slice(h * dv, (h + 1) * dv)
            qh = q_in[:, ks]
            vh = v_ref[rs, vs]
            scores = jnp.where(tril, _bdot_nt(qh, k_in[:, ks]), 0.0)
            st = state_ref[h]
            o = _bdot(scores, vh) + _bdot_nt(qh, st)
            state_ref[h] = st * dec[:, ks] + _bdot_tn(vh, k_st[:, ks])
            on = o * lax.rsqrt(jnp.mean(o * o, axis=-1, keepdims=True) + NORM_EPS) * gain_ref[:, vs]
            g = gate_ref[rs, vs]
            o_ref[rs, vs] = on * (g * _sigmoid(g))


def _gla(qkvgu, loga, gain, bsz, seq, tb):
    t = bsz * seq
    dk = loga.shape[1] // GLA_HEADS
    dv = gain.shape[1] // GLA_HEADS
    nq, nv = GLA_HEADS * dk, GLA_HEADS * dv
    nt = seq // tb
    row = lambda b, i: b * nt + i
    kern = functools.partial(_gla_kernel, n_chunks=tb // GLA_CHUNK, dk=dk, dv=dv)
    return pl.pallas_call(
        kern,
        grid=(bsz, nt),
        in_specs=[pl.BlockSpec((tb, nq), lambda b, i: (row(b, i), 0)),
                  pl.BlockSpec((tb, nq), lambda b, i: (row(b, i), 1)),
                  pl.BlockSpec((tb, nv), lambda b, i: (row(b, i), 1)),
                  pl.BlockSpec((tb, nv), lambda b, i: (row(b, i), 2)),
                  pl.BlockSpec((tb, nq), lambda b, i: (row(b, i), 0)),
                  pl.BlockSpec(gain.shape, lambda b, i: (0, 0))],
        out_specs=pl.BlockSpec((tb, nv), lambda b, i: (row(b, i), 0)),
        out_shape=jax.ShapeDtypeStruct((t, nv), F32),
        scratch_shapes=[pltpu.VMEM((GLA_HEADS, dv, dk), F32)],
        compiler_params=_cparams("parallel", "arbitrary"),
        name="gla",
    )(qkvgu, qkvgu, qkvgu, qkvgu, loga, gain)


def _s5_kernel(u_ref, tz_ref, ws_ref, wc_ref, a_ref, y_ref, s_scr, h_scr, *, n_pairs, bsz):
    n2 = 2 * S5_STATE
    u = u_ref[0]
    s_scr[...] = _hdot(u, ws_ref[0])
    a1 = a_ref[0, 0:1, :]
    a2 = a_ref[0, 1:2, :]

    def body(j, carry):
        h, hs = carry
        r0 = pl.multiple_of(j * (2 * bsz), 2 * bsz)
        blk = s_scr[pl.ds(r0, 2 * bsz), :]
        h1 = a1 * h + a2 * hs + blk[0:bsz, 0:n2]
        hs1 = a1 * hs - a2 * h + blk[0:bsz, n2:2 * n2]
        h2 = a1 * h1 + a2 * hs1 + blk[bsz:2 * bsz, 0:n2]
        hs2 = a1 * hs1 - a2 * h1 + blk[bsz:2 * bsz, n2:2 * n2]
        h_scr[pl.ds(r0, 2 * bsz), :] = jnp.concatenate([h, h1], axis=0)
        return h2, hs2

    zero = jnp.zeros((bsz, n2), F32)
    lax.fori_loop(0, n_pairs, body, (zero, zero))
    y_ref[0] = _hdot(u, tz_ref[0]) + _hdot(h_scr[...], wc_ref[0])


def _s5(u_g, tz, ws, wc, a_pow, bsz):
    groups, rows, width = u_g.shape
    assert 2 * bsz == V7X_SUBLANES, "the chunk scan walks two chunks per 8-row tile"
    n_pairs = rows // (2 * bsz)
    per_g = lambda a: pl.BlockSpec((1,) + a.shape[1:], lambda g: (g, 0, 0))
    kern = functools.partial(_s5_kernel, n_pairs=n_pairs, bsz=bsz)
    return pl.pallas_call(
        kern,
        grid=(groups,),
        in_specs=[per_g(u_g), per_g(tz), per_g(ws), per_g(wc), per_g(a_pow)],
        out_specs=per_g(u_g),
        out_shape=jax.ShapeDtypeStruct(u_g.shape, F32),
        scratch_shapes=[pltpu.VMEM((rows, ws.shape[2]), F32), pltpu.VMEM((rows, wc.shape[1]), F32)],
        compiler_params=_cparams("parallel"),
        name="s5",
    )(u_g, tz, ws, wc, a_pow)


def _s5_weights(lam_re, lam_im, log_step, b_re, b_im, c_re, c_im):
    tc = S5_CHUNK
    groups, n = lam_re.shape
    lr = jnp.minimum(lam_re.astype(F32), -1e-4)
    li = lam_im.astype(F32)
    delta = jnp.exp(log_step.astype(F32))[:, None]
    tau = jnp.arange(tc + 1, dtype=F32)[:, None, None]
    mag = jnp.exp(tau * (lr * delta))
    ang = tau * (li * delta)
    pw_re, pw_im = mag * jnp.cos(ang), mag * jnp.sin(ang)
    num_re, num_im = pw_re[1] - 1.0, pw_im[1]
    den = lr * lr + li * li
    f_re = (num_re * lr + num_im * li) / den
    f_im = (num_im * lr - num_re * li) / den
    b_re, b_im = b_re.astype(F32), b_im.astype(F32)
    bb_re = f_re[..., None] * b_re - f_im[..., None] * b_im
    bb_im = f_re[..., None] * b_im + f_im[..., None] * b_re
    c_re, c_im = c_re.astype(F32), c_im.astype(F32)
    cp_re = c_re[None] * pw_re[:, :, None, :] - c_im[None] * pw_im[:, :, None, :]
    cp_im = c_re[None] * pw_im[:, :, None, :] + c_im[None] * pw_re[:, :, None, :]
    ein = functools.partial(jnp.einsum, precision=HIGHEST)
    kern = ein('tgcn,gnd->tgcd', cp_re[:tc], bb_re) - ein('tgcn,gnd->tgcd', cp_im[:tc], bb_im)
    j = jnp.arange(tc)[:, None]
    i = jnp.arange(tc)[None, :]
    diff = i - j
    ksel = jnp.where((diff >= 0)[:, :, None, None, None], kern[jnp.clip(diff, 0, tc - 1)], 0.0)
    tz = jnp.transpose(ksel, (2, 0, 4, 1, 3)).reshape(groups, tc * S5_GROUP, tc * S5_GROUP)
    rev_re, rev_im = pw_re[tc - 1::-1][:tc], pw_im[tc - 1::-1][:tc]
    s_re = rev_re[:, :, :, None] * bb_re[None] - rev_im[:, :, :, None] * bb_im[None]
    s_im = rev_re[:, :, :, None] * bb_im[None] + rev_im[:, :, :, None] * bb_re[None]
    s_re = jnp.transpose(s_re, (1, 0, 3, 2)).reshape(groups, tc * S5_GROUP, n)
    s_im = jnp.transpose(s_im, (1, 0, 3, 2)).reshape(groups, tc * S5_GROUP, n)
    ws = jnp.concatenate([s_re, s_im, s_im, s_re], axis=-1)
    o_re = jnp.transpose(cp_re[1:], (1, 3, 0, 2)).reshape(groups, n, tc * S5_GROUP)
    o_im = jnp.transpose(cp_im[1:], (1, 3, 0, 2)).reshape(groups, n, tc * S5_GROUP)
    wc = jnp.concatenate([o_re, -o_im], axis=1)
    a_pow = jnp.stack([jnp.concatenate([pw_re[tc], pw_re[tc]], axis=-1),
                       jnp.concatenate([-pw_im[tc], pw_im[tc]], axis=-1)], axis=1)
    return tz, ws, wc, a_pow


def _ab_out_kernel(h_ref, oa_ref, ys_ref, u_ref, d_ref, wglu_ref, bglu_ref, wo_ref, g_ref, out_ref):
    mw = oa_ref.shape[1]
    y = ys_ref[...] + d_ref[...] * u_ref[...]
    ob = _gelu_tanh(y) * _sigmoid(_bdot(y, wglu_ref[...]) + bglu_ref[...])
    mix = _bdot(oa_ref[...], wo_ref[0:mw, :]) + _bdot(ob, wo_ref[mw:2 * mw, :])
    out_ref[...] = h_ref[...] + _rms(mix, g_ref[...])


def _ab_out(h, o_a, y_ssm, qkvgu, d_skip, w_glu, b_glu, w_out, gain, tm):
    t, d = h.shape
    mw = o_a.shape[1]
    full = lambda a: pl.BlockSpec(a.shape, lambda i: (0, 0))
    rows = lambda w, cb=0: pl.BlockSpec((tm, w), lambda i: (i, cb))
    return pl.pallas_call(
        _ab_out_kernel,
        grid=(t // tm,),
        in_specs=[rows(d), rows(mw), rows(mw), rows(mw, qkvgu.shape[1] // mw - 1), full(d_skip), full(w_glu),
                  full(b_glu), full(w_out), full(gain)],
        out_specs=rows(d),
        out_shape=jax.ShapeDtypeStruct((t, d), F32),
        compiler_params=_cparams("parallel"),
        name="ab_out",
    )(h, o_a, y_ssm, qkvgu, d_skip, w_glu, b_glu, w_out, gain)


def _mem_kv_kernel(mem_ref, g_ref, wk_ref, wv_ref, k_ref, v_ref):
    mn = _rms(mem_ref[...], g_ref[...]).astype(BF16)
    k_ref[...] = jnp.dot(mn, wk_ref[...], preferred_element_type=F32).astype(BF16)
    v_ref[...] = jnp.dot(mn, wv_ref[...], preferred_element_type=F32).astype(BF16)


def _mem_kv(mem, gain, wk, wv, tm):
    t, d = mem.shape
    full = lambda a: pl.BlockSpec(a.shape, lambda i: (0, 0))
    rows = pl.BlockSpec((tm, d), lambda i: (i, 0))
    return pl.pallas_call(
        _mem_kv_kernel,
        grid=(t // tm,),
        in_specs=[rows, full(gain), full(wk), full(wv)],
        out_specs=[rows, rows],
        out_shape=[jax.ShapeDtypeStruct((t, d), BF16), jax.ShapeDtypeStruct((t, d), BF16)],
        compiler_params=_cparams("parallel"),
        name="mem_kv",
    )(mem, gain, wk, wv)


def _xattn_kernel(h_ref, k_ref, v_ref, wq_ref, wo_ref, gq_ref, go_ref, out_ref, o_scr):
    h = h_ref[...]
    d = h.shape[1]
    hd = d // XA_HEADS
    q = jnp.dot(_rms(h, gq_ref[...]).astype(BF16), wq_ref[...], preferred_element_type=F32)
    for hh in range(XA_HEADS):
        cs = slice(hh * hd, (hh + 1) * hd)
        s = _bdot_nt(q[:, cs], k_ref[:, cs]) * (hd ** -0.5)
        e = jnp.exp(s - jnp.max(s, axis=-1, keepdims=True))
        p = e / jnp.sum(e, axis=-1, keepdims=True)
        o_scr[:, cs] = _bdot(p, v_ref[:, cs]).astype(BF16)
    xa = jnp.dot(o_scr[...], wo_ref[...], preferred_element_type=F32)
    out_ref[...] = h + _rms(xa, go_ref[...])


def _xattn(h, k, v, wq, wo, gq, go, bsz, seq, mem_len, tm):
    t, d = h.shape
    nt = seq // tm
    full = lambda a: pl.BlockSpec(a.shape, lambda b, i: (0, 0))
    rows = pl.BlockSpec((tm, d), lambda b, i: (b * nt + i, 0))
    kv = pl.BlockSpec((mem_len, d), lambda b, i: (b, 0))
    return pl.pallas_call(
        _xattn_kernel,
        grid=(bsz, nt),
        in_specs=[rows, kv, kv, full(wq), full(wo), full(gq), full(go)],
        out_specs=rows,
        out_shape=jax.ShapeDtypeStruct((t, d), F32),
        scratch_shapes=[pltpu.VMEM((tm, d), BF16)],
        compiler_params=_cparams("parallel", "parallel"),
        name="xattn",
    )(h, k, v, wq, wo, gq, go)


def _mlp_kernel(h_ref, gi_ref, go_ref, w1_ref, w2_ref, out_ref, xn_scr, acc_scr):
    j = pl.program_id(1)

    @pl.when(j == 0)
    def _():
        xn_scr[...] = _rms(h_ref[...], gi_ref[...]).astype(BF16)
        acc_scr[...] = jnp.zeros_like(acc_scr)

    a = jnp.maximum(jnp.dot(xn_scr[...], w1_ref[...], preferred_element_type=F32), 0.0)
    acc_scr[...] += jnp.dot((a * a).astype(BF16), w2_ref[...], preferred_element_type=F32)

    @pl.when(j == pl.num_programs(1) - 1)
    def _():
        out_ref[...] = h_ref[...] + _rms(acc_scr[...], go_ref[...])


def _mlp(h, gi, go, w1, w2, tm, tf):
    t, d = h.shape
    dff = w1.shape[1]
    full = lambda a: pl.BlockSpec(a.shape, lambda i, j: (0, 0))
    rows = pl.BlockSpec((tm, d), lambda i, j: (i, 0))
    return pl.pallas_call(
        _mlp_kernel,
        grid=(t // tm, dff // tf),
        in_specs=[rows, full(gi), full(go), pl.BlockSpec((d, tf), lambda i, j: (0, j)),
                  pl.BlockSpec((tf, d), lambda i, j: (j, 0))],
        out_specs=rows,
        out_shape=jax.ShapeDtypeStruct((t, d), F32),
        scratch_shapes=[pltpu.VMEM((tm, d), BF16), pltpu.VMEM((tm, d), F32)],
        compiler_params=_cparams("parallel", "arbitrary"),
        name="mlp",
    )(h, gi, go, w1, w2)


def _cd_in_kernel(h_ref, g_ref, wbig_ref, wsm_ref, mub_ref, mus_ref, w0_ref, w2_ref, a0_ref, a2_ref, g2_ref,
                  kk_ref, ka_ref, hsum_ref, cw_ref, cb_ref, wa_ref, ba_ref, wx_ref, bx_ref, lam_ref,
                  r_o, lw_o, k_o, v_o, kk_o, a_o, g_o, la_o, lb_o, gate_o,
                  carry_b, carry_s, carry_x, *, mw):
    tm = h_ref.shape[0]

    @pl.when(pl.program_id(1) == 0)
    def _():
        carry_b[...] = jnp.zeros_like(carry_b)
        carry_s[...] = jnp.zeros_like(carry_s)
        carry_x[...] = jnp.zeros_like(carry_x)

    hn = _rms(h_ref[...], g_ref[...]).astype(BF16)
    pb = jnp.dot(hn, wbig_ref[...], preferred_element_type=F32)
    ps = jnp.dot(hn, wsm_ref[...], preferred_element_type=F32)
    p3 = pb[:, 0:3 * mw]
    xb = pb[:, 3 * mw:4 * mw]
    gate_o[...] = pb[:, 4 * mw:5 * mw]

    prev3 = _shift_rows(p3, 1, carry_b[...])
    prevs = _shift_rows(ps, 1, carry_s[...])
    carry_b[...] = p3[tm - V7X_SUBLANES:tm, :]
    carry_s[...] = ps[tm - V7X_SUBLANES:tm, :]
    p3 = p3 + (prev3 - p3) * mub_ref[...]
    ps = ps + (prevs - ps) * mus_ref[...]
    r = p3[:, 0:mw]
    k = p3[:, mw:2 * mw]
    v = p3[:, 2 * mw:3 * mw]
    lora = ps[:, 0:128]
    wlog = -_softplus(-(w0_ref[...] + _bdot(jnp.tanh(lora), w2_ref[...]))) - 0.5
    a = _sigmoid(a0_ref[...] + _bdot(lora, a2_ref[...]))
    kkr = k * kk_ref[...]
    norm = jnp.sqrt(_split_dot(kkr * kkr, hsum_ref[...]))
    r_o[...] = r
    lw_o[...] = -jnp.exp(wlog)
    k_o[...] = k * (1.0 + (a - 1.0) * ka_ref[...])
    v_o[...] = v
    kk_o[...] = kkr / jnp.maximum(norm, 1e-12)
    a_o[...] = a
    g_o[...] = _bdot(_sigmoid(ps[:, 128:256]), g2_ref[...])

    cx = carry_x[...]
    xc = cb_ref[...] + cw_ref[LRU_CONV - 1:LRU_CONV, :] * xb
    for sh in range(1, LRU_CONV):
        xc = xc + cw_ref[LRU_CONV - 1 - sh:LRU_CONV - sh, :] * _shift_rows(xb, sh, cx)
    carry_x[...] = xb[tm - V7X_SUBLANES:tm, :]
    rg = _sigmoid(_bdot(xc, wa_ref[...]) + ba_ref[...])
    ig = _sigmoid(_bdot(xc, wx_ref[...]) + bx_ref[...])
    log_a = -LRU_C * rg * _softplus(-lam_ref[...])
    la_o[...] = jnp.exp(log_a)
    th = jnp.tanh(log_a)
    lb_o[...] = jnp.sqrt(-2.0 * th / (1.0 - th)) * (ig * xc)


def _cd_in(h, gain, wbig, wsm, vecs, mats, bsz, seq, tm, mw):
    t, d = h.shape
    nt = seq // tm
    full = lambda a: pl.BlockSpec(a.shape, lambda b, i: (0, 0))
    rows = lambda w: pl.BlockSpec((tm, w), lambda b, i: (b * nt + i, 0))
    (mub, mus, w0, a0, kk_w, ka_w, cw, cb, ba, bx, lam) = vecs
    (w2p, a2p, g2, hsum, wa, wx) = mats
    args = (h, gain, wbig, wsm, mub, mus, w0, w2p, a0, a2p, g2, kk_w, ka_w, hsum, cw, cb, wa, ba, wx, bx, lam)
    kern = functools.partial(_cd_in_kernel, mw=mw)
    return pl.pallas_call(
        kern,
        grid=(bsz, nt),
        in_specs=[rows(d)] + [full(a) for a in args[1:]],
        out_specs=[rows(mw)] * 10,
        out_shape=[jax.ShapeDtypeStruct((t, mw), F32)] * 10,
        scratch_shapes=[pltpu.VMEM((V7X_SUBLANES, 3 * mw), F32), pltpu.VMEM((V7X_SUBLANES, wsm.shape[1]), F32),
                        pltpu.VMEM((V7X_SUBLANES, mw), F32)],
        compiler_params=_cparams("parallel", "arbitrary"),
        name="cd_in",
    )(*args)


def _rwkv_kernel(r_ref, lw_ref, k_ref, v_ref, kk_ref, a_ref, g_ref, rk_ref, lng_ref, lnb_ref, hsum_ref,
                 o_ref, state_ref, *, slab_heads):
    c = RWKV_CHUNK
    hd = RWKV_HEAD
    sw = slab_heads * hd
    n_slabs = r_ref.shape[1] // sw

    @pl.when(pl.program_id(1) == 0)
    def _():
        state_ref[...] = jnp.zeros_like(state_ref)

    tril_f = _tril_mask(c).astype(F32)
    lw = lw_ref[...]
    cum = _hdot(tril_f, lw)
    cum_last = cum[c - 1:c, :]
    r = r_ref[...]
    k = k_ref[...]
    v = v_ref[...]
    kk = kk_ref[...]
    bvec = kk * a_ref[...]
    inv_g = jnp.exp(-cum)
    to_end = jnp.exp(cum_last - cum)
    rt = r * jnp.exp(cum)
    kp = kk * jnp.exp(cum - lw)
    be = bvec * inv_g
    kh = k * inv_g
    bb = bvec * to_end
    kb = k * to_end
    g_end = jnp.exp(cum_last)

    n = slab_heads * c
    ri = lax.broadcasted_iota(jnp.int32, (n, n), 0)
    ci = lax.broadcasted_iota(jnp.int32, (n, n), 1)
    same_blk = (ri // c) == (ci // c)
    strict_m = same_blk & (ci < ri)
    incl_m = same_blk & (ci <= ri)
    eye_n = ri == ci
    si = lax.broadcasted_iota(jnp.int32, (sw, sw), 0)
    sj = lax.broadcasted_iota(jnp.int32, (sw, sw), 1)
    head_m = (si // hd) == (sj // hd)
    eye_s = si == sj
    lane_head = lax.broadcasted_iota(jnp.int32, (c, sw), 1) // hd

    def stack_heads(x):
        return jnp.concatenate([jnp.where(lane_head == h, x, 0.0) for h in range(slab_heads)], axis=0)

    def tile_rows(x):
        return jnp.concatenate([x] * slab_heads, axis=0)

    def unstack(x):
        out = x[0:c]
        for h in range(1, slab_heads):
            out = out + x[h * c:(h + 1) * c]
        return out

    y_parts = []
    for s in range(n_slabs):
        ls = slice(s * sw, (s + 1) * sw)
        lk = stack_heads(kp[:, ls])
        lr = stack_heads(rt[:, ls])
        vst = stack_heads(v[:, ls])
        rb = tile_rows(be[:, ls])
        rkk = tile_rows(kh[:, ls])
        a_kb = jnp.where(strict_m, _hdot_nt(lk, rb), 0.0)
        a_kk = jnp.where(strict_m, _hdot_nt(lk, rkk), 0.0)
        a_rb = jnp.where(incl_m, _hdot_nt(lr, rb), 0.0)
        a_rk = jnp.where(incl_m, _hdot_nt(lr, rkk), 0.0)
        x = -a_kb
        tinv = jnp.where(eye_n, 1.0, 0.0) + x
        steps = int(math.log2(c)) - 1
        for _ in range(steps):
            x = _hdot(x, x)
            tinv = tinv + _hdot(tinv, x)
        w_t = _hdot(tinv, lk)
        u_t = _hdot(tinv, _hdot(a_kk, vst))
        q_t = unstack(lr - _hdot(a_rb, w_t))
        y0 = unstack(_hdot(a_rk, vst) - _hdot(a_rb, u_t))
        w_tok = unstack(w_t)
        u_tok = unstack(u_t)
        p = state_ref[s]
        y_parts.append(_hdot(q_t, p) + y0)
        m = jnp.where(eye_s, g_end[:, ls], 0.0) - jnp.where(head_m, _hdot_tn(bb[:, ls], w_tok), 0.0)
        dd = jnp.where(head_m, _hdot_tn(kb[:, ls], v[:, ls]) - _hdot_tn(bb[:, ls], u_tok), 0.0)
        state_ref[s] = _hdot(m, p) + dd
    y = jnp.concatenate(y_parts, axis=1) if n_slabs > 1 else y_parts[0]

    hsum = hsum_ref[...]
    mean = _split_dot(y, hsum) * (1.0 / hd)
    yc = y - mean
    var = _split_dot(yc * yc, hsum) * (1.0 / hd)
    yn = yc * lax.rsqrt(var + RWKV_GN_EPS) * lng_ref[...] + lnb_ref[...]
    bonus = _split_dot(r * k * rk_ref[...], hsum) * v
    o_ref[...] = (yn + bonus) * g_ref[...]


def _rwkv(r, lw, k, v, kk, a, g, rk, lng, lnb, hsum, bsz, seq, slab_heads=4):
    t, mw = r.shape
    c = RWKV_CHUNK
    nt = seq // c
    sw = slab_heads * RWKV_HEAD
    full = lambda x: pl.BlockSpec(x.shape, lambda b, i: (0, 0))
    rows = pl.BlockSpec((c, mw), lambda b, i: (b * nt + i, 0))
    kern = functools.partial(_rwkv_kernel, slab_heads=slab_heads)
    return pl.pallas_call(
        kern,
        grid=(bsz, nt),
        in_specs=[rows] * 7 + [full(rk), full(lng), full(lnb), full(hsum)],
        out_specs=rows,
        out_shape=jax.ShapeDtypeStruct((t, mw), F32),
        scratch_shapes=[pltpu.VMEM((mw // sw, sw, sw), F32)],
        compiler_params=_cparams("parallel", "arbitrary"),
        name="rwkv",
    )(r, lw, k, v, kk, a, g, rk, lng, lnb, hsum)


def _lru_kernel(a_ref, b_ref, gate_ref, o_ref, carry_ref):
    tm = a_ref.shape[0]
    sub = V7X_SUBLANES

    @pl.when(pl.program_id(1) == 0)
    def _():
        carry_ref[...] = jnp.zeros_like(carry_ref)

    rowi = lax.broadcasted_iota(jnp.int32, (sub, a_ref.shape[1]), 0)

    def body(j, h):
        r0 = pl.multiple_of(j * sub, sub)
        a = a_ref[pl.ds(r0, sub), :]
        b = b_ref[pl.ds(r0, sub), :]
        sh = 1
        while sh < sub:
            a_sh = jnp.where(rowi >= sh, pltpu.roll(a, sh, axis=0), 1.0)
            b_sh = jnp.where(rowi >= sh, pltpu.roll(b, sh, axis=0), 0.0)
            b = b + a * b_sh
            a = a * a_sh
            sh *= 2
        hb = b + a * h
        o_ref[pl.ds(r0, sub), :] = hb * _gelu_tanh(gate_ref[pl.ds(r0, sub), :])
        return hb[sub - 1:sub, :]

    carry_ref[0:1, :] = lax.fori_loop(0, tm // sub, body, carry_ref[0:1, :])


def _lru(a, b, gate, bsz, seq, tm):
    t, mw = a.shape
    nt = seq // tm
    rows = pl.BlockSpec((tm, mw), lambda bi, i: (bi * nt + i, 0))
    return pl.pallas_call(
        _lru_kernel,
        grid=(bsz, nt),
        in_specs=[rows, rows, rows],
        out_specs=rows,
        out_shape=jax.ShapeDtypeStruct((t, mw), F32),
        scratch_shapes=[pltpu.VMEM((V7X_SUBLANES, mw), F32)],
        compiler_params=_cparams("parallel", "arbitrary"),
        name="lru",
    )(a, b, gate)


def _cd_out_kernel(h_ref, oc_ref, od_ref, wo_ref, g_ref, out_ref):
    mw = oc_ref.shape[1]
    mix = _bdot(oc_ref[...], wo_ref[0:mw, :]) + _bdot(od_ref[...], wo_ref[mw:2 * mw, :])
    out_ref[...] = h_ref[...] + _rms(mix, g_ref[...])


def _cd_out(h, o_c, o_d, w_out, gain, tm):
    t, d = h.shape
    mw = o_c.shape[1]
    full = lambda a: pl.BlockSpec(a.shape, lambda i: (0, 0))
    rows = lambda w: pl.BlockSpec((tm, w), lambda i: (i, 0))
    return pl.pallas_call(
        _cd_out_kernel,
        grid=(t // tm,),
        in_specs=[rows(d), rows(mw), rows(mw), full(w_out), full(gain)],
        out_specs=rows(d),
        out_shape=jax.ShapeDtypeStruct((t, d), F32),
        compiler_params=_cparams("parallel"),
        name="cd_out",
    )(h, o_c, o_d, w_out, gain)


def _row(vec):
    return vec.astype(F32).reshape(1, -1)


def _block_diag(blocks):
    nb, bi, bo = blocks.shape
    eye = jnp.eye(nb, dtype=blocks.dtype)
    return (eye[:, None, :, None] * blocks[:, :, None, :]).reshape(nb * bi, nb * bo)


def _head_sum_matrix(width, head):
    idx = jnp.arange(width) // head
    return (idx[:, None] == idx[None, :]).astype(BF16)


def _layer_ab(h, gains, p, bsz, seq, tm):
    d = h.shape[1]
    mw = d // 2
    dk = p['gla_b_decay'].shape[0] // GLA_HEADS
    nq = GLA_HEADS * dk
    w_in = p['ab_w_in']
    rank = p['gla_w_decay2'].shape[0]
    o_dlr = 2 * nq + 2 * mw
    w_main = jnp.concatenate([w_in[:, :o_dlr], w_in[:, o_dlr + rank:]], axis=1).astype(BF16)
    w_dlr = jnp.pad(w_in[:, o_dlr:o_dlr + rank], ((0, 0), (0, 128 - rank))).astype(BF16)
    w_decay2 = jnp.pad(p['gla_w_decay2'].astype(F32), ((0, 128 - rank), (0, 0)))
    qkvgu, loga = _ab_in(h, _row(gains[0]), w_main, w_dlr, w_decay2, _row(p['gla_b_decay']), tm)
    o_a = _gla(qkvgu, loga, _row(p['gla_norm_gain']), bsz, seq, tb=256)

    groups = mw // S5_GROUP
    tc = S5_CHUNK
    u = qkvgu[:, 2 * nq + 2 * mw:]
    u_g = jnp.transpose(u.reshape(bsz, seq // tc, tc, groups, S5_GROUP), (3, 1, 0, 2, 4))
    u_g = u_g.reshape(groups, (seq // tc) * bsz, tc * S5_GROUP)
    tz, ws, wc, a_pow = _s5_weights(p['s5_lambda_re'], p['s5_lambda_im'], p['s5_log_step'], p['s5_b_re'],
                                    p['s5_b_im'], p['s5_c_re'], p['s5_c_im'])
    y_g = _s5(u_g, tz, ws, wc, a_pow, bsz)
    y_ssm = jnp.transpose(y_g.reshape(groups, seq // tc, bsz, tc, S5_GROUP), (2, 1, 3, 0, 4))
    y_ssm = y_ssm.reshape(bsz * seq, mw)
    return _ab_out(h, o_a, y_ssm, qkvgu, _row(p['s5_d']), p['s5_w_glu'].astype(BF16), _row(p['s5_b_glu']),
                   p['ab_w_out'].astype(BF16), _row(gains[1]), tm)


def _layer_cd(h, gains, p, bsz, seq, tm):
    d = h.shape[1]
    mw = d // 2
    w_in = p['cd_w_in']
    dr = p['rwkv_w2'].shape[0]
    ar = p['rwkv_a2'].shape[0]
    gr = p['rwkv_g2'].shape[0]
    assert dr + ar == 128 and gr == 128
    o = [0, mw, mw + dr, 2 * mw + dr, 3 * mw + dr, 3 * mw + dr + ar, 3 * mw + dr + ar + gr]
    col = lambda i, width: w_in[:, o[i]:o[i] + width]
    wbig = jnp.concatenate([col(0, mw), col(2, mw), col(3, mw), w_in[:, o[6]:]], axis=1).astype(BF16)
    wsm = jnp.concatenate([col(1, dr), col(4, ar), col(5, gr)], axis=1).astype(BF16)
    mu = p['rwkv_mu'].astype(F32)
    mseg = lambda i, width: mu[o[i]:o[i] + width]
    mub = jnp.concatenate([mseg(0, mw), mseg(2, mw), mseg(3, mw)]).reshape(1, -1)
    mus = jnp.concatenate([mseg(1, dr), mseg(4, ar), mseg(5, gr)]).reshape(1, -1)
    w2p = jnp.pad(p['rwkv_w2'], ((0, ar), (0, 0))).astype(BF16)
    a2p = jnp.pad(p['rwkv_a2'], ((dr, 0), (0, 0))).astype(BF16)
    hsum = _head_sum_matrix(mw, RWKV_HEAD)
    vecs = (mub, mus, _row(p['rwkv_w0']), _row(p['rwkv_a0']), _row(p['rwkv_k_k']), _row(p['rwkv_k_a']),
            p['lru_conv_w'].astype(F32), _row(p['lru_conv_b']), _row(p['lru_b_a']), _row(p['lru_b_x']),
            _row(p['lru_lambda']))
    mats = (w2p, a2p, p['rwkv_g2'].astype(BF16), hsum, _block_diag(p['lru_w_a']).astype(BF16),
            _block_diag(p['lru_w_x']).astype(BF16))
    r, lw, k, v, kk, a, g, la, lb, gate = _cd_in(h, _row(gains[0]), wbig, wsm, vecs, mats, bsz, seq, tm, mw)
    o_c = _rwkv(r, lw, k, v, kk, a, g, _row(p['rwkv_r_k']), _row(p['rwkv_ln_gain']), _row(p['rwkv_ln_bias']),
                hsum, bsz, seq)
    o_d = _lru(la, lb, gate, bsz, seq, tm)
    return _cd_out(h, o_c, o_d, p['cd_w_out'].astype(BF16), _row(gains[1]), tm)


def kernel(x, mem, norm_gain, xa_wq, xa_wk, xa_wv, xa_wo, mlp_w1, mlp_w2, ab_w_in, gla_w_decay2, gla_b_decay, gla_norm_gain, s5_lambda_re, s5_lambda_im, s5_log_step, s5_b_re, s5_b_im, s5_c_re, s5_c_im, s5_d, s5_w_glu, s5_b_glu, ab_w_out, cd_w_in, rwkv_mu, rwkv_w0, rwkv_w2, rwkv_a0, rwkv_a2, rwkv_g2, rwkv_k_k, rwkv_k_a, rwkv_r_k, rwkv_ln_gain, rwkv_ln_bias, lru_conv_w, lru_conv_b, lru_w_a, lru_b_a, lru_w_x, lru_b_x, lru_lambda, cd_w_out):
    bsz, seq, d = x.shape
    mem_len = mem.shape[1]
    depth = norm_gain.shape[0]
    tm = min(512, seq)
    ab = dict(ab_w_in=ab_w_in, gla_w_decay2=gla_w_decay2, gla_b_decay=gla_b_decay, gla_norm_gain=gla_norm_gain,
              s5_lambda_re=s5_lambda_re, s5_lambda_im=s5_lambda_im, s5_log_step=s5_log_step, s5_b_re=s5_b_re,
              s5_b_im=s5_b_im, s5_c_re=s5_c_re, s5_c_im=s5_c_im, s5_d=s5_d, s5_w_glu=s5_w_glu, s5_b_glu=s5_b_glu,
              ab_w_out=ab_w_out)
    cd = dict(cd_w_in=cd_w_in, rwkv_mu=rwkv_mu, rwkv_w0=rwkv_w0, rwkv_w2=rwkv_w2, rwkv_a0=rwkv_a0,
              rwkv_a2=rwkv_a2, rwkv_g2=rwkv_g2, rwkv_k_k=rwkv_k_k, rwkv_k_a=rwkv_k_a, rwkv_r_k=rwkv_r_k,
              rwkv_ln_gain=rwkv_ln_gain, rwkv_ln_bias=rwkv_ln_bias, lru_conv_w=lru_conv_w, lru_conv_b=lru_conv_b,
              lru_w_a=lru_w_a, lru_b_a=lru_b_a, lru_w_x=lru_w_x, lru_b_x=lru_b_x, lru_lambda=lru_lambda,
              cd_w_out=cd_w_out)
    h = x.astype(F32).reshape(bsz * seq, d)
    mem2 = mem.astype(F32).reshape(bsz * mem_len, d)
    for layer in range(depth):
        g = norm_gain[layer]
        i = layer // 2
        if layer % 2 == 0:
            h = _layer_ab(h, g, {n: w[i] for n, w in ab.items()}, bsz, seq, tm)
        else:
            h = _layer_cd(h, g, {n: w[i] for n, w in cd.items()}, bsz, seq, tm)
        km, vm = _mem_kv(mem2, _row(g[6]), xa_wk[layer].astype(BF16), xa_wv[layer].astype(BF16), mem_len)
        h = _xattn(h, km, vm, xa_wq[layer].astype(BF16), xa_wo[layer].astype(BF16), _row(g[2]), _row(g[3]),
                   bsz, seq, mem_len, tm)
        h = _mlp(h, _row(g[4]), _row(g[5]), mlp_w1[layer].astype(BF16), mlp_w2[layer].astype(BF16),
                 tm=min(1024, bsz * seq), tf=1024)
    return h.reshape(bsz, seq, d).astype(x.dtype)
```

```python
import functools
import math

import jax
import jax.numpy as jnp
from jax import lax
from jax.experimental import pallas as pl
from jax.experimental.pallas import tpu as pltpu

F32 = jnp.float32
BF16 = jnp.bfloat16
HIGHEST = lax.Precision.HIGHEST

NORM_EPS = 1e-6
GLA_HEADS = 4
GLA_TAU = 16.0
GLA_CHUNK = 64
S5_GROUP = 16
S5_STATE = 64
S5_CHUNK = 16
RWKV_HEAD = 64
RWKV_CHUNK = 64
RWKV_GN_EPS = 64e-5
LRU_CONV = 4
LRU_C = 8.0
XA_HEADS = 4

V7X_SUBLANES = 8
VMEM_LIMIT_BYTES = 48 * 1024 * 1024


def _cparams(*semantics):
    return pltpu.CompilerParams(dimension_semantics=semantics, vmem_limit_bytes=VMEM_LIMIT_BYTES)


def _rms(x, gain):
    return x * lax.rsqrt(jnp.mean(x * x, axis=-1, keepdims=True) + NORM_EPS) * gain


def _bdot(a, b):
    return jnp.dot(a.astype(BF16), b.astype(BF16), preferred_element_type=F32)


def _bdot_nt(a, b):
    return lax.dot_general(a.astype(BF16), b.astype(BF16), (((1,), (1,)), ((), ())),
                           preferred_element_type=F32)


def _bdot_tn(a, b):
    return lax.dot_general(a.astype(BF16), b.astype(BF16), (((0,), (0,)), ((), ())),
                           preferred_element_type=F32)


def _hdot(a, b):
    return jnp.dot(a, b, precision=HIGHEST, preferred_element_type=F32)


def _hdot_nt(a, b):
    return lax.dot_general(a, b, (((1,), (1,)), ((), ())), precision=HIGHEST,
                           preferred_element_type=F32)


def _hdot_tn(a, b):
    return lax.dot_general(a, b, (((0,), (0,)), ((), ())), precision=HIGHEST,
                           preferred_element_type=F32)


def _split_dot(x, w_bf16):
    hi = x.astype(BF16)
    lo = (x - hi.astype(F32)).astype(BF16)
    return (jnp.dot(hi, w_bf16, preferred_element_type=F32)
            + jnp.dot(lo, w_bf16, preferred_element_type=F32))


def _sigmoid(x):
    return 1.0 / (1.0 + jnp.exp(-x))


def _softplus(x):
    return jnp.maximum(x, 0.0) + jnp.log1p(jnp.exp(-jnp.abs(x)))


def _gelu_tanh(x):
    c = math.sqrt(2.0 / math.pi)
    return x * (0.5 * (1.0 + jnp.tanh(c * (x + 0.044715 * (x * x * x)))))


def _tril_mask(n, strict=False):
    row = lax.broadcasted_iota(jnp.int32, (n, n), 0)
    col = lax.broadcasted_iota(jnp.int32, (n, n), 1)
    return (col < row) if strict else (col <= row)


def _shift_rows(x, shift, carry):
    rolled = pltpu.roll(x, shift, axis=0)
    crolled = pltpu.roll(carry, shift, axis=0)
    rowi = lax.broadcasted_iota(jnp.int32, carry.shape, 0)
    first = jnp.where(rowi < shift, crolled, rolled[0:V7X_SUBLANES])
    return jnp.concatenate([first, rolled[V7X_SUBLANES:]], axis=0)


def _ab_in_kernel(h_ref, g_ref, w_ref, wd_ref, wd2_ref, bd_ref, out_ref, loga_ref):
    hn = _rms(h_ref[...], g_ref[...]).astype(BF16)
    out_ref[...] = jnp.dot(hn, w_ref[...], preferred_element_type=F32)
    dlr = jnp.dot(hn, wd_ref[...], preferred_element_type=F32)
    z = _hdot(dlr, wd2_ref[...]) + bd_ref[...]
    loga_ref[...] = (jnp.minimum(z, 0.0) - jnp.log1p(jnp.exp(-jnp.abs(z)))) * (1.0 / GLA_TAU)


def _ab_in(h, gain, w_main, w_dlr, w_decay2, b_decay, tm):
    t, d = h.shape
    n_main = w_main.shape[1]
    n_dk = w_decay2.shape[1]
    full = lambda a: pl.BlockSpec(a.shape, lambda i: (0, 0))
    return pl.pallas_call(
        _ab_in_kernel,
        grid=(t // tm,),
        in_specs=[pl.BlockSpec((tm, d), lambda i: (i, 0)), full(gain), full(w_main), full(w_dlr),
                  full(w_decay2), full(b_decay)],
        out_specs=[pl.BlockSpec((tm, n_main), lambda i: (i, 0)), pl.BlockSpec((tm, n_dk), lambda i: (i, 0))],
        out_shape=[jax.ShapeDtypeStruct((t, n_main), F32), jax.ShapeDtypeStruct((t, n_dk), F32)],
        compiler_params=_cparams("parallel"),
        name="ab_in",
    )(h, gain, w_main, w_dlr, w_decay2, b_decay)


def _gla_kernel(q_ref, k_ref, v_ref, gate_ref, la_ref, gain_ref, o_ref, state_ref, *, n_chunks, dk, dv):
    c = GLA_CHUNK

    @pl.when(pl.program_id(1) == 0)
    def _():
        state_ref[...] = jnp.zeros_like(state_ref)

    tril = _tril_mask(c)
    tril_f = tril.astype(F32)
    scale = dk ** -0.5
    for ci in range(n_chunks):
        rs = slice(ci * c, (ci + 1) * c)
        b = _hdot(tril_f, la_ref[rs, :])
        b_last = b[c - 1:c, :]
        q_in = q_ref[rs, :] * jnp.exp(b) * scale
        k = k_ref[rs, :]
        k_in = k * jnp.exp(-b)
        k_st = k * jnp.exp(b_last - b)
        dec = jnp.exp(b_last)
        for h in range(GLA_HEADS):
            ks = slice(h * dk, (h + 1) * dk)
            vs = slice(h * dv, (h + 1) * dv)
            qh = q_in[:, ks]
            vh = v_ref[rs, vs]
            scores = jnp.where(tril, _bdot_nt(qh, k_in[:, ks]), 0.0)
            st = state_ref[h]
            o = _bdot(scores, vh) + _bdot_nt(qh, st)
            state_ref[h] = st * dec[:, ks] + _bdot_tn(vh, k_st[:, ks])
            on = o * lax.rsqrt(jnp.mean(o * o, axis=-1, keepdims=True) + NORM_EPS) * gain_ref[:, vs]
            g = gate_ref[rs, vs]
            o_ref[rs, vs] = on * (g * _sigmoid(g))


def _gla(qkvgu, loga, gain, bsz, seq, tb):
    t = bsz * seq
    dk = loga.shape[1] // GLA_HEADS
    dv = gain.shape[1] // GLA_HEADS
    nq, nv = GLA_HEADS * dk, GLA_HEADS * dv
    nt = seq // tb
    row = lambda b, i: b * nt + i
    kern = functools.partial(_gla_kernel, n_chunks=tb // GLA_CHUNK, dk=dk, dv=dv)
    return pl.pallas_call(
        kern,
        grid=(bsz, nt),
        in_specs=[pl.BlockSpec((tb, nq), lambda b, i: (row(b, i), 0)),
                  pl.BlockSpec((tb, nq), lambda b, i: (row(b, i), 1)),
                  pl.BlockSpec((tb, nv), lambda b, i: (row(b, i), 1)),
                  pl.BlockSpec((tb, nv), lambda b, i: (row(b, i), 2)),
                  pl.BlockSpec((tb, nq), lambda b, i: (row(b, i), 0)),
                  pl.BlockSpec(gain.shape, lambda b, i: (0, 0))],
        out_specs=pl.BlockSpec((tb, nv), lambda b, i: (row(b, i), 0)),
        out_shape=jax.ShapeDtypeStruct((t, nv), F32),
        scratch_shapes=[pltpu.VMEM((GLA_HEADS, dv, dk), F32)],
        compiler_params=_cparams("parallel", "arbitrary"),
        name="gla",
    )(qkvgu, qkvgu, qkvgu, qkvgu, loga, gain)


def _s5_kernel(u_ref, tz_ref, ws_ref, wc_ref, a_ref, y_ref, s_scr, h_scr, *, n_pairs, bsz):
    n2 = 2 * S5_STATE
    u = u_ref[0]
    s_scr[...] = _hdot(u, ws_ref[0])
    a1 = a_ref[0, 0:1, :]
    a2 = a_ref[0, 1:2, :]

    def body(j, carry):
        h, hs = carry
        r0 = pl.multiple_of(j * (2 * bsz), 2 * bsz)
        blk = s_scr[pl.ds(r0, 2 * bsz), :]
        h1 = a1 * h + a2 * hs + blk[0:bsz, 0:n2]
        hs1 = a1 * hs - a2 * h + blk[0:bsz, n2:2 * n2]
        h2 = a1 * h1 + a2 * hs1 + blk[bsz:2 * bsz, 0:n2]
        hs2 = a1 * hs1 - a2 * h1 + blk[bsz:2 * bsz, n2:2 * n2]
        h_scr[pl.ds(r0, 2 * bsz), :] = jnp.concatenate([h, h1], axis=0)
        return h2, hs2

    zero = jnp.zeros((bsz, n2), F32)
    lax.fori_loop(0, n_pairs, body, (zero, zero))
    y_ref[0] = _hdot(u, tz_ref[0]) + _hdot(h_scr[...], wc_ref[0])


def _s5(u_g, tz, ws, wc, a_pow, bsz):
    groups, rows, width = u_g.shape
    assert 2 * bsz == V7X_SUBLANES, "the chunk scan walks two chunks per 8-row tile"
    n_pairs = rows // (2 * bsz)
    per_g = lambda a: pl.BlockSpec((1,) + a.shape[1:], lambda g: (g, 0, 0))
    kern = functools.partial(_s5_kernel, n_pairs=n_pairs, bsz=bsz)
    return pl.pallas_call(
        kern,
        grid=(groups,),
        in_specs=[per_g(u_g), per_g(tz), per_g(ws), per_g(wc), per_g(a_pow)],
        out_specs=per_g(u_g),
        out_shape=jax.ShapeDtypeStruct(u_g.shape, F32),
        scratch_shapes=[pltpu.VMEM((rows, ws.shape[2]), F32), pltpu.VMEM((rows, wc.shape[1]), F32)],
        compiler_params=_cparams("parallel"),
        name="s5",
    )(u_g, tz, ws, wc, a_pow)


def _s5_weights(lam_re, lam_im, log_step, b_re, b_im, c_re, c_im):
    tc = S5_CHUNK
    groups, n = lam_re.shape
    lr = jnp.minimum(lam_re.astype(F32), -1e-4)
    li = lam_im.astype(F32)
    delta = jnp.exp(log_step.astype(F32))[:, None]
    tau = jnp.arange(tc + 1, dtype=F32)[:, None, None]
    mag = jnp.exp(tau * (lr * delta))
    ang = tau * (li * delta)
    pw_re, pw_im = mag * jnp.cos(ang), mag * jnp.sin(ang)
    num_re, num_im = pw_re[1] - 1.0, pw_im[1]
    den = lr * lr + li * li
    f_re = (num_re * lr + num_im * li) / den
    f_im = (num_im * lr - num_re * li) / den
    b_re, b_im = b_re.astype(F32), b_im.astype(F32)
    bb_re = f_re[..., None] * b_re - f_im[..., None] * b_im
    bb_im = f_re[..., None] * b_im + f_im[..., None] * b_re
    c_re, c_im = c_re.astype(F32), c_im.astype(F32)
    cp_re = c_re[None] * pw_re[:, :, None, :] - c_im[None] * pw_im[:, :, None, :]
    cp_im = c_re[None] * pw_im[:, :, None, :] + c_im[None] * pw_re[:, :, None, :]
    ein = functools.partial(jnp.einsum, precision=HIGHEST)
    kern = ein('tgcn,gnd->tgcd', cp_re[:tc], bb_re) - ein('tgcn,gnd->tgcd', cp_im[:tc], bb_im)
    j = jnp.arange(tc)[:, None]
    i = jnp.arange(tc)[None, :]
    diff = i - j
    ksel = jnp.where((diff >= 0)[:, :, None, None, None], kern[jnp.clip(diff, 0, tc - 1)], 0.0)
    tz = jnp.transpose(ksel, (2, 0, 4, 1, 3)).reshape(groups, tc * S5_GROUP, tc * S5_GROUP)
    rev_re, rev_im = pw_re[tc - 1::-1][:tc], pw_im[tc - 1::-1][:tc]
    s_re = rev_re[:, :, :, None] * bb_re[None] - rev_im[:, :, :, None] * bb_im[None]
    s_im = rev_re[:, :, :, None] * bb_im[None] + rev_im[:, :, :, None] * bb_re[None]
    s_re = jnp.transpose(s_re, (1, 0, 3, 2)).reshape(groups, tc * S5_GROUP, n)
    s_im = jnp.transpose(s_im, (1, 0, 3, 2)).reshape(groups, tc * S5_GROUP, n)
    ws = jnp.concatenate([s_re, s_im, s_im, s_re], axis=-1)
    o_re = jnp.transpose(cp_re[1:], (1, 3, 0, 2)).reshape(groups, n, tc * S5_GROUP)
    o_im = jnp.transpose(cp_im[1:], (1, 3, 0, 2)).reshape(groups, n, tc * S5_GROUP)
    wc = jnp.concatenate([o_re, -o_im], axis=1)
    a_pow = jnp.stack([jnp.concatenate([pw_re[tc], pw_re[tc]], axis=-1),
                       jnp.concatenate([-pw_im[tc], pw_im[tc]], axis=-1)], axis=1)
    return tz, ws, wc, a_pow


def _ab_out_kernel(h_ref, oa_ref, ys_ref, u_ref, d_ref, wglu_ref, bglu_ref, wo_ref, g_ref, out_ref):
    mw = oa_ref.shape[1]
    y = ys_ref[...] + d_ref[...] * u_ref[...]
    ob = _gelu_tanh(y) * _sigmoid(_bdot(y, wglu_ref[...]) + bglu_ref[...])
    mix = _bdot(oa_ref[...], wo_ref[0:mw, :]) + _bdot(ob, wo_ref[mw:2 * mw, :])
    out_ref[...] = h_ref[...] + _rms(mix, g_ref[...])


def _ab_out(h, o_a, y_ssm, qkvgu, d_skip, w_glu, b_glu, w_out, gain, tm):
    t, d = h.shape
    mw = o_a.shape[1]
    full = lambda a: pl.BlockSpec(a.shape, lambda i: (0, 0))
    rows = lambda w, cb=0: pl.BlockSpec((tm, w), lambda i: (i, cb))
    return pl.pallas_call(
        _ab_out_kernel,
        grid=(t // tm,),
        in_specs=[rows(d), rows(mw), rows(mw), rows(mw, qkvgu.shape[1] // mw - 1), full(d_skip), full(w_glu),
                  full(b_glu), full(w_out), full(gain)],
        out_specs=rows(d),
        out_shape=jax.ShapeDtypeStruct((t, d), F32),
        compiler_params=_cparams("parallel"),
        name="ab_out",
    )(h, o_a, y_ssm, qkvgu, d_skip, w_glu, b_glu, w_out, gain)


def _mem_kv_kernel(mem_ref, g_ref, wk_ref, wv_ref, k_ref, v_ref):
    mn = _rms(mem_ref[...], g_ref[...]).astype(BF16)
    k_ref[...] = jnp.dot(mn, wk_ref[...], preferred_element_type=F32).astype(BF16)
    v_ref[...] = jnp.dot(mn, wv_ref[...], preferred_element_type=F32).astype(BF16)


def _mem_kv(mem, gain, wk, wv, tm):
    t, d = mem.shape
    full = lambda a: pl.BlockSpec(a.shape, lambda i: (0, 0))
    rows = pl.BlockSpec((tm, d), lambda i: (i, 0))
    return pl.pallas_call(
        _mem_kv_kernel,
        grid=(t // tm,),
        in_specs=[rows, full(gain), full(wk), full(wv)],
        out_specs=[rows, rows],
        out_shape=[jax.ShapeDtypeStruct((t, d), BF16), jax.ShapeDtypeStruct((t, d), BF16)],
        compiler_params=_cparams("parallel"),
        name="mem_kv",
    )(mem, gain, wk, wv)


def _xattn_kernel(h_ref, k_ref, v_ref, wq_ref, wo_ref, gq_ref, go_ref, out_ref, o_scr):
    h = h_ref[...]
    d = h.shape[1]
    hd = d // XA_HEADS
    q = jnp.dot(_rms(h, gq_ref[...]).astype(BF16), wq_ref[...], preferred_element_type=F32)
    for hh in range(XA_HEADS):
        cs = slice(hh * hd, (hh + 1) * hd)
        s = _bdot_nt(q[:, cs], k_ref[:, cs]) * (hd ** -0.5)
        e = jnp.exp(s - jnp.max(s, axis=-1, keepdims=True))
        p = e / jnp.sum(e, axis=-1, keepdims=True)
        o_scr[:, cs] = _bdot(p, v_ref[:, cs]).astype(BF16)
    xa = jnp.dot(o_scr[...], wo_ref[...], preferred_element_type=F32)
    out_ref[...] = h + _rms(xa, go_ref[...])


def _xattn(h, k, v, wq, wo, gq, go, bsz, seq, mem_len, tm):
    t, d = h.shape
    nt = seq // tm
    full = lambda a: pl.BlockSpec(a.shape, lambda b, i: (0, 0))
    rows = pl.BlockSpec((tm, d), lambda b, i: (b * nt + i, 0))
    kv = pl.BlockSpec((mem_len, d), lambda b, i: (b, 0))
    return pl.pallas_call(
        _xattn_kernel,
        grid=(bsz, nt),
        in_specs=[rows, kv, kv, full(wq), full(wo), full(gq), full(go)],
        out_specs=rows,
        out_shape=jax.ShapeDtypeStruct((t, d), F32),
        scratch_shapes=[pltpu.VMEM((tm, d), BF16)],
        compiler_params=_cparams("parallel", "parallel"),
        name="xattn",
    )(h, k, v, wq, wo, gq, go)


def _mlp_kernel(h_ref, gi_ref, go_ref, w1_ref, w2_ref, out_ref, xn_scr, acc_scr):
    j = pl.program_id(1)

    @pl.when(j == 0)
    def _():
        xn_scr[...] = _rms(h_ref[...], gi_ref[...]).astype(BF16)
        acc_scr[...] = jnp.zeros_like(acc_scr)

    a = jnp.maximum(jnp.dot(xn_scr[...], w1_ref[...], preferred_element_type=F32), 0.0)
    acc_scr[...] += jnp.dot((a * a).astype(BF16), w2_ref[...], preferred_element_type=F32)

    @pl.when(j == pl.num_programs(1) - 1)
    def _():
        out_ref[...] = h_ref[...] + _rms(acc_scr[...], go_ref[...])


def _mlp(h, gi, go, w1, w2, tm, tf):
    t, d = h.shape
    dff = w1.shape[1]
    full = lambda a: pl.BlockSpec(a.shape, lambda i, j: (0, 0))
    rows = pl.BlockSpec((tm, d), lambda i, j: (i, 0))
    return pl.pallas_call(
        _mlp_kernel,
        grid=(t // tm, dff // tf),
        in_specs=[rows, full(gi), full(go), pl.BlockSpec((d, tf), lambda i, j: (0, j)),
                  pl.BlockSpec((tf, d), lambda i, j: (j, 0))],
        out_specs=rows,
        out_shape=jax.ShapeDtypeStruct((t, d), F32),
        scratch_shapes=[pltpu.VMEM((tm, d), BF16), pltpu.VMEM((tm, d), F32)],
        compiler_params=_cparams("parallel", "arbitrary"),
        name="mlp",
    )(h, gi, go, w1, w2)


def _cd_in_kernel(h_ref, g_ref, wbig_ref, wsm_ref, mub_ref, mus_ref, w0_ref, w2_ref, a0_ref, a2_ref, g2_ref,
                  kk_ref, ka_ref, hsum_ref, cw_ref, cb_ref, wa_ref, ba_ref, wx_ref, bx_ref, lam_ref,
                  r_o, lw_o, k_o, v_o, kk_o, a_o, g_o, la_o, lb_o, gate_o,
                  carry_b, carry_s, carry_x, *, mw):
    tm = h_ref.shape[0]

    @pl.when(pl.program_id(1) == 0)
    def _():
        carry_b[...] = jnp.zeros_like(carry_b)
        carry_s[...] = jnp.zeros_like(carry_s)
        carry_x[...] = jnp.zeros_like(carry_x)

    hn = _rms(h_ref[...], g_ref[...]).astype(BF16)
    pb = jnp.dot(hn, wbig_ref[...], preferred_element_type=F32)
    ps = jnp.dot(hn, wsm_ref[...], preferred_element_type=F32)
    p3 = pb[:, 0:3 * mw]
    xb = pb[:, 3 * mw:4 * mw]
    gate_o[...] = pb[:, 4 * mw:5 * mw]

    prev3 = _shift_rows(p3, 1, carry_b[...])
    prevs = _shift_rows(ps, 1, carry_s[...])
    carry_b[...] = p3[tm - V7X_SUBLANES:tm, :]
    carry_s[...] = ps[tm - V7X_SUBLANES:tm, :]
    p3 = p3 + (prev3 - p3) * mub_ref[...]
    ps = ps + (prevs - ps) * mus_ref[...]
    r = p3[:, 0:mw]
    k = p3[:, mw:2 * mw]
    v = p3[:, 2 * mw:3 * mw]
    lora = ps[:, 0:128]
    wlog = -_softplus(-(w0_ref[...] + _bdot(jnp.tanh(lora), w2_ref[...]))) - 0.5
    a = _sigmoid(a0_ref[...] + _bdot(lora, a2_ref[...]))
    kkr = k * kk_ref[...]
    norm = jnp.sqrt(_split_dot(kkr * kkr, hsum_ref[...]))
    r_o[...] = r
    lw_o[...] = -jnp.exp(wlog)
    k_o[...] = k * (1.0 + (a - 1.0) * ka_ref[...])
    v_o[...] = v
    kk_o[...] = kkr / jnp.maximum(norm, 1e-12)
    a_o[...] = a
    g_o[...] = _bdot(_sigmoid(ps[:, 128:256]), g2_ref[...])

    cx = carry_x[...]
    xc = cb_ref[...] + cw_ref[LRU_CONV - 1:LRU_CONV, :] * xb
    for sh in range(1, LRU_CONV):
        xc = xc + cw_ref[LRU_CONV - 1 - sh:LRU_CONV - sh, :] * _shift_rows(xb, sh, cx)
    carry_x[...] = xb[tm - V7X_SUBLANES:tm, :]
    rg = _sigmoid(_bdot(xc, wa_ref[...]) + ba_ref[...])
    ig = _sigmoid(_bdot(xc, wx_ref[...]) + bx_ref[...])
    log_a = -LRU_C * rg * _softplus(-lam_ref[...])
    la_o[...] = jnp.exp(log_a)
    th = jnp.tanh(log_a)
    lb_o[...] = jnp.sqrt(-2.0 * th / (1.0 - th)) * (ig * xc)


def _cd_in(h, gain, wbig, wsm, vecs, mats, bsz, seq, tm, mw):
    t, d = h.shape
    nt = seq // tm
    full = lambda a: pl.BlockSpec(a.shape, lambda b, i: (0, 0))
    rows = lambda w: pl.BlockSpec((tm, w), lambda b, i: (b * nt + i, 0))
    (mub, mus, w0, a0, kk_w, ka_w, cw, cb, ba, bx, lam) = vecs
    (w2p, a2p, g2, hsum, wa, wx) = mats
    args = (h, gain, wbig, wsm, mub, mus, w0, w2p, a0, a2p, g2, kk_w, ka_w, hsum, cw, cb, wa, ba, wx, bx, lam)
    kern = functools.partial(_cd_in_kernel, mw=mw)
    return pl.pallas_call(
        kern,
        grid=(bsz, nt),
        in_specs=[rows(d)] + [full(a) for a in args[1:]],
        out_specs=[rows(mw)] * 10,
        out_shape=[jax.ShapeDtypeStruct((t, mw), F32)] * 10,
        scratch_shapes=[pltpu.VMEM((V7X_SUBLANES, 3 * mw), F32), pltpu.VMEM((V7X_SUBLANES, wsm.shape[1]), F32),
                        pltpu.VMEM((V7X_SUBLANES, mw), F32)],
        compiler_params=_cparams("parallel", "arbitrary"),
        name="cd_in",
    )(*args)


def _rwkv_kernel(r_ref, lw_ref, k_ref, v_ref, kk_ref, a_ref, g_ref, rk_ref, lng_ref, lnb_ref, hsum_ref,
                 o_ref, state_ref, *, slab_heads, n_chunks):
    c = RWKV_CHUNK
    hd = RWKV_HEAD
    sw = slab_heads * hd
    n_slabs = r_ref.shape[1] // sw

    @pl.when(pl.program_id(1) == 0)
    def _():
        state_ref[...] = jnp.zeros_like(state_ref)

    tril_f = _tril_mask(c).astype(F32)
    n = slab_heads * c
    ri = lax.broadcasted_iota(jnp.int32, (n, n), 0)
    ci = lax.broadcasted_iota(jnp.int32, (n, n), 1)
    same_blk = (ri // c) == (ci // c)
    strict_m = same_blk & (ci < ri)
    incl_m = same_blk & (ci <= ri)
    eye_f = jnp.where(ri == ci, 1.0, 0.0)
    si = lax.broadcasted_iota(jnp.int32, (sw, sw), 0)
    sj = lax.broadcasted_iota(jnp.int32, (sw, sw), 1)
    head_m = (si // hd) == (sj // hd)
    eye_s = si == sj
    lane_head = lax.broadcasted_iota(jnp.int32, (c, sw), 1) // hd
    hsum = hsum_ref[...]

    def stack_heads(x):
        return jnp.concatenate([jnp.where(lane_head == h, x, 0.0) for h in range(slab_heads)], axis=0)

    def tile_rows(x):
        return jnp.concatenate([x] * slab_heads, axis=0)

    def unstack(x):
        out = x[0:c]
        for h in range(1, slab_heads):
            out = out + x[h * c:(h + 1) * c]
        return out

    for chunk in range(n_chunks):
        rs = slice(chunk * c, (chunk + 1) * c)
        lw = lw_ref[rs, :]
        cum = _hdot(tril_f, lw)
        cum_last = cum[c - 1:c, :]
        r = r_ref[rs, :]
        k = k_ref[rs, :]
        v = v_ref[rs, :]
        kk = kk_ref[rs, :]
        bvec = kk * a_ref[rs, :]
        inv_g = jnp.exp(-cum)
        to_end = jnp.exp(cum_last - cum)
        rt = r * jnp.exp(cum)
        kp = kk * jnp.exp(cum - lw)
        be = bvec * inv_g
        kh = k * inv_g
        bb = bvec * to_end
        kb = k * to_end
        g_end = jnp.exp(cum_last)

        y_parts = []
        for s in range(n_slabs):
            ls = slice(s * sw, (s + 1) * sw)
            lk = stack_heads(kp[:, ls])
            lr = stack_heads(rt[:, ls])
            vst = stack_heads(v[:, ls])
            rb = tile_rows(be[:, ls])
            rkk = tile_rows(kh[:, ls])
            a_kb = jnp.where(strict_m, _bdot_nt(lk, rb), 0.0)
            a_kk = jnp.where(strict_m, _bdot_nt(lk, rkk), 0.0)
            a_rb = jnp.where(incl_m, _bdot_nt(lr, rb), 0.0)
            a_rk = jnp.where(incl_m, _bdot_nt(lr, rkk), 0.0)
            x = -a_kb
            tinv = eye_f + x
            for _ in range(int(math.log2(c)) - 1):
                x = _bdot(x, x)
                tinv = tinv + _bdot(tinv, x)
            resid = eye_f - tinv - _hdot(a_kb, tinv)
            tinv = tinv + _bdot(tinv, resid)
            w_t = _bdot(tinv, lk)
            u_t = _bdot(tinv, _bdot(a_kk, vst))
            q_t = unstack(lr - _bdot(a_rb, w_t))
            y0 = unstack(_bdot(a_rk, vst) - _bdot(a_rb, u_t))
            w_tok = unstack(w_t)
            u_tok = unstack(u_t)
            p = state_ref[s]
            y_parts.append(_bdot(q_t, p) + y0)
            m = jnp.where(eye_s, g_end[:, ls], 0.0) - jnp.where(head_m, _bdot_tn(bb[:, ls], w_tok), 0.0)
            dd = jnp.where(head_m, _bdot_tn(kb[:, ls], v[:, ls]) - _bdot_tn(bb[:, ls], u_tok), 0.0)
            state_ref[s] = _bdot(m, p) + dd
        y = jnp.concatenate(y_parts, axis=1) if n_slabs > 1 else y_parts[0]

        mean = _split_dot(y, hsum) * (1.0 / hd)
        yc = y - mean
        var = _split_dot(yc * yc, hsum) * (1.0 / hd)
        yn = yc * lax.rsqrt(var + RWKV_GN_EPS) * lng_ref[...] + lnb_ref[...]
        bonus = _split_dot(r * k * rk_ref[...], hsum) * v
        o_ref[rs, :] = (yn + bonus) * g_ref[rs, :]


def _rwkv(r, lw, k, v, kk, a, g, rk, lng, lnb, hsum, bsz, seq, slab_heads=4, n_chunks=2):
    t, mw = r.shape
    c = RWKV_CHUNK * n_chunks
    nt = seq // c
    sw = slab_heads * RWKV_HEAD
    full = lambda x: pl.BlockSpec(x.shape, lambda b, i: (0, 0))
    rows = pl.BlockSpec((c, mw), lambda b, i: (b * nt + i, 0))
    kern = functools.partial(_rwkv_kernel, slab_heads=slab_heads, n_chunks=n_chunks)
    return pl.pallas_call(
        kern,
        grid=(bsz, nt),
        in_specs=[rows] * 7 + [full(rk), full(lng), full(lnb), full(hsum)],
        out_specs=rows,
        out_shape=jax.ShapeDtypeStruct((t, mw), F32),
        scratch_shapes=[pltpu.VMEM((mw // sw, sw, sw), F32)],
        compiler_params=_cparams("parallel", "arbitrary"),
        name="rwkv",
    )(r, lw, k, v, kk, a, g, rk, lng, lnb, hsum)


def _lru_kernel(a_ref, b_ref, gate_ref, o_ref, carry_ref):
    tm = a_ref.shape[0]
    sub = V7X_SUBLANES

    @pl.when(pl.program_id(1) == 0)
    def _():
        carry_ref[...] = jnp.zeros_like(carry_ref)

    rowi = lax.broadcasted_iota(jnp.int32, (sub, a_ref.shape[1]), 0)

    def body(j, h):
        r0 = pl.multiple_of(j * sub, sub)
        a = a_ref[pl.ds(r0, sub), :]
        b = b_ref[pl.ds(r0, sub), :]
        sh = 1
        while sh < sub:
            a_sh = jnp.where(rowi >= sh, pltpu.roll(a, sh, axis=0), 1.0)
            b_sh = jnp.where(rowi >= sh, pltpu.roll(b, sh, axis=0), 0.0)
            b = b + a * b_sh
            a = a * a_sh
            sh *= 2
        hb = b + a * h
        o_ref[pl.ds(r0, sub), :] = hb * _gelu_tanh(gate_ref[pl.ds(r0, sub), :])
        return hb[sub - 1:sub, :]

    carry_ref[0:1, :] = lax.fori_loop(0, tm // sub, body, carry_ref[0:1, :])


def _lru(a, b, gate, bsz, seq, tm):
    t, mw = a.shape
    nt = seq // tm
    rows = pl.BlockSpec((tm, mw), lambda bi, i: (bi * nt + i, 0))
    return pl.pallas_call(
        _lru_kernel,
        grid=(bsz, nt),
        in_specs=[rows, rows, rows],
        out_specs=rows,
        out_shape=jax.ShapeDtypeStruct((t, mw), F32),
        scratch_shapes=[pltpu.VMEM((V7X_SUBLANES, mw), F32)],
        compiler_params=_cparams("parallel", "arbitrary"),
        name="lru",
    )(a, b, gate)


def _cd_out_kernel(h_ref, oc_ref, od_ref, wo_ref, g_ref, out_ref):
    mw = oc_ref.shape[1]
    mix = _bdot(oc_ref[...], wo_ref[0:mw, :]) + _bdot(od_ref[...], wo_ref[mw:2 * mw, :])
    out_ref[...] = h_ref[...] + _rms(mix, g_ref[...])


def _cd_out(h, o_c, o_d, w_out, gain, tm):
    t, d = h.shape
    mw = o_c.shape[1]
    full = lambda a: pl.BlockSpec(a.shape, lambda i: (0, 0))
    rows = lambda w: pl.BlockSpec((tm, w), lambda i: (i, 0))
    return pl.pallas_call(
        _cd_out_kernel,
        grid=(t // tm,),
        in_specs=[rows(d), rows(mw), rows(mw), full(w_out), full(gain)],
        out_specs=rows(d),
        out_shape=jax.ShapeDtypeStruct((t, d), F32),
        compiler_params=_cparams("parallel"),
        name="cd_out",
    )(h, o_c, o_d, w_out, gain)


def _row(vec):
    return vec.astype(F32).reshape(1, -1)


def _block_diag(blocks):
    nb, bi, bo = blocks.shape
    eye = jnp.eye(nb, dtype=blocks.dtype)
    return (eye[:, None, :, None] * blocks[:, :, None, :]).reshape(nb * bi, nb * bo)


def _head_sum_matrix(width, head):
    idx = jnp.arange(width) // head
    return (idx[:, None] == idx[None, :]).astype(BF16)


def _layer_ab(h, gains, p, bsz, seq, tm):
    d = h.shape[1]
    mw = d // 2
    dk = p['gla_b_decay'].shape[0] // GLA_HEADS
    nq = GLA_HEADS * dk
    w_in = p['ab_w_in']
    rank = p['gla_w_decay2'].shape[0]
    o_dlr = 2 * nq + 2 * mw
    w_main = jnp.concatenate([w_in[:, :o_dlr], w_in[:, o_dlr + rank:]], axis=1).astype(BF16)
    w_dlr = jnp.pad(w_in[:, o_dlr:o_dlr + rank], ((0, 0), (0, 128 - rank))).astype(BF16)
    w_decay2 = jnp.pad(p['gla_w_decay2'].astype(F32), ((0, 128 - rank), (0, 0)))
    qkvgu, loga = _ab_in(h, _row(gains[0]), w_main, w_dlr, w_decay2, _row(p['gla_b_decay']), tm)
    o_a = _gla(qkvgu, loga, _row(p['gla_norm_gain']), bsz, seq, tb=256)

    groups = mw // S5_GROUP
    tc = S5_CHUNK
    u = qkvgu[:, 2 * nq + 2 * mw:]
    u_g = jnp.transpose(u.reshape(bsz, seq // tc, tc, groups, S5_GROUP), (3, 1, 0, 2, 4))
    u_g = u_g.reshape(groups, (seq // tc) * bsz, tc * S5_GROUP)
    tz, ws, wc, a_pow = _s5_weights(p['s5_lambda_re'], p['s5_lambda_im'], p['s5_log_step'], p['s5_b_re'],
                                    p['s5_b_im'], p['s5_c_re'], p['s5_c_im'])
    y_g = _s5(u_g, tz, ws, wc, a_pow, bsz)
    y_ssm = jnp.transpose(y_g.reshape(groups, seq // tc, bsz, tc, S5_GROUP), (2, 1, 3, 0, 4))
    y_ssm = y_ssm.reshape(bsz * seq, mw)
    return _ab_out(h, o_a, y_ssm, qkvgu, _row(p['s5_d']), p['s5_w_glu'].astype(BF16), _row(p['s5_b_glu']),
                   p['ab_w_out'].astype(BF16), _row(gains[1]), tm)


def _layer_cd(h, gains, p, bsz, seq, tm):
    d = h.shape[1]
    mw = d // 2
    w_in = p['cd_w_in']
    dr = p['rwkv_w2'].shape[0]
    ar = p['rwkv_a2'].shape[0]
    gr = p['rwkv_g2'].shape[0]
    assert dr + ar == 128 and gr == 128
    o = [0, mw, mw + dr, 2 * mw + dr, 3 * mw + dr, 3 * mw + dr + ar, 3 * mw + dr + ar + gr]
    col = lambda i, width: w_in[:, o[i]:o[i] + width]
    wbig = jnp.concatenate([col(0, mw), col(2, mw), col(3, mw), w_in[:, o[6]:]], axis=1).astype(BF16)
    wsm = jnp.concatenate([col(1, dr), col(4, ar), col(5, gr)], axis=1).astype(BF16)
    mu = p['rwkv_mu'].astype(F32)
    mseg = lambda i, width: mu[o[i]:o[i] + width]
    mub = jnp.concatenate([mseg(0, mw), mseg(2, mw), mseg(3, mw)]).reshape(1, -1)
    mus = jnp.concatenate([mseg(1, dr), mseg(4, ar), mseg(5, gr)]).reshape(1, -1)
    w2p = jnp.pad(p['rwkv_w2'], ((0, ar), (0, 0))).astype(BF16)
    a2p = jnp.pad(p['rwkv_a2'], ((dr, 0), (0, 0))).astype(BF16)
    hsum = _head_sum_matrix(mw, RWKV_HEAD)
    vecs = (mub, mus, _row(p['rwkv_w0']), _row(p['rwkv_a0']), _row(p['rwkv_k_k']), _row(p['rwkv_k_a']),
            p['lru_conv_w'].astype(F32), _row(p['lru_conv_b']), _row(p['lru_b_a']), _row(p['lru_b_x']),
            _row(p['lru_lambda']))
    mats = (w2p, a2p, p['rwkv_g2'].astype(BF16), hsum, _block_diag(p['lru_w_a']).astype(BF16),
            _block_diag(p['lru_w_x']).astype(BF16))
    r, lw, k, v, kk, a, g, la, lb, gate = _cd_in(h, _row(gains[0]), wbig, wsm, vecs, mats, bsz, seq, tm, mw)
    o_c = _rwkv(r, lw, k, v, kk, a, g, _row(p['rwkv_r_k']), _row(p['rwkv_ln_gain']), _row(p['rwkv_ln_bias']),
                hsum, bsz, seq)
    o_d = _lru(la, lb, gate, bsz, seq, tm)
    return _cd_out(h, o_c, o_d, p['cd_w_out'].astype(BF16), _row(gains[1]), tm)


def kernel(x, mem, norm_gain, xa_wq, xa_wk, xa_wv, xa_wo, mlp_w1, mlp_w2, ab_w_in, gla_w_decay2, gla_b_decay, gla_norm_gain, s5_lambda_re, s5_lambda_im, s5_log_step, s5_b_re, s5_b_im, s5_c_re, s5_c_im, s5_d, s5_w_glu, s5_b_glu, ab_w_out, cd_w_in, rwkv_mu, rwkv_w0, rwkv_w2, rwkv_a0, rwkv_a2, rwkv_g2, rwkv_k_k, rwkv_k_a, rwkv_r_k, rwkv_ln_gain, rwkv_ln_bias, lru_conv_w, lru_conv_b, lru_w_a, lru_b_a, lru_w_x, lru_b_x, lru_lambda, cd_w_out):
    bsz, seq, d = x.shape
    mem_len = mem.shape[1]
    depth = norm_gain.shape[0]
    tm = min(512, seq)
    ab = dict(ab_w_in=ab_w_in, gla_w_decay2=gla_w_decay2, gla_b_decay=gla_b_decay, gla_norm_gain=gla_norm_gain,
              s5_lambda_re=s5_lambda_re, s5_lambda_im=s5_lambda_im, s5_log_step=s5_log_step, s5_b_re=s5_b_re,
              s5_b_im=s5_b_im, s5_c_re=s5_c_re, s5_c_im=s5_c_im, s5_d=s5_d, s5_w_glu=s5_w_glu, s5_b_glu=s5_b_glu,
              ab_w_out=ab_w_out)
    cd = dict(cd_w_in=cd_w_in, rwkv_mu=rwkv_mu, rwkv_w0=rwkv_w0, rwkv_w2=rwkv_w2, rwkv_a0=rwkv_a0,
              rwkv_a2=rwkv_a2, rwkv_g2=rwkv_g2, rwkv_k_k=rwkv_k_k, rwkv_k_a=rwkv_k_a, rwkv_r_k=rwkv_r_k,
              rwkv_ln_gain=rwkv_ln_gain, rwkv_ln_bias=rwkv_ln_bias, lru_conv_w=lru_conv_w, lru_conv_b=lru_conv_b,
              lru_w_a=lru_w_a, lru_b_a=lru_b_a, lru_w_x=lru_w_x, lru_b_x=lru_b_x, lru_lambda=lru_lambda,
              cd_w_out=cd_w_out)
    h = x.astype(F32).reshape(bsz * seq, d)
    mem2 = mem.astype(F32).reshape(bsz * mem_len, d)
    for layer in range(depth):
        g = norm_gain[layer]
        i = layer // 2
        if layer % 2 == 0:
            h = _layer_ab(h, g, {n: w[i] for n, w in ab.items()}, bsz, seq, tm)
        else:
            h = _layer_cd(h, g, {n: w[i] for n, w in cd.items()}, bsz, seq, tm)
        km, vm = _mem_kv(mem2, _row(g[6]), xa_wk[layer].astype(BF16), xa_wv[layer].astype(BF16), mem_len)
        h = _xattn(h, km, vm, xa_wq[layer].astype(BF16), xa_wo[layer].astype(BF16), _row(g[2]), _row(g[3]),
                   bsz, seq, mem_len, tm)
        h = _mlp(h, _row(g[4]), _row(g[5]), mlp_w1[layer].astype(BF16), mlp_w2[layer].astype(BF16),
                 tm=min(1024, bsz * seq), tf=1024)
    return h.reshape(bsz, seq, d).astype(x.dtype)
```

```python
import functools
import math

import jax
import jax.numpy as jnp
from jax import lax
from jax.experimental import pallas as pl
from jax.experimental.pallas import tpu as pltpu

F32 = jnp.float32
BF16 = jnp.bfloat16
HIGHEST = lax.Precision.HIGHEST

NORM_EPS = 1e-6
GLA_HEADS = 4
GLA_TAU = 16.0
GLA_CHUNK = 64
S5_GROUP = 16
S5_STATE = 64
S5_CHUNK = 16
RWKV_HEAD = 64
RWKV_CHUNK = 64
RWKV_GN_EPS = 64e-5
LRU_CONV = 4
LRU_C = 8.0
XA_HEADS = 4

V7X_SUBLANES = 8
VMEM_LIMIT_BYTES = 48 * 1024 * 1024


def _cparams(*semantics):
    return pltpu.CompilerParams(dimension_semantics=semantics, vmem_limit_bytes=VMEM_LIMIT_BYTES)


def _rms(x, gain):
    return x * lax.rsqrt(jnp.mean(x * x, axis=-1, keepdims=True) + NORM_EPS) * gain


def _bdot(a, b):
    return jnp.dot(a.astype(BF16), b.astype(BF16), preferred_element_type=F32)


def _bdot_nt(a, b):
    return lax.dot_general(a.astype(BF16), b.astype(BF16), (((1,), (1,)), ((), ())),
                           preferred_element_type=F32)


def _bdot_tn(a, b):
    return lax.dot_general(a.astype(BF16), b.astype(BF16), (((0,), (0,)), ((), ())),
                           preferred_element_type=F32)


def _hdot(a, b):
    return jnp.dot(a, b, precision=HIGHEST, preferred_element_type=F32)


def _hdot_nt(a, b):
    return lax.dot_general(a, b, (((1,), (1,)), ((), ())), precision=HIGHEST,
                           preferred_element_type=F32)


def _hdot_tn(a, b):
    return lax.dot_general(a, b, (((0,), (0,)), ((), ())), precision=HIGHEST,
                           preferred_element_type=F32)


def _split_dot(x, w_bf16):
    hi = x.astype(BF16)
    lo = (x - hi.astype(F32)).astype(BF16)
    return (jnp.dot(hi, w_bf16, preferred_element_type=F32)
            + jnp.dot(lo, w_bf16, preferred_element_type=F32))


def _sigmoid(x):
    return 1.0 / (1.0 + jnp.exp(-x))


def _softplus(x):
    return jnp.maximum(x, 0.0) + jnp.log1p(jnp.exp(-jnp.abs(x)))


def _gelu_tanh(x):
    c = math.sqrt(2.0 / math.pi)
    return x * (0.5 * (1.0 + jnp.tanh(c * (x + 0.044715 * (x * x * x)))))


def _tril_mask(n, strict=False):
    row = lax.broadcasted_iota(jnp.int32, (n, n), 0)
    col = lax.broadcasted_iota(jnp.int32, (n, n), 1)
    return (col < row) if strict else (col <= row)


def _shift_rows(x, shift, carry):
    rolled = pltpu.roll(x, shift, axis=0)
    crolled = pltpu.roll(carry, shift, axis=0)
    rowi = lax.broadcasted_iota(jnp.int32, carry.shape, 0)
    first = jnp.where(rowi < shift, crolled, rolled[0:V7X_SUBLANES])
    return jnp.concatenate([first, rolled[V7X_SUBLANES:]], axis=0)


def _ab_in_kernel(h_ref, g_ref, w_ref, wd_ref, wd2_ref, bd_ref, out_ref, loga_ref):
    hn = _rms(h_ref[...], g_ref[...]).astype(BF16)
    out_ref[...] = jnp.dot(hn, w_ref[...], preferred_element_type=F32)
    dlr = jnp.dot(hn, wd_ref[...], preferred_element_type=F32)
    z = _hdot(dlr, wd2_ref[...]) + bd_ref[...]
    loga_ref[...] = (jnp.minimum(z, 0.0) - jnp.log1p(jnp.exp(-jnp.abs(z)))) * (1.0 / GLA_TAU)


def _ab_in(h, gain, w_main, w_dlr, w_decay2, b_decay, tm):
    t, d = h.shape
    n_main = w_main.shape[1]
    n_dk = w_decay2.shape[1]
    full = lambda a: pl.BlockSpec(a.shape, lambda i: (0, 0))
    return pl.pallas_call(
        _ab_in_kernel,
        grid=(t // tm,),
        in_specs=[pl.BlockSpec((tm, d), lambda i: (i, 0)), full(gain), full(w_main), full(w_dlr),
                  full(w_decay2), full(b_decay)],
        out_specs=[pl.BlockSpec((tm, n_main), lambda i: (i, 0)), pl.BlockSpec((tm, n_dk), lambda i: (i, 0))],
        out_shape=[jax.ShapeDtypeStruct((t, n_main), F32), jax.ShapeDtypeStruct((t, n_dk), F32)],
        compiler_params=_cparams("parallel"),
        name="ab_in",
    )(h, gain, w_main, w_dlr, w_decay2, b_decay)


def _gla_kernel(q_ref, k_ref, v_ref, gate_ref, la_ref, gain_ref, o_ref, state_ref, *, n_chunks, dk, dv):
    c = GLA_CHUNK

    @pl.when(pl.program_id(1) == 0)
    def _():
        state_ref[...] = jnp.zeros_like(state_ref)

    tril = _tril_mask(c)
    tril_f = tril.astype(F32)
    scale = dk ** -0.5
    for ci in range(n_chunks):
        rs = slice(ci * c, (ci + 1) * c)
        b = _hdot(tril_f, la_ref[rs, :])
        b_last = b[c - 1:c, :]
        q_in = q_ref[rs, :] * jnp.exp(b) * scale
        k = k_ref[rs, :]
        k_in = k * jnp.exp(-b)
        k_st = k * jnp.exp(b_last - b)
        dec = jnp.exp(b_last)
        for h in range(GLA_HEADS):
            ks = slice(h * dk, (h + 1) * dk)
            vs = slice(h * dv, (h + 1) * dv)
            qh = q_in[:, ks]
            vh = v_ref[rs, vs]
            scores = jnp.where(tril, _bdot_nt(qh, k_in[:, ks]), 0.0)
            st = state_ref[h]
            o = _bdot(scores, vh) + _bdot_nt(qh, st)
            state_ref[h] = st * dec[:, ks] + _bdot_tn(vh, k_st[:, ks])
            on = o * lax.rsqrt(jnp.mean(o * o, axis=-1, keepdims=True) + NORM_EPS) * gain_ref[:, vs]
            g = gate_ref[rs, vs]
            o_ref[rs, vs] = on * (g * _sigmoid(g))


def _gla(qkvgu, loga, gain, bsz, seq, tb):
    t = bsz * seq
    dk = loga.shape[1] // GLA_HEADS
    dv = gain.shape[1] // GLA_HEADS
    nq, nv = GLA_HEADS * dk, GLA_HEADS * dv
    nt = seq // tb
    row = lambda b, i: b * nt + i
    kern = functools.partial(_gla_kernel, n_chunks=tb // GLA_CHUNK, dk=dk, dv=dv)
    return pl.pallas_call(
        kern,
        grid=(bsz, nt),
        in_specs=[pl.BlockSpec((tb, nq), lambda b, i: (row(b, i), 0)),
                  pl.BlockSpec((tb, nq), lambda b, i: (row(b, i), 1)),
                  pl.BlockSpec((tb, nv), lambda b, i: (row(b, i), 1)),
                  pl.BlockSpec((tb, nv), lambda b, i: (row(b, i), 2)),
                  pl.BlockSpec((tb, nq), lambda b, i: (row(b, i), 0)),
                  pl.BlockSpec(gain.shape, lambda b, i: (0, 0))],
        out_specs=pl.BlockSpec((tb, nv), lambda b, i: (row(b, i), 0)),
        out_shape=jax.ShapeDtypeStruct((t, nv), F32),
        scratch_shapes=[pltpu.VMEM((GLA_HEADS, dv, dk), F32)],
        compiler_params=_cparams("parallel", "arbitrary"),
        name="gla",
    )(qkvgu, qkvgu, qkvgu, qkvgu, loga, gain)


def _s5_kernel(u_ref, tz_ref, ws_ref, wc_ref, a_ref, y_ref, s_scr, h_scr, *, n_pairs, bsz):
    n2 = 2 * S5_STATE
    u = u_ref[0]
    s_scr[...] = _hdot(u, ws_ref[0])
    a1 = a_ref[0, 0:1, :]
    a2 = a_ref[0, 1:2, :]

    def body(j, carry):
        h, hs = carry
        r0 = pl.multiple_of(j * (2 * bsz), 2 * bsz)
        blk = s_scr[pl.ds(r0, 2 * bsz), :]
        h1 = a1 * h + a2 * hs + blk[0:bsz, 0:n2]
        hs1 = a1 * hs - a2 * h + blk[0:bsz, n2:2 * n2]
        h2 = a1 * h1 + a2 * hs1 + blk[bsz:2 * bsz, 0:n2]
        hs2 = a1 * hs1 - a2 * h1 + blk[bsz:2 * bsz, n2:2 * n2]
        h_scr[pl.ds(r0, 2 * bsz), :] = jnp.concatenate([h, h1], axis=0)
        return h2, hs2

    zero = jnp.zeros((bsz, n2), F32)
    lax.fori_loop(0, n_pairs, body, (zero, zero))
    y_ref[0] = _hdot(u, tz_ref[0]) + _hdot(h_scr[...], wc_ref[0])


def _s5(u_g, tz, ws, wc, a_pow, bsz):
    groups, rows, width = u_g.shape
    assert 2 * bsz == V7X_SUBLANES, "the chunk scan walks two chunks per 8-row tile"
    n_pairs = rows // (2 * bsz)
    per_g = lambda a: pl.BlockSpec((1,) + a.shape[1:], lambda g: (g, 0, 0))
    kern = functools.partial(_s5_kernel, n_pairs=n_pairs, bsz=bsz)
    return pl.pallas_call(
        kern,
        grid=(groups,),
        in_specs=[per_g(u_g), per_g(tz), per_g(ws), per_g(wc), per_g(a_pow)],
        out_specs=per_g(u_g),
        out_shape=jax.ShapeDtypeStruct(u_g.shape, F32),
        scratch_shapes=[pltpu.VMEM((rows, ws.shape[2]), F32), pltpu.VMEM((rows, wc.shape[1]), F32)],
        compiler_params=_cparams("parallel"),
        name="s5",
    )(u_g, tz, ws, wc, a_pow)


def _s5_weights(lam_re, lam_im, log_step, b_re, b_im, c_re, c_im):
    tc = S5_CHUNK
    groups, n = lam_re.shape
    lr = jnp.minimum(lam_re.astype(F32), -1e-4)
    li = lam_im.astype(F32)
    delta = jnp.exp(log_step.astype(F32))[:, None]
    tau = jnp.arange(tc + 1, dtype=F32)[:, None, None]
    mag = jnp.exp(tau * (lr * delta))
    ang = tau * (li * delta)
    pw_re, pw_im = mag * jnp.cos(ang), mag * jnp.sin(ang)
    num_re, num_im = pw_re[1] - 1.0, pw_im[1]
    den = lr * lr + li * li
    f_re = (num_re * lr + num_im * li) / den
    f_im = (num_im * lr - num_re * li) / den
    b_re, b_im = b_re.astype(F32), b_im.astype(F32)
    bb_re = f_re[..., None] * b_re - f_im[..., None] * b_im
    bb_im = f_re[..., None] * b_im + f_im[..., None] * b_re
    c_re, c_im = c_re.astype(F32), c_im.astype(F32)
    cp_re = c_re[None] * pw_re[:, :, None, :] - c_im[None] * pw_im[:, :, None, :]
    cp_im = c_re[None] * pw_im[:, :, None, :] + c_im[None] * pw_re[:, :, None, :]
    ein = functools.partial(jnp.einsum, precision=HIGHEST)
    kern = ein('tgcn,gnd->tgcd', cp_re[:tc], bb_re) - ein('tgcn,gnd->tgcd', cp_im[:tc], bb_im)
    j = jnp.arange(tc)[:, None]
    i = jnp.arange(tc)[None, :]
    diff = i - j
    ksel = jnp.where((diff >= 0)[:, :, None, None, None], kern[jnp.clip(diff, 0, tc - 1)], 0.0)
    tz = jnp.transpose(ksel, (2, 0, 4, 1, 3)).reshape(groups, tc * S5_GROUP, tc * S5_GROUP)
    rev_re, rev_im = pw_re[tc - 1::-1][:tc], pw_im[tc - 1::-1][:tc]
    s_re = rev_re[:, :, :, None] * bb_re[None] - rev_im[:, :, :, None] * bb_im[None]
    s_im = rev_re[:, :, :, None] * bb_im[None] + rev_im[:, :, :, None] * bb_re[None]
    s_re = jnp.transpose(s_re, (1, 0, 3, 2)).reshape(groups, tc * S5_GROUP, n)
    s_im = jnp.transpose(s_im, (1, 0, 3, 2)).reshape(groups, tc * S5_GROUP, n)
    ws = jnp.concatenate([s_re, s_im, s_im, s_re], axis=-1)
    o_re = jnp.transpose(cp_re[1:], (1, 3, 0, 2)).reshape(groups, n, tc * S5_GROUP)
    o_im = jnp.transpose(cp_im[1:], (1, 3, 0, 2)).reshape(groups, n, tc * S5_GROUP)
    wc = jnp.concatenate([o_re, -o_im], axis=1)
    a_pow = jnp.stack([jnp.concatenate([pw_re[tc], pw_re[tc]], axis=-1),
                       jnp.concatenate([-pw_im[tc], pw_im[tc]], axis=-1)], axis=1)
    return tz, ws, wc, a_pow


def _ab_out_kernel(h_ref, oa_ref, ys_ref, u_ref, d_ref, wglu_ref, bglu_ref, wo_ref, g_ref, out_ref):
    mw = oa_ref.shape[1]
    y = ys_ref[...] + d_ref[...] * u_ref[...]
    ob = _gelu_tanh(y) * _sigmoid(_bdot(y, wglu_ref[...]) + bglu_ref[...])
    mix = _bdot(oa_ref[...], wo_ref[0:mw, :]) + _bdot(ob, wo_ref[mw:2 * mw, :])
    out_ref[...] = h_ref[...] + _rms(mix, g_ref[...])


def _ab_out(h, o_a, y_ssm, qkvgu, d_skip, w_glu, b_glu, w_out, gain, tm):
    t, d = h.shape
    mw = o_a.shape[1]
    full = lambda a: pl.BlockSpec(a.shape, lambda i: (0, 0))
    rows = lambda w, cb=0: pl.BlockSpec((tm, w), lambda i: (i, cb))
    return pl.pallas_call(
        _ab_out_kernel,
        grid=(t // tm,),
        in_specs=[rows(d), rows(mw), rows(mw), rows(mw, qkvgu.shape[1] // mw - 1), full(d_skip), full(w_glu),
                  full(b_glu), full(w_out), full(gain)],
        out_specs=rows(d),
        out_shape=jax.ShapeDtypeStruct((t, d), F32),
        compiler_params=_cparams("parallel"),
        name="ab_out",
    )(h, o_a, y_ssm, qkvgu, d_skip, w_glu, b_glu, w_out, gain)


def _mem_kv_kernel(mem_ref, g_ref, wk_ref, wv_ref, k_ref, v_ref):
    mn = _rms(mem_ref[...], g_ref[...]).astype(BF16)
    k_ref[...] = jnp.dot(mn, wk_ref[...], preferred_element_type=F32).astype(BF16)
    v_ref[...] = jnp.dot(mn, wv_ref[...], preferred_element_type=F32).astype(BF16)


def _mem_kv(mem, gain, wk, wv, tm):
    t, d = mem.shape
    full = lambda a: pl.BlockSpec(a.shape, lambda i: (0, 0))
    rows = pl.BlockSpec((tm, d), lambda i: (i, 0))
    return pl.pallas_call(
        _mem_kv_kernel,
        grid=(t // tm,),
        in_specs=[rows, full(gain), full(wk), full(wv)],
        out_specs=[rows, rows],
        out_shape=[jax.ShapeDtypeStruct((t, d), BF16), jax.ShapeDtypeStruct((t, d), BF16)],
        compiler_params=_cparams("parallel"),
        name="mem_kv",
    )(mem, gain, wk, wv)


def _xattn_kernel(h_ref, k_ref, v_ref, wq_ref, wo_ref, gq_ref, go_ref, out_ref, o_scr):
    h = h_ref[...]
    d = h.shape[1]
    hd = d // XA_HEADS
    q = jnp.dot(_rms(h, gq_ref[...]).astype(BF16), wq_ref[...], preferred_element_type=F32)
    for hh in range(XA_HEADS):
        cs = slice(hh * hd, (hh + 1) * hd)
        s = _bdot_nt(q[:, cs], k_ref[:, cs]) * (hd ** -0.5)
        e = jnp.exp(s - jnp.max(s, axis=-1, keepdims=True))
        p = e / jnp.sum(e, axis=-1, keepdims=True)
        o_scr[:, cs] = _bdot(p, v_ref[:, cs]).astype(BF16)
    xa = jnp.dot(o_scr[...], wo_ref[...], preferred_element_type=F32)
    out_ref[...] = h + _rms(xa, go_ref[...])


def _xattn(h, k, v, wq, wo, gq, go, bsz, seq, mem_len, tm):
    t, d = h.shape
    nt = seq // tm
    full = lambda a: pl.BlockSpec(a.shape, lambda b, i: (0, 0))
    rows = pl.BlockSpec((tm, d), lambda b, i: (b * nt + i, 0))
    kv = pl.BlockSpec((mem_len, d), lambda b, i: (b, 0))
    return pl.pallas_call(
        _xattn_kernel,
        grid=(bsz, nt),
        in_specs=[rows, kv, kv, full(wq), full(wo), full(gq), full(go)],
        out_specs=rows,
        out_shape=jax.ShapeDtypeStruct((t, d), F32),
        scratch_shapes=[pltpu.VMEM((tm, d), BF16)],
        compiler_params=_cparams("parallel", "parallel"),
        name="xattn",
    )(h, k, v, wq, wo, gq, go)


def _mlp_kernel(h_ref, gi_ref, go_ref, w1_ref, w2_ref, out_ref, xn_scr, acc_scr):
    j = pl.program_id(1)

    @pl.when(j == 0)
    def _():
        xn_scr[...] = _rms(h_ref[...], gi_ref[...]).astype(BF16)
        acc_scr[...] = jnp.zeros_like(acc_scr)

    a = jnp.maximum(jnp.dot(xn_scr[...], w1_ref[...], preferred_element_type=F32), 0.0)
    acc_scr[...] += jnp.dot((a * a).astype(BF16), w2_ref[...], preferred_element_type=F32)

    @pl.when(j == pl.num_programs(1) - 1)
    def _():
        out_ref[...] = h_ref[...] + _rms(acc_scr[...], go_ref[...])


def _mlp(h, gi, go, w1, w2, tm, tf):
    t, d = h.shape
    dff = w1.shape[1]
    full = lambda a: pl.BlockSpec(a.shape, lambda i, j: (0, 0))
    rows = pl.BlockSpec((tm, d), lambda i, j: (i, 0))
    return pl.pallas_call(
        _mlp_kernel,
        grid=(t // tm, dff // tf),
        in_specs=[rows, full(gi), full(go), pl.BlockSpec((d, tf), lambda i, j: (0, j)),
                  pl.BlockSpec((tf, d), lambda i, j: (j, 0))],
        out_specs=rows,
        out_shape=jax.ShapeDtypeStruct((t, d), F32),
        scratch_shapes=[pltpu.VMEM((tm, d), BF16), pltpu.VMEM((tm, d), F32)],
        compiler_params=_cparams("parallel", "arbitrary"),
        name="mlp",
    )(h, gi, go, w1, w2)


def _cd_in_kernel(h_ref, g_ref, wbig_ref, wsm_ref, mub_ref, mus_ref, w0_ref, w2_ref, a0_ref, a2_ref, g2_ref,
                  kk_ref, ka_ref, hsum_ref, cw_ref, cb_ref, wa_ref, ba_ref, wx_ref, bx_ref, lam_ref,
                  r_o, lw_o, k_o, v_o, kk_o, a_o, g_o, la_o, lb_o, gate_o,
                  carry_b, carry_s, carry_x, *, mw):
    tm = h_ref.shape[0]

    @pl.when(pl.program_id(1) == 0)
    def _():
        carry_b[...] = jnp.zeros_like(carry_b)
        carry_s[...] = jnp.zeros_like(carry_s)
        carry_x[...] = jnp.zeros_like(carry_x)

    hn = _rms(h_ref[...], g_ref[...]).astype(BF16)
    pb = jnp.dot(hn, wbig_ref[...], preferred_element_type=F32)
    ps = jnp.dot(hn, wsm_ref[...], preferred_element_type=F32)
    p3 = pb[:, 0:3 * mw]
    xb = pb[:, 3 * mw:4 * mw]
    gate_o[...] = pb[:, 4 * mw:5 * mw]

    prev3 = _shift_rows(p3, 1, carry_b[...])
    prevs = _shift_rows(ps, 1, carry_s[...])
    carry_b[...] = p3[tm - V7X_SUBLANES:tm, :]
    carry_s[...] = ps[tm - V7X_SUBLANES:tm, :]
    p3 = p3 + (prev3 - p3) * mub_ref[...]
    ps = ps + (prevs - ps) * mus_ref[...]
    r = p3[:, 0:mw]
    k = p3[:, mw:2 * mw]
    v = p3[:, 2 * mw:3 * mw]
    lora = ps[:, 0:128]
    wlog = -_softplus(-(w0_ref[...] + _bdot(jnp.tanh(lora), w2_ref[...]))) - 0.5
    a = _sigmoid(a0_ref[...] + _bdot(lora, a2_ref[...]))
    kkr = k * kk_ref[...]
    norm = jnp.sqrt(_split_dot(kkr * kkr, hsum_ref[...]))
    r_o[...] = r
    lw_o[...] = -jnp.exp(wlog)
    k_o[...] = k * (1.0 + (a - 1.0) * ka_ref[...])
    v_o[...] = v
    kk_o[...] = kkr / jnp.maximum(norm, 1e-12)
    a_o[...] = a
    g_o[...] = _bdot(_sigmoid(ps[:, 128:256]), g2_ref[...])

    cx = carry_x[...]
    xc = cb_ref[...] + cw_ref[LRU_CONV - 1:LRU_CONV, :] * xb
    for sh in range(1, LRU_CONV):
        xc = xc + cw_ref[LRU_CONV - 1 - sh:LRU_CONV - sh, :] * _shift_rows(xb, sh, cx)
    carry_x[...] = xb[tm - V7X_SUBLANES:tm, :]
    rg = _sigmoid(_bdot(xc, wa_ref[...]) + ba_ref[...])
    ig = _sigmoid(_bdot(xc, wx_ref[...]) + bx_ref[...])
    log_a = -LRU_C * rg * _softplus(-lam_ref[...])
    la_o[...] = jnp.exp(log_a)
    th = jnp.tanh(log_a)
    lb_o[...] = jnp.sqrt(-2.0 * th / (1.0 - th)) * (ig * xc)


def _cd_in(h, gain, wbig, wsm, vecs, mats, bsz, seq, tm, mw):
    t, d = h.shape
    nt = seq // tm
    full = lambda a: pl.BlockSpec(a.shape, lambda b, i: (0, 0))
    rows = lambda w: pl.BlockSpec((tm, w), lambda b, i: (b * nt + i, 0))
    (mub, mus, w0, a0, kk_w, ka_w, cw, cb, ba, bx, lam) = vecs
    (w2p, a2p, g2, hsum, wa, wx) = mats
    args = (h, gain, wbig, wsm, mub, mus, w0, w2p, a0, a2p, g2, kk_w, ka_w, hsum, cw, cb, wa, ba, wx, bx, lam)
    kern = functools.partial(_cd_in_kernel, mw=mw)
    return pl.pallas_call(
        kern,
        grid=(bsz, nt),
        in_specs=[rows(d)] + [full(a) for a in args[1:]],
        out_specs=[rows(mw)] * 10,
        out_shape=[jax.ShapeDtypeStruct((t, mw), F32)] * 10,
        scratch_shapes=[pltpu.VMEM((V7X_SUBLANES, 3 * mw), F32), pltpu.VMEM((V7X_SUBLANES, wsm.shape[1]), F32),
                        pltpu.VMEM((V7X_SUBLANES, mw), F32)],
        compiler_params=_cparams("parallel", "arbitrary"),
        name="cd_in",
    )(*args)


def _split_bf16(x):
    hi = x.astype(BF16)
    return hi, (x - hi.astype(F32)).astype(BF16)


def _rwkv_kernel(r_ref, lw_ref, k_ref, v_ref, kk_ref, a_ref, g_ref, rk_ref, lng_ref, lnb_ref, hsum_ref,
                 o_ref, state_ref, y_scr, *, slab_heads, n_chunks):
    c = RWKV_CHUNK
    hd = RWKV_HEAD
    assert c == hd, "one block mask serves both the (head, s) and the (head, d) layouts"
    sw = slab_heads * hd
    n_slabs = r_ref.shape[1] // sw

    @pl.when(pl.program_id(1) == 0)
    def _():
        state_ref[...] = jnp.zeros_like(state_ref)

    tril_f = _tril_mask(c).astype(F32)
    blk_m = (lax.broadcasted_iota(jnp.int32, (sw, sw), 0) // c) == (lax.broadcasted_iota(jnp.int32, (sw, sw), 1) // c)
    wide_t = lax.broadcasted_iota(jnp.int32, (c, sw), 0)
    wide_s = lax.broadcasted_iota(jnp.int32, (c, sw), 1) % c
    strict_w = wide_s < wide_t
    incl_w = wide_s <= wide_t
    eye_w = jnp.where(wide_s == wide_t, 1.0, 0.0)

    def bdiag(x):
        x16 = x.astype(BF16)
        return jnp.where(blk_m, jnp.concatenate([x16] * slab_heads, axis=0), jnp.zeros((), BF16))

    def mm(a, w16):
        return jnp.dot(a.astype(BF16), w16, preferred_element_type=F32)

    nt_dims = (((1,), (1,)), ((), ()))
    chains = []
    for chunk in range(n_chunks):
        rs = slice(chunk * c, (chunk + 1) * c)
        lw = lw_ref[rs, :]
        cum = _hdot(tril_f, lw)
        cum_last = cum[c - 1:c, :]
        r = r_ref[rs, :]
        k = k_ref[rs, :]
        v = v_ref[rs, :]
        kk = kk_ref[rs, :]
        bvec = kk * a_ref[rs, :]
        inv_g = jnp.exp(-cum)
        to_end = jnp.exp(cum_last - cum)
        rt = r * jnp.exp(cum)
        kp = kk * jnp.exp(cum - lw)
        be = bvec * inv_g
        kh = k * inv_g
        bb = bvec * to_end
        kb = k * to_end
        g_end = jnp.exp(cum_last)
        for s in range(n_slabs):
            ls = slice(s * sw, (s + 1) * sw)
            chains.append(dict(rs=rs, ls=ls, slab=s, rt=rt[:, ls], kp=kp[:, ls], v=v[:, ls], be=be[:, ls],
                               kh=kh[:, ls], kb=kb[:, ls], bb=bb[:, ls], g_end=g_end[:, ls]))

    for ch in chains:
        lhs2 = jnp.concatenate([ch['kp'], ch['rt']], axis=0).astype(BF16)
        ab = lax.dot_general(lhs2, bdiag(ch['be']), nt_dims, preferred_element_type=F32)
        ak = lax.dot_general(lhs2, bdiag(ch['kh']), nt_dims, preferred_element_type=F32)
        ch['a_kb'] = jnp.where(strict_w, ab[0:c], 0.0)
        ch['a_rb'] = jnp.where(incl_w, ab[c:2 * c], 0.0).astype(BF16)
        ch['a_kr'] = jnp.concatenate([jnp.where(strict_w, ak[0:c], 0.0), jnp.where(incl_w, ak[c:2 * c], 0.0)],
                                     axis=0).astype(BF16)
    for ch in chains:
        x = -ch['a_kb']
        ch['t'] = eye_w + x
        ch['x'] = mm(x, bdiag(x))
    for _ in range(int(math.log2(c)) - 2):
        for ch in chains:
            res = mm(jnp.concatenate([ch['t'], ch['x']], axis=0), bdiag(ch['x']))
            ch['t'] = ch['t'] + res[0:c]
            ch['x'] = res[c:2 * c]
    for ch in chains:
        ch['t'] = ch['t'] + mm(ch['t'], bdiag(ch['x']))
    for ch in chains:
        l_hi, l_lo = _split_bf16(ch['a_kb'])
        t_hi, t_lo = _split_bf16(ch['t'])
        lt = mm(jnp.concatenate([l_hi, l_lo], axis=0), bdiag(t_hi))
        ch['resid'] = eye_w - ch['t'] - (lt[0:c] + lt[c:2 * c] + mm(l_hi, bdiag(t_lo)))
    for ch in chains:
        ch['t16'] = (ch['t'] + mm(ch['t'], bdiag(ch['resid']))).astype(BF16)
    for ch in chains:
        ch['w_tok'] = mm(ch['t16'], bdiag(ch['kp']))
        ch['av'] = mm(ch['a_kr'], bdiag(ch['v']))
    for ch in chains:
        ch['u_tok'] = mm(ch['t16'], bdiag(ch['av'][0:c]))
    for ch in chains:
        q_tok = ch['rt'] - mm(ch['a_rb'], bdiag(ch['w_tok']))
        ch['qw'] = jnp.concatenate([q_tok, ch['w_tok']], axis=0).astype(BF16)
        ch['y0'] = ch['av'][c:2 * c] - mm(ch['a_rb'], bdiag(ch['u_tok']))
        ch['kb2'] = jnp.concatenate([ch['kb'], ch['bb']], axis=0).astype(BF16)

    for ch in chains:
        s = ch['slab']
        p_t = state_ref[s]
        res = lax.dot_general(ch['qw'], p_t.astype(BF16), nt_dims, preferred_element_type=F32)
        y_scr[ch['rs'], ch['ls']] = res[0:c] + ch['y0']
        u_all = -(ch['u_tok'] + res[c:2 * c])
        upd = lax.dot_general(jnp.concatenate([ch['v'], u_all], axis=0).astype(BF16), ch['kb2'],
                              (((0,), (0,)), ((), ())), preferred_element_type=F32)
        state_ref[s] = p_t * ch['g_end'] + jnp.where(blk_m, upd, 0.0)

    hsum = hsum_ref[...]
    y = y_scr[...]
    r = r_ref[...]
    v = v_ref[...]
    mean = _split_dot(y, hsum) * (1.0 / hd)
    yc = y - mean
    var = _split_dot(yc * yc, hsum) * (1.0 / hd)
    yn = yc * lax.rsqrt(var + RWKV_GN_EPS) * lng_ref[...] + lnb_ref[...]
    bonus = _split_dot(r * k_ref[...] * rk_ref[...], hsum) * v
    o_ref[...] = (yn + bonus) * g_ref[...]


def _rwkv(r, lw, k, v, kk, a, g, rk, lng, lnb, hsum, bsz, seq, slab_heads=4, n_chunks=4):
    t, mw = r.shape
    c = RWKV_CHUNK * n_chunks
    nt = seq // c
    sw = slab_heads * RWKV_HEAD
    full = lambda x: pl.BlockSpec(x.shape, lambda b, i: (0, 0))
    rows = pl.BlockSpec((c, mw), lambda b, i: (b * nt + i, 0))
    kern = functools.partial(_rwkv_kernel, slab_heads=slab_heads, n_chunks=n_chunks)
    return pl.pallas_call(
        kern,
        grid=(bsz, nt),
        in_specs=[rows] * 7 + [full(rk), full(lng), full(lnb), full(hsum)],
        out_specs=rows,
        out_shape=jax.ShapeDtypeStruct((t, mw), F32),
        scratch_shapes=[pltpu.VMEM((mw // sw, sw, sw), F32), pltpu.VMEM((c, mw), F32)],
        compiler_params=_cparams("parallel", "arbitrary"),
        name="rwkv",
    )(r, lw, k, v, kk, a, g, rk, lng, lnb, hsum)


def _lru_kernel(a_ref, b_ref, gate_ref, o_ref, carry_ref):
    tm = a_ref.shape[0]
    sub = V7X_SUBLANES

    @pl.when(pl.program_id(1) == 0)
    def _():
        carry_ref[...] = jnp.zeros_like(carry_ref)

    rowi = lax.broadcasted_iota(jnp.int32, (sub, a_ref.shape[1]), 0)

    def body(j, h):
        r0 = pl.multiple_of(j * sub, sub)
        a = a_ref[pl.ds(r0, sub), :]
        b = b_ref[pl.ds(r0, sub), :]
        sh = 1
        while sh < sub:
            a_sh = jnp.where(rowi >= sh, pltpu.roll(a, sh, axis=0), 1.0)
            b_sh = jnp.where(rowi >= sh, pltpu.roll(b, sh, axis=0), 0.0)
            b = b + a * b_sh
            a = a * a_sh
            sh *= 2
        hb = b + a * h
        o_ref[pl.ds(r0, sub), :] = hb * _gelu_tanh(gate_ref[pl.ds(r0, sub), :])
        return hb[sub - 1:sub, :]

    carry_ref[0:1, :] = lax.fori_loop(0, tm // sub, body, carry_ref[0:1, :])


def _lru(a, b, gate, bsz, seq, tm):
    t, mw = a.shape
    nt = seq // tm
    rows = pl.BlockSpec((tm, mw), lambda bi, i: (bi * nt + i, 0))
    return pl.pallas_call(
        _lru_kernel,
        grid=(bsz, nt),
        in_specs=[rows, rows, rows],
        out_specs=rows,
        out_shape=jax.ShapeDtypeStruct((t, mw), F32),
        scratch_shapes=[pltpu.VMEM((V7X_SUBLANES, mw), F32)],
        compiler_params=_cparams("parallel", "arbitrary"),
        name="lru",
    )(a, b, gate)


def _cd_out_kernel(h_ref, oc_ref, od_ref, wo_ref, g_ref, out_ref):
    mw = oc_ref.shape[1]
    mix = _bdot(oc_ref[...], wo_ref[0:mw, :]) + _bdot(od_ref[...], wo_ref[mw:2 * mw, :])
    out_ref[...] = h_ref[...] + _rms(mix, g_ref[...])


def _cd_out(h, o_c, o_d, w_out, gain, tm):
    t, d = h.shape
    mw = o_c.shape[1]
    full = lambda a: pl.BlockSpec(a.shape, lambda i: (0, 0))
    rows = lambda w: pl.BlockSpec((tm, w), lambda i: (i, 0))
    return pl.pallas_call(
        _cd_out_kernel,
        grid=(t // tm,),
        in_specs=[rows(d), rows(mw), rows(mw), full(w_out), full(gain)],
        out_specs=rows(d),
        out_shape=jax.ShapeDtypeStruct((t, d), F32),
        compiler_params=_cparams("parallel"),
        name="cd_out",
    )(h, o_c, o_d, w_out, gain)


def _row(vec):
    return vec.astype(F32).reshape(1, -1)


def _block_diag(blocks):
    nb, bi, bo = blocks.shape
    eye = jnp.eye(nb, dtype=blocks.dtype)
    return (eye[:, None, :, None] * blocks[:, :, None, :]).reshape(nb * bi, nb * bo)


def _head_sum_matrix(width, head):
    idx = jnp.arange(width) // head
    return (idx[:, None] == idx[None, :]).astype(BF16)


def _layer_ab(h, gains, p, bsz, seq, tm):
    d = h.shape[1]
    mw = d // 2
    dk = p['gla_b_decay'].shape[0] // GLA_HEADS
    nq = GLA_HEADS * dk
    w_in = p['ab_w_in']
    rank = p['gla_w_decay2'].shape[0]
    o_dlr = 2 * nq + 2 * mw
    w_main = jnp.concatenate([w_in[:, :o_dlr], w_in[:, o_dlr + rank:]], axis=1).astype(BF16)
    w_dlr = jnp.pad(w_in[:, o_dlr:o_dlr + rank], ((0, 0), (0, 128 - rank))).astype(BF16)
    w_decay2 = jnp.pad(p['gla_w_decay2'].astype(F32), ((0, 128 - rank), (0, 0)))
    qkvgu, loga = _ab_in(h, _row(gains[0]), w_main, w_dlr, w_decay2, _row(p['gla_b_decay']), tm)
    o_a = _gla(qkvgu, loga, _row(p['gla_norm_gain']), bsz, seq, tb=256)

    groups = mw // S5_GROUP
    tc = S5_CHUNK
    u = qkvgu[:, 2 * nq + 2 * mw:]
    u_g = jnp.transpose(u.reshape(bsz, seq // tc, tc, groups, S5_GROUP), (3, 1, 0, 2, 4))
    u_g = u_g.reshape(groups, (seq // tc) * bsz, tc * S5_GROUP)
    tz, ws, wc, a_pow = _s5_weights(p['s5_lambda_re'], p['s5_lambda_im'], p['s5_log_step'], p['s5_b_re'],
                                    p['s5_b_im'], p['s5_c_re'], p['s5_c_im'])
    y_g = _s5(u_g, tz, ws, wc, a_pow, bsz)
    y_ssm = jnp.transpose(y_g.reshape(groups, seq // tc, bsz, tc, S5_GROUP), (2, 1, 3, 0, 4))
    y_ssm = y_ssm.reshape(bsz * seq, mw)
    return _ab_out(h, o_a, y_ssm, qkvgu, _row(p['s5_d']), p['s5_w_glu'].astype(BF16), _row(p['s5_b_glu']),
                   p['ab_w_out'].astype(BF16), _row(gains[1]), tm)


def _layer_cd(h, gains, p, bsz, seq, tm):
    d = h.shape[1]
    mw = d // 2
    w_in = p['cd_w_in']
    dr = p['rwkv_w2'].shape[0]
    ar = p['rwkv_a2'].shape[0]
    gr = p['rwkv_g2'].shape[0]
    assert dr + ar == 128 and gr == 128
    o = [0, mw, mw + dr, 2 * mw + dr, 3 * mw + dr, 3 * mw + dr + ar, 3 * mw + dr + ar + gr]
    col = lambda i, width: w_in[:, o[i]:o[i] + width]
    wbig = jnp.concatenate([col(0, mw), col(2, mw), col(3, mw), w_in[:, o[6]:]], axis=1).astype(BF16)
    wsm = jnp.concatenate([col(1, dr), col(4, ar), col(5, gr)], axis=1).astype(BF16)
    mu = p['rwkv_mu'].astype(F32)
    mseg = lambda i, width: mu[o[i]:o[i] + width]
    mub = jnp.concatenate([mseg(0, mw), mseg(2, mw), mseg(3, mw)]).reshape(1, -1)
    mus = jnp.concatenate([mseg(1, dr), mseg(4, ar), mseg(5, gr)]).reshape(1, -1)
    w2p = jnp.pad(p['rwkv_w2'], ((0, ar), (0, 0))).astype(BF16)
    a2p = jnp.pad(p['rwkv_a2'], ((dr, 0), (0, 0))).astype(BF16)
    hsum = _head_sum_matrix(mw, RWKV_HEAD)
    vecs = (mub, mus, _row(p['rwkv_w0']), _row(p['rwkv_a0']), _row(p['rwkv_k_k']), _row(p['rwkv_k_a']),
            p['lru_conv_w'].astype(F32), _row(p['lru_conv_b']), _row(p['lru_b_a']), _row(p['lru_b_x']),
            _row(p['lru_lambda']))
    mats = (w2p, a2p, p['rwkv_g2'].astype(BF16), hsum, _block_diag(p['lru_w_a']).astype(BF16),
            _block_diag(p['lru_w_x']).astype(BF16))
    r, lw, k, v, kk, a, g, la, lb, gate = _cd_in(h, _row(gains[0]), wbig, wsm, vecs, mats, bsz, seq, tm, mw)
    o_c = _rwkv(r, lw, k, v, kk, a, g, _row(p['rwkv_r_k']), _row(p['rwkv_ln_gain']), _row(p['rwkv_ln_bias']),
                hsum, bsz, seq)
    o_d = _lru(la, lb, gate, bsz, seq, tm)
    return _cd_out(h, o_c, o_d, p['cd_w_out'].astype(BF16), _row(gains[1]), tm)


def kernel(x, mem, norm_gain, xa_wq, xa_wk, xa_wv, xa_wo, mlp_w1, mlp_w2, ab_w_in, gla_w_decay2, gla_b_decay, gla_norm_gain, s5_lambda_re, s5_lambda_im, s5_log_step, s5_b_re, s5_b_im, s5_c_re, s5_c_im, s5_d, s5_w_glu, s5_b_glu, ab_w_out, cd_w_in, rwkv_mu, rwkv_w0, rwkv_w2, rwkv_a0, rwkv_a2, rwkv_g2, rwkv_k_k, rwkv_k_a, rwkv_r_k, rwkv_ln_gain, rwkv_ln_bias, lru_conv_w, lru_conv_b, lru_w_a, lru_b_a, lru_w_x, lru_b_x, lru_lambda, cd_w_out):
    bsz, seq, d = x.shape
    mem_len = mem.shape[1]
    depth = norm_gain.shape[0]
    tm = min(512, seq)
    ab = dict(ab_w_in=ab_w_in, gla_w_decay2=gla_w_decay2, gla_b_decay=gla_b_decay, gla_norm_gain=gla_norm_gain,
              s5_lambda_re=s5_lambda_re, s5_lambda_im=s5_lambda_im, s5_log_step=s5_log_step, s5_b_re=s5_b_re,
              s5_b_im=s5_b_im, s5_c_re=s5_c_re, s5_c_im=s5_c_im, s5_d=s5_d, s5_w_glu=s5_w_glu, s5_b_glu=s5_b_glu,
              ab_w_out=ab_w_out)
    cd = dict(cd_w_in=cd_w_in, rwkv_mu=rwkv_mu, rwkv_w0=rwkv_w0, rwkv_w2=rwkv_w2, rwkv_a0=rwkv_a0,
              rwkv_a2=rwkv_a2, rwkv_g2=rwkv_g2, rwkv_k_k=rwkv_k_k, rwkv_k_a=rwkv_k_a, rwkv_r_k=rwkv_r_k,
              rwkv_ln_gain=rwkv_ln_gain, rwkv_ln_bias=rwkv_ln_bias, lru_conv_w=lru_conv_w, lru_conv_b=lru_conv_b,
              lru_w_a=lru_w_a, lru_b_a=lru_b_a, lru_w_x=lru_w_x, lru_b_x=lru_b_x, lru_lambda=lru_lambda,
              cd_w_out=cd_w_out)
    h = x.astype(F32).reshape(bsz * seq, d)
    mem2 = mem.astype(F32).reshape(bsz * mem_len, d)
    for layer in range(depth):
        g = norm_gain[layer]
        i = layer // 2
        if layer % 2 == 0:
            h = _layer_ab(h, g, {n: w[i] for n, w in ab.items()}, bsz, seq, tm)
        else:
            h = _layer_cd(h, g, {n: w[i] for n, w in cd.items()}, bsz, seq, tm)
        km, vm = _mem_kv(mem2, _row(g[6]), xa_wk[layer].astype(BF16), xa_wv[layer].astype(BF16), mem_len)
        h = _xattn(h, km, vm, xa_wq[layer].astype(BF16), xa_wo[layer].astype(BF16), _row(g[2]), _row(g[3]),
                   bsz, seq, mem_len, tm)
        h = _mlp(h, _row(g[4]), _row(g[5]), mlp_w1[layer].astype(BF16), mlp_w2[layer].astype(BF16),
                 tm=min(1024, bsz * seq), tf=1024)
    return h.reshape(bsz, seq, d).astype(x.dtype)
```

```python
import functools
import math

import jax
import jax.numpy as jnp
from jax import lax
from jax.experimental import pallas as pl
from jax.experimental.pallas import tpu as pltpu

F32 = jnp.float32
BF16 = jnp.bfloat16
HIGHEST = lax.Precision.HIGHEST

NORM_EPS = 1e-6
GLA_HEADS = 4
GLA_TAU = 16.0
GLA_CHUNK = 64
S5_GROUP = 16
S5_STATE = 64
S5_CHUNK = 16
RWKV_HEAD = 64
RWKV_CHUNK = 64
RWKV_GN_EPS = 64e-5
LRU_CONV = 4
LRU_C = 8.0
XA_HEADS = 4

V7X_SUBLANES = 8
VMEM_LIMIT_BYTES = 48 * 1024 * 1024


def _cparams(*semantics):
    return pltpu.CompilerParams(dimension_semantics=semantics, vmem_limit_bytes=VMEM_LIMIT_BYTES)


def _rms(x, gain):
    return x * lax.rsqrt(jnp.mean(x * x, axis=-1, keepdims=True) + NORM_EPS) * gain


def _bdot(a, b):
    return jnp.dot(a.astype(BF16), b.astype(BF16), preferred_element_type=F32)


def _bdot_nt(a, b):
    return lax.dot_general(a.astype(BF16), b.astype(BF16), (((1,), (1,)), ((), ())),
                           preferred_element_type=F32)


def _bdot_tn(a, b):
    return lax.dot_general(a.astype(BF16), b.astype(BF16), (((0,), (0,)), ((), ())),
                           preferred_element_type=F32)


def _hdot(a, b):
    return jnp.dot(a, b, precision=HIGHEST, preferred_element_type=F32)


def _hdot_nt(a, b):
    return lax.dot_general(a, b, (((1,), (1,)), ((), ())), precision=HIGHEST,
                           preferred_element_type=F32)


def _hdot_tn(a, b):
    return lax.dot_general(a, b, (((0,), (0,)), ((), ())), precision=HIGHEST,
                           preferred_element_type=F32)


def _split_bf16(x):
    hi = x.astype(BF16)
    return hi, (x - hi.astype(F32)).astype(BF16)


def _split_dot(x, w_bf16):
    hi, lo = _split_bf16(x)
    return (jnp.dot(hi, w_bf16, preferred_element_type=F32)
            + jnp.dot(lo, w_bf16, preferred_element_type=F32))


def _sigmoid(x):
    return 1.0 / (1.0 + jnp.exp(-x))


def _softplus(x):
    return jnp.maximum(x, 0.0) + jnp.log1p(jnp.exp(-jnp.abs(x)))


def _gelu_tanh(x):
    c = math.sqrt(2.0 / math.pi)
    return x * (0.5 * (1.0 + jnp.tanh(c * (x + 0.044715 * (x * x * x)))))


def _tril_mask(n, strict=False):
    row = lax.broadcasted_iota(jnp.int32, (n, n), 0)
    col = lax.broadcasted_iota(jnp.int32, (n, n), 1)
    return (col < row) if strict else (col <= row)


def _shift_rows(x, shift, carry):
    rolled = pltpu.roll(x, shift, axis=0)
    crolled = pltpu.roll(carry, shift, axis=0)
    rowi = lax.broadcasted_iota(jnp.int32, carry.shape, 0)
    first = jnp.where(rowi < shift, crolled, rolled[0:V7X_SUBLANES])
    return jnp.concatenate([first, rolled[V7X_SUBLANES:]], axis=0)


def _ab_in_kernel(h_ref, g_ref, w_ref, wd_ref, wd2_ref, bd_ref, perm_ref, out_ref, loga_ref, up_ref):
    hn = _rms(h_ref[...], g_ref[...]).astype(BF16)
    out = jnp.dot(hn, w_ref[...], preferred_element_type=F32)
    out_ref[...] = out
    dlr = jnp.dot(hn, wd_ref[...], preferred_element_type=F32)
    z = _hdot(dlr, wd2_ref[...]) + bd_ref[...]
    loga_ref[...] = (jnp.minimum(z, 0.0) - jnp.log1p(jnp.exp(-jnp.abs(z)))) * (1.0 / GLA_TAU)
    mw = up_ref.shape[1]
    u16 = out[:, out.shape[1] - mw:].astype(BF16)
    up_ref[...] = jnp.dot(perm_ref[...], u16, preferred_element_type=F32).astype(BF16)


def _ab_in(h, gain, w_main, w_dlr, w_decay2, b_decay, perm, tm, mw):
    t, d = h.shape
    n_main = w_main.shape[1]
    n_dk = w_decay2.shape[1]
    full = lambda a: pl.BlockSpec(a.shape, lambda i: (0, 0))
    rows = lambda w: pl.BlockSpec((tm, w), lambda i: (i, 0))
    return pl.pallas_call(
        _ab_in_kernel,
        grid=(t // tm,),
        in_specs=[rows(d), full(gain), full(w_main), full(w_dlr), full(w_decay2), full(b_decay), full(perm)],
        out_specs=[rows(n_main), rows(n_dk), rows(mw)],
        out_shape=[jax.ShapeDtypeStruct((t, n_main), F32), jax.ShapeDtypeStruct((t, n_dk), F32),
                   jax.ShapeDtypeStruct((t, mw), BF16)],
        compiler_params=_cparams("parallel"),
        name="ab_in",
    )(h, gain, w_main, w_dlr, w_decay2, b_decay, perm)


def _chunk_step_perm(tm, tc):
    dst = jnp.arange(tm)
    src = (dst % (tm // tc)) * tc + dst // (tm // tc)
    return (src[:, None] == jnp.arange(tm)[None, :]).astype(BF16)


def _gla_kernel(q_ref, k_ref, v_ref, gate_ref, la_ref, gain_ref, o_ref, state_ref, *, n_chunks, dk, dv):
    c = GLA_CHUNK

    @pl.when(pl.program_id(1) == 0)
    def _():
        state_ref[...] = jnp.zeros_like(state_ref)

    tril = _tril_mask(c)
    tril_f = tril.astype(F32)
    scale = dk ** -0.5
    for ci in range(n_chunks):
        rs = slice(ci * c, (ci + 1) * c)
        b = _hdot(tril_f, la_ref[rs, :])
        b_last = b[c - 1:c, :]
        q_in = q_ref[rs, :] * jnp.exp(b) * scale
        k = k_ref[rs, :]
        k_in = k * jnp.exp(-b)
        k_st = k * jnp.exp(b_last - b)
        dec = jnp.exp(b_last)
        for h in range(GLA_HEADS):
            ks = slice(h * dk, (h + 1) * dk)
            vs = slice(h * dv, (h + 1) * dv)
            qh = q_in[:, ks]
            vh = v_ref[rs, vs]
            scores = jnp.where(tril, _bdot_nt(qh, k_in[:, ks]), 0.0)
            st = state_ref[h]
            o = _bdot(scores, vh) + _bdot_nt(qh, st)
            state_ref[h] = st * dec[:, ks] + _bdot_tn(vh, k_st[:, ks])
            on = o * lax.rsqrt(jnp.mean(o * o, axis=-1, keepdims=True) + NORM_EPS) * gain_ref[:, vs]
            g = gate_ref[rs, vs]
            o_ref[rs, vs] = on * (g * _sigmoid(g))


def _gla(qkvgu, loga, gain, bsz, seq, tb):
    t = bsz * seq
    dk = loga.shape[1] // GLA_HEADS
    dv = gain.shape[1] // GLA_HEADS
    nq, nv = GLA_HEADS * dk, GLA_HEADS * dv
    nt = seq // tb
    row = lambda b, i: b * nt + i
    kern = functools.partial(_gla_kernel, n_chunks=tb // GLA_CHUNK, dk=dk, dv=dv)
    return pl.pallas_call(
        kern,
        grid=(bsz, nt),
        in_specs=[pl.BlockSpec((tb, nq), lambda b, i: (row(b, i), 0)),
                  pl.BlockSpec((tb, nq), lambda b, i: (row(b, i), 1)),
                  pl.BlockSpec((tb, nv), lambda b, i: (row(b, i), 1)),
                  pl.BlockSpec((tb, nv), lambda b, i: (row(b, i), 2)),
                  pl.BlockSpec((tb, nq), lambda b, i: (row(b, i), 0)),
                  pl.BlockSpec(gain.shape, lambda b, i: (0, 0))],
        out_specs=pl.BlockSpec((tb, nv), lambda b, i: (row(b, i), 0)),
        out_shape=jax.ShapeDtypeStruct((t, nv), F32),
        scratch_shapes=[pltpu.VMEM((GLA_HEADS, dv, dk), F32)],
        compiler_params=_cparams("parallel", "arbitrary"),
        name="gla",
    )(qkvgu, qkvgu, qkvgu, qkvgu, loga, gain)


def _s5_kernel(u_ref, tz_ref, ws_ref, wc_ref, a_ref, y_ref, u_scr, yg_scr, s_scr, h_scr, *,
               n_pairs, bsz, seq, tm):
    n2 = 2 * S5_STATE
    tc, ch = S5_CHUNK, S5_GROUP
    lanes = u_ref.shape[1]
    per_half = lanes // ch
    kt = tm // tc
    n_tiles = seq // tm
    g_lo = pl.program_id(1)
    lane = lax.broadcasted_iota(jnp.int32, (kt, lanes), 1)
    step_m = [(lane >= ch * j8) & (lane < ch * (j8 + 1)) for j8 in range(per_half)]
    sh_in = [lax.rem(ch * (j8 - g_lo) + lanes, jnp.int32(lanes)) for j8 in range(per_half)]
    sh_out = [lax.rem(ch * (g_lo - j8) + lanes, jnp.int32(lanes)) for j8 in range(per_half)]
    own = (lane >= ch * g_lo) & (lane < ch * (g_lo + 1))

    def gather_tile(i, _):
        for b in range(bsz):
            for half in range(tc // per_half):
                acc = jnp.zeros((kt, lanes), F32)
                for j8 in range(per_half):
                    r0 = pl.multiple_of(b * seq + i * tm + (half * per_half + j8) * kt, kt)
                    x = u_ref[pl.ds(r0, kt), :].astype(F32)
                    acc = jnp.where(step_m[j8], pltpu.roll(x, sh_in[j8], axis=1), acc)
                u_scr[half, i, pl.ds(b, kt, stride=bsz), :] = acc
        return 0

    lax.fori_loop(0, n_tiles, gather_tile, 0)
    rows = n_tiles * kt * bsz
    n_half = tc // per_half
    u = jnp.concatenate([u_scr[hf].reshape(rows, lanes) for hf in range(n_half)], axis=1).astype(BF16)
    s_scr[...] = jnp.dot(u, ws_ref[0], preferred_element_type=F32)
    a1 = a_ref[0, 0:1, :]
    a2 = a_ref[0, 1:2, :]

    def body(j, carry):
        h, hs = carry
        r0 = pl.multiple_of(j * (2 * bsz), 2 * bsz)
        blk = s_scr[pl.ds(r0, 2 * bsz), :]
        h1 = a1 * h + a2 * hs + blk[0:bsz, 0:n2]
        hs1 = a1 * hs - a2 * h + blk[0:bsz, n2:2 * n2]
        h2 = a1 * h1 + a2 * hs1 + blk[bsz:2 * bsz, 0:n2]
        hs2 = a1 * hs1 - a2 * h1 + blk[bsz:2 * bsz, n2:2 * n2]
        h_scr[pl.ds(r0, 2 * bsz), :] = jnp.concatenate([h, h1], axis=0)
        return h2, hs2

    zero = jnp.zeros((bsz, n2), F32)
    lax.fori_loop(0, n_pairs, body, (zero, zero))
    yg = (jnp.dot(u, tz_ref[0], preferred_element_type=F32)
          + jnp.dot(h_scr[...].astype(BF16), wc_ref[0], preferred_element_type=F32))
    for hf in range(n_half):
        yg_scr[hf] = yg[:, hf * lanes:(hf + 1) * lanes].reshape(n_tiles, kt * bsz, lanes)

    @pl.when(g_lo == 0)
    def _():
        y_ref[...] = jnp.zeros_like(y_ref)

    def scatter_tile(i, _):
        for b in range(bsz):
            for half in range(tc // per_half):
                yb = yg_scr[half, i, pl.ds(b, kt, stride=bsz), :]
                for j8 in range(per_half):
                    r0 = pl.multiple_of(b * seq + i * tm + (half * per_half + j8) * kt, kt)
                    y_ref[pl.ds(r0, kt), :] = jnp.where(own, pltpu.roll(yb, sh_out[j8], axis=1),
                                                        y_ref[pl.ds(r0, kt), :])
        return 0

    lax.fori_loop(0, n_tiles, scatter_tile, 0)


def _s5(u_perm, tz, ws, wc, a_pow, bsz, seq, tm):
    t, mw = u_perm.shape
    lanes = 128
    groups = tz.shape[0]
    g_per_blk = lanes // S5_GROUP
    assert 2 * bsz == V7X_SUBLANES, "the chunk scan walks two chunks per 8-row tile"
    rows = (seq // S5_CHUNK) * bsz
    n_pairs = rows // (2 * bsz)
    per_g = lambda a: pl.BlockSpec((1,) + a.shape[1:], lambda q, g: (q * g_per_blk + g, 0, 0))
    blk = pl.BlockSpec((t, lanes), lambda q, g: (0, q))
    kern = functools.partial(_s5_kernel, n_pairs=n_pairs, bsz=bsz, seq=seq, tm=tm)
    kt = tm // S5_CHUNK
    return pl.pallas_call(
        kern,
        grid=(mw // lanes, g_per_blk),
        in_specs=[blk, per_g(tz), per_g(ws), per_g(wc), per_g(a_pow)],
        out_specs=blk,
        out_shape=jax.ShapeDtypeStruct((t, mw), F32),
        scratch_shapes=[pltpu.VMEM((tz.shape[1] // lanes, seq // tm, kt * bsz, lanes), F32),
                        pltpu.VMEM((tz.shape[2] // lanes, seq // tm, kt * bsz, lanes), F32),
                        pltpu.VMEM((rows, ws.shape[2]), F32), pltpu.VMEM((rows, wc.shape[1]), F32)],
        compiler_params=_cparams("parallel", "arbitrary"),
        name="s5",
    )(u_perm, tz, ws, wc, a_pow)


def _s5_weights(lam_re, lam_im, log_step, b_re, b_im, c_re, c_im):
    tc = S5_CHUNK
    groups, n = lam_re.shape
    lr = jnp.minimum(lam_re.astype(F32), -1e-4)
    li = lam_im.astype(F32)
    delta = jnp.exp(log_step.astype(F32))[:, None]
    tau = jnp.arange(tc + 1, dtype=F32)[:, None, None]
    mag = jnp.exp(tau * (lr * delta))
    ang = tau * (li * delta)
    pw_re, pw_im = mag * jnp.cos(ang), mag * jnp.sin(ang)
    num_re, num_im = pw_re[1] - 1.0, pw_im[1]
    den = lr * lr + li * li
    f_re = (num_re * lr + num_im * li) / den
    f_im = (num_im * lr - num_re * li) / den
    b_re, b_im = b_re.astype(F32), b_im.astype(F32)
    bb_re = f_re[..., None] * b_re - f_im[..., None] * b_im
    bb_im = f_re[..., None] * b_im + f_im[..., None] * b_re
    c_re, c_im = c_re.astype(F32), c_im.astype(F32)
    cp_re = c_re[None] * pw_re[:, :, None, :] - c_im[None] * pw_im[:, :, None, :]
    cp_im = c_re[None] * pw_im[:, :, None, :] + c_im[None] * pw_re[:, :, None, :]
    ein = functools.partial(jnp.einsum, precision=HIGHEST)
    kern = ein('tgcn,gnd->tgcd', cp_re[:tc], bb_re) - ein('tgcn,gnd->tgcd', cp_im[:tc], bb_im)
    j = jnp.arange(tc)[:, None]
    i = jnp.arange(tc)[None, :]
    diff = i - j
    ksel = jnp.where((diff >= 0)[:, :, None, None, None], kern[jnp.clip(diff, 0, tc - 1)], 0.0)
    tz = jnp.transpose(ksel, (2, 0, 4, 1, 3)).reshape(groups, tc * S5_GROUP, tc * S5_GROUP)
    rev_re, rev_im = pw_re[tc - 1::-1][:tc], pw_im[tc - 1::-1][:tc]
    s_re = rev_re[:, :, :, None] * bb_re[None] - rev_im[:, :, :, None] * bb_im[None]
    s_im = rev_re[:, :, :, None] * bb_im[None] + rev_im[:, :, :, None] * bb_re[None]
    s_re = jnp.transpose(s_re, (1, 0, 3, 2)).reshape(groups, tc * S5_GROUP, n)
    s_im = jnp.transpose(s_im, (1, 0, 3, 2)).reshape(groups, tc * S5_GROUP, n)
    ws = jnp.concatenate([s_re, s_im, s_im, s_re], axis=-1)
    o_re = jnp.transpose(cp_re[1:], (1, 3, 0, 2)).reshape(groups, n, tc * S5_GROUP)
    o_im = jnp.transpose(cp_im[1:], (1, 3, 0, 2)).reshape(groups, n, tc * S5_GROUP)
    wc = jnp.concatenate([o_re, -o_im], axis=1)
    a_pow = jnp.stack([jnp.concatenate([pw_re[tc], pw_re[tc]], axis=-1),
                       jnp.concatenate([-pw_im[tc], pw_im[tc]], axis=-1)], axis=1)
    return tz, ws, wc, a_pow


def _ab_out_kernel(h_ref, oa_ref, ys_ref, u_ref, d_ref, wglu_ref, bglu_ref, wo_ref, g_ref, perm_ref, out_ref):
    mw = oa_ref.shape[1]
    hi, lo = _split_bf16(ys_ref[...])
    y_ssm = (jnp.dot(perm_ref[...], hi, preferred_element_type=F32)
             + jnp.dot(perm_ref[...], lo, preferred_element_type=F32))
    y = y_ssm + d_ref[...] * u_ref[...]
    ob = _gelu_tanh(y) * _sigmoid(_bdot(y, wglu_ref[...]) + bglu_ref[...])
    mix = _bdot(oa_ref[...], wo_ref[0:mw, :]) + _bdot(ob, wo_ref[mw:2 * mw, :])
    out_ref[...] = h_ref[...] + _rms(mix, g_ref[...])


def _ab_out(h, o_a, y_ssm, qkvgu, d_skip, w_glu, b_glu, w_out, gain, perm_t, tm):
    t, d = h.shape
    mw = o_a.shape[1]
    full = lambda a: pl.BlockSpec(a.shape, lambda i: (0, 0))
    rows = lambda w, cb=0: pl.BlockSpec((tm, w), lambda i: (i, cb))
    return pl.pallas_call(
        _ab_out_kernel,
        grid=(t // tm,),
        in_specs=[rows(d), rows(mw), rows(mw), rows(mw, qkvgu.shape[1] // mw - 1), full(d_skip), full(w_glu),
                  full(b_glu), full(w_out), full(gain), full(perm_t)],
        out_specs=rows(d),
        out_shape=jax.ShapeDtypeStruct((t, d), F32),
        compiler_params=_cparams("parallel"),
        name="ab_out",
    )(h, o_a, y_ssm, qkvgu, d_skip, w_glu, b_glu, w_out, gain, perm_t)


def _mem_kv_kernel(mem_ref, g_ref, wk_ref, wv_ref, k_ref, v_ref):
    mn = _rms(mem_ref[...], g_ref[...]).astype(BF16)
    k_ref[...] = jnp.dot(mn, wk_ref[...], preferred_element_type=F32).astype(BF16)
    v_ref[...] = jnp.dot(mn, wv_ref[...], preferred_element_type=F32).astype(BF16)


def _mem_kv(mem, gain, wk, wv, tm):
    t, d = mem.shape
    full = lambda a: pl.BlockSpec(a.shape, lambda i: (0, 0))
    rows = pl.BlockSpec((tm, d), lambda i: (i, 0))
    return pl.pallas_call(
        _mem_kv_kernel,
        grid=(t // tm,),
        in_specs=[rows, full(gain), full(wk), full(wv)],
        out_specs=[rows, rows],
        out_shape=[jax.ShapeDtypeStruct((t, d), BF16), jax.ShapeDtypeStruct((t, d), BF16)],
        compiler_params=_cparams("parallel"),
        name="mem_kv",
    )(mem, gain, wk, wv)


def _xattn_kernel(h_ref, k_ref, v_ref, wq_ref, wo_ref, gq_ref, go_ref, out_ref, o_scr):
    h = h_ref[...]
    d = h.shape[1]
    hd = d // XA_HEADS
    q = jnp.dot(_rms(h, gq_ref[...]).astype(BF16), wq_ref[...], preferred_element_type=F32)
    for hh in range(XA_HEADS):
        cs = slice(hh * hd, (hh + 1) * hd)
        s = _bdot_nt(q[:, cs], k_ref[:, cs]) * (hd ** -0.5)
        e = jnp.exp(s - jnp.max(s, axis=-1, keepdims=True))
        p = e / jnp.sum(e, axis=-1, keepdims=True)
        o_scr[:, cs] = _bdot(p, v_ref[:, cs]).astype(BF16)
    xa = jnp.dot(o_scr[...], wo_ref[...], preferred_element_type=F32)
    out_ref[...] = h + _rms(xa, go_ref[...])


def _xattn(h, k, v, wq, wo, gq, go, bsz, seq, mem_len, tm):
    t, d = h.shape
    nt = seq // tm
    full = lambda a: pl.BlockSpec(a.shape, lambda b, i: (0, 0))
    rows = pl.BlockSpec((tm, d), lambda b, i: (b * nt + i, 0))
    kv = pl.BlockSpec((mem_len, d), lambda b, i: (b, 0))
    return pl.pallas_call(
        _xattn_kernel,
        grid=(bsz, nt),
        in_specs=[rows, kv, kv, full(wq), full(wo), full(gq), full(go)],
        out_specs=rows,
        out_shape=jax.ShapeDtypeStruct((t, d), F32),
        scratch_shapes=[pltpu.VMEM((tm, d), BF16)],
        compiler_params=_cparams("parallel", "parallel"),
        name="xattn",
    )(h, k, v, wq, wo, gq, go)


def _mlp_kernel(h_ref, gi_ref, go_ref, w1_ref, w2_ref, out_ref, xn_scr, acc_scr):
    j = pl.program_id(1)

    @pl.when(j == 0)
    def _():
        xn_scr[...] = _rms(h_ref[...], gi_ref[...]).astype(BF16)
        acc_scr[...] = jnp.zeros_like(acc_scr)

    a = jnp.maximum(jnp.dot(xn_scr[...], w1_ref[...], preferred_element_type=F32), 0.0)
    acc_scr[...] += jnp.dot((a * a).astype(BF16), w2_ref[...], preferred_element_type=F32)

    @pl.when(j == pl.num_programs(1) - 1)
    def _():
        out_ref[...] = h_ref[...] + _rms(acc_scr[...], go_ref[...])


def _mlp(h, gi, go, w1, w2, tm, tf):
    t, d = h.shape
    dff = w1.shape[1]
    full = lambda a: pl.BlockSpec(a.shape, lambda i, j: (0, 0))
    rows = pl.BlockSpec((tm, d), lambda i, j: (i, 0))
    return pl.pallas_call(
        _mlp_kernel,
        grid=(t // tm, dff // tf),
        in_specs=[rows, full(gi), full(go), pl.BlockSpec((d, tf), lambda i, j: (0, j)),
                  pl.BlockSpec((tf, d), lambda i, j: (j, 0))],
        out_specs=rows,
        out_shape=jax.ShapeDtypeStruct((t, d), F32),
        scratch_shapes=[pltpu.VMEM((tm, d), BF16), pltpu.VMEM((tm, d), F32)],
        compiler_params=_cparams("parallel", "arbitrary"),
        name="mlp",
    )(h, gi, go, w1, w2)


def _cd_in_kernel(h_ref, g_ref, wbig_ref, wsm_ref, mub_ref, mus_ref, w0_ref, w2_ref, a0_ref, a2_ref, g2_ref,
                  kk_ref, ka_ref, hsum_ref, cw_ref, cb_ref, wa_ref, ba_ref, wx_ref, bx_ref, lam_ref,
                  r_o, lw_o, k_o, v_o, kk_o, a_o, g_o, la_o, lb_o, gate_o,
                  carry_b, carry_s, carry_x, *, mw):
    tm = h_ref.shape[0]

    @pl.when(pl.program_id(1) == 0)
    def _():
        carry_b[...] = jnp.zeros_like(carry_b)
        carry_s[...] = jnp.zeros_like(carry_s)
        carry_x[...] = jnp.zeros_like(carry_x)

    hn = _rms(h_ref[...], g_ref[...]).astype(BF16)
    pb = jnp.dot(hn, wbig_ref[...], preferred_element_type=F32)
    ps = jnp.dot(hn, wsm_ref[...], preferred_element_type=F32)
    p3 = pb[:, 0:3 * mw]
    xb = pb[:, 3 * mw:4 * mw]
    gate_o[...] = pb[:, 4 * mw:5 * mw]

    prev3 = _shift_rows(p3, 1, carry_b[...])
    prevs = _shift_rows(ps, 1, carry_s[...])
    carry_b[...] = p3[tm - V7X_SUBLANES:tm, :]
    carry_s[...] = ps[tm - V7X_SUBLANES:tm, :]
    p3 = p3 + (prev3 - p3) * mub_ref[...]
    ps = ps + (prevs - ps) * mus_ref[...]
    r = p3[:, 0:mw]
    k = p3[:, mw:2 * mw]
    v = p3[:, 2 * mw:3 * mw]
    lora = ps[:, 0:128]
    wlog = -_softplus(-(w0_ref[...] + _bdot(jnp.tanh(lora), w2_ref[...]))) - 0.5
    a = _sigmoid(a0_ref[...] + _bdot(lora, a2_ref[...]))
    kkr = k * kk_ref[...]
    norm = jnp.sqrt(_split_dot(kkr * kkr, hsum_ref[...]))
    r_o[...] = r
    lw_o[...] = -jnp.exp(wlog)
    k_o[...] = k * (1.0 + (a - 1.0) * ka_ref[...])
    v_o[...] = v
    kk_o[...] = kkr / jnp.maximum(norm, 1e-12)
    a_o[...] = a
    g_o[...] = _bdot(_sigmoid(ps[:, 128:256]), g2_ref[...])

    cx = carry_x[...]
    xc = cb_ref[...] + cw_ref[LRU_CONV - 1:LRU_CONV, :] * xb
    for sh in range(1, LRU_CONV):
        xc = xc + cw_ref[LRU_CONV - 1 - sh:LRU_CONV - sh, :] * _shift_rows(xb, sh, cx)
    carry_x[...] = xb[tm - V7X_SUBLANES:tm, :]
    rg = _sigmoid(_bdot(xc, wa_ref[...]) + ba_ref[...])
    ig = _sigmoid(_bdot(xc, wx_ref[...]) + bx_ref[...])
    log_a = -LRU_C * rg * _softplus(-lam_ref[...])
    la_o[...] = jnp.exp(log_a)
    th = jnp.tanh(log_a)
    lb_o[...] = jnp.sqrt(-2.0 * th / (1.0 - th)) * (ig * xc)


def _cd_in(h, gain, wbig, wsm, vecs, mats, bsz, seq, tm, mw):
    t, d = h.shape
    nt = seq // tm
    full = lambda a: pl.BlockSpec(a.shape, lambda b, i: (0, 0))
    rows = lambda w: pl.BlockSpec((tm, w), lambda b, i: (b * nt + i, 0))
    (mub, mus, w0, a0, kk_w, ka_w, cw, cb, ba, bx, lam) = vecs
    (w2p, a2p, g2, hsum, wa, wx) = mats
    args = (h, gain, wbig, wsm, mub, mus, w0, w2p, a0, a2p, g2, kk_w, ka_w, hsum, cw, cb, wa, ba, wx, bx, lam)
    kern = functools.partial(_cd_in_kernel, mw=mw)
    return pl.pallas_call(
        kern,
        grid=(bsz, nt),
        in_specs=[rows(d)] + [full(a) for a in args[1:]],
        out_specs=[rows(mw)] * 10,
        out_shape=[jax.ShapeDtypeStruct((t, mw), F32)] * 10,
        scratch_shapes=[pltpu.VMEM((V7X_SUBLANES, 3 * mw), F32), pltpu.VMEM((V7X_SUBLANES, wsm.shape[1]), F32),
                        pltpu.VMEM((V7X_SUBLANES, mw), F32)],
        compiler_params=_cparams("parallel", "arbitrary"),
        name="cd_in",
    )(*args)


def _rwkv_kernel(r_ref, lw_ref, k_ref, v_ref, kk_ref, a_ref, g_ref, rk_ref, lng_ref, lnb_ref, hsum_ref,
                 o_ref, state_ref, y_scr, *, slab_heads, n_chunks):
    c = RWKV_CHUNK
    hd = RWKV_HEAD
    assert c == hd, "one block mask serves both the (head, s) and the (head, d) layouts"
    sw = slab_heads * hd
    n_slabs = r_ref.shape[1] // sw

    @pl.when(pl.program_id(1) == 0)
    def _():
        state_ref[...] = jnp.zeros_like(state_ref)

    tril_f = _tril_mask(c).astype(F32)
    blk_m = (lax.broadcasted_iota(jnp.int32, (sw, sw), 0) // c) == (lax.broadcasted_iota(jnp.int32, (sw, sw), 1) // c)
    wide_t = lax.broadcasted_iota(jnp.int32, (c, sw), 0)
    wide_s = lax.broadcasted_iota(jnp.int32, (c, sw), 1) % c
    strict_w = wide_s < wide_t
    incl_w = wide_s <= wide_t
    eye_w = jnp.where(wide_s == wide_t, 1.0, 0.0)

    def bdiag(x):
        x16 = x.astype(BF16)
        return jnp.where(blk_m, jnp.concatenate([x16] * slab_heads, axis=0), jnp.zeros((), BF16))

    def mm(a, w16):
        return jnp.dot(a.astype(BF16), w16, preferred_element_type=F32)

    nt_dims = (((1,), (1,)), ((), ()))
    chains = []
    for chunk in range(n_chunks):
        rs = slice(chunk * c, (chunk + 1) * c)
        lw = lw_ref[rs, :]
        cum = _hdot(tril_f, lw)
        cum_last = cum[c - 1:c, :]
        r = r_ref[rs, :]
        k = k_ref[rs, :]
        v = v_ref[rs, :]
        kk = kk_ref[rs, :]
        bvec = kk * a_ref[rs, :]
        inv_g = jnp.exp(-cum)
        to_end = jnp.exp(cum_last - cum)
        rt = r * jnp.exp(cum)
        kp = kk * jnp.exp(cum - lw)
        be = bvec * inv_g
        kh = k * inv_g
        bb = bvec * to_end
        kb = k * to_end
        g_end = jnp.exp(cum_last)
        for s in range(n_slabs):
            ls = slice(s * sw, (s + 1) * sw)
            chains.append(dict(rs=rs, ls=ls, slab=s, rt=rt[:, ls], kp=kp[:, ls], v=v[:, ls], be=be[:, ls],
                               kh=kh[:, ls], kb=kb[:, ls], bb=bb[:, ls], g_end=g_end[:, ls]))

    for ch in chains:
        lhs2 = jnp.concatenate([ch['kp'], ch['rt']], axis=0).astype(BF16)
        ab = lax.dot_general(lhs2, bdiag(ch['be']), nt_dims, preferred_element_type=F32)
        ak = lax.dot_general(lhs2, bdiag(ch['kh']), nt_dims, preferred_element_type=F32)
        ch['a_kb'] = jnp.where(strict_w, ab[0:c], 0.0)
        ch['a_rb'] = jnp.where(incl_w, ab[c:2 * c], 0.0).astype(BF16)
        ch['a_kr'] = jnp.concatenate([jnp.where(strict_w, ak[0:c], 0.0), jnp.where(incl_w, ak[c:2 * c], 0.0)],
                                     axis=0).astype(BF16)
    for ch in chains:
        x = -ch['a_kb']
        ch['t'] = eye_w + x
        ch['x'] = mm(x, bdiag(x))
    for _ in range(int(math.log2(c)) - 2):
        for ch in chains:
            res = mm(jnp.concatenate([ch['t'], ch['x']], axis=0), bdiag(ch['x']))
            ch['t'] = ch['t'] + res[0:c]
            ch['x'] = res[c:2 * c]
    for ch in chains:
        ch['t'] = ch['t'] + mm(ch['t'], bdiag(ch['x']))
    for ch in chains:
        l_hi, l_lo = _split_bf16(ch['a_kb'])
        t_hi, t_lo = _split_bf16(ch['t'])
        lt = mm(jnp.concatenate([l_hi, l_lo], axis=0), bdiag(t_hi))
        ch['resid'] = eye_w - ch['t'] - (lt[0:c] + lt[c:2 * c] + mm(l_hi, bdiag(t_lo)))
    for ch in chains:
        ch['t16'] = (ch['t'] + mm(ch['t'], bdiag(ch['resid']))).astype(BF16)
    for ch in chains:
        ch['w_tok'] = mm(ch['t16'], bdiag(ch['kp']))
        ch['av'] = mm(ch['a_kr'], bdiag(ch['v']))
    for ch in chains:
        ch['u_tok'] = mm(ch['t16'], bdiag(ch['av'][0:c]))
    for ch in chains:
        q_tok = ch['rt'] - mm(ch['a_rb'], bdiag(ch['w_tok']))
        ch['qw'] = jnp.concatenate([q_tok, ch['w_tok']], axis=0).astype(BF16)
        ch['y0'] = ch['av'][c:2 * c] - mm(ch['a_rb'], bdiag(ch['u_tok']))
        ch['kb2'] = jnp.concatenate([ch['kb'], ch['bb']], axis=0).astype(BF16)

    for ch in chains:
        s = ch['slab']
        p_t = state_ref[s]
        res = lax.dot_general(ch['qw'], p_t.astype(BF16), nt_dims, preferred_element_type=F32)
        y_scr[ch['rs'], ch['ls']] = res[0:c] + ch['y0']
        u_all = -(ch['u_tok'] + res[c:2 * c])
        upd = lax.dot_general(jnp.concatenate([ch['v'], u_all], axis=0).astype(BF16), ch['kb2'],
                              (((0,), (0,)), ((), ())), preferred_element_type=F32)
        state_ref[s] = p_t * ch['g_end'] + jnp.where(blk_m, upd, 0.0)

    hsum = hsum_ref[...]
    y = y_scr[...]
    r = r_ref[...]
    v = v_ref[...]
    mean = _split_dot(y, hsum) * (1.0 / hd)
    yc = y - mean
    var = _split_dot(yc * yc, hsum) * (1.0 / hd)
    yn = yc * lax.rsqrt(var + RWKV_GN_EPS) * lng_ref[...] + lnb_ref[...]
    bonus = _split_dot(r * k_ref[...] * rk_ref[...], hsum) * v
    o_ref[...] = (yn + bonus) * g_ref[...]


def _rwkv(r, lw, k, v, kk, a, g, rk, lng, lnb, hsum, bsz, seq, slab_heads=4, n_chunks=4):
    t, mw = r.shape
    c = RWKV_CHUNK * n_chunks
    nt = seq // c
    sw = slab_heads * RWKV_HEAD
    full = lambda x: pl.BlockSpec(x.shape, lambda b, i: (0, 0))
    rows = pl.BlockSpec((c, mw), lambda b, i: (b * nt + i, 0))
    kern = functools.partial(_rwkv_kernel, slab_heads=slab_heads, n_chunks=n_chunks)
    return pl.pallas_call(
        kern,
        grid=(bsz, nt),
        in_specs=[rows] * 7 + [full(rk), full(lng), full(lnb), full(hsum)],
        out_specs=rows,
        out_shape=jax.ShapeDtypeStruct((t, mw), F32),
        scratch_shapes=[pltpu.VMEM((mw // sw, sw, sw), F32), pltpu.VMEM((c, mw), F32)],
        compiler_params=_cparams("parallel", "arbitrary"),
        name="rwkv",
    )(r, lw, k, v, kk, a, g, rk, lng, lnb, hsum)


def _lru_kernel(a_ref, b_ref, gate_ref, o_ref, carry_ref):
    tm = a_ref.shape[0]
    sub = V7X_SUBLANES

    @pl.when(pl.program_id(1) == 0)
    def _():
        carry_ref[...] = jnp.zeros_like(carry_ref)

    rowi = lax.broadcasted_iota(jnp.int32, (sub, a_ref.shape[1]), 0)

    def body(j, h):
        r0 = pl.multiple_of(j * sub, sub)
        a = a_ref[pl.ds(r0, sub), :]
        b = b_ref[pl.ds(r0, sub), :]
        sh = 1
        while sh < sub:
            a_sh = jnp.where(rowi >= sh, pltpu.roll(a, sh, axis=0), 1.0)
            b_sh = jnp.where(rowi >= sh, pltpu.roll(b, sh, axis=0), 0.0)
            b = b + a * b_sh
            a = a * a_sh
            sh *= 2
        hb = b + a * h
        o_ref[pl.ds(r0, sub), :] = hb * _gelu_tanh(gate_ref[pl.ds(r0, sub), :])
        return hb[sub - 1:sub, :]

    carry_ref[0:1, :] = lax.fori_loop(0, tm // sub, body, carry_ref[0:1, :])


def _lru(a, b, gate, bsz, seq, tm):
    t, mw = a.shape
    nt = seq // tm
    rows = pl.BlockSpec((tm, mw), lambda bi, i: (bi * nt + i, 0))
    return pl.pallas_call(
        _lru_kernel,
        grid=(bsz, nt),
        in_specs=[rows, rows, rows],
        out_specs=rows,
        out_shape=jax.ShapeDtypeStruct((t, mw), F32),
        scratch_shapes=[pltpu.VMEM((V7X_SUBLANES, mw), F32)],
        compiler_params=_cparams("parallel", "arbitrary"),
        name="lru",
    )(a, b, gate)


def _cd_out_kernel(h_ref, oc_ref, od_ref, wo_ref, g_ref, out_ref):
    mw = oc_ref.shape[1]
    mix = _bdot(oc_ref[...], wo_ref[0:mw, :]) + _bdot(od_ref[...], wo_ref[mw:2 * mw, :])
    out_ref[...] = h_ref[...] + _rms(mix, g_ref[...])


def _cd_out(h, o_c, o_d, w_out, gain, tm):
    t, d = h.shape
    mw = o_c.shape[1]
    full = lambda a: pl.BlockSpec(a.shape, lambda i: (0, 0))
    rows = lambda w: pl.BlockSpec((tm, w), lambda i: (i, 0))
    return pl.pallas_call(
        _cd_out_kernel,
        grid=(t // tm,),
        in_specs=[rows(d), rows(mw), rows(mw), full(w_out), full(gain)],
        out_specs=rows(d),
        out_shape=jax.ShapeDtypeStruct((t, d), F32),
        compiler_params=_cparams("parallel"),
        name="cd_out",
    )(h, o_c, o_d, w_out, gain)


def _row(vec):
    return vec.astype(F32).reshape(1, -1)


def _block_diag(blocks):
    nb, bi, bo = blocks.shape
    eye = jnp.eye(nb, dtype=blocks.dtype)
    return (eye[:, None, :, None] * blocks[:, :, None, :]).reshape(nb * bi, nb * bo)


def _head_sum_matrix(width, head):
    idx = jnp.arange(width) // head
    return (idx[:, None] == idx[None, :]).astype(BF16)


def _layer_ab(h, gains, p, bsz, seq, tm):
    d = h.shape[1]
    mw = d // 2
    dk = p['gla_b_decay'].shape[0] // GLA_HEADS
    nq = GLA_HEADS * dk
    w_in = p['ab_w_in']
    rank = p['gla_w_decay2'].shape[0]
    o_dlr = 2 * nq + 2 * mw
    w_main = jnp.concatenate([w_in[:, :o_dlr], w_in[:, o_dlr + rank:]], axis=1).astype(BF16)
    w_dlr = jnp.pad(w_in[:, o_dlr:o_dlr + rank], ((0, 0), (0, 128 - rank))).astype(BF16)
    w_decay2 = jnp.pad(p['gla_w_decay2'].astype(F32), ((0, 128 - rank), (0, 0)))
    perm = _chunk_step_perm(tm, S5_CHUNK)
    qkvgu, loga, u_perm = _ab_in(h, _row(gains[0]), w_main, w_dlr, w_decay2, _row(p['gla_b_decay']), perm, tm, mw)
    o_a = _gla(qkvgu, loga, _row(p['gla_norm_gain']), bsz, seq, tb=256)
    tz, ws, wc, a_pow = _s5_weights(p['s5_lambda_re'], p['s5_lambda_im'], p['s5_log_step'], p['s5_b_re'],
                                    p['s5_b_im'], p['s5_c_re'], p['s5_c_im'])
    y_perm = _s5(u_perm, tz.astype(BF16), ws.astype(BF16), wc.astype(BF16), a_pow, bsz, seq, tm)
    return _ab_out(h, o_a, y_perm, qkvgu, _row(p['s5_d']), p['s5_w_glu'].astype(BF16), _row(p['s5_b_glu']),
                   p['ab_w_out'].astype(BF16), _row(gains[1]), perm.T, tm)


def _layer_cd(h, gains, p, bsz, seq, tm):
    d = h.shape[1]
    mw = d // 2
    w_in = p['cd_w_in']
    dr = p['rwkv_w2'].shape[0]
    ar = p['rwkv_a2'].shape[0]
    gr = p['rwkv_g2'].shape[0]
    assert dr + ar == 128 and gr == 128
    o = [0, mw, mw + dr, 2 * mw + dr, 3 * mw + dr, 3 * mw + dr + ar, 3 * mw + dr + ar + gr]
    col = lambda i, width: w_in[:, o[i]:o[i] + width]
    wbig = jnp.concatenate([col(0, mw), col(2, mw), col(3, mw), w_in[:, o[6]:]], axis=1).astype(BF16)
    wsm = jnp.concatenate([col(1, dr), col(4, ar), col(5, gr)], axis=1).astype(BF16)
    mu = p['rwkv_mu'].astype(F32)
    mseg = lambda i, width: mu[o[i]:o[i] + width]
    mub = jnp.concatenate([mseg(0, mw), mseg(2, mw), mseg(3, mw)]).reshape(1, -1)
    mus = jnp.concatenate([mseg(1, dr), mseg(4, ar), mseg(5, gr)]).reshape(1, -1)
    w2p = jnp.pad(p['rwkv_w2'], ((0, ar), (0, 0))).astype(BF16)
    a2p = jnp.pad(p['rwkv_a2'], ((dr, 0), (0, 0))).astype(BF16)
    hsum = _head_sum_matrix(mw, RWKV_HEAD)
    vecs = (mub, mus, _row(p['rwkv_w0']), _row(p['rwkv_a0']), _row(p['rwkv_k_k']), _row(p['rwkv_k_a']),
            p['lru_conv_w'].astype(F32), _row(p['lru_conv_b']), _row(p['lru_b_a']), _row(p['lru_b_x']),
            _row(p['lru_lambda']))
    mats = (w2p, a2p, p['rwkv_g2'].astype(BF16), hsum, _block_diag(p['lru_w_a']).astype(BF16),
            _block_diag(p['lru_w_x']).astype(BF16))
    r, lw, k, v, kk, a, g, la, lb, gate = _cd_in(h, _row(gains[0]), wbig, wsm, vecs, mats, bsz, seq, tm, mw)
    o_c = _rwkv(r, lw, k, v, kk, a, g, _row(p['rwkv_r_k']), _row(p['rwkv_ln_gain']), _row(p['rwkv_ln_bias']),
                hsum, bsz, seq)
    o_d = _lru(la, lb, gate, bsz, seq, tm)
    return _cd_out(h, o_c, o_d, p['cd_w_out'].astype(BF16), _row(gains[1]), tm)


def kernel(x, mem, norm_gain, xa_wq, xa_wk, xa_wv, xa_wo, mlp_w1, mlp_w2, ab_w_in, gla_w_decay2, gla_b_decay, gla_norm_gain, s5_lambda_re, s5_lambda_im, s5_log_step, s5_b_re, s5_b_im, s5_c_re, s5_c_im, s5_d, s5_w_glu, s5_b_glu, ab_w_out, cd_w_in, rwkv_mu, rwkv_w0, rwkv_w2, rwkv_a0, rwkv_a2, rwkv_g2, rwkv_k_k, rwkv_k_a, rwkv_r_k, rwkv_ln_gain, rwkv_ln_bias, lru_conv_w, lru_conv_b, lru_w_a, lru_b_a, lru_w_x, lru_b_x, lru_lambda, cd_w_out):
    bsz, seq, d = x.shape
    mem_len = mem.shape[1]
    depth = norm_gain.shape[0]
    tm = min(512, seq)
    ab = dict(ab_w_in=ab_w_in, gla_w_decay2=gla_w_decay2, gla_b_decay=gla_b_decay, gla_norm_gain=gla_norm_gain,
              s5_lambda_re=s5_lambda_re, s5_lambda_im=s5_lambda_im, s5_log_step=s5_log_step, s5_b_re=s5_b_re,
              s5_b_im=s5_b_im, s5_c_re=s5_c_re, s5_c_im=s5_c_im, s5_d=s5_d, s5_w_glu=s5_w_glu, s5_b_glu=s5_b_glu,
              ab_w_out=ab_w_out)
    cd = dict(cd_w_in=cd_w_in, rwkv_mu=rwkv_mu, rwkv_w0=rwkv_w0, rwkv_w2=rwkv_w2, rwkv_a0=rwkv_a0,
              rwkv_a2=rwkv_a2, rwkv_g2=rwkv_g2, rwkv_k_k=rwkv_k_k, rwkv_k_a=rwkv_k_a, rwkv_r_k=rwkv_r_k,
              rwkv_ln_gain=rwkv_ln_gain, rwkv_ln_bias=rwkv_ln_bias, lru_conv_w=lru_conv_w, lru_conv_b=lru_conv_b,
              lru_w_a=lru_w_a, lru_b_a=lru_b_a, lru_w_x=lru_w_x, lru_b_x=lru_b_x, lru_lambda=lru_lambda,
              cd_w_out=cd_w_out)
    h = x.astype(F32).reshape(bsz * seq, d)
    mem2 = mem.astype(F32).reshape(bsz * mem_len, d)
    for layer in range(depth):
        g = norm_gain[layer]
        i = layer // 2
        if layer % 2 == 0:
            h = _layer_ab(h, g, {n: w[i] for n, w in ab.items()}, bsz, seq, tm)
        else:
            h = _layer_cd(h, g, {n: w[i] for n, w in cd.items()}, bsz, seq, tm)
        km, vm = _mem_kv(mem2, _row(g[6]), xa_wk[layer].astype(BF16), xa_wv[layer].astype(BF16), mem_len)
        h = _xattn(h, km, vm, xa_wq[layer].astype(BF16), xa_wo[layer].astype(BF16), _row(g[2]), _row(g[3]),
                   bsz, seq, mem_len, tm)
        h = _mlp(h, _row(g[4]), _row(g[5]), mlp_w1[layer].astype(BF16), mlp_w2[layer].astype(BF16),
                 tm=min(1024, bsz * seq), tf=1024)
    return h.reshape(bsz, seq, d).astype(x.dtype)
```

```python
import functools
import math

import jax
import jax.numpy as jnp
from jax import lax
from jax.experimental import pallas as pl
from jax.experimental.pallas import tpu as pltpu

F32 = jnp.float32
BF16 = jnp.bfloat16
HIGHEST = lax.Precision.HIGHEST

NORM_EPS = 1e-6
GLA_HEADS = 4
GLA_TAU = 16.0
GLA_CHUNK = 64
S5_GROUP = 16
S5_STATE = 64
S5_CHUNK = 16
RWKV_HEAD = 64
RWKV_CHUNK = 64
RWKV_GN_EPS = 64e-5
LRU_CONV = 4
LRU_C = 8.0
XA_HEADS = 4

V7X_SUBLANES = 8
CD_IN_OUT_DTYPES = (BF16, F32, BF16, BF16, BF16, BF16, BF16, F32, F32, F32)
VMEM_LIMIT_BYTES = 48 * 1024 * 1024


def _cparams(*semantics):
    return pltpu.CompilerParams(dimension_semantics=semantics, vmem_limit_bytes=VMEM_LIMIT_BYTES)


def _rms(x, gain):
    return x * lax.rsqrt(jnp.mean(x * x, axis=-1, keepdims=True) + NORM_EPS) * gain


def _bdot(a, b):
    return jnp.dot(a.astype(BF16), b.astype(BF16), preferred_element_type=F32)


def _bdot_nt(a, b):
    return lax.dot_general(a.astype(BF16), b.astype(BF16), (((1,), (1,)), ((), ())),
                           preferred_element_type=F32)


def _bdot_tn(a, b):
    return lax.dot_general(a.astype(BF16), b.astype(BF16), (((0,), (0,)), ((), ())),
                           preferred_element_type=F32)


def _hdot(a, b):
    return jnp.dot(a, b, precision=HIGHEST, preferred_element_type=F32)


def _hdot_nt(a, b):
    return lax.dot_general(a, b, (((1,), (1,)), ((), ())), precision=HIGHEST,
                           preferred_element_type=F32)


def _hdot_tn(a, b):
    return lax.dot_general(a, b, (((0,), (0,)), ((), ())), precision=HIGHEST,
                           preferred_element_type=F32)


def _split_bf16(x):
    hi = x.astype(BF16)
    return hi, (x - hi.astype(F32)).astype(BF16)


def _split_dot(x, w_bf16):
    hi, lo = _split_bf16(x)
    return (jnp.dot(hi, w_bf16, preferred_element_type=F32)
            + jnp.dot(lo, w_bf16, preferred_element_type=F32))


def _split3_dot(w_bf16, x):
    hi = x.astype(BF16)
    r1 = x - hi.astype(F32)
    mid = r1.astype(BF16)
    lo = (r1 - mid.astype(F32)).astype(BF16)
    return (jnp.dot(w_bf16, hi, preferred_element_type=F32) + jnp.dot(w_bf16, mid, preferred_element_type=F32)
            + jnp.dot(w_bf16, lo, preferred_element_type=F32))


def _sigmoid(x):
    return 1.0 / (1.0 + jnp.exp(-x))


def _softplus(x):
    return jnp.maximum(x, 0.0) + jnp.log(1.0 + jnp.exp(-jnp.abs(x)))


def _gelu_tanh(x):
    c = math.sqrt(2.0 / math.pi)
    return x * (0.5 * (1.0 + jnp.tanh(c * (x + 0.044715 * (x * x * x)))))


def _tril_mask(n, strict=False):
    row = lax.broadcasted_iota(jnp.int32, (n, n), 0)
    col = lax.broadcasted_iota(jnp.int32, (n, n), 1)
    return (col < row) if strict else (col <= row)


def _shift_rows(x, shift, carry):
    rolled = pltpu.roll(x, shift, axis=0)
    crolled = pltpu.roll(carry, shift, axis=0)
    rowi = lax.broadcasted_iota(jnp.int32, carry.shape, 0)
    first = jnp.where(rowi < shift, crolled, rolled[0:V7X_SUBLANES])
    return jnp.concatenate([first, rolled[V7X_SUBLANES:]], axis=0)


def _ab_in_kernel(h_ref, g_ref, w_ref, wd_ref, wd2_ref, bd_ref, perm_ref, out_ref, loga_ref, up_ref):
    hn = _rms(h_ref[...], g_ref[...]).astype(BF16)
    out = jnp.dot(hn, w_ref[...], preferred_element_type=F32)
    out_ref[...] = out.astype(BF16)
    dlr = jnp.dot(hn, wd_ref[...], preferred_element_type=F32)
    z = _hdot(dlr, wd2_ref[...]) + bd_ref[...]
    loga_ref[...] = -_softplus(-z) * (1.0 / GLA_TAU)
    mw = up_ref.shape[1]
    u16 = out[:, out.shape[1] - mw:].astype(BF16)
    up_ref[...] = jnp.dot(perm_ref[...], u16, preferred_element_type=F32).astype(BF16)


def _ab_in(h, gain, w_main, w_dlr, w_decay2, b_decay, perm, tm, mw):
    t, d = h.shape
    n_main = w_main.shape[1]
    n_dk = w_decay2.shape[1]
    full = lambda a: pl.BlockSpec(a.shape, lambda i: (0, 0))
    rows = lambda w: pl.BlockSpec((tm, w), lambda i: (i, 0))
    return pl.pallas_call(
        _ab_in_kernel,
        grid=(t // tm,),
        in_specs=[rows(d), full(gain), full(w_main), full(w_dlr), full(w_decay2), full(b_decay), full(perm)],
        out_specs=[rows(n_main), rows(n_dk), rows(mw)],
        out_shape=[jax.ShapeDtypeStruct((t, n_main), BF16), jax.ShapeDtypeStruct((t, n_dk), F32),
                   jax.ShapeDtypeStruct((t, mw), BF16)],
        compiler_params=_cparams("parallel"),
        name="ab_in",
    )(h, gain, w_main, w_dlr, w_decay2, b_decay, perm)


def _chunk_step_perm(tm, tc):
    dst = jnp.arange(tm)
    src = (dst % (tm // tc)) * tc + dst // (tm // tc)
    return (src[:, None] == jnp.arange(tm)[None, :]).astype(BF16)


def _gla_kernel(q_ref, k_ref, v_ref, gate_ref, la_ref, gain_ref, o_ref, state_ref, *, n_chunks, dk, dv):
    c = GLA_CHUNK

    @pl.when(pl.program_id(1) == 0)
    def _():
        state_ref[...] = jnp.zeros_like(state_ref)

    tb = n_chunks * c
    tril = _tril_mask(c)
    scale = dk ** -0.5
    ri = lax.broadcasted_iota(jnp.int32, (tb, tb), 0)
    ci = lax.broadcasted_iota(jnp.int32, (tb, tb), 1)
    blk_tril = jnp.where(((ri // c) == (ci // c)) & (ci <= ri), 1.0, 0.0).astype(BF16)
    b_all = _split3_dot(blk_tril, la_ref[...])
    lane_head = lax.broadcasted_iota(jnp.int32, (c, GLA_HEADS * dk), 1) // dk
    nt_dims = (((1,), (1,)), ((), ()))
    tn_dims = (((0,), (0,)), ((), ()))

    chunks = []
    for i in range(n_chunks):
        rs = slice(i * c, (i + 1) * c)
        b = b_all[rs, :]
        b_last = b[c - 1:c, :]
        k = k_ref[rs, :].astype(F32)
        chunks.append(dict(rs=rs, q_in=(q_ref[rs, :].astype(F32) * jnp.exp(b) * scale).astype(BF16),
                           k_in=(k * jnp.exp(-b)).astype(BF16), k_st=k * jnp.exp(b_last - b),
                           dec=jnp.exp(b_last)))
    for ch in chunks:
        ch['v'] = [v_ref[ch['rs'], h * dv:(h + 1) * dv].astype(BF16) for h in range(GLA_HEADS)]
        zero16 = jnp.zeros((), BF16)
        ch['scores'] = [
            jnp.where(tril, lax.dot_general(jnp.where(lane_head == h, ch['q_in'], zero16), ch['k_in'], nt_dims,
                                            preferred_element_type=F32), 0.0).astype(BF16)
            for h in range(GLA_HEADS)]
    for ch in chunks:
        ch['o'] = [jnp.dot(ch['scores'][h], ch['v'][h], preferred_element_type=F32) for h in range(GLA_HEADS)]
        ch['d_state'] = [
            lax.dot_general(ch['v'][h], jnp.where(lane_head == h, ch['k_st'], 0.0).astype(BF16), tn_dims,
                            preferred_element_type=F32) for h in range(GLA_HEADS)]

    st = [state_ref[h] for h in range(GLA_HEADS)]
    for ch in chunks:
        ch['st'] = [s.astype(BF16) for s in st]
        st = [st[h] * ch['dec'] + ch['d_state'][h] for h in range(GLA_HEADS)]
    for h in range(GLA_HEADS):
        state_ref[h] = st[h]
    for ch in chunks:
        for h in range(GLA_HEADS):
            vs = slice(h * dv, (h + 1) * dv)
            o = ch['o'][h] + lax.dot_general(ch['q_in'], ch['st'][h], nt_dims, preferred_element_type=F32)
            on = o * lax.rsqrt(jnp.mean(o * o, axis=-1, keepdims=True) + NORM_EPS) * gain_ref[:, vs]
            g = gate_ref[ch['rs'], vs].astype(F32)
            o_ref[ch['rs'], vs] = (on * (g * _sigmoid(g))).astype(o_ref.dtype)


def _gla(qkvgu, loga, gain, bsz, seq, tb):
    t = bsz * seq
    dk = loga.shape[1] // GLA_HEADS
    dv = gain.shape[1] // GLA_HEADS
    nq, nv = GLA_HEADS * dk, GLA_HEADS * dv
    nt = seq // tb
    row = lambda b, i: b * nt + i
    kern = functools.partial(_gla_kernel, n_chunks=tb // GLA_CHUNK, dk=dk, dv=dv)
    return pl.pallas_call(
        kern,
        grid=(bsz, nt),
        in_specs=[pl.BlockSpec((tb, nq), lambda b, i: (row(b, i), 0)),
                  pl.BlockSpec((tb, nq), lambda b, i: (row(b, i), 1)),
                  pl.BlockSpec((tb, nv), lambda b, i: (row(b, i), 1)),
                  pl.BlockSpec((tb, nv), lambda b, i: (row(b, i), 2)),
                  pl.BlockSpec((tb, nq), lambda b, i: (row(b, i), 0)),
                  pl.BlockSpec(gain.shape, lambda b, i: (0, 0))],
        out_specs=pl.BlockSpec((tb, nv), lambda b, i: (row(b, i), 0)),
        out_shape=jax.ShapeDtypeStruct((t, nv), BF16),
        scratch_shapes=[pltpu.VMEM((GLA_HEADS, dv, nq), F32)],
        compiler_params=_cparams("parallel", "arbitrary"),
        name="gla",
    )(qkvgu, qkvgu, qkvgu, qkvgu, loga, gain)


def _s5_kernel(u_ref, tz_ref, ws_ref, wc_ref, a_ref, y_ref, u_scr, yg_scr, s_scr, h_scr, *,
               n_pairs, bsz, seq, tm):
    n2 = 2 * S5_STATE
    tc, ch = S5_CHUNK, S5_GROUP
    lanes = u_ref.shape[1]
    per_half = lanes // ch
    kt = tm // tc
    n_tiles = seq // tm
    g_lo = pl.program_id(1)
    lane = lax.broadcasted_iota(jnp.int32, (kt, lanes), 1)
    step_m = [(lane >= ch * j8) & (lane < ch * (j8 + 1)) for j8 in range(per_half)]
    sh_in = [lax.rem(ch * (j8 - g_lo) + lanes, jnp.int32(lanes)) for j8 in range(per_half)]
    sh_out = [lax.rem(ch * (g_lo - j8) + lanes, jnp.int32(lanes)) for j8 in range(per_half)]
    own = (lane >= ch * g_lo) & (lane < ch * (g_lo + 1))

    def gather_tile(i, _):
        for b in range(bsz):
            for half in range(tc // per_half):
                acc = jnp.zeros((kt, lanes), F32)
                for j8 in range(per_half):
                    r0 = pl.multiple_of(b * seq + i * tm + (half * per_half + j8) * kt, kt)
                    x = u_ref[pl.ds(r0, kt), :].astype(F32)
                    acc = jnp.where(step_m[j8], pltpu.roll(x, sh_in[j8], axis=1), acc)
                u_scr[half, i, pl.ds(b, kt, stride=bsz), :] = acc
        return 0

    lax.fori_loop(0, n_tiles, gather_tile, 0)
    rows = n_tiles * kt * bsz
    n_half = tc // per_half
    u = jnp.concatenate([u_scr[hf].reshape(rows, lanes) for hf in range(n_half)], axis=1).astype(BF16)
    s_scr[...] = jnp.dot(u, ws_ref[0], preferred_element_type=F32)
    a1 = a_ref[0, 0:1, :]
    a2 = a_ref[0, 1:2, :]

    def body(j, carry):
        h, hs = carry
        r0 = pl.multiple_of(j * (2 * bsz), 2 * bsz)
        blk = s_scr[pl.ds(r0, 2 * bsz), :]
        h1 = a1 * h + a2 * hs + blk[0:bsz, 0:n2]
        hs1 = a1 * hs - a2 * h + blk[0:bsz, n2:2 * n2]
        h2 = a1 * h1 + a2 * hs1 + blk[bsz:2 * bsz, 0:n2]
        hs2 = a1 * hs1 - a2 * h1 + blk[bsz:2 * bsz, n2:2 * n2]
        h_scr[pl.ds(r0, 2 * bsz), :] = jnp.concatenate([h, h1], axis=0)
        return h2, hs2

    zero = jnp.zeros((bsz, n2), F32)
    lax.fori_loop(0, n_pairs, body, (zero, zero))
    yg = (jnp.dot(u, tz_ref[0], preferred_element_type=F32)
          + jnp.dot(h_scr[...].astype(BF16), wc_ref[0], preferred_element_type=F32))
    for hf in range(n_half):
        yg_scr[hf] = yg[:, hf * lanes:(hf + 1) * lanes].reshape(n_tiles, kt * bsz, lanes)

    @pl.when(g_lo == 0)
    def _():
        y_ref[...] = jnp.zeros_like(y_ref)

    def scatter_tile(i, _):
        for b in range(bsz):
            for half in range(tc // per_half):
                yb = yg_scr[half, i, pl.ds(b, kt, stride=bsz), :]
                for j8 in range(per_half):
                    r0 = pl.multiple_of(b * seq + i * tm + (half * per_half + j8) * kt, kt)
                    y_ref[pl.ds(r0, kt), :] = jnp.where(own, pltpu.roll(yb, sh_out[j8], axis=1),
                                                        y_ref[pl.ds(r0, kt), :])
        return 0

    lax.fori_loop(0, n_tiles, scatter_tile, 0)


def _s5(u_perm, tz, ws, wc, a_pow, bsz, seq, tm):
    t, mw = u_perm.shape
    lanes = 128
    groups = tz.shape[0]
    g_per_blk = lanes // S5_GROUP
    assert 2 * bsz == V7X_SUBLANES, "the chunk scan walks two chunks per 8-row tile"
    rows = (seq // S5_CHUNK) * bsz
    n_pairs = rows // (2 * bsz)
    per_g = lambda a: pl.BlockSpec((1,) + a.shape[1:], lambda q, g: (q * g_per_blk + g, 0, 0))
    blk = pl.BlockSpec((t, lanes), lambda q, g: (0, q))
    kern = functools.partial(_s5_kernel, n_pairs=n_pairs, bsz=bsz, seq=seq, tm=tm)
    kt = tm // S5_CHUNK
    return pl.pallas_call(
        kern,
        grid=(mw // lanes, g_per_blk),
        in_specs=[blk, per_g(tz), per_g(ws), per_g(wc), per_g(a_pow)],
        out_specs=blk,
        out_shape=jax.ShapeDtypeStruct((t, mw), F32),
        scratch_shapes=[pltpu.VMEM((tz.shape[1] // lanes, seq // tm, kt * bsz, lanes), F32),
                        pltpu.VMEM((tz.shape[2] // lanes, seq // tm, kt * bsz, lanes), F32),
                        pltpu.VMEM((rows, ws.shape[2]), F32), pltpu.VMEM((rows, wc.shape[1]), F32)],
        compiler_params=_cparams("parallel", "arbitrary"),
        name="s5",
    )(u_perm, tz, ws, wc, a_pow)


def _s5_weights(lam_re, lam_im, log_step, b_re, b_im, c_re, c_im):
    tc = S5_CHUNK
    groups, n = lam_re.shape
    lr = jnp.minimum(lam_re.astype(F32), -1e-4)
    li = lam_im.astype(F32)
    delta = jnp.exp(log_step.astype(F32))[:, None]
    tau = jnp.arange(tc + 1, dtype=F32)[:, None, None]
    mag = jnp.exp(tau * (lr * delta))
    ang = tau * (li * delta)
    pw_re, pw_im = mag * jnp.cos(ang), mag * jnp.sin(ang)
    num_re, num_im = pw_re[1] - 1.0, pw_im[1]
    den = lr * lr + li * li
    f_re = (num_re * lr + num_im * li) / den
    f_im = (num_im * lr - num_re * li) / den
    b_re, b_im = b_re.astype(F32), b_im.astype(F32)
    bb_re = f_re[..., None] * b_re - f_im[..., None] * b_im
    bb_im = f_re[..., None] * b_im + f_im[..., None] * b_re
    c_re, c_im = c_re.astype(F32), c_im.astype(F32)
    cp_re = c_re[None] * pw_re[:, :, None, :] - c_im[None] * pw_im[:, :, None, :]
    cp_im = c_re[None] * pw_im[:, :, None, :] + c_im[None] * pw_re[:, :, None, :]
    ein = functools.partial(jnp.einsum, precision=HIGHEST)
    kern = ein('tgcn,gnd->tgcd', cp_re[:tc], bb_re) - ein('tgcn,gnd->tgcd', cp_im[:tc], bb_im)
    j = jnp.arange(tc)[:, None]
    i = jnp.arange(tc)[None, :]
    diff = i - j
    ksel = jnp.where((diff >= 0)[:, :, None, None, None], kern[jnp.clip(diff, 0, tc - 1)], 0.0)
    tz = jnp.transpose(ksel, (2, 0, 4, 1, 3)).reshape(groups, tc * S5_GROUP, tc * S5_GROUP)
    rev_re, rev_im = pw_re[tc - 1::-1][:tc], pw_im[tc - 1::-1][:tc]
    s_re = rev_re[:, :, :, None] * bb_re[None] - rev_im[:, :, :, None] * bb_im[None]
    s_im = rev_re[:, :, :, None] * bb_im[None] + rev_im[:, :, :, None] * bb_re[None]
    s_re = jnp.transpose(s_re, (1, 0, 3, 2)).reshape(groups, tc * S5_GROUP, n)
    s_im = jnp.transpose(s_im, (1, 0, 3, 2)).reshape(groups, tc * S5_GROUP, n)
    ws = jnp.concatenate([s_re, s_im, s_im, s_re], axis=-1)
    o_re = jnp.transpose(cp_re[1:], (1, 3, 0, 2)).reshape(groups, n, tc * S5_GROUP)
    o_im = jnp.transpose(cp_im[1:], (1, 3, 0, 2)).reshape(groups, n, tc * S5_GROUP)
    wc = jnp.concatenate([o_re, -o_im], axis=1)
    a_pow = jnp.stack([jnp.concatenate([pw_re[tc], pw_re[tc]], axis=-1),
                       jnp.concatenate([-pw_im[tc], pw_im[tc]], axis=-1)], axis=1)
    return tz, ws, wc, a_pow


def _ab_out_kernel(h_ref, oa_ref, ys_ref, u_ref, d_ref, wglu_ref, bglu_ref, wo_ref, g_ref, perm_ref, out_ref):
    mw = oa_ref.shape[1]
    hi, lo = _split_bf16(ys_ref[...])
    y_ssm = (jnp.dot(perm_ref[...], hi, preferred_element_type=F32)
             + jnp.dot(perm_ref[...], lo, preferred_element_type=F32))
    y = y_ssm + d_ref[...] * u_ref[...].astype(F32)
    ob = _gelu_tanh(y) * _sigmoid(_bdot(y, wglu_ref[...]) + bglu_ref[...])
    mix = _bdot(oa_ref[...], wo_ref[0:mw, :]) + _bdot(ob, wo_ref[mw:2 * mw, :])
    out_ref[...] = h_ref[...] + _rms(mix, g_ref[...])


def _ab_out(h, o_a, y_ssm, qkvgu, d_skip, w_glu, b_glu, w_out, gain, perm_t, tm):
    t, d = h.shape
    mw = o_a.shape[1]
    full = lambda a: pl.BlockSpec(a.shape, lambda i: (0, 0))
    rows = lambda w, cb=0: pl.BlockSpec((tm, w), lambda i: (i, cb))
    return pl.pallas_call(
        _ab_out_kernel,
        grid=(t // tm,),
        in_specs=[rows(d), rows(mw), rows(mw), rows(mw, qkvgu.shape[1] // mw - 1), full(d_skip), full(w_glu),
                  full(b_glu), full(w_out), full(gain), full(perm_t)],
        out_specs=rows(d),
        out_shape=jax.ShapeDtypeStruct((t, d), F32),
        compiler_params=_cparams("parallel"),
        name="ab_out",
    )(h, o_a, y_ssm, qkvgu, d_skip, w_glu, b_glu, w_out, gain, perm_t)


def _mem_kv_kernel(mem_ref, g_ref, wk_ref, wv_ref, k_ref, v_ref):
    mn = _rms(mem_ref[...], g_ref[...]).astype(BF16)
    k_ref[...] = jnp.dot(mn, wk_ref[...], preferred_element_type=F32).astype(BF16)
    v_ref[...] = jnp.dot(mn, wv_ref[...], preferred_element_type=F32).astype(BF16)


def _mem_kv(mem, gain, wk, wv, tm):
    t, d = mem.shape
    full = lambda a: pl.BlockSpec(a.shape, lambda i: (0, 0))
    rows = pl.BlockSpec((tm, d), lambda i: (i, 0))
    return pl.pallas_call(
        _mem_kv_kernel,
        grid=(t // tm,),
        in_specs=[rows, full(gain), full(wk), full(wv)],
        out_specs=[rows, rows],
        out_shape=[jax.ShapeDtypeStruct((t, d), BF16), jax.ShapeDtypeStruct((t, d), BF16)],
        compiler_params=_cparams("parallel"),
        name="mem_kv",
    )(mem, gain, wk, wv)


def _xattn_kernel(h_ref, k_ref, v_ref, wq_ref, wo_ref, gq_ref, go_ref, out_ref, o_scr):
    h = h_ref[...]
    d = h.shape[1]
    hd = d // XA_HEADS
    q = jnp.dot(_rms(h, gq_ref[...]).astype(BF16), wq_ref[...], preferred_element_type=F32)
    for hh in range(XA_HEADS):
        cs = slice(hh * hd, (hh + 1) * hd)
        s = _bdot_nt(q[:, cs], k_ref[:, cs]) * (hd ** -0.5)
        e = jnp.exp(s - jnp.max(s, axis=-1, keepdims=True))
        p = e / jnp.sum(e, axis=-1, keepdims=True)
        o_scr[:, cs] = _bdot(p, v_ref[:, cs]).astype(BF16)
    xa = jnp.dot(o_scr[...], wo_ref[...], preferred_element_type=F32)
    out_ref[...] = h + _rms(xa, go_ref[...])


def _xattn(h, k, v, wq, wo, gq, go, bsz, seq, mem_len, tm):
    t, d = h.shape
    nt = seq // tm
    full = lambda a: pl.BlockSpec(a.shape, lambda b, i: (0, 0))
    rows = pl.BlockSpec((tm, d), lambda b, i: (b * nt + i, 0))
    kv = pl.BlockSpec((mem_len, d), lambda b, i: (b, 0))
    return pl.pallas_call(
        _xattn_kernel,
        grid=(bsz, nt),
        in_specs=[rows, kv, kv, full(wq), full(wo), full(gq), full(go)],
        out_specs=rows,
        out_shape=jax.ShapeDtypeStruct((t, d), F32),
        scratch_shapes=[pltpu.VMEM((tm, d), BF16)],
        compiler_params=_cparams("parallel", "parallel"),
        name="xattn",
    )(h, k, v, wq, wo, gq, go)


def _mlp_kernel(h_ref, gi_ref, go_ref, w1_ref, w2_ref, out_ref, xn_scr, acc_scr):
    j = pl.program_id(1)

    @pl.when(j == 0)
    def _():
        xn_scr[...] = _rms(h_ref[...], gi_ref[...]).astype(BF16)
        acc_scr[...] = jnp.zeros_like(acc_scr)

    a = jnp.maximum(jnp.dot(xn_scr[...], w1_ref[...], preferred_element_type=F32), 0.0)
    acc_scr[...] += jnp.dot((a * a).astype(BF16), w2_ref[...], preferred_element_type=F32)

    @pl.when(j == pl.num_programs(1) - 1)
    def _():
        out_ref[...] = h_ref[...] + _rms(acc_scr[...], go_ref[...])


def _mlp(h, gi, go, w1, w2, tm, tf):
    t, d = h.shape
    dff = w1.shape[1]
    full = lambda a: pl.BlockSpec(a.shape, lambda i, j: (0, 0))
    rows = pl.BlockSpec((tm, d), lambda i, j: (i, 0))
    return pl.pallas_call(
        _mlp_kernel,
        grid=(t // tm, dff // tf),
        in_specs=[rows, full(gi), full(go), pl.BlockSpec((d, tf), lambda i, j: (0, j)),
                  pl.BlockSpec((tf, d), lambda i, j: (j, 0))],
        out_specs=rows,
        out_shape=jax.ShapeDtypeStruct((t, d), F32),
        scratch_shapes=[pltpu.VMEM((tm, d), BF16), pltpu.VMEM((tm, d), F32)],
        compiler_params=_cparams("parallel", "arbitrary"),
        name="mlp",
    )(h, gi, go, w1, w2)


def _cd_in_kernel(h_ref, g_ref, wbig_ref, wsm_ref, mub_ref, mus_ref, w0_ref, w2_ref, a0_ref, a2_ref, g2_ref,
                  kk_ref, ka_ref, hsum_ref, cw_ref, cb_ref, wa_ref, ba_ref, wx_ref, bx_ref, lam_ref,
                  r_o, lw_o, k_o, v_o, kk_o, a_o, g_o, la_o, lb_o, gate_o,
                  carry_b, carry_s, carry_x, *, mw, n_sub):
    tm = h_ref.shape[0]

    @pl.when(pl.program_id(1) == 0)
    def _():
        carry_b[...] = jnp.zeros_like(carry_b)
        carry_s[...] = jnp.zeros_like(carry_s)
        carry_x[...] = jnp.zeros_like(carry_x)

    ts = tm // n_sub
    proj = []
    for sb in range(n_sub):
        hn = _rms(h_ref[sb * ts:(sb + 1) * ts, :], g_ref[...]).astype(BF16)
        proj.append((jnp.dot(hn, wbig_ref[...], preferred_element_type=F32),
                     jnp.dot(hn, wsm_ref[...], preferred_element_type=F32)))

    for sb in range(n_sub):
        rs = slice(sb * ts, (sb + 1) * ts)
        pb, ps = proj[sb]
        p3 = pb[:, 0:3 * mw]
        xb = pb[:, 3 * mw:4 * mw]
        gate_o[rs, :] = pb[:, 4 * mw:5 * mw]

        prev3 = _shift_rows(p3, 1, carry_b[...])
        prevs = _shift_rows(ps, 1, carry_s[...])
        carry_b[...] = p3[ts - V7X_SUBLANES:ts, :]
        carry_s[...] = ps[ts - V7X_SUBLANES:ts, :]
        p3 = p3 + (prev3 - p3) * mub_ref[...]
        ps = ps + (prevs - ps) * mus_ref[...]
        r = p3[:, 0:mw]
        k = p3[:, mw:2 * mw]
        v = p3[:, 2 * mw:3 * mw]
        lora = ps[:, 0:128]
        wlog = -_softplus(-(w0_ref[...] + _bdot(jnp.tanh(lora), w2_ref[...]))) - 0.5
        a = _sigmoid(a0_ref[...] + _bdot(lora, a2_ref[...]))
        kkr = k * kk_ref[...]
        norm = jnp.sqrt(_bdot(kkr * kkr, hsum_ref[...]))
        r_o[rs, :] = r.astype(BF16)
        lw_o[rs, :] = -jnp.exp(wlog)
        k_o[rs, :] = (k * (1.0 + (a - 1.0) * ka_ref[...])).astype(BF16)
        v_o[rs, :] = v.astype(BF16)
        kk_o[rs, :] = (kkr / jnp.maximum(norm, 1e-12)).astype(BF16)
        a_o[rs, :] = a.astype(BF16)
        g_o[rs, :] = _bdot(_sigmoid(ps[:, 128:256]), g2_ref[...]).astype(BF16)

        cx = carry_x[...]
        xc = cb_ref[...] + cw_ref[LRU_CONV - 1:LRU_CONV, :] * xb
        for sh in range(1, LRU_CONV):
            xc = xc + cw_ref[LRU_CONV - 1 - sh:LRU_CONV - sh, :] * _shift_rows(xb, sh, cx)
        carry_x[...] = xb[ts - V7X_SUBLANES:ts, :]
        rg = _sigmoid(_bdot(xc, wa_ref[...]) + ba_ref[...])
        ig = _sigmoid(_bdot(xc, wx_ref[...]) + bx_ref[...])
        log_a = -LRU_C * rg * _softplus(-lam_ref[...])
        la_o[rs, :] = jnp.exp(log_a)
        th = jnp.tanh(log_a)
        lb_o[rs, :] = jnp.sqrt(-2.0 * th / (1.0 - th)) * (ig * xc)


def _cd_in(h, gain, wbig, wsm, vecs, mats, bsz, seq, tm, mw):
    t, d = h.shape
    nt = seq // tm
    full = lambda a: pl.BlockSpec(a.shape, lambda b, i: (0, 0))
    rows = lambda w: pl.BlockSpec((tm, w), lambda b, i: (b * nt + i, 0))
    (mub, mus, w0, a0, kk_w, ka_w, cw, cb, ba, bx, lam) = vecs
    (w2p, a2p, g2, hsum, wa, wx) = mats
    args = (h, gain, wbig, wsm, mub, mus, w0, w2p, a0, a2p, g2, kk_w, ka_w, hsum, cw, cb, wa, ba, wx, bx, lam)
    kern = functools.partial(_cd_in_kernel, mw=mw, n_sub=4)
    return pl.pallas_call(
        kern,
        grid=(bsz, nt),
        in_specs=[rows(d)] + [full(a) for a in args[1:]],
        out_specs=[rows(mw)] * 10,
        out_shape=[jax.ShapeDtypeStruct((t, mw), dt) for dt in CD_IN_OUT_DTYPES],
        scratch_shapes=[pltpu.VMEM((V7X_SUBLANES, 3 * mw), F32), pltpu.VMEM((V7X_SUBLANES, wsm.shape[1]), F32),
                        pltpu.VMEM((V7X_SUBLANES, mw), F32)],
        compiler_params=_cparams("parallel", "arbitrary"),
        name="cd_in",
    )(*args)


def _rwkv_kernel(r_ref, lw_ref, k_ref, v_ref, kk_ref, a_ref, g_ref, rk_ref, lng_ref, lnb_ref, hsum_ref,
                 o_ref, state_ref, y_scr, *, slab_heads, n_chunks):
    c = RWKV_CHUNK
    hd = RWKV_HEAD
    assert c == hd, "one block mask serves both the (head, s) and the (head, d) layouts"
    sw = slab_heads * hd
    n_slabs = r_ref.shape[1] // sw

    @pl.when(pl.program_id(1) == 0)
    def _():
        state_ref[...] = jnp.zeros_like(state_ref)

    tb = n_chunks * c
    tri = lax.broadcasted_iota(jnp.int32, (tb, tb), 0)
    tci = lax.broadcasted_iota(jnp.int32, (tb, tb), 1)
    blk_tril = jnp.where(((tri // c) == (tci // c)) & (tci <= tri), 1.0, 0.0).astype(BF16)
    cum_all = _split3_dot(blk_tril, lw_ref[...])
    blk_m = (lax.broadcasted_iota(jnp.int32, (sw, sw), 0) // c) == (lax.broadcasted_iota(jnp.int32, (sw, sw), 1) // c)
    wide_t = lax.broadcasted_iota(jnp.int32, (c, sw), 0)
    wide_s = lax.broadcasted_iota(jnp.int32, (c, sw), 1) % c
    strict_w = wide_s < wide_t
    incl_w = wide_s <= wide_t
    eye_w = jnp.where(wide_s == wide_t, 1.0, 0.0)

    def bdiag(x):
        x16 = x.astype(BF16)
        return jnp.where(blk_m, jnp.concatenate([x16] * slab_heads, axis=0), jnp.zeros((), BF16))

    def mm(a, w16):
        return jnp.dot(a.astype(BF16), w16, preferred_element_type=F32)

    nt_dims = (((1,), (1,)), ((), ()))
    chains = []
    for chunk in range(n_chunks):
        rs = slice(chunk * c, (chunk + 1) * c)
        lw = lw_ref[rs, :]
        cum = cum_all[rs, :]
        cum_last = cum[c - 1:c, :]
        r = r_ref[rs, :].astype(F32)
        k = k_ref[rs, :].astype(F32)
        v = v_ref[rs, :].astype(F32)
        kk = kk_ref[rs, :].astype(F32)
        bvec = kk * a_ref[rs, :].astype(F32)
        inv_g = jnp.exp(-cum)
        to_end = jnp.exp(cum_last - cum)
        rt = r * jnp.exp(cum)
        kp = kk * jnp.exp(cum - lw)
        be = bvec * inv_g
        kh = k * inv_g
        bb = bvec * to_end
        kb = k * to_end
        g_end = jnp.exp(cum_last)
        for s in range(n_slabs):
            ls = slice(s * sw, (s + 1) * sw)
            chains.append(dict(rs=rs, ls=ls, slab=s, rt=rt[:, ls], kp=kp[:, ls], v=v[:, ls], be=be[:, ls],
                               kh=kh[:, ls], kb=kb[:, ls], bb=bb[:, ls], g_end=g_end[:, ls]))

    for ch in chains:
        lhs2 = jnp.concatenate([ch['kp'], ch['rt']], axis=0).astype(BF16)
        ab = lax.dot_general(lhs2, bdiag(ch['be']), nt_dims, preferred_element_type=F32)
        ak = lax.dot_general(lhs2, bdiag(ch['kh']), nt_dims, preferred_element_type=F32)
        ch['a_kb'] = jnp.where(strict_w, ab[0:c], 0.0)
        ch['a_rb'] = jnp.where(incl_w, ab[c:2 * c], 0.0).astype(BF16)
        ch['a_kr'] = jnp.concatenate([jnp.where(strict_w, ak[0:c], 0.0), jnp.where(incl_w, ak[c:2 * c], 0.0)],
                                     axis=0).astype(BF16)
    for ch in chains:
        x = -ch['a_kb']
        ch['t'] = eye_w + x
        ch['x'] = mm(x, bdiag(x))
    for _ in range(int(math.log2(c)) - 2):
        for ch in chains:
            res = mm(jnp.concatenate([ch['t'], ch['x']], axis=0), bdiag(ch['x']))
            ch['t'] = ch['t'] + res[0:c]
            ch['x'] = res[c:2 * c]
    for ch in chains:
        ch['t'] = ch['t'] + mm(ch['t'], bdiag(ch['x']))
    for ch in chains:
        l_hi, l_lo = _split_bf16(ch['a_kb'])
        t_hi, t_lo = _split_bf16(ch['t'])
        lt = mm(jnp.concatenate([l_hi, l_lo], axis=0), bdiag(t_hi))
        ch['resid'] = eye_w - ch['t'] - (lt[0:c] + lt[c:2 * c] + mm(l_hi, bdiag(t_lo)))
    for ch in chains:
        ch['t16'] = (ch['t'] + mm(ch['t'], bdiag(ch['resid']))).astype(BF16)
    for ch in chains:
        ch['w_tok'] = mm(ch['t16'], bdiag(ch['kp']))
        ch['av'] = mm(ch['a_kr'], bdiag(ch['v']))
    for ch in chains:
        ch['u_tok'] = mm(ch['t16'], bdiag(ch['av'][0:c]))
    for ch in chains:
        q_tok = ch['rt'] - mm(ch['a_rb'], bdiag(ch['w_tok']))
        ch['qw'] = jnp.concatenate([q_tok, ch['w_tok']], axis=0).astype(BF16)
        ch['y0'] = ch['av'][c:2 * c] - mm(ch['a_rb'], bdiag(ch['u_tok']))
        ch['kb2'] = jnp.concatenate([ch['kb'], ch['bb']], axis=0).astype(BF16)

    for ch in chains:
        s = ch['slab']
        p_t = state_ref[s]
        res = lax.dot_general(ch['qw'], p_t.astype(BF16), nt_dims, preferred_element_type=F32)
        y_scr[ch['rs'], ch['ls']] = res[0:c] + ch['y0']
        u_all = -(ch['u_tok'] + res[c:2 * c])
        upd = lax.dot_general(jnp.concatenate([ch['v'], u_all], axis=0).astype(BF16), ch['kb2'],
                              (((0,), (0,)), ((), ())), preferred_element_type=F32)
        state_ref[s] = p_t * ch['g_end'] + jnp.where(blk_m, upd, 0.0)

    hsum = hsum_ref[...]
    y = y_scr[...]
    r = r_ref[...].astype(F32)
    v = v_ref[...].astype(F32)
    mean = _split_dot(y, hsum) * (1.0 / hd)
    yc = y - mean
    var = _bdot(yc * yc, hsum) * (1.0 / hd)
    yn = yc * lax.rsqrt(var + RWKV_GN_EPS) * lng_ref[...] + lnb_ref[...]
    bonus = _bdot(r * k_ref[...].astype(F32) * rk_ref[...], hsum) * v
    o_ref[...] = ((yn + bonus) * g_ref[...].astype(F32)).astype(o_ref.dtype)


def _rwkv(r, lw, k, v, kk, a, g, rk, lng, lnb, hsum, bsz, seq, slab_heads=4, n_chunks=4):
    t, mw = r.shape
    c = RWKV_CHUNK * n_chunks
    nt = seq // c
    sw = slab_heads * RWKV_HEAD
    full = lambda x: pl.BlockSpec(x.shape, lambda b, i: (0, 0))
    rows = pl.BlockSpec((c, mw), lambda b, i: (b * nt + i, 0))
    kern = functools.partial(_rwkv_kernel, slab_heads=slab_heads, n_chunks=n_chunks)
    return pl.pallas_call(
        kern,
        grid=(bsz, nt),
        in_specs=[rows] * 7 + [full(rk), full(lng), full(lnb), full(hsum)],
        out_specs=rows,
        out_shape=jax.ShapeDtypeStruct((t, mw), BF16),
        scratch_shapes=[pltpu.VMEM((mw // sw, sw, sw), F32), pltpu.VMEM((c, mw), F32)],
        compiler_params=_cparams("parallel", "arbitrary"),
        name="rwkv",
    )(r, lw, k, v, kk, a, g, rk, lng, lnb, hsum)


def _lru_kernel(a_ref, b_ref, gate_ref, o_ref, carry_ref):
    tm = a_ref.shape[0]
    sub = V7X_SUBLANES

    @pl.when(pl.program_id(1) == 0)
    def _():
        carry_ref[...] = jnp.zeros_like(carry_ref)

    rowi = lax.broadcasted_iota(jnp.int32, (sub, a_ref.shape[1]), 0)

    def scan8(a, b, h):
        sh = 1
        while sh < sub:
            a_sh = jnp.where(rowi >= sh, pltpu.roll(a, sh, axis=0), 1.0)
            b_sh = jnp.where(rowi >= sh, pltpu.roll(b, sh, axis=0), 0.0)
            b = b + a * b_sh
            a = a * a_sh
            sh *= 2
        return b + a * h

    def body(j, h):
        r0 = pl.multiple_of(j * (2 * sub), 2 * sub)
        a = a_ref[pl.ds(r0, 2 * sub), :]
        b = b_ref[pl.ds(r0, 2 * sub), :]
        h1 = scan8(a[0:sub], b[0:sub], h)
        h2 = scan8(a[sub:2 * sub], b[sub:2 * sub], h1[sub - 1:sub, :])
        gate = gate_ref[pl.ds(r0, 2 * sub), :].astype(F32)
        o_ref[pl.ds(r0, 2 * sub), :] = (jnp.concatenate([h1, h2], axis=0) * _gelu_tanh(gate)).astype(o_ref.dtype)
        return h2[sub - 1:sub, :]

    carry_ref[0:1, :] = lax.fori_loop(0, tm // (2 * sub), body, carry_ref[0:1, :])


def _lru(a, b, gate, bsz, seq, tm):
    t, mw = a.shape
    nt = seq // tm
    rows = pl.BlockSpec((tm, mw), lambda bi, i: (bi * nt + i, 0))
    return pl.pallas_call(
        _lru_kernel,
        grid=(bsz, nt),
        in_specs=[rows, rows, rows],
        out_specs=rows,
        out_shape=jax.ShapeDtypeStruct((t, mw), BF16),
        scratch_shapes=[pltpu.VMEM((V7X_SUBLANES, mw), F32)],
        compiler_params=_cparams("parallel", "arbitrary"),
        name="lru",
    )(a, b, gate)


def _cd_out_kernel(h_ref, oc_ref, od_ref, wo_ref, g_ref, out_ref):
    mw = oc_ref.shape[1]
    mix = _bdot(oc_ref[...], wo_ref[0:mw, :]) + _bdot(od_ref[...], wo_ref[mw:2 * mw, :])
    out_ref[...] = h_ref[...] + _rms(mix, g_ref[...])


def _cd_out(h, o_c, o_d, w_out, gain, tm):
    t, d = h.shape
    mw = o_c.shape[1]
    full = lambda a: pl.BlockSpec(a.shape, lambda i: (0, 0))
    rows = lambda w: pl.BlockSpec((tm, w), lambda i: (i, 0))
    return pl.pallas_call(
        _cd_out_kernel,
        grid=(t // tm,),
        in_specs=[rows(d), rows(mw), rows(mw), full(w_out), full(gain)],
        out_specs=rows(d),
        out_shape=jax.ShapeDtypeStruct((t, d), F32),
        compiler_params=_cparams("parallel"),
        name="cd_out",
    )(h, o_c, o_d, w_out, gain)


def _row(vec):
    return vec.astype(F32).reshape(1, -1)


def _block_diag(blocks):
    nb, bi, bo = blocks.shape
    eye = jnp.eye(nb, dtype=blocks.dtype)
    return (eye[:, None, :, None] * blocks[:, :, None, :]).reshape(nb * bi, nb * bo)


def _head_sum_matrix(width, head):
    idx = jnp.arange(width) // head
    return (idx[:, None] == idx[None, :]).astype(BF16)


def _layer_ab(h, gains, p, bsz, seq, tm):
    d = h.shape[1]
    mw = d // 2
    dk = p['gla_b_decay'].shape[0] // GLA_HEADS
    nq = GLA_HEADS * dk
    w_in = p['ab_w_in']
    rank = p['gla_w_decay2'].shape[0]
    o_dlr = 2 * nq + 2 * mw
    w_main = jnp.concatenate([w_in[:, :o_dlr], w_in[:, o_dlr + rank:]], axis=1).astype(BF16)
    w_dlr = jnp.pad(w_in[:, o_dlr:o_dlr + rank], ((0, 0), (0, 128 - rank))).astype(BF16)
    w_decay2 = jnp.pad(p['gla_w_decay2'].astype(F32), ((0, 128 - rank), (0, 0)))
    perm = _chunk_step_perm(tm, S5_CHUNK)
    qkvgu, loga, u_perm = _ab_in(h, _row(gains[0]), w_main, w_dlr, w_decay2, _row(p['gla_b_decay']), perm, tm, mw)
    o_a = _gla(qkvgu, loga, _row(p['gla_norm_gain']), bsz, seq, tb=256)
    tz, ws, wc, a_pow = _s5_weights(p['s5_lambda_re'], p['s5_lambda_im'], p['s5_log_step'], p['s5_b_re'],
                                    p['s5_b_im'], p['s5_c_re'], p['s5_c_im'])
    y_perm = _s5(u_perm, tz.astype(BF16), ws.astype(BF16), wc.astype(BF16), a_pow, bsz, seq, tm)
    return _ab_out(h, o_a, y_perm, qkvgu, _row(p['s5_d']), p['s5_w_glu'].astype(BF16), _row(p['s5_b_glu']),
                   p['ab_w_out'].astype(BF16), _row(gains[1]), perm.T, tm)


def _layer_cd(h, gains, p, bsz, seq, tm):
    d = h.shape[1]
    mw = d // 2
    w_in = p['cd_w_in']
    dr = p['rwkv_w2'].shape[0]
    ar = p['rwkv_a2'].shape[0]
    gr = p['rwkv_g2'].shape[0]
    assert dr + ar == 128 and gr == 128
    o = [0, mw, mw + dr, 2 * mw + dr, 3 * mw + dr, 3 * mw + dr + ar, 3 * mw + dr + ar + gr]
    col = lambda i, width: w_in[:, o[i]:o[i] + width]
    wbig = jnp.concatenate([col(0, mw), col(2, mw), col(3, mw), w_in[:, o[6]:]], axis=1).astype(BF16)
    wsm = jnp.concatenate([col(1, dr), col(4, ar), col(5, gr)], axis=1).astype(BF16)
    mu = p['rwkv_mu'].astype(F32)
    mseg = lambda i, width: mu[o[i]:o[i] + width]
    mub = jnp.concatenate([mseg(0, mw), mseg(2, mw), mseg(3, mw)]).reshape(1, -1)
    mus = jnp.concatenate([mseg(1, dr), mseg(4, ar), mseg(5, gr)]).reshape(1, -1)
    w2p = jnp.pad(p['rwkv_w2'], ((0, ar), (0, 0))).astype(BF16)
    a2p = jnp.pad(p['rwkv_a2'], ((dr, 0), (0, 0))).astype(BF16)
    hsum = _head_sum_matrix(mw, RWKV_HEAD)
    vecs = (mub, mus, _row(p['rwkv_w0']), _row(p['rwkv_a0']), _row(p['rwkv_k_k']), _row(p['rwkv_k_a']),
            p['lru_conv_w'].astype(F32), _row(p['lru_conv_b']), _row(p['lru_b_a']), _row(p['lru_b_x']),
            _row(p['lru_lambda']))
    mats = (w2p, a2p, p['rwkv_g2'].astype(BF16), hsum, _block_diag(p['lru_w_a']).astype(BF16),
            _block_diag(p['lru_w_x']).astype(BF16))
    r, lw, k, v, kk, a, g, la, lb, gate = _cd_in(h, _row(gains[0]), wbig, wsm, vecs, mats, bsz, seq, tm, mw)
    o_c = _rwkv(r, lw, k, v, kk, a, g, _row(p['rwkv_r_k']), _row(p['rwkv_ln_gain']), _row(p['rwkv_ln_bias']),
                hsum, bsz, seq)
    o_d = _lru(la, lb, gate, bsz, seq, tm)
    return _cd_out(h, o_c, o_d, p['cd_w_out'].astype(BF16), _row(gains[1]), tm)


def kernel(x, mem, norm_gain, xa_wq, xa_wk, xa_wv, xa_wo, mlp_w1, mlp_w2, ab_w_in, gla_w_decay2, gla_b_decay, gla_norm_gain, s5_lambda_re, s5_lambda_im, s5_log_step, s5_b_re, s5_b_im, s5_c_re, s5_c_im, s5_d, s5_w_glu, s5_b_glu, ab_w_out, cd_w_in, rwkv_mu, rwkv_w0, rwkv_w2, rwkv_a0, rwkv_a2, rwkv_g2, rwkv_k_k, rwkv_k_a, rwkv_r_k, rwkv_ln_gain, rwkv_ln_bias, lru_conv_w, lru_conv_b, lru_w_a, lru_b_a, lru_w_x, lru_b_x, lru_lambda, cd_w_out):
    bsz, seq, d = x.shape
    mem_len = mem.shape[1]
    depth = norm_gain.shape[0]
    tm = min(512, seq)
    ab = dict(ab_w_in=ab_w_in, gla_w_decay2=gla_w_decay2, gla_b_decay=gla_b_decay, gla_norm_gain=gla_norm_gain,
              s5_lambda_re=s5_lambda_re, s5_lambda_im=s5_lambda_im, s5_log_step=s5_log_step, s5_b_re=s5_b_re,
              s5_b_im=s5_b_im, s5_c_re=s5_c_re, s5_c_im=s5_c_im, s5_d=s5_d, s5_w_glu=s5_w_glu, s5_b_glu=s5_b_glu,
              ab_w_out=ab_w_out)
    cd = dict(cd_w_in=cd_w_in, rwkv_mu=rwkv_mu, rwkv_w0=rwkv_w0, rwkv_w2=rwkv_w2, rwkv_a0=rwkv_a0,
              rwkv_a2=rwkv_a2, rwkv_g2=rwkv_g2, rwkv_k_k=rwkv_k_k, rwkv_k_a=rwkv_k_a, rwkv_r_k=rwkv_r_k,
              rwkv_ln_gain=rwkv_ln_gain, rwkv_ln_bias=rwkv_ln_bias, lru_conv_w=lru_conv_w, lru_conv_b=lru_conv_b,
              lru_w_a=lru_w_a, lru_b_a=lru_b_a, lru_w_x=lru_w_x, lru_b_x=lru_b_x, lru_lambda=lru_lambda,
              cd_w_out=cd_w_out)
    h = x.astype(F32).reshape(bsz * seq, d)
    mem2 = mem.astype(F32).reshape(bsz * mem_len, d)
    for layer in range(depth):
        g = norm_gain[layer]
        i = layer // 2
        if layer % 2 == 0:
            h = _layer_ab(h, g, {n: w[i] for n, w in ab.items()}, bsz, seq, tm)
        else:
            h = _layer_cd(h, g, {n: w[i] for n, w in cd.items()}, bsz, seq, tm)
        km, vm = _mem_kv(mem2, _row(g[6]), xa_wk[layer].astype(BF16), xa_wv[layer].astype(BF16), mem_len)
        h = _xattn(h, km, vm, xa_wq[layer].astype(BF16), xa_wo[layer].astype(BF16), _row(g[2]), _row(g[3]),
                   bsz, seq, mem_len, tm)
        h = _mlp(h, _row(g[4]), _row(g[5]), mlp_w1[layer].astype(BF16), mlp_w2[layer].astype(BF16),
                 tm=min(1024, bsz * seq), tf=1024)
    return h.reshape(bsz, seq, d).astype(x.dtype)
```

```python
import functools
import math

import jax
import jax.numpy as jnp
from jax import lax
from jax.experimental import pallas as pl
from jax.experimental.pallas import tpu as pltpu

F32 = jnp.float32
BF16 = jnp.bfloat16
HIGHEST = lax.Precision.HIGHEST

NORM_EPS = 1e-6
GLA_HEADS = 4
GLA_TAU = 16.0
GLA_CHUNK = 64
S5_GROUP = 16
S5_STATE = 64
S5_CHUNK = 16
RWKV_HEAD = 64
RWKV_CHUNK = 64
RWKV_GN_EPS = 64e-5
LRU_CONV = 4
LRU_C = 8.0
XA_HEADS = 4

V7X_SUBLANES = 8
CD_IN_OUT_DTYPES = (BF16, F32, BF16, BF16, BF16, BF16, BF16, F32, F32, F32)
VMEM_LIMIT_BYTES = 48 * 1024 * 1024


def _cparams(*semantics):
    return pltpu.CompilerParams(dimension_semantics=semantics, vmem_limit_bytes=VMEM_LIMIT_BYTES)


def _rms(x, gain):
    return x * lax.rsqrt(jnp.mean(x * x, axis=-1, keepdims=True) + NORM_EPS) * gain


def _bdot(a, b):
    return jnp.dot(a.astype(BF16), b.astype(BF16), preferred_element_type=F32)


def _bdot_nt(a, b):
    return lax.dot_general(a.astype(BF16), b.astype(BF16), (((1,), (1,)), ((), ())),
                           preferred_element_type=F32)


def _bdot_tn(a, b):
    return lax.dot_general(a.astype(BF16), b.astype(BF16), (((0,), (0,)), ((), ())),
                           preferred_element_type=F32)


def _hdot(a, b):
    return jnp.dot(a, b, precision=HIGHEST, preferred_element_type=F32)


def _hdot_nt(a, b):
    return lax.dot_general(a, b, (((1,), (1,)), ((), ())), precision=HIGHEST,
                           preferred_element_type=F32)


def _hdot_tn(a, b):
    return lax.dot_general(a, b, (((0,), (0,)), ((), ())), precision=HIGHEST,
                           preferred_element_type=F32)


def _split_bf16(x):
    hi = x.astype(BF16)
    return hi, (x - hi.astype(F32)).astype(BF16)


def _split_dot(x, w_bf16):
    hi, lo = _split_bf16(x)
    return (jnp.dot(hi, w_bf16, preferred_element_type=F32)
            + jnp.dot(lo, w_bf16, preferred_element_type=F32))


def _split3_dot(w_bf16, x):
    hi = x.astype(BF16)
    r1 = x - hi.astype(F32)
    mid = r1.astype(BF16)
    lo = (r1 - mid.astype(F32)).astype(BF16)
    return (jnp.dot(w_bf16, hi, preferred_element_type=F32) + jnp.dot(w_bf16, mid, preferred_element_type=F32)
            + jnp.dot(w_bf16, lo, preferred_element_type=F32))


def _sigmoid(x):
    return 1.0 / (1.0 + jnp.exp(-x))


def _softplus(x):
    return jnp.maximum(x, 0.0) + jnp.log(1.0 + jnp.exp(-jnp.abs(x)))


def _gelu_tanh(x):
    c = math.sqrt(2.0 / math.pi)
    return x * (0.5 * (1.0 + jnp.tanh(c * (x + 0.044715 * (x * x * x)))))


def _tril_mask(n, strict=False):
    row = lax.broadcasted_iota(jnp.int32, (n, n), 0)
    col = lax.broadcasted_iota(jnp.int32, (n, n), 1)
    return (col < row) if strict else (col <= row)


def _shift_rows(x, shift, carry):
    rolled = pltpu.roll(x, shift, axis=0)
    crolled = pltpu.roll(carry, shift, axis=0)
    rowi = lax.broadcasted_iota(jnp.int32, carry.shape, 0)
    first = jnp.where(rowi < shift, crolled, rolled[0:V7X_SUBLANES])
    return jnp.concatenate([first, rolled[V7X_SUBLANES:]], axis=0)


def _ab_in_kernel(h_ref, g_ref, w_ref, wd_ref, wd2_ref, bd_ref, perm_ref, out_ref, loga_ref, up_ref):
    hn = _rms(h_ref[...], g_ref[...]).astype(BF16)
    out = jnp.dot(hn, w_ref[...], preferred_element_type=F32)
    out_ref[...] = out.astype(BF16)
    dlr = jnp.dot(hn, wd_ref[...], preferred_element_type=F32)
    z = _hdot(dlr, wd2_ref[...]) + bd_ref[...]
    loga_ref[...] = -_softplus(-z) * (1.0 / GLA_TAU)
    mw = up_ref.shape[1]
    u16 = out[:, out.shape[1] - mw:].astype(BF16)
    up_ref[...] = jnp.dot(perm_ref[...], u16, preferred_element_type=F32).astype(BF16)


def _ab_in(h, gain, w_main, w_dlr, w_decay2, b_decay, perm, tm, mw):
    t, d = h.shape
    n_main = w_main.shape[1]
    n_dk = w_decay2.shape[1]
    full = lambda a: pl.BlockSpec(a.shape, lambda i: (0, 0))
    rows = lambda w: pl.BlockSpec((tm, w), lambda i: (i, 0))
    return pl.pallas_call(
        _ab_in_kernel,
        grid=(t // tm,),
        in_specs=[rows(d), full(gain), full(w_main), full(w_dlr), full(w_decay2), full(b_decay), full(perm)],
        out_specs=[rows(n_main), rows(n_dk), rows(mw)],
        out_shape=[jax.ShapeDtypeStruct((t, n_main), BF16), jax.ShapeDtypeStruct((t, n_dk), F32),
                   jax.ShapeDtypeStruct((t, mw), BF16)],
        compiler_params=_cparams("parallel"),
        name="ab_in",
    )(h, gain, w_main, w_dlr, w_decay2, b_decay, perm)


def _chunk_step_perm(tm, tc):
    dst = jnp.arange(tm)
    src = (dst % (tm // tc)) * tc + dst // (tm // tc)
    return (src[:, None] == jnp.arange(tm)[None, :]).astype(BF16)


def _gla_kernel(q_ref, k_ref, v_ref, gate_ref, la_ref, gain_ref, o_ref, state_ref, *, n_chunks, dk, dv):
    c = GLA_CHUNK

    @pl.when(pl.program_id(1) == 0)
    def _():
        state_ref[...] = jnp.zeros_like(state_ref)

    tb = n_chunks * c
    tril = _tril_mask(c)
    scale = dk ** -0.5
    ri = lax.broadcasted_iota(jnp.int32, (tb, tb), 0)
    ci = lax.broadcasted_iota(jnp.int32, (tb, tb), 1)
    blk_tril = jnp.where(((ri // c) == (ci // c)) & (ci <= ri), 1.0, 0.0).astype(BF16)
    b_all = _split3_dot(blk_tril, la_ref[...])
    lane_head = lax.broadcasted_iota(jnp.int32, (c, GLA_HEADS * dk), 1) // dk
    nt_dims = (((1,), (1,)), ((), ()))
    tn_dims = (((0,), (0,)), ((), ()))

    chunks = []
    for i in range(n_chunks):
        rs = slice(i * c, (i + 1) * c)
        b = b_all[rs, :]
        b_last = b[c - 1:c, :]
        k = k_ref[rs, :].astype(F32)
        chunks.append(dict(rs=rs, q_in=(q_ref[rs, :].astype(F32) * jnp.exp(b) * scale).astype(BF16),
                           k_in=(k * jnp.exp(-b)).astype(BF16), k_st=k * jnp.exp(b_last - b),
                           dec=jnp.exp(b_last)))
    for ch in chunks:
        ch['v'] = [v_ref[ch['rs'], h * dv:(h + 1) * dv].astype(BF16) for h in range(GLA_HEADS)]
        zero16 = jnp.zeros((), BF16)
        ch['scores'] = [
            jnp.where(tril, lax.dot_general(jnp.where(lane_head == h, ch['q_in'], zero16), ch['k_in'], nt_dims,
                                            preferred_element_type=F32), 0.0).astype(BF16)
            for h in range(GLA_HEADS)]
    for ch in chunks:
        ch['o'] = [jnp.dot(ch['scores'][h], ch['v'][h], preferred_element_type=F32) for h in range(GLA_HEADS)]
        ch['d_state'] = [
            lax.dot_general(ch['v'][h], jnp.where(lane_head == h, ch['k_st'], 0.0).astype(BF16), tn_dims,
                            preferred_element_type=F32) for h in range(GLA_HEADS)]

    st = [state_ref[h] for h in range(GLA_HEADS)]
    for ch in chunks:
        ch['st'] = [s.astype(BF16) for s in st]
        st = [st[h] * ch['dec'] + ch['d_state'][h] for h in range(GLA_HEADS)]
    for h in range(GLA_HEADS):
        state_ref[h] = st[h]
    for ch in chunks:
        for h in range(GLA_HEADS):
            vs = slice(h * dv, (h + 1) * dv)
            o = ch['o'][h] + lax.dot_general(ch['q_in'], ch['st'][h], nt_dims, preferred_element_type=F32)
            on = o * lax.rsqrt(jnp.mean(o * o, axis=-1, keepdims=True) + NORM_EPS) * gain_ref[:, vs]
            g = gate_ref[ch['rs'], vs].astype(F32)
            o_ref[ch['rs'], vs] = (on * (g * _sigmoid(g))).astype(o_ref.dtype)


def _gla(qkvgu, loga, gain, bsz, seq, tb):
    t = bsz * seq
    dk = loga.shape[1] // GLA_HEADS
    dv = gain.shape[1] // GLA_HEADS
    nq, nv = GLA_HEADS * dk, GLA_HEADS * dv
    nt = seq // tb
    row = lambda b, i: b * nt + i
    kern = functools.partial(_gla_kernel, n_chunks=tb // GLA_CHUNK, dk=dk, dv=dv)
    return pl.pallas_call(
        kern,
        grid=(bsz, nt),
        in_specs=[pl.BlockSpec((tb, nq), lambda b, i: (row(b, i), 0)),
                  pl.BlockSpec((tb, nq), lambda b, i: (row(b, i), 1)),
                  pl.BlockSpec((tb, nv), lambda b, i: (row(b, i), 1)),
                  pl.BlockSpec((tb, nv), lambda b, i: (row(b, i), 2)),
                  pl.BlockSpec((tb, nq), lambda b, i: (row(b, i), 0)),
                  pl.BlockSpec(gain.shape, lambda b, i: (0, 0))],
        out_specs=pl.BlockSpec((tb, nv), lambda b, i: (row(b, i), 0)),
        out_shape=jax.ShapeDtypeStruct((t, nv), BF16),
        scratch_shapes=[pltpu.VMEM((GLA_HEADS, dv, nq), F32)],
        compiler_params=_cparams("parallel", "arbitrary"),
        name="gla",
    )(qkvgu, qkvgu, qkvgu, qkvgu, loga, gain)


def _transpose_lane_chunks(sets, ch):
    n = len(sets[0])
    chunk = lax.broadcasted_iota(jnp.int32, sets[0][0].shape, 1) // ch
    sets = [list(xs) for xs in sets]
    d = n // 2
    while d >= 1:
        low_half = (chunk & d) == 0
        for xs in sets:
            for r in range(n):
                if r & d:
                    continue
                lo, hi = xs[r], xs[r + d]
                xs[r] = jnp.where(low_half, lo, pltpu.roll(hi, d * ch, axis=1))
                xs[r + d] = jnp.where(low_half, pltpu.roll(lo, (n - d) * ch, axis=1), hi)
        d //= 2
    return sets


def _s5_kernel(u_ref, kgen_ref, ws_ref, wc_ref, a_ref, y_ref, uy_scr, s_scr, h_scr, *,
               n_pairs, bsz, seq, tm):
    n2 = 2 * S5_STATE
    tc, ch = S5_CHUNK, S5_GROUP
    lanes = u_ref.shape[1]
    n_grp = lanes // ch
    n_half = tc // n_grp
    kt = tm // tc
    n_tiles = seq // tm
    rows = n_tiles * kt * bsz

    def tile_rows(b, i, step):
        return pl.ds(pl.multiple_of(b * seq + i * tm + step * kt, kt), kt)

    def gather_tile(i, _):
        for b in range(bsz):
            sets = [[u_ref[tile_rows(b, i, half * n_grp + j8), :].astype(F32) for j8 in range(n_grp)]
                    for half in range(n_half)]
            for half, xs in enumerate(_transpose_lane_chunks(sets, ch)):
                for g, x in enumerate(xs):
                    uy_scr[g, half, i, pl.ds(b, kt, stride=bsz), :] = x
        return 0

    lax.fori_loop(0, n_tiles, gather_tile, 0)

    def group_body(g, _):
        _s5_group(g, kgen_ref, ws_ref, wc_ref, a_ref, uy_scr, s_scr, h_scr, n_pairs=n_pairs, bsz=bsz, rows=rows,
                  lanes=lanes, n_half=n_half, n_tiles=n_tiles, kt=kt)
        return 0

    lax.fori_loop(0, n_grp, group_body, 0)

    def scatter_tile(i, _):
        for b in range(bsz):
            sets = [[uy_scr[g, half, i, pl.ds(b, kt, stride=bsz), :] for g in range(n_grp)]
                    for half in range(n_half)]
            for half, ys in enumerate(_transpose_lane_chunks(sets, ch)):
                for j8, y in enumerate(ys):
                    y_ref[tile_rows(b, i, half * n_grp + j8), :] = y
        return 0

    lax.fori_loop(0, n_tiles, scatter_tile, 0)


def _s5_group(g, kgen_ref, ws_ref, wc_ref, a_ref, uy_scr, s_scr, h_scr, *, n_pairs, bsz, rows, lanes, n_half,
              n_tiles, kt):
    n2 = 2 * S5_STATE
    tc, ch = S5_CHUNK, S5_GROUP
    u = jnp.concatenate([uy_scr[g, hf].reshape(rows, lanes) for hf in range(n_half)], axis=1).astype(BF16)
    s_scr[...] = jnp.dot(u, ws_ref[g], preferred_element_type=F32)
    a1 = a_ref[g, 0:1, :]
    a2 = a_ref[g, 1:2, :]

    def body(j, carry):
        h, hs = carry
        r0 = pl.multiple_of(j * (2 * bsz), 2 * bsz)
        blk = s_scr[pl.ds(r0, 2 * bsz), :]
        h1 = a1 * h + a2 * hs + blk[0:bsz, 0:n2]
        hs1 = a1 * hs - a2 * h + blk[0:bsz, n2:2 * n2]
        h2 = a1 * h1 + a2 * hs1 + blk[bsz:2 * bsz, 0:n2]
        hs2 = a1 * hs1 - a2 * h1 + blk[bsz:2 * bsz, n2:2 * n2]
        h_scr[pl.ds(r0, 2 * bsz), :] = jnp.concatenate([h, h1], axis=0)
        return h2, hs2

    zero = jnp.zeros((bsz, n2), F32)
    lax.fori_loop(0, n_pairs, body, (zero, zero))
    kg = kgen_ref[g]
    lane_w = lax.broadcasted_iota(jnp.int32, kg.shape, 1)
    tz = jnp.concatenate([kg] + [jnp.where(lane_w >= ch * j, pltpu.roll(kg, ch * j, axis=1), 0.0)
                                 for j in range(1, tc)], axis=0).astype(BF16)
    yg = (jnp.dot(u, tz, preferred_element_type=F32)
          + jnp.dot(h_scr[...].astype(BF16), wc_ref[g], preferred_element_type=F32))
    for hf in range(n_half):
        uy_scr[g, hf] = yg[:, hf * lanes:(hf + 1) * lanes].reshape(n_tiles, kt * bsz, lanes)


def _s5(u_perm, kgen, ws, wc, a_pow, bsz, seq, tm):
    t, mw = u_perm.shape
    lanes = 128
    width = kgen.shape[2]
    g_per_blk = lanes // S5_GROUP
    assert 2 * bsz == V7X_SUBLANES, "the chunk scan walks two chunks per 8-row tile"
    rows = (seq // S5_CHUNK) * bsz
    n_pairs = rows // (2 * bsz)
    per_blk = lambda a: pl.BlockSpec((g_per_blk,) + a.shape[1:], lambda q: (q, 0, 0))
    blk = pl.BlockSpec((t, lanes), lambda q: (0, q))
    kern = functools.partial(_s5_kernel, n_pairs=n_pairs, bsz=bsz, seq=seq, tm=tm)
    kt = tm // S5_CHUNK
    return pl.pallas_call(
        kern,
        grid=(mw // lanes,),
        in_specs=[blk, per_blk(kgen), per_blk(ws), per_blk(wc), per_blk(a_pow)],
        out_specs=blk,
        out_shape=jax.ShapeDtypeStruct((t, mw), F32),
        scratch_shapes=[pltpu.VMEM((g_per_blk, width // lanes, seq // tm, kt * bsz, lanes), F32),
                        pltpu.VMEM((rows, ws.shape[2]), F32), pltpu.VMEM((rows, wc.shape[1]), F32)],
        compiler_params=_cparams("parallel"),
        name="s5",
    )(u_perm, kgen, ws, wc, a_pow)


def _s5_weights(lam_re, lam_im, log_step, b_re, b_im, c_re, c_im):
    tc = S5_CHUNK
    groups, n = lam_re.shape
    lr = jnp.minimum(lam_re.astype(F32), -1e-4)
    li = lam_im.astype(F32)
    delta = jnp.exp(log_step.astype(F32))[:, None]
    tau = jnp.arange(tc + 1, dtype=F32)
    mag = jnp.exp((lr * delta)[..., None] * tau)
    ang = (li * delta)[..., None] * tau
    pw_re, pw_im = mag * jnp.cos(ang), mag * jnp.sin(ang)
    num_re, num_im = pw_re[..., 1] - 1.0, pw_im[..., 1]
    den = lr * lr + li * li
    f_re = (num_re * lr + num_im * li) / den
    f_im = (num_im * lr - num_re * li) / den
    b_re, b_im = b_re.astype(F32), b_im.astype(F32)
    bb_re = f_re[..., None] * b_re - f_im[..., None] * b_im
    bb_im = f_re[..., None] * b_im + f_im[..., None] * b_re
    ct_re = jnp.transpose(c_re.astype(F32), (0, 2, 1))
    ct_im = jnp.transpose(c_im.astype(F32), (0, 2, 1))
    cp_re = ct_re[:, :, None, :] * pw_re[..., None] - ct_im[:, :, None, :] * pw_im[..., None]
    cp_im = ct_re[:, :, None, :] * pw_im[..., None] + ct_im[:, :, None, :] * pw_re[..., None]
    width = tc * S5_GROUP
    ein = functools.partial(jnp.einsum, precision=HIGHEST)
    kgen = (ein('gnd,gnx->gdx', bb_re, cp_re[:, :, :tc].reshape(groups, n, width))
            - ein('gnd,gnx->gdx', bb_im, cp_im[:, :, :tc].reshape(groups, n, width)))
    wc = jnp.concatenate([cp_re[:, :, 1:].reshape(groups, n, width),
                          -cp_im[:, :, 1:].reshape(groups, n, width)], axis=1)
    rev_re = jnp.transpose(pw_re[..., tc - 1::-1], (0, 2, 1))
    rev_im = jnp.transpose(pw_im[..., tc - 1::-1], (0, 2, 1))
    bt_re = jnp.transpose(bb_re, (0, 2, 1))
    bt_im = jnp.transpose(bb_im, (0, 2, 1))
    s_re = (rev_re[:, :, None, :] * bt_re[:, None] - rev_im[:, :, None, :] * bt_im[:, None]).reshape(groups, width, n)
    s_im = (rev_re[:, :, None, :] * bt_im[:, None] + rev_im[:, :, None, :] * bt_re[:, None]).reshape(groups, width, n)
    ws = jnp.concatenate([s_re, s_im, s_im, s_re], axis=-1)
    a_pow = jnp.stack([jnp.concatenate([pw_re[..., tc], pw_re[..., tc]], axis=-1),
                       jnp.concatenate([-pw_im[..., tc], pw_im[..., tc]], axis=-1)], axis=1)
    return kgen, ws, wc, a_pow


def _ab_out_kernel(h_ref, oa_ref, ys_ref, u_ref, d_ref, wglu_ref, bglu_ref, wo_ref, g_ref, perm_ref, out_ref):
    mw = oa_ref.shape[1]
    hi, lo = _split_bf16(ys_ref[...])
    y_ssm = (jnp.dot(perm_ref[...], hi, preferred_element_type=F32)
             + jnp.dot(perm_ref[...], lo, preferred_element_type=F32))
    y = y_ssm + d_ref[...] * u_ref[...].astype(F32)
    ob = _gelu_tanh(y) * _sigmoid(_bdot(y, wglu_ref[...]) + bglu_ref[...])
    mix = _bdot(oa_ref[...], wo_ref[0:mw, :]) + _bdot(ob, wo_ref[mw:2 * mw, :])
    out_ref[...] = h_ref[...] + _rms(mix, g_ref[...])


def _ab_out(h, o_a, y_ssm, qkvgu, d_skip, w_glu, b_glu, w_out, gain, perm_t, tm):
    t, d = h.shape
    mw = o_a.shape[1]
    full = lambda a: pl.BlockSpec(a.shape, lambda i: (0, 0))
    rows = lambda w, cb=0: pl.BlockSpec((tm, w), lambda i: (i, cb))
    return pl.pallas_call(
        _ab_out_kernel,
        grid=(t // tm,),
        in_specs=[rows(d), rows(mw), rows(mw), rows(mw, qkvgu.shape[1] // mw - 1), full(d_skip), full(w_glu),
                  full(b_glu), full(w_out), full(gain), full(perm_t)],
        out_specs=rows(d),
        out_shape=jax.ShapeDtypeStruct((t, d), F32),
        compiler_params=_cparams("parallel"),
        name="ab_out",
    )(h, o_a, y_ssm, qkvgu, d_skip, w_glu, b_glu, w_out, gain, perm_t)


def _mem_kv_kernel(mem_ref, g_ref, wk_ref, wv_ref, k_ref, v_ref):
    mn = _rms(mem_ref[...], g_ref[...]).astype(BF16)
    k_ref[...] = jnp.dot(mn, wk_ref[...], preferred_element_type=F32).astype(BF16)
    v_ref[...] = jnp.dot(mn, wv_ref[...], preferred_element_type=F32).astype(BF16)


def _mem_kv(mem, gain, wk, wv, tm):
    t, d = mem.shape
    full = lambda a: pl.BlockSpec(a.shape, lambda i: (0, 0))
    rows = pl.BlockSpec((tm, d), lambda i: (i, 0))
    return pl.pallas_call(
        _mem_kv_kernel,
        grid=(t // tm,),
        in_specs=[rows, full(gain), full(wk), full(wv)],
        out_specs=[rows, rows],
        out_shape=[jax.ShapeDtypeStruct((t, d), BF16), jax.ShapeDtypeStruct((t, d), BF16)],
        compiler_params=_cparams("parallel"),
        name="mem_kv",
    )(mem, gain, wk, wv)


def _xattn_kernel(h_ref, k_ref, v_ref, wq_ref, wo_ref, gq_ref, go_ref, out_ref, o_scr):
    h = h_ref[...]
    d = h.shape[1]
    hd = d // XA_HEADS
    q = jnp.dot(_rms(h, gq_ref[...]).astype(BF16), wq_ref[...], preferred_element_type=F32)
    for hh in range(XA_HEADS):
        cs = slice(hh * hd, (hh + 1) * hd)
        s = _bdot_nt(q[:, cs], k_ref[:, cs]) * (hd ** -0.5)
        e = jnp.exp(s - jnp.max(s, axis=-1, keepdims=True))
        p = e / jnp.sum(e, axis=-1, keepdims=True)
        o_scr[:, cs] = _bdot(p, v_ref[:, cs]).astype(BF16)
    xa = jnp.dot(o_scr[...], wo_ref[...], preferred_element_type=F32)
    out_ref[...] = h + _rms(xa, go_ref[...])


def _xattn(h, k, v, wq, wo, gq, go, bsz, seq, mem_len, tm):
    t, d = h.shape
    nt = seq // tm
    full = lambda a: pl.BlockSpec(a.shape, lambda b, i: (0, 0))
    rows = pl.BlockSpec((tm, d), lambda b, i: (b * nt + i, 0))
    kv = pl.BlockSpec((mem_len, d), lambda b, i: (b, 0))
    return pl.pallas_call(
        _xattn_kernel,
        grid=(bsz, nt),
        in_specs=[rows, kv, kv, full(wq), full(wo), full(gq), full(go)],
        out_specs=rows,
        out_shape=jax.ShapeDtypeStruct((t, d), F32),
        scratch_shapes=[pltpu.VMEM((tm, d), BF16)],
        compiler_params=_cparams("parallel", "parallel"),
        name="xattn",
    )(h, k, v, wq, wo, gq, go)


def _mlp_kernel(h_ref, gi_ref, go_ref, w1_ref, w2_ref, out_ref, xn_scr, acc_scr):
    j = pl.program_id(1)

    @pl.when(j == 0)
    def _():
        xn_scr[...] = _rms(h_ref[...], gi_ref[...]).astype(BF16)
        acc_scr[...] = jnp.zeros_like(acc_scr)

    a = jnp.maximum(jnp.dot(xn_scr[...], w1_ref[...], preferred_element_type=F32), 0.0)
    acc_scr[...] += jnp.dot((a * a).astype(BF16), w2_ref[...], preferred_element_type=F32)

    @pl.when(j == pl.num_programs(1) - 1)
    def _():
        out_ref[...] = h_ref[...] + _rms(acc_scr[...], go_ref[...])


def _mlp(h, gi, go, w1, w2, tm, tf):
    t, d = h.shape
    dff = w1.shape[1]
    full = lambda a: pl.BlockSpec(a.shape, lambda i, j: (0, 0))
    rows = pl.BlockSpec((tm, d), lambda i, j: (i, 0))
    return pl.pallas_call(
        _mlp_kernel,
        grid=(t // tm, dff // tf),
        in_specs=[rows, full(gi), full(go), pl.BlockSpec((d, tf), lambda i, j: (0, j)),
                  pl.BlockSpec((tf, d), lambda i, j: (j, 0))],
        out_specs=rows,
        out_shape=jax.ShapeDtypeStruct((t, d), F32),
        scratch_shapes=[pltpu.VMEM((tm, d), BF16), pltpu.VMEM((tm, d), F32)],
        compiler_params=_cparams("parallel", "arbitrary"),
        name="mlp",
    )(h, gi, go, w1, w2)


def _cd_in_kernel(h_ref, g_ref, wbig_ref, wsm_ref, mub_ref, mus_ref, w0_ref, w2_ref, a0_ref, a2_ref, g2_ref,
                  kk_ref, ka_ref, hsum_ref, cw_ref, cb_ref, wa_ref, ba_ref, wx_ref, bx_ref, lam_ref,
                  r_o, lw_o, k_o, v_o, kk_o, a_o, g_o, la_o, lb_o, gate_o,
                  carry_b, carry_s, carry_x, *, mw, n_sub):
    tm = h_ref.shape[0]

    @pl.when(pl.program_id(1) == 0)
    def _():
        carry_b[...] = jnp.zeros_like(carry_b)
        carry_s[...] = jnp.zeros_like(carry_s)
        carry_x[...] = jnp.zeros_like(carry_x)

    ts = tm // n_sub
    proj = []
    for sb in range(n_sub):
        hn = _rms(h_ref[sb * ts:(sb + 1) * ts, :], g_ref[...]).astype(BF16)
        proj.append((jnp.dot(hn, wbig_ref[...], preferred_element_type=F32),
                     jnp.dot(hn, wsm_ref[...], preferred_element_type=F32)))

    for sb in range(n_sub):
        rs = slice(sb * ts, (sb + 1) * ts)
        pb, ps = proj[sb]
        p3 = pb[:, 0:3 * mw]
        xb = pb[:, 3 * mw:4 * mw]
        gate_o[rs, :] = pb[:, 4 * mw:5 * mw]

        prev3 = _shift_rows(p3, 1, carry_b[...])
        prevs = _shift_rows(ps, 1, carry_s[...])
        carry_b[...] = p3[ts - V7X_SUBLANES:ts, :]
        carry_s[...] = ps[ts - V7X_SUBLANES:ts, :]
        p3 = p3 + (prev3 - p3) * mub_ref[...]
        ps = ps + (prevs - ps) * mus_ref[...]
        r = p3[:, 0:mw]
        k = p3[:, mw:2 * mw]
        v = p3[:, 2 * mw:3 * mw]
        lora = ps[:, 0:128]
        wlog = -_softplus(-(w0_ref[...] + _bdot(jnp.tanh(lora), w2_ref[...]))) - 0.5
        a = _sigmoid(a0_ref[...] + _bdot(lora, a2_ref[...]))
        kkr = k * kk_ref[...]
        norm = jnp.sqrt(_bdot(kkr * kkr, hsum_ref[...]))
        r_o[rs, :] = r.astype(BF16)
        lw_o[rs, :] = -jnp.exp(wlog)
        k_o[rs, :] = (k * (1.0 + (a - 1.0) * ka_ref[...])).astype(BF16)
        v_o[rs, :] = v.astype(BF16)
        kk_o[rs, :] = (kkr / jnp.maximum(norm, 1e-12)).astype(BF16)
        a_o[rs, :] = a.astype(BF16)
        g_o[rs, :] = _bdot(_sigmoid(ps[:, 128:256]), g2_ref[...]).astype(BF16)

        cx = carry_x[...]
        xc = cb_ref[...] + cw_ref[LRU_CONV - 1:LRU_CONV, :] * xb
        for sh in range(1, LRU_CONV):
            xc = xc + cw_ref[LRU_CONV - 1 - sh:LRU_CONV - sh, :] * _shift_rows(xb, sh, cx)
        carry_x[...] = xb[ts - V7X_SUBLANES:ts, :]
        rg = _sigmoid(_bdot(xc, wa_ref[...]) + ba_ref[...])
        ig = _sigmoid(_bdot(xc, wx_ref[...]) + bx_ref[...])
        log_a = -LRU_C * rg * _softplus(-lam_ref[...])
        la_o[rs, :] = jnp.exp(log_a)
        th = jnp.tanh(log_a)
        lb_o[rs, :] = jnp.sqrt(-2.0 * th / (1.0 - th)) * (ig * xc)


def _cd_in(h, gain, wbig, wsm, vecs, mats, bsz, seq, tm, mw):
    t, d = h.shape
    nt = seq // tm
    full = lambda a: pl.BlockSpec(a.shape, lambda b, i: (0, 0))
    rows = lambda w: pl.BlockSpec((tm, w), lambda b, i: (b * nt + i, 0))
    (mub, mus, w0, a0, kk_w, ka_w, cw, cb, ba, bx, lam) = vecs
    (w2p, a2p, g2, hsum, wa, wx) = mats
    args = (h, gain, wbig, wsm, mub, mus, w0, w2p, a0, a2p, g2, kk_w, ka_w, hsum, cw, cb, wa, ba, wx, bx, lam)
    kern = functools.partial(_cd_in_kernel, mw=mw, n_sub=4)
    return pl.pallas_call(
        kern,
        grid=(bsz, nt),
        in_specs=[rows(d)] + [full(a) for a in args[1:]],
        out_specs=[rows(mw)] * 10,
        out_shape=[jax.ShapeDtypeStruct((t, mw), dt) for dt in CD_IN_OUT_DTYPES],
        scratch_shapes=[pltpu.VMEM((V7X_SUBLANES, 3 * mw), F32), pltpu.VMEM((V7X_SUBLANES, wsm.shape[1]), F32),
                        pltpu.VMEM((V7X_SUBLANES, mw), F32)],
        compiler_params=_cparams("parallel", "arbitrary"),
        name="cd_in",
    )(*args)


def _rwkv_kernel(r_ref, lw_ref, k_ref, v_ref, kk_ref, a_ref, g_ref, rk_ref, lng_ref, lnb_ref, hsum_ref,
                 o_ref, state_ref, y_scr, *, slab_heads, n_chunks):
    c = RWKV_CHUNK
    hd = RWKV_HEAD
    assert c == hd, "one block mask serves both the (head, s) and the (head, d) layouts"
    sw = slab_heads * hd
    n_slabs = r_ref.shape[1] // sw

    @pl.when(pl.program_id(1) == 0)
    def _():
        state_ref[...] = jnp.zeros_like(state_ref)

    tb = n_chunks * c
    tri = lax.broadcasted_iota(jnp.int32, (tb, tb), 0)
    tci = lax.broadcasted_iota(jnp.int32, (tb, tb), 1)
    blk_tril = jnp.where(((tri // c) == (tci // c)) & (tci <= tri), 1.0, 0.0).astype(BF16)
    cum_all = _split3_dot(blk_tril, lw_ref[...])
    blk_m = (lax.broadcasted_iota(jnp.int32, (sw, sw), 0) // c) == (lax.broadcasted_iota(jnp.int32, (sw, sw), 1) // c)
    wide_t = lax.broadcasted_iota(jnp.int32, (c, sw), 0)
    wide_s = lax.broadcasted_iota(jnp.int32, (c, sw), 1) % c
    strict_w = wide_s < wide_t
    incl_w = wide_s <= wide_t
    eye_w = jnp.where(wide_s == wide_t, 1.0, 0.0)

    def bdiag(x):
        x16 = x.astype(BF16)
        return jnp.where(blk_m, jnp.concatenate([x16] * slab_heads, axis=0), jnp.zeros((), BF16))

    def mm(a, w16):
        return jnp.dot(a.astype(BF16), w16, preferred_element_type=F32)

    nt_dims = (((1,), (1,)), ((), ()))
    chains = []
    for chunk in range(n_chunks):
        rs = slice(chunk * c, (chunk + 1) * c)
        lw = lw_ref[rs, :]
        cum = cum_all[rs, :]
        cum_last = cum[c - 1:c, :]
        r = r_ref[rs, :].astype(F32)
        k = k_ref[rs, :].astype(F32)
        v = v_ref[rs, :].astype(F32)
        kk = kk_ref[rs, :].astype(F32)
        bvec = kk * a_ref[rs, :].astype(F32)
        inv_g = jnp.exp(-cum)
        to_end = jnp.exp(cum_last - cum)
        rt = r * jnp.exp(cum)
        kp = kk * jnp.exp(cum - lw)
        be = bvec * inv_g
        kh = k * inv_g
        bb = bvec * to_end
        kb = k * to_end
        g_end = jnp.exp(cum_last)
        for s in range(n_slabs):
            ls = slice(s * sw, (s + 1) * sw)
            chains.append(dict(rs=rs, ls=ls, slab=s, rt=rt[:, ls], kp=kp[:, ls], v=v[:, ls], be=be[:, ls],
                               kh=kh[:, ls], kb=kb[:, ls], bb=bb[:, ls], g_end=g_end[:, ls]))

    for ch in chains:
        lhs2 = jnp.concatenate([ch['kp'], ch['rt']], axis=0).astype(BF16)
        ab = lax.dot_general(lhs2, bdiag(ch['be']), nt_dims, preferred_element_type=F32)
        ak = lax.dot_general(lhs2, bdiag(ch['kh']), nt_dims, preferred_element_type=F32)
        ch['a_kb'] = jnp.where(strict_w, ab[0:c], 0.0)
        ch['a_rb'] = jnp.where(incl_w, ab[c:2 * c], 0.0).astype(BF16)
        ch['a_kr'] = jnp.concatenate([jnp.where(strict_w, ak[0:c], 0.0), jnp.where(incl_w, ak[c:2 * c], 0.0)],
                                     axis=0).astype(BF16)
    for ch in chains:
        x = -ch['a_kb']
        ch['t'] = eye_w + x
        ch['x'] = mm(x, bdiag(x))
    for _ in range(int(math.log2(c)) - 2):
        for ch in chains:
            res = mm(jnp.concatenate([ch['t'], ch['x']], axis=0), bdiag(ch['x']))
            ch['t'] = ch['t'] + res[0:c]
            ch['x'] = res[c:2 * c]
    for ch in chains:
        ch['t'] = ch['t'] + mm(ch['t'], bdiag(ch['x']))
    for ch in chains:
        l_hi, l_lo = _split_bf16(ch['a_kb'])
        t_hi, t_lo = _split_bf16(ch['t'])
        lt = mm(jnp.concatenate([l_hi, l_lo], axis=0), bdiag(t_hi))
        ch['resid'] = eye_w - ch['t'] - (lt[0:c] + lt[c:2 * c] + mm(l_hi, bdiag(t_lo)))
    for ch in chains:
        ch['t16'] = (ch['t'] + mm(ch['t'], bdiag(ch['resid']))).astype(BF16)
    for ch in chains:
        ch['w_tok'] = mm(ch['t16'], bdiag(ch['kp']))
        ch['av'] = mm(ch['a_kr'], bdiag(ch['v']))
    for ch in chains:
        ch['u_tok'] = mm(ch['t16'], bdiag(ch['av'][0:c]))
    for ch in chains:
        q_tok = ch['rt'] - mm(ch['a_rb'], bdiag(ch['w_tok']))
        ch['qw'] = jnp.concatenate([q_tok, ch['w_tok']], axis=0).astype(BF16)
        ch['y0'] = ch['av'][c:2 * c] - mm(ch['a_rb'], bdiag(ch['u_tok']))
        ch['kb2'] = jnp.concatenate([ch['kb'], ch['bb']], axis=0).astype(BF16)

    for ch in chains:
        s = ch['slab']
        p_t = state_ref[s]
        res = lax.dot_general(ch['qw'], p_t.astype(BF16), nt_dims, preferred_element_type=F32)
        y_scr[ch['rs'], ch['ls']] = res[0:c] + ch['y0']
        u_all = -(ch['u_tok'] + res[c:2 * c])
        upd = lax.dot_general(jnp.concatenate([ch['v'], u_all], axis=0).astype(BF16), ch['kb2'],
                              (((0,), (0,)), ((), ())), preferred_element_type=F32)
        state_ref[s] = p_t * ch['g_end'] + jnp.where(blk_m, upd, 0.0)

    hsum = hsum_ref[...]
    y = y_scr[...]
    r = r_ref[...].astype(F32)
    v = v_ref[...].astype(F32)
    mean = _split_dot(y, hsum) * (1.0 / hd)
    yc = y - mean
    var = _bdot(yc * yc, hsum) * (1.0 / hd)
    yn = yc * lax.rsqrt(var + RWKV_GN_EPS) * lng_ref[...] + lnb_ref[...]
    bonus = _bdot(r * k_ref[...].astype(F32) * rk_ref[...], hsum) * v
    o_ref[...] = ((yn + bonus) * g_ref[...].astype(F32)).astype(o_ref.dtype)


def _rwkv(r, lw, k, v, kk, a, g, rk, lng, lnb, hsum, bsz, seq, slab_heads=4, n_chunks=4):
    t, mw = r.shape
    c = RWKV_CHUNK * n_chunks
    nt = seq // c
    sw = slab_heads * RWKV_HEAD
    full = lambda x: pl.BlockSpec(x.shape, lambda b, i: (0, 0))
    rows = pl.BlockSpec((c, mw), lambda b, i: (b * nt + i, 0))
    kern = functools.partial(_rwkv_kernel, slab_heads=slab_heads, n_chunks=n_chunks)
    return pl.pallas_call(
        kern,
        grid=(bsz, nt),
        in_specs=[rows] * 7 + [full(rk), full(lng), full(lnb), full(hsum)],
        out_specs=rows,
        out_shape=jax.ShapeDtypeStruct((t, mw), BF16),
        scratch_shapes=[pltpu.VMEM((mw // sw, sw, sw), F32), pltpu.VMEM((c, mw), F32)],
        compiler_params=_cparams("parallel", "arbitrary"),
        name="rwkv",
    )(r, lw, k, v, kk, a, g, rk, lng, lnb, hsum)


def _lru_kernel(a_ref, b_ref, gate_ref, o_ref, carry_ref):
    tm = a_ref.shape[0]
    sub = V7X_SUBLANES

    @pl.when(pl.program_id(1) == 0)
    def _():
        carry_ref[...] = jnp.zeros_like(carry_ref)

    rowi = lax.broadcasted_iota(jnp.int32, (sub, a_ref.shape[1]), 0)

    def scan8(a, b, h):
        sh = 1
        while sh < sub:
            a_sh = jnp.where(rowi >= sh, pltpu.roll(a, sh, axis=0), 1.0)
            b_sh = jnp.where(rowi >= sh, pltpu.roll(b, sh, axis=0), 0.0)
            b = b + a * b_sh
            a = a * a_sh
            sh *= 2
        return b + a * h

    def body(j, h):
        r0 = pl.multiple_of(j * (2 * sub), 2 * sub)
        a = a_ref[pl.ds(r0, 2 * sub), :]
        b = b_ref[pl.ds(r0, 2 * sub), :]
        h1 = scan8(a[0:sub], b[0:sub], h)
        h2 = scan8(a[sub:2 * sub], b[sub:2 * sub], h1[sub - 1:sub, :])
        gate = gate_ref[pl.ds(r0, 2 * sub), :].astype(F32)
        o_ref[pl.ds(r0, 2 * sub), :] = (jnp.concatenate([h1, h2], axis=0) * _gelu_tanh(gate)).astype(o_ref.dtype)
        return h2[sub - 1:sub, :]

    carry_ref[0:1, :] = lax.fori_loop(0, tm // (2 * sub), body, carry_ref[0:1, :])


def _lru(a, b, gate, bsz, seq, tm):
    t, mw = a.shape
    nt = seq // tm
    rows = pl.BlockSpec((tm, mw), lambda bi, i: (bi * nt + i, 0))
    return pl.pallas_call(
        _lru_kernel,
        grid=(bsz, nt),
        in_specs=[rows, rows, rows],
        out_specs=rows,
        out_shape=jax.ShapeDtypeStruct((t, mw), BF16),
        scratch_shapes=[pltpu.VMEM((V7X_SUBLANES, mw), F32)],
        compiler_params=_cparams("parallel", "arbitrary"),
        name="lru",
    )(a, b, gate)


def _cd_out_kernel(h_ref, oc_ref, od_ref, wo_ref, g_ref, out_ref):
    mw = oc_ref.shape[1]
    mix = _bdot(oc_ref[...], wo_ref[0:mw, :]) + _bdot(od_ref[...], wo_ref[mw:2 * mw, :])
    out_ref[...] = h_ref[...] + _rms(mix, g_ref[...])


def _cd_out(h, o_c, o_d, w_out, gain, tm):
    t, d = h.shape
    mw = o_c.shape[1]
    full = lambda a: pl.BlockSpec(a.shape, lambda i: (0, 0))
    rows = lambda w: pl.BlockSpec((tm, w), lambda i: (i, 0))
    return pl.pallas_call(
        _cd_out_kernel,
        grid=(t // tm,),
        in_specs=[rows(d), rows(mw), rows(mw), full(w_out), full(gain)],
        out_specs=rows(d),
        out_shape=jax.ShapeDtypeStruct((t, d), F32),
        compiler_params=_cparams("parallel"),
        name="cd_out",
    )(h, o_c, o_d, w_out, gain)


def _row(vec):
    return vec.astype(F32).reshape(1, -1)


def _block_diag(blocks):
    nb, bi, bo = blocks.shape
    eye = jnp.eye(nb, dtype=blocks.dtype)
    return (eye[:, None, :, None] * blocks[:, :, None, :]).reshape(nb * bi, nb * bo)


def _head_sum_matrix(width, head):
    idx = jnp.arange(width) // head
    return (idx[:, None] == idx[None, :]).astype(BF16)


def _layer_ab(h, gains, p, bsz, seq, tm):
    d = h.shape[1]
    mw = d // 2
    dk = p['gla_b_decay'].shape[0] // GLA_HEADS
    nq = GLA_HEADS * dk
    w_in = p['ab_w_in']
    rank = p['gla_w_decay2'].shape[0]
    o_dlr = 2 * nq + 2 * mw
    w_main = jnp.concatenate([w_in[:, :o_dlr], w_in[:, o_dlr + rank:]], axis=1).astype(BF16)
    w_dlr = jnp.pad(w_in[:, o_dlr:o_dlr + rank], ((0, 0), (0, 128 - rank))).astype(BF16)
    w_decay2 = jnp.pad(p['gla_w_decay2'].astype(F32), ((0, 128 - rank), (0, 0)))
    perm = _chunk_step_perm(tm, S5_CHUNK)
    qkvgu, loga, u_perm = _ab_in(h, _row(gains[0]), w_main, w_dlr, w_decay2, _row(p['gla_b_decay']), perm, tm, mw)
    o_a = _gla(qkvgu, loga, _row(p['gla_norm_gain']), bsz, seq, tb=256)
    kgen, ws, wc, a_pow = _s5_weights(p['s5_lambda_re'], p['s5_lambda_im'], p['s5_log_step'], p['s5_b_re'],
                                      p['s5_b_im'], p['s5_c_re'], p['s5_c_im'])
    y_perm = _s5(u_perm, kgen, ws.astype(BF16), wc.astype(BF16), a_pow, bsz, seq, tm)
    return _ab_out(h, o_a, y_perm, qkvgu, _row(p['s5_d']), p['s5_w_glu'].astype(BF16), _row(p['s5_b_glu']),
                   p['ab_w_out'].astype(BF16), _row(gains[1]), perm.T, tm)


def _layer_cd(h, gains, p, bsz, seq, tm):
    d = h.shape[1]
    mw = d // 2
    w_in = p['cd_w_in']
    dr = p['rwkv_w2'].shape[0]
    ar = p['rwkv_a2'].shape[0]
    gr = p['rwkv_g2'].shape[0]
    assert dr + ar == 128 and gr == 128
    o = [0, mw, mw + dr, 2 * mw + dr, 3 * mw + dr, 3 * mw + dr + ar, 3 * mw + dr + ar + gr]
    col = lambda i, width: w_in[:, o[i]:o[i] + width]
    wbig = jnp.concatenate([col(0, mw), col(2, mw), col(3, mw), w_in[:, o[6]:]], axis=1).astype(BF16)
    wsm = jnp.concatenate([col(1, dr), col(4, ar), col(5, gr)], axis=1).astype(BF16)
    mu = p['rwkv_mu'].astype(F32)
    mseg = lambda i, width: mu[o[i]:o[i] + width]
    mub = jnp.concatenate([mseg(0, mw), mseg(2, mw), mseg(3, mw)]).reshape(1, -1)
    mus = jnp.concatenate([mseg(1, dr), mseg(4, ar), mseg(5, gr)]).reshape(1, -1)
    w2p = jnp.pad(p['rwkv_w2'], ((0, ar), (0, 0))).astype(BF16)
    a2p = jnp.pad(p['rwkv_a2'], ((dr, 0), (0, 0))).astype(BF16)
    hsum = _head_sum_matrix(mw, RWKV_HEAD)
    vecs = (mub, mus, _row(p['rwkv_w0']), _row(p['rwkv_a0']), _row(p['rwkv_k_k']), _row(p['rwkv_k_a']),
            p['lru_conv_w'].astype(F32), _row(p['lru_conv_b']), _row(p['lru_b_a']), _row(p['lru_b_x']),
            _row(p['lru_lambda']))
    mats = (w2p, a2p, p['rwkv_g2'].astype(BF16), hsum, _block_diag(p['lru_w_a']).astype(BF16),
            _block_diag(p['lru_w_x']).astype(BF16))
    r, lw, k, v, kk, a, g, la, lb, gate = _cd_in(h, _row(gains[0]), wbig, wsm, vecs, mats, bsz, seq, tm, mw)
    o_c = _rwkv(r, lw, k, v, kk, a, g, _row(p['rwkv_r_k']), _row(p['rwkv_ln_gain']), _row(p['rwkv_ln_bias']),
                hsum, bsz, seq)
    o_d = _lru(la, lb, gate, bsz, seq, tm)
    return _cd_out(h, o_c, o_d, p['cd_w_out'].astype(BF16), _row(gains[1]), tm)


def kernel(x, mem, norm_gain, xa_wq, xa_wk, xa_wv, xa_wo, mlp_w1, mlp_w2, ab_w_in, gla_w_decay2, gla_b_decay, gla_norm_gain, s5_lambda_re, s5_lambda_im, s5_log_step, s5_b_re, s5_b_im, s5_c_re, s5_c_im, s5_d, s5_w_glu, s5_b_glu, ab_w_out, cd_w_in, rwkv_mu, rwkv_w0, rwkv_w2, rwkv_a0, rwkv_a2, rwkv_g2, rwkv_k_k, rwkv_k_a, rwkv_r_k, rwkv_ln_gain, rwkv_ln_bias, lru_conv_w, lru_conv_b, lru_w_a, lru_b_a, lru_w_x, lru_b_x, lru_lambda, cd_w_out):
    bsz, seq, d = x.shape
    mem_len = mem.shape[1]
    depth = norm_gain.shape[0]
    tm = min(512, seq)
    ab = dict(ab_w_in=ab_w_in, gla_w_decay2=gla_w_decay2, gla_b_decay=gla_b_decay, gla_norm_gain=gla_norm_gain,
              s5_lambda_re=s5_lambda_re, s5_lambda_im=s5_lambda_im, s5_log_step=s5_log_step, s5_b_re=s5_b_re,
              s5_b_im=s5_b_im, s5_c_re=s5_c_re, s5_c_im=s5_c_im, s5_d=s5_d, s5_w_glu=s5_w_glu, s5_b_glu=s5_b_glu,
              ab_w_out=ab_w_out)
    cd = dict(cd_w_in=cd_w_in, rwkv_mu=rwkv_mu, rwkv_w0=rwkv_w0, rwkv_w2=rwkv_w2, rwkv_a0=rwkv_a0,
              rwkv_a2=rwkv_a2, rwkv_g2=rwkv_g2, rwkv_k_k=rwkv_k_k, rwkv_k_a=rwkv_k_a, rwkv_r_k=rwkv_r_k,
              rwkv_ln_gain=rwkv_ln_gain, rwkv_ln_bias=rwkv_ln_bias, lru_conv_w=lru_conv_w, lru_conv_b=lru_conv_b,
              lru_w_a=lru_w_a, lru_b_a=lru_b_a, lru_w_x=lru_w_x, lru_b_x=lru_b_x, lru_lambda=lru_lambda,
              cd_w_out=cd_w_out)
    h = x.astype(F32).reshape(bsz * seq, d)
    mem2 = mem.astype(F32).reshape(bsz * mem_len, d)
    for layer in range(depth):
        g = norm_gain[layer]
        i = layer // 2
        if layer % 2 == 0:
            h = _layer_ab(h, g, {n: w[i] for n, w in ab.items()}, bsz, seq, tm)
        else:
            h = _layer_cd(h, g, {n: w[i] for n, w in cd.items()}, bsz, seq, tm)
        km, vm = _mem_kv(mem2, _row(g[6]), xa_wk[layer].astype(BF16), xa_wv[layer].astype(BF16), mem_len)
        h = _xattn(h, km, vm, xa_wq[layer].astype(BF16), xa_wo[layer].astype(BF16), _row(g[2]), _row(g[3]),
                   bsz, seq, mem_len, tm)
        h = _mlp(h, _row(g[4]), _row(g[5]), mlp_w1[layer].astype(BF16), mlp_w2[layer].astype(BF16),
                 tm=min(1024, bsz * seq), tf=1024)
    return h.reshape(bsz, seq, d).astype(x.dtype)
```

```python
import functools
import math

import jax
import jax.numpy as jnp
from jax import lax
from jax.experimental import pallas as pl
from jax.experimental.pallas import tpu as pltpu

F32 = jnp.float32
BF16 = jnp.bfloat16
HIGHEST = lax.Precision.HIGHEST

NORM_EPS = 1e-6
GLA_HEADS = 4
GLA_TAU = 16.0
GLA_CHUNK = 64
S5_GROUP = 16
S5_STATE = 64
S5_CHUNK = 16
RWKV_HEAD = 64
RWKV_CHUNK = 64
RWKV_GN_EPS = 64e-5
LRU_CONV = 4
LRU_C = 8.0
XA_HEADS = 4

V7X_SUBLANES = 8
CD_IN_OUT_DTYPES = (BF16, F32, BF16, BF16, BF16, BF16, BF16, F32, F32, F32)
VMEM_LIMIT_BYTES = 48 * 1024 * 1024


def _cparams(*semantics):
    return pltpu.CompilerParams(dimension_semantics=semantics, vmem_limit_bytes=VMEM_LIMIT_BYTES)


def _rms(x, gain):
    return x * lax.rsqrt(jnp.mean(x * x, axis=-1, keepdims=True) + NORM_EPS) * gain


def _bdot(a, b):
    return jnp.dot(a.astype(BF16), b.astype(BF16), preferred_element_type=F32)


def _bdot_nt(a, b):
    return lax.dot_general(a.astype(BF16), b.astype(BF16), (((1,), (1,)), ((), ())),
                           preferred_element_type=F32)


def _bdot_tn(a, b):
    return lax.dot_general(a.astype(BF16), b.astype(BF16), (((0,), (0,)), ((), ())),
                           preferred_element_type=F32)


def _hdot(a, b):
    return jnp.dot(a, b, precision=HIGHEST, preferred_element_type=F32)


def _hdot_nt(a, b):
    return lax.dot_general(a, b, (((1,), (1,)), ((), ())), precision=HIGHEST,
                           preferred_element_type=F32)


def _hdot_tn(a, b):
    return lax.dot_general(a, b, (((0,), (0,)), ((), ())), precision=HIGHEST,
                           preferred_element_type=F32)


def _split_bf16(x):
    hi = x.astype(BF16)
    return hi, (x - hi.astype(F32)).astype(BF16)


def _split_dot(x, w_bf16):
    hi, lo = _split_bf16(x)
    return (jnp.dot(hi, w_bf16, preferred_element_type=F32)
            + jnp.dot(lo, w_bf16, preferred_element_type=F32))


def _split3_dot(w_bf16, x):
    hi = x.astype(BF16)
    r1 = x - hi.astype(F32)
    mid = r1.astype(BF16)
    lo = (r1 - mid.astype(F32)).astype(BF16)
    return (jnp.dot(w_bf16, hi, preferred_element_type=F32) + jnp.dot(w_bf16, mid, preferred_element_type=F32)
            + jnp.dot(w_bf16, lo, preferred_element_type=F32))


def _sigmoid(x):
    return 1.0 / (1.0 + jnp.exp(-x))


def _softplus(x):
    return jnp.maximum(x, 0.0) + jnp.log(1.0 + jnp.exp(-jnp.abs(x)))


def _gelu_tanh(x):
    c = math.sqrt(2.0 / math.pi)
    return x * (0.5 * (1.0 + jnp.tanh(c * (x + 0.044715 * (x * x * x)))))


def _tril_mask(n, strict=False):
    row = lax.broadcasted_iota(jnp.int32, (n, n), 0)
    col = lax.broadcasted_iota(jnp.int32, (n, n), 1)
    return (col < row) if strict else (col <= row)


def _shift_rows(x, shift, carry):
    rolled = pltpu.roll(x, shift, axis=0)
    crolled = pltpu.roll(carry, shift, axis=0)
    rowi = lax.broadcasted_iota(jnp.int32, carry.shape, 0)
    first = jnp.where(rowi < shift, crolled, rolled[0:V7X_SUBLANES])
    return jnp.concatenate([first, rolled[V7X_SUBLANES:]], axis=0)


def _ab_in_kernel(h_ref, g_ref, w_ref, wd_ref, wd2_ref, bd_ref, perm_ref, out_ref, loga_ref, up_ref):
    hn = _rms(h_ref[...], g_ref[...]).astype(BF16)
    out = jnp.dot(hn, w_ref[...], preferred_element_type=F32)
    out_ref[...] = out.astype(BF16)
    dlr = jnp.dot(hn, wd_ref[...], preferred_element_type=F32)
    z = _hdot(dlr, wd2_ref[...]) + bd_ref[...]
    loga_ref[...] = -_softplus(-z) * (1.0 / GLA_TAU)
    mw = up_ref.shape[1]
    u16 = out[:, out.shape[1] - mw:].astype(BF16)
    up_ref[...] = jnp.dot(perm_ref[...], u16, preferred_element_type=F32).astype(BF16)


def _ab_in(h, gain, w_main, w_dlr, w_decay2, b_decay, perm, tm, mw):
    t, d = h.shape
    n_main = w_main.shape[1]
    n_dk = w_decay2.shape[1]
    full = lambda a: pl.BlockSpec(a.shape, lambda i: (0, 0))
    rows = lambda w: pl.BlockSpec((tm, w), lambda i: (i, 0))
    return pl.pallas_call(
        _ab_in_kernel,
        grid=(t // tm,),
        in_specs=[rows(d), full(gain), full(w_main), full(w_dlr), full(w_decay2), full(b_decay), full(perm)],
        out_specs=[rows(n_main), rows(n_dk), rows(mw)],
        out_shape=[jax.ShapeDtypeStruct((t, n_main), BF16), jax.ShapeDtypeStruct((t, n_dk), F32),
                   jax.ShapeDtypeStruct((t, mw), BF16)],
        compiler_params=_cparams("parallel"),
        name="ab_in",
    )(h, gain, w_main, w_dlr, w_decay2, b_decay, perm)


def _chunk_step_perm(tm, tc):
    dst = jnp.arange(tm)
    src = (dst % (tm // tc)) * tc + dst // (tm // tc)
    return (src[:, None] == jnp.arange(tm)[None, :]).astype(BF16)


def _gla_kernel(q_ref, k_ref, v_ref, gate_ref, la_ref, gain_ref, o_ref, state_ref, *, n_chunks, dk, dv):
    c = GLA_CHUNK

    @pl.when(pl.program_id(1) == 0)
    def _():
        state_ref[...] = jnp.zeros_like(state_ref)

    tb = n_chunks * c
    tril = _tril_mask(c)
    scale = dk ** -0.5
    ri = lax.broadcasted_iota(jnp.int32, (tb, tb), 0)
    ci = lax.broadcasted_iota(jnp.int32, (tb, tb), 1)
    blk_tril = jnp.where(((ri // c) == (ci // c)) & (ci <= ri), 1.0, 0.0).astype(BF16)
    b_all = _split3_dot(blk_tril, la_ref[...])
    lane_head = lax.broadcasted_iota(jnp.int32, (c, GLA_HEADS * dk), 1) // dk
    nt_dims = (((1,), (1,)), ((), ()))
    tn_dims = (((0,), (0,)), ((), ()))

    chunks = []
    for i in range(n_chunks):
        rs = slice(i * c, (i + 1) * c)
        b = b_all[rs, :]
        b_last = b[c - 1:c, :]
        k = k_ref[rs, :].astype(F32)
        chunks.append(dict(rs=rs, q_in=(q_ref[rs, :].astype(F32) * jnp.exp(b) * scale).astype(BF16),
                           k_in=(k * jnp.exp(-b)).astype(BF16), k_st=k * jnp.exp(b_last - b),
                           dec=jnp.exp(b_last)))
    for ch in chunks:
        ch['v'] = [v_ref[ch['rs'], h * dv:(h + 1) * dv].astype(BF16) for h in range(GLA_HEADS)]
        zero16 = jnp.zeros((), BF16)
        ch['scores'] = [
            jnp.where(tril, lax.dot_general(jnp.where(lane_head == h, ch['q_in'], zero16), ch['k_in'], nt_dims,
                                            preferred_element_type=F32), 0.0).astype(BF16)
            for h in range(GLA_HEADS)]
    for ch in chunks:
        ch['o'] = [jnp.dot(ch['scores'][h], ch['v'][h], preferred_element_type=F32) for h in range(GLA_HEADS)]
        ch['d_state'] = [
            lax.dot_general(ch['v'][h], jnp.where(lane_head == h, ch['k_st'], 0.0).astype(BF16), tn_dims,
                            preferred_element_type=F32) for h in range(GLA_HEADS)]

    st = [state_ref[h] for h in range(GLA_HEADS)]
    for ch in chunks:
        ch['st'] = [s.astype(BF16) for s in st]
        st = [st[h] * ch['dec'] + ch['d_state'][h] for h in range(GLA_HEADS)]
    for h in range(GLA_HEADS):
        state_ref[h] = st[h]
    for ch in chunks:
        for h in range(GLA_HEADS):
            vs = slice(h * dv, (h + 1) * dv)
            o = ch['o'][h] + lax.dot_general(ch['q_in'], ch['st'][h], nt_dims, preferred_element_type=F32)
            on = o * lax.rsqrt(jnp.mean(o * o, axis=-1, keepdims=True) + NORM_EPS) * gain_ref[:, vs]
            g = gate_ref[ch['rs'], vs].astype(F32)
            o_ref[ch['rs'], vs] = (on * (g * _sigmoid(g))).astype(o_ref.dtype)


def _gla(qkvgu, loga, gain, bsz, seq, tb):
    t = bsz * seq
    dk = loga.shape[1] // GLA_HEADS
    dv = gain.shape[1] // GLA_HEADS
    nq, nv = GLA_HEADS * dk, GLA_HEADS * dv
    nt = seq // tb
    row = lambda b, i: b * nt + i
    kern = functools.partial(_gla_kernel, n_chunks=tb // GLA_CHUNK, dk=dk, dv=dv)
    return pl.pallas_call(
        kern,
        grid=(bsz, nt),
        in_specs=[pl.BlockSpec((tb, nq), lambda b, i: (row(b, i), 0)),
                  pl.BlockSpec((tb, nq), lambda b, i: (row(b, i), 1)),
                  pl.BlockSpec((tb, nv), lambda b, i: (row(b, i), 1)),
                  pl.BlockSpec((tb, nv), lambda b, i: (row(b, i), 2)),
                  pl.BlockSpec((tb, nq), lambda b, i: (row(b, i), 0)),
                  pl.BlockSpec(gain.shape, lambda b, i: (0, 0))],
        out_specs=pl.BlockSpec((tb, nv), lambda b, i: (row(b, i), 0)),
        out_shape=jax.ShapeDtypeStruct((t, nv), BF16),
        scratch_shapes=[pltpu.VMEM((GLA_HEADS, dv, nq), F32)],
        compiler_params=_cparams("parallel", "arbitrary"),
        name="gla",
    )(qkvgu, qkvgu, qkvgu, qkvgu, loga, gain)


def _transpose_lane_chunks(sets, ch):
    n = len(sets[0])
    chunk = lax.broadcasted_iota(jnp.int32, sets[0][0].shape, 1) // ch
    sets = [list(xs) for xs in sets]
    d = n // 2
    while d >= 1:
        low_half = (chunk & d) == 0
        for xs in sets:
            for r in range(n):
                if r & d:
                    continue
                lo, hi = xs[r], xs[r + d]
                xs[r] = jnp.where(low_half, lo, pltpu.roll(hi, d * ch, axis=1))
                xs[r + d] = jnp.where(low_half, pltpu.roll(lo, (n - d) * ch, axis=1), hi)
        d //= 2
    return sets


def _s5_kernel(u_ref, kgen_ref, ws_ref, wc_ref, a_ref, y_ref, uy_scr, s_scr, h_scr, *,
               n_pairs, bsz, seq, tm):
    n2 = 2 * S5_STATE
    tc, ch = S5_CHUNK, S5_GROUP
    lanes = u_ref.shape[1]
    n_grp = lanes // ch
    n_half = tc // n_grp
    kt = tm // tc
    n_tiles = seq // tm
    rows = n_tiles * kt * bsz

    def tile_rows(b, i, step):
        return pl.ds(pl.multiple_of(b * seq + i * tm + step * kt, kt), kt)

    def gather_tile(i, _):
        for b in range(bsz):
            sets = [[u_ref[tile_rows(b, i, half * n_grp + j8), :].astype(F32) for j8 in range(n_grp)]
                    for half in range(n_half)]
            for half, xs in enumerate(_transpose_lane_chunks(sets, ch)):
                for g, x in enumerate(xs):
                    uy_scr[g, half, i, pl.ds(b, kt, stride=bsz), :] = x
        return 0

    lax.fori_loop(0, n_tiles, gather_tile, 0)

    def group_body(g, _):
        _s5_group(g, kgen_ref, ws_ref, wc_ref, a_ref, uy_scr, s_scr, h_scr, n_pairs=n_pairs, bsz=bsz, rows=rows,
                  lanes=lanes, n_half=n_half, n_tiles=n_tiles, kt=kt)
        return 0

    lax.fori_loop(0, n_grp, group_body, 0)

    def scatter_tile(i, _):
        for b in range(bsz):
            sets = [[uy_scr[g, half, i, pl.ds(b, kt, stride=bsz), :] for g in range(n_grp)]
                    for half in range(n_half)]
            for half, ys in enumerate(_transpose_lane_chunks(sets, ch)):
                for j8, y in enumerate(ys):
                    y_ref[tile_rows(b, i, half * n_grp + j8), :] = y
        return 0

    lax.fori_loop(0, n_tiles, scatter_tile, 0)


def _s5_group(g, kgen_ref, ws_ref, wc_ref, a_ref, uy_scr, s_scr, h_scr, *, n_pairs, bsz, rows, lanes, n_half,
              n_tiles, kt):
    n2 = 2 * S5_STATE
    tc, ch = S5_CHUNK, S5_GROUP
    u = jnp.concatenate([uy_scr[g, hf].reshape(rows, lanes) for hf in range(n_half)], axis=1).astype(BF16)
    s_scr[...] = jnp.dot(u, ws_ref[g], preferred_element_type=F32)
    a1 = a_ref[g, 0:1, :]
    a2 = a_ref[g, 1:2, :]

    def body(j, carry):
        h, hs = carry
        r0 = pl.multiple_of(j * (2 * bsz), 2 * bsz)
        blk = s_scr[pl.ds(r0, 2 * bsz), :]
        h1 = a1 * h + a2 * hs + blk[0:bsz, 0:n2]
        hs1 = a1 * hs - a2 * h + blk[0:bsz, n2:2 * n2]
        h2 = a1 * h1 + a2 * hs1 + blk[bsz:2 * bsz, 0:n2]
        hs2 = a1 * hs1 - a2 * h1 + blk[bsz:2 * bsz, n2:2 * n2]
        h_scr[pl.ds(r0, 2 * bsz), :] = jnp.concatenate([h, h1], axis=0)
        return h2, hs2

    zero = jnp.zeros((bsz, n2), F32)
    lax.fori_loop(0, n_pairs, body, (zero, zero))
    kg = kgen_ref[g]
    lane_w = lax.broadcasted_iota(jnp.int32, kg.shape, 1)
    tz = jnp.concatenate([kg] + [jnp.where(lane_w >= ch * j, pltpu.roll(kg, ch * j, axis=1), 0.0)
                                 for j in range(1, tc)], axis=0).astype(BF16)
    yg = (jnp.dot(u, tz, preferred_element_type=F32)
          + jnp.dot(h_scr[...].astype(BF16), wc_ref[g], preferred_element_type=F32))
    for hf in range(n_half):
        uy_scr[g, hf] = yg[:, hf * lanes:(hf + 1) * lanes].reshape(n_tiles, kt * bsz, lanes)


def _s5(u_perm, kgen, ws, wc, a_pow, bsz, seq, tm):
    t, mw = u_perm.shape
    lanes = 128
    width = kgen.shape[2]
    g_per_blk = lanes // S5_GROUP
    assert 2 * bsz == V7X_SUBLANES, "the chunk scan walks two chunks per 8-row tile"
    rows = (seq // S5_CHUNK) * bsz
    n_pairs = rows // (2 * bsz)
    per_blk = lambda a: pl.BlockSpec((g_per_blk,) + a.shape[1:], lambda q: (q, 0, 0))
    blk = pl.BlockSpec((t, lanes), lambda q: (0, q))
    kern = functools.partial(_s5_kernel, n_pairs=n_pairs, bsz=bsz, seq=seq, tm=tm)
    kt = tm // S5_CHUNK
    return pl.pallas_call(
        kern,
        grid=(mw // lanes,),
        in_specs=[blk, per_blk(kgen), per_blk(ws), per_blk(wc), per_blk(a_pow)],
        out_specs=blk,
        out_shape=jax.ShapeDtypeStruct((t, mw), F32),
        scratch_shapes=[pltpu.VMEM((g_per_blk, width // lanes, seq // tm, kt * bsz, lanes), F32),
                        pltpu.VMEM((rows, ws.shape[2]), F32), pltpu.VMEM((rows, wc.shape[1]), F32)],
        compiler_params=_cparams("parallel"),
        name="s5",
    )(u_perm, kgen, ws, wc, a_pow)


def _s5_weights(lam_re, lam_im, log_step, b_re, b_im, c_re, c_im):
    tc = S5_CHUNK
    groups, n = lam_re.shape
    lr = jnp.minimum(lam_re.astype(F32), -1e-4)
    li = lam_im.astype(F32)
    delta = jnp.exp(log_step.astype(F32))[:, None]
    tau = jnp.arange(tc + 1, dtype=F32)
    mag = jnp.exp((lr * delta)[..., None] * tau)
    ang = (li * delta)[..., None] * tau
    pw_re, pw_im = mag * jnp.cos(ang), mag * jnp.sin(ang)
    num_re, num_im = pw_re[..., 1] - 1.0, pw_im[..., 1]
    den = lr * lr + li * li
    f_re = (num_re * lr + num_im * li) / den
    f_im = (num_im * lr - num_re * li) / den
    b_re, b_im = b_re.astype(F32), b_im.astype(F32)
    bb_re = f_re[..., None] * b_re - f_im[..., None] * b_im
    bb_im = f_re[..., None] * b_im + f_im[..., None] * b_re
    ct_re = jnp.transpose(c_re.astype(F32), (0, 2, 1))
    ct_im = jnp.transpose(c_im.astype(F32), (0, 2, 1))
    cp_re = ct_re[:, :, None, :] * pw_re[..., None] - ct_im[:, :, None, :] * pw_im[..., None]
    cp_im = ct_re[:, :, None, :] * pw_im[..., None] + ct_im[:, :, None, :] * pw_re[..., None]
    width = tc * S5_GROUP
    ein = functools.partial(jnp.einsum, precision=HIGHEST)
    kgen = (ein('gnd,gnx->gdx', bb_re, cp_re[:, :, :tc].reshape(groups, n, width))
            - ein('gnd,gnx->gdx', bb_im, cp_im[:, :, :tc].reshape(groups, n, width)))
    wc = jnp.concatenate([cp_re[:, :, 1:].reshape(groups, n, width),
                          -cp_im[:, :, 1:].reshape(groups, n, width)], axis=1)
    rev_re = jnp.transpose(pw_re[..., tc - 1::-1], (0, 2, 1))
    rev_im = jnp.transpose(pw_im[..., tc - 1::-1], (0, 2, 1))
    bt_re = jnp.transpose(bb_re, (0, 2, 1))
    bt_im = jnp.transpose(bb_im, (0, 2, 1))
    s_re = (rev_re[:, :, None, :] * bt_re[:, None] - rev_im[:, :, None, :] * bt_im[:, None]).reshape(groups, width, n)
    s_im = (rev_re[:, :, None, :] * bt_im[:, None] + rev_im[:, :, None, :] * bt_re[:, None]).reshape(groups, width, n)
    ws = jnp.concatenate([s_re, s_im, s_im, s_re], axis=-1)
    a_pow = jnp.stack([jnp.concatenate([pw_re[..., tc], pw_re[..., tc]], axis=-1),
                       jnp.concatenate([-pw_im[..., tc], pw_im[..., tc]], axis=-1)], axis=1)
    return kgen, ws, wc, a_pow


def _ab_out_kernel(h_ref, oa_ref, ys_ref, u_ref, d_ref, wglu_ref, bglu_ref, wo_ref, g_ref, perm_ref, out_ref):
    mw = oa_ref.shape[1]
    hi, lo = _split_bf16(ys_ref[...])
    y_ssm = (jnp.dot(perm_ref[...], hi, preferred_element_type=F32)
             + jnp.dot(perm_ref[...], lo, preferred_element_type=F32))
    y = y_ssm + d_ref[...] * u_ref[...].astype(F32)
    ob = _gelu_tanh(y) * _sigmoid(_bdot(y, wglu_ref[...]) + bglu_ref[...])
    mix = _bdot(oa_ref[...], wo_ref[0:mw, :]) + _bdot(ob, wo_ref[mw:2 * mw, :])
    out_ref[...] = h_ref[...] + _rms(mix, g_ref[...])


def _ab_out(h, o_a, y_ssm, qkvgu, d_skip, w_glu, b_glu, w_out, gain, perm_t, tm):
    t, d = h.shape
    mw = o_a.shape[1]
    full = lambda a: pl.BlockSpec(a.shape, lambda i: (0, 0))
    rows = lambda w, cb=0: pl.BlockSpec((tm, w), lambda i: (i, cb))
    return pl.pallas_call(
        _ab_out_kernel,
        grid=(t // tm,),
        in_specs=[rows(d), rows(mw), rows(mw), rows(mw, qkvgu.shape[1] // mw - 1), full(d_skip), full(w_glu),
                  full(b_glu), full(w_out), full(gain), full(perm_t)],
        out_specs=rows(d),
        out_shape=jax.ShapeDtypeStruct((t, d), F32),
        compiler_params=_cparams("parallel"),
        name="ab_out",
    )(h, o_a, y_ssm, qkvgu, d_skip, w_glu, b_glu, w_out, gain, perm_t)


def _mem_kv_kernel(mem_ref, g_ref, wk_ref, wv_ref, k_ref, v_ref):
    mn = _rms(mem_ref[...], g_ref[...]).astype(BF16)
    k_ref[...] = jnp.dot(mn, wk_ref[...], preferred_element_type=F32).astype(BF16)
    v_ref[...] = jnp.dot(mn, wv_ref[...], preferred_element_type=F32).astype(BF16)


def _mem_kv(mem, gain, wk, wv, tm):
    t, d = mem.shape
    full = lambda a: pl.BlockSpec(a.shape, lambda i: (0, 0))
    rows = pl.BlockSpec((tm, d), lambda i: (i, 0))
    return pl.pallas_call(
        _mem_kv_kernel,
        grid=(t // tm,),
        in_specs=[rows, full(gain), full(wk), full(wv)],
        out_specs=[rows, rows],
        out_shape=[jax.ShapeDtypeStruct((t, d), BF16), jax.ShapeDtypeStruct((t, d), BF16)],
        compiler_params=_cparams("parallel"),
        name="mem_kv",
    )(mem, gain, wk, wv)


def _xattn_kernel(h_ref, k_ref, v_ref, wq_ref, wo_ref, gq_ref, go_ref, out_ref, o_scr):
    tm, d = h_ref.shape
    hd = d // XA_HEADS
    n_sub = 2
    ts = tm // n_sub
    subs = [slice(i * ts, (i + 1) * ts) for i in range(n_sub)]
    xn = [_rms(h_ref[rs, :], gq_ref[...]).astype(BF16) for rs in subs]
    q = [jnp.dot(x, wq_ref[...], preferred_element_type=F32).astype(BF16) for x in xn]
    for hh in range(XA_HEADS):
        cs = slice(hh * hd, (hh + 1) * hd)
        for i, rs in enumerate(subs):
            s = lax.dot_general(q[i][:, cs], k_ref[:, cs], (((1,), (1,)), ((), ())),
                                preferred_element_type=F32) * (hd ** -0.5)
            e = jnp.exp(s - jnp.max(s, axis=-1, keepdims=True))
            p = e / jnp.sum(e, axis=-1, keepdims=True)
            o_scr[rs, cs] = _bdot(p, v_ref[:, cs]).astype(BF16)
    xa = [jnp.dot(o_scr[rs, :], wo_ref[...], preferred_element_type=F32) for rs in subs]
    for i, rs in enumerate(subs):
        out_ref[rs, :] = h_ref[rs, :] + _rms(xa[i], go_ref[...])


def _xattn(h, k, v, wq, wo, gq, go, bsz, seq, mem_len, tm):
    t, d = h.shape
    nt = seq // tm
    full = lambda a: pl.BlockSpec(a.shape, lambda b, i: (0, 0))
    rows = pl.BlockSpec((tm, d), lambda b, i: (b * nt + i, 0))
    kv = pl.BlockSpec((mem_len, d), lambda b, i: (b, 0))
    return pl.pallas_call(
        _xattn_kernel,
        grid=(bsz, nt),
        in_specs=[rows, kv, kv, full(wq), full(wo), full(gq), full(go)],
        out_specs=rows,
        out_shape=jax.ShapeDtypeStruct((t, d), F32),
        scratch_shapes=[pltpu.VMEM((tm, d), BF16)],
        compiler_params=_cparams("parallel", "parallel"),
        name="xattn",
    )(h, k, v, wq, wo, gq, go)


def _mlp_kernel(h_ref, gi_ref, go_ref, w1_ref, w2_ref, out_ref, *, tf):
    h = h_ref[...]
    xn = _rms(h, gi_ref[...]).astype(BF16)
    dff = w1_ref.shape[1]
    acts = []
    for j in range(dff // tf):
        a = jnp.maximum(jnp.dot(xn, w1_ref[:, j * tf:(j + 1) * tf], preferred_element_type=F32), 0.0)
        acts.append((a * a).astype(BF16))
    ff = jnp.dot(jnp.concatenate(acts, axis=1), w2_ref[...], preferred_element_type=F32)
    out_ref[...] = h + _rms(ff, go_ref[...])


def _mlp(h, gi, go, w1, w2, tm, tf):
    t, d = h.shape
    full = lambda a: pl.BlockSpec(a.shape, lambda i: (0, 0))
    resident = lambda a: pl.BlockSpec(a.shape, lambda i: (0, 0), pipeline_mode=pl.Buffered(1))
    rows = pl.BlockSpec((tm, d), lambda i: (i, 0))
    return pl.pallas_call(
        functools.partial(_mlp_kernel, tf=tf),
        grid=(t // tm,),
        in_specs=[rows, full(gi), full(go), resident(w1), resident(w2)],
        out_specs=rows,
        out_shape=jax.ShapeDtypeStruct((t, d), F32),
        compiler_params=_cparams("parallel"),
        name="mlp",
    )(h, gi, go, w1, w2)


def _cd_in_kernel(h_ref, g_ref, wbig_ref, wsm_ref, mub_ref, mus_ref, w0_ref, w2_ref, a0_ref, a2_ref, g2_ref,
                  kk_ref, ka_ref, hsum_ref, cw_ref, cb_ref, wa_ref, ba_ref, wx_ref, bx_ref, lam_ref,
                  r_o, lw_o, k_o, v_o, kk_o, a_o, g_o, la_o, lb_o, gate_o,
                  carry_b, carry_s, carry_x, *, mw, n_sub):
    tm = h_ref.shape[0]

    @pl.when(pl.program_id(1) == 0)
    def _():
        carry_b[...] = jnp.zeros_like(carry_b)
        carry_s[...] = jnp.zeros_like(carry_s)
        carry_x[...] = jnp.zeros_like(carry_x)

    ts = tm // n_sub
    proj = []
    for sb in range(n_sub):
        hn = _rms(h_ref[sb * ts:(sb + 1) * ts, :], g_ref[...]).astype(BF16)
        proj.append((jnp.dot(hn, wbig_ref[...], preferred_element_type=F32),
                     jnp.dot(hn, wsm_ref[...], preferred_element_type=F32)))

    for sb in range(n_sub):
        rs = slice(sb * ts, (sb + 1) * ts)
        pb, ps = proj[sb]
        p3 = pb[:, 0:3 * mw]
        xb = pb[:, 3 * mw:4 * mw]
        gate_o[rs, :] = pb[:, 4 * mw:5 * mw]

        prev3 = _shift_rows(p3, 1, carry_b[...])
        prevs = _shift_rows(ps, 1, carry_s[...])
        carry_b[...] = p3[ts - V7X_SUBLANES:ts, :]
        carry_s[...] = ps[ts - V7X_SUBLANES:ts, :]
        p3 = p3 + (prev3 - p3) * mub_ref[...]
        ps = ps + (prevs - ps) * mus_ref[...]
        r = p3[:, 0:mw]
        k = p3[:, mw:2 * mw]
        v = p3[:, 2 * mw:3 * mw]
        lora = ps[:, 0:128]
        wlog = -_softplus(-(w0_ref[...] + _bdot(jnp.tanh(lora), w2_ref[...]))) - 0.5
        a = _sigmoid(a0_ref[...] + _bdot(lora, a2_ref[...]))
        kkr = k * kk_ref[...]
        norm = jnp.sqrt(_bdot(kkr * kkr, hsum_ref[...]))
        r_o[rs, :] = r.astype(BF16)
        lw_o[rs, :] = -jnp.exp(wlog)
        k_o[rs, :] = (k * (1.0 + (a - 1.0) * ka_ref[...])).astype(BF16)
        v_o[rs, :] = v.astype(BF16)
        kk_o[rs, :] = (kkr / jnp.maximum(norm, 1e-12)).astype(BF16)
        a_o[rs, :] = a.astype(BF16)
        g_o[rs, :] = _bdot(_sigmoid(ps[:, 128:256]), g2_ref[...]).astype(BF16)

        cx = carry_x[...]
        xc = cb_ref[...] + cw_ref[LRU_CONV - 1:LRU_CONV, :] * xb
        for sh in range(1, LRU_CONV):
            xc = xc + cw_ref[LRU_CONV - 1 - sh:LRU_CONV - sh, :] * _shift_rows(xb, sh, cx)
        carry_x[...] = xb[ts - V7X_SUBLANES:ts, :]
        rg = _sigmoid(_bdot(xc, wa_ref[...]) + ba_ref[...])
        ig = _sigmoid(_bdot(xc, wx_ref[...]) + bx_ref[...])
        log_a = -LRU_C * rg * _softplus(-lam_ref[...])
        la_o[rs, :] = jnp.exp(log_a)
        th = jnp.tanh(log_a)
        lb_o[rs, :] = jnp.sqrt(-2.0 * th / (1.0 - th)) * (ig * xc)


def _cd_in(h, gain, wbig, wsm, vecs, mats, bsz, seq, tm, mw):
    t, d = h.shape
    nt = seq // tm
    full = lambda a: pl.BlockSpec(a.shape, lambda b, i: (0, 0))
    rows = lambda w: pl.BlockSpec((tm, w), lambda b, i: (b * nt + i, 0))
    (mub, mus, w0, a0, kk_w, ka_w, cw, cb, ba, bx, lam) = vecs
    (w2p, a2p, g2, hsum, wa, wx) = mats
    args = (h, gain, wbig, wsm, mub, mus, w0, w2p, a0, a2p, g2, kk_w, ka_w, hsum, cw, cb, wa, ba, wx, bx, lam)
    kern = functools.partial(_cd_in_kernel, mw=mw, n_sub=4)
    return pl.pallas_call(
        kern,
        grid=(bsz, nt),
        in_specs=[rows(d)] + [full(a) for a in args[1:]],
        out_specs=[rows(mw)] * 10,
        out_shape=[jax.ShapeDtypeStruct((t, mw), dt) for dt in CD_IN_OUT_DTYPES],
        scratch_shapes=[pltpu.VMEM((V7X_SUBLANES, 3 * mw), F32), pltpu.VMEM((V7X_SUBLANES, wsm.shape[1]), F32),
                        pltpu.VMEM((V7X_SUBLANES, mw), F32)],
        compiler_params=_cparams("parallel", "arbitrary"),
        name="cd_in",
    )(*args)


def _rwkv_kernel(r_ref, lw_ref, k_ref, v_ref, kk_ref, a_ref, g_ref, rk_ref, lng_ref, lnb_ref, hsum_ref,
                 o_ref, state_ref, y_scr, *, slab_heads, n_chunks):
    c = RWKV_CHUNK
    hd = RWKV_HEAD
    assert c == hd, "one block mask serves both the (head, s) and the (head, d) layouts"
    sw = slab_heads * hd
    n_slabs = r_ref.shape[1] // sw

    @pl.when(pl.program_id(1) == 0)
    def _():
        state_ref[...] = jnp.zeros_like(state_ref)

    tb = n_chunks * c
    tri = lax.broadcasted_iota(jnp.int32, (tb, tb), 0)
    tci = lax.broadcasted_iota(jnp.int32, (tb, tb), 1)
    blk_tril = jnp.where(((tri // c) == (tci // c)) & (tci <= tri), 1.0, 0.0).astype(BF16)
    cum_all = _split3_dot(blk_tril, lw_ref[...])
    blk_m = (lax.broadcasted_iota(jnp.int32, (sw, sw), 0) // c) == (lax.broadcasted_iota(jnp.int32, (sw, sw), 1) // c)
    wide_t = lax.broadcasted_iota(jnp.int32, (c, sw), 0)
    wide_s = lax.broadcasted_iota(jnp.int32, (c, sw), 1) % c
    strict_w = wide_s < wide_t
    incl_w = wide_s <= wide_t
    eye_w = jnp.where(wide_s == wide_t, 1.0, 0.0)

    def bdiag(x):
        x16 = x.astype(BF16)
        return jnp.where(blk_m, jnp.concatenate([x16] * slab_heads, axis=0), jnp.zeros((), BF16))

    def mm(a, w16):
        return jnp.dot(a.astype(BF16), w16, preferred_element_type=F32)

    nt_dims = (((1,), (1,)), ((), ()))
    chains = []
    for chunk in range(n_chunks):
        rs = slice(chunk * c, (chunk + 1) * c)
        lw = lw_ref[rs, :]
        cum = cum_all[rs, :]
        cum_last = cum[c - 1:c, :]
        r = r_ref[rs, :].astype(F32)
        k = k_ref[rs, :].astype(F32)
        v = v_ref[rs, :].astype(F32)
        kk = kk_ref[rs, :].astype(F32)
        bvec = kk * a_ref[rs, :].astype(F32)
        inv_g = jnp.exp(-cum)
        to_end = jnp.exp(cum_last - cum)
        rt = r * jnp.exp(cum)
        kp = kk * jnp.exp(cum - lw)
        be = bvec * inv_g
        kh = k * inv_g
        bb = bvec * to_end
        kb = k * to_end
        g_end = jnp.exp(cum_last)
        for s in range(n_slabs):
            ls = slice(s * sw, (s + 1) * sw)
            chains.append(dict(rs=rs, ls=ls, slab=s, rt=rt[:, ls], kp=kp[:, ls], v=v[:, ls], be=be[:, ls],
                               kh=kh[:, ls], kb=kb[:, ls], bb=bb[:, ls], g_end=g_end[:, ls]))

    for ch in chains:
        lhs2 = jnp.concatenate([ch['kp'], ch['rt']], axis=0).astype(BF16)
        ab = lax.dot_general(lhs2, bdiag(ch['be']), nt_dims, preferred_element_type=F32)
        ak = lax.dot_general(lhs2, bdiag(ch['kh']), nt_dims, preferred_element_type=F32)
        ch['a_kb'] = jnp.where(strict_w, ab[0:c], 0.0)
        ch['a_rb'] = jnp.where(incl_w, ab[c:2 * c], 0.0).astype(BF16)
        ch['a_kr'] = jnp.concatenate([jnp.where(strict_w, ak[0:c], 0.0), jnp.where(incl_w, ak[c:2 * c], 0.0)],
                                     axis=0).astype(BF16)
    for ch in chains:
        x = -ch['a_kb']
        ch['t'] = eye_w + x
        ch['x'] = mm(x, bdiag(x))
    for _ in range(int(math.log2(c)) - 2):
        for ch in chains:
            res = mm(jnp.concatenate([ch['t'], ch['x']], axis=0), bdiag(ch['x']))
            ch['t'] = ch['t'] + res[0:c]
            ch['x'] = res[c:2 * c]
    for ch in chains:
        ch['t'] = ch['t'] + mm(ch['t'], bdiag(ch['x']))
    for ch in chains:
        l_hi, l_lo = _split_bf16(ch['a_kb'])
        t_hi, t_lo = _split_bf16(ch['t'])
        lt = mm(jnp.concatenate([l_hi, l_lo], axis=0), bdiag(t_hi))
        ch['resid'] = eye_w - ch['t'] - (lt[0:c] + lt[c:2 * c] + mm(l_hi, bdiag(t_lo)))
    for ch in chains:
        ch['t16'] = (ch['t'] + mm(ch['t'], bdiag(ch['resid']))).astype(BF16)
    for ch in chains:
        ch['w_tok'] = mm(ch['t16'], bdiag(ch['kp']))
        ch['av'] = mm(ch['a_kr'], bdiag(ch['v']))
    for ch in chains:
        ch['u_tok'] = mm(ch['t16'], bdiag(ch['av'][0:c]))
    tn_dims = (((0,), (0,)), ((), ()))
    for ch in chains:
        ch['q16'] = (ch['rt'] - mm(ch['a_rb'], bdiag(ch['w_tok']))).astype(BF16)
        ch['y0'] = ch['av'][c:2 * c] - mm(ch['a_rb'], bdiag(ch['u_tok']))
    for ch in chains:
        bb16 = ch['bb'].astype(BF16)
        wb = lax.dot_general(ch['w_tok'].astype(BF16), bb16, tn_dims, preferred_element_type=F32)
        ch['wb'] = jnp.where(blk_m, wb, 0.0).astype(BF16)
        d_t = lax.dot_general(jnp.concatenate([ch['v'], -ch['u_tok']], axis=0).astype(BF16),
                              jnp.concatenate([ch['kb'].astype(BF16), bb16], axis=0), tn_dims,
                              preferred_element_type=F32)
        ch['d_t'] = jnp.where(blk_m, d_t, 0.0)

    for ch in chains:
        s = ch['slab']
        p_t = state_ref[s]
        ch['p16'] = p_t.astype(BF16)
        state_ref[s] = p_t * ch['g_end'] - jnp.dot(ch['p16'], ch['wb'], preferred_element_type=F32) + ch['d_t']
    for ch in chains:
        y_scr[ch['rs'], ch['ls']] = ch['y0'] + lax.dot_general(ch['q16'], ch['p16'], nt_dims,
                                                               preferred_element_type=F32)

    hsum = hsum_ref[...]
    y = y_scr[...]
    r = r_ref[...].astype(F32)
    v = v_ref[...].astype(F32)
    mean = _split_dot(y, hsum) * (1.0 / hd)
    yc = y - mean
    var = _bdot(yc * yc, hsum) * (1.0 / hd)
    yn = yc * lax.rsqrt(var + RWKV_GN_EPS) * lng_ref[...] + lnb_ref[...]
    bonus = _bdot(r * k_ref[...].astype(F32) * rk_ref[...], hsum) * v
    o_ref[...] = ((yn + bonus) * g_ref[...].astype(F32)).astype(o_ref.dtype)


def _rwkv(r, lw, k, v, kk, a, g, rk, lng, lnb, hsum, bsz, seq, slab_heads=4, n_chunks=4):
    t, mw = r.shape
    c = RWKV_CHUNK * n_chunks
    nt = seq // c
    sw = slab_heads * RWKV_HEAD
    full = lambda x: pl.BlockSpec(x.shape, lambda b, i: (0, 0))
    rows = pl.BlockSpec((c, mw), lambda b, i: (b * nt + i, 0))
    kern = functools.partial(_rwkv_kernel, slab_heads=slab_heads, n_chunks=n_chunks)
    return pl.pallas_call(
        kern,
        grid=(bsz, nt),
        in_specs=[rows] * 7 + [full(rk), full(lng), full(lnb), full(hsum)],
        out_specs=rows,
        out_shape=jax.ShapeDtypeStruct((t, mw), BF16),
        scratch_shapes=[pltpu.VMEM((mw // sw, sw, sw), F32), pltpu.VMEM((c, mw), F32)],
        compiler_params=_cparams("parallel", "arbitrary"),
        name="rwkv",
    )(r, lw, k, v, kk, a, g, rk, lng, lnb, hsum)


def _lru_kernel(a_ref, b_ref, gate_ref, o_ref, carry_ref):
    tm = a_ref.shape[0]
    sub = V7X_SUBLANES

    @pl.when(pl.program_id(1) == 0)
    def _():
        carry_ref[...] = jnp.zeros_like(carry_ref)

    rowi = lax.broadcasted_iota(jnp.int32, (sub, a_ref.shape[1]), 0)

    def scan8(a, b, h):
        sh = 1
        while sh < sub:
            a_sh = jnp.where(rowi >= sh, pltpu.roll(a, sh, axis=0), 1.0)
            b_sh = jnp.where(rowi >= sh, pltpu.roll(b, sh, axis=0), 0.0)
            b = b + a * b_sh
            a = a * a_sh
            sh *= 2
        return b + a * h

    def body(j, h):
        r0 = pl.multiple_of(j * (2 * sub), 2 * sub)
        a = a_ref[pl.ds(r0, 2 * sub), :]
        b = b_ref[pl.ds(r0, 2 * sub), :]
        h1 = scan8(a[0:sub], b[0:sub], h)
        h2 = scan8(a[sub:2 * sub], b[sub:2 * sub], h1[sub - 1:sub, :])
        gate = gate_ref[pl.ds(r0, 2 * sub), :].astype(F32)
        o_ref[pl.ds(r0, 2 * sub), :] = (jnp.concatenate([h1, h2], axis=0) * _gelu_tanh(gate)).astype(o_ref.dtype)
        return h2[sub - 1:sub, :]

    carry_ref[0:1, :] = lax.fori_loop(0, tm // (2 * sub), body, carry_ref[0:1, :])


def _lru(a, b, gate, bsz, seq, tm):
    t, mw = a.shape
    nt = seq // tm
    rows = pl.BlockSpec((tm, mw), lambda bi, i: (bi * nt + i, 0))
    return pl.pallas_call(
        _lru_kernel,
        grid=(bsz, nt),
        in_specs=[rows, rows, rows],
        out_specs=rows,
        out_shape=jax.ShapeDtypeStruct((t, mw), BF16),
        scratch_shapes=[pltpu.VMEM((V7X_SUBLANES, mw), F32)],
        compiler_params=_cparams("parallel", "arbitrary"),
        name="lru",
    )(a, b, gate)


def _cd_out_kernel(h_ref, oc_ref, od_ref, wo_ref, g_ref, out_ref):
    mw = oc_ref.shape[1]
    mix = _bdot(oc_ref[...], wo_ref[0:mw, :]) + _bdot(od_ref[...], wo_ref[mw:2 * mw, :])
    out_ref[...] = h_ref[...] + _rms(mix, g_ref[...])


def _cd_out(h, o_c, o_d, w_out, gain, tm):
    t, d = h.shape
    mw = o_c.shape[1]
    full = lambda a: pl.BlockSpec(a.shape, lambda i: (0, 0))
    rows = lambda w: pl.BlockSpec((tm, w), lambda i: (i, 0))
    return pl.pallas_call(
        _cd_out_kernel,
        grid=(t // tm,),
        in_specs=[rows(d), rows(mw), rows(mw), full(w_out), full(gain)],
        out_specs=rows(d),
        out_shape=jax.ShapeDtypeStruct((t, d), F32),
        compiler_params=_cparams("parallel"),
        name="cd_out",
    )(h, o_c, o_d, w_out, gain)


def _row(vec):
    return vec.astype(F32).reshape(1, -1)


def _block_diag(blocks):
    nb, bi, bo = blocks.shape
    eye = jnp.eye(nb, dtype=blocks.dtype)
    return (eye[:, None, :, None] * blocks[:, :, None, :]).reshape(nb * bi, nb * bo)


def _head_sum_matrix(width, head):
    idx = jnp.arange(width) // head
    return (idx[:, None] == idx[None, :]).astype(BF16)


def _layer_ab(h, gains, p, bsz, seq, tm):
    d = h.shape[1]
    mw = d // 2
    dk = p['gla_b_decay'].shape[0] // GLA_HEADS
    nq = GLA_HEADS * dk
    w_in = p['ab_w_in']
    rank = p['gla_w_decay2'].shape[0]
    o_dlr = 2 * nq + 2 * mw
    w_main = jnp.concatenate([w_in[:, :o_dlr], w_in[:, o_dlr + rank:]], axis=1).astype(BF16)
    w_dlr = jnp.pad(w_in[:, o_dlr:o_dlr + rank], ((0, 0), (0, 128 - rank))).astype(BF16)
    w_decay2 = jnp.pad(p['gla_w_decay2'].astype(F32), ((0, 128 - rank), (0, 0)))
    perm = _chunk_step_perm(tm, S5_CHUNK)
    qkvgu, loga, u_perm = _ab_in(h, _row(gains[0]), w_main, w_dlr, w_decay2, _row(p['gla_b_decay']), perm, tm, mw)
    o_a = _gla(qkvgu, loga, _row(p['gla_norm_gain']), bsz, seq, tb=256)
    kgen, ws, wc, a_pow = _s5_weights(p['s5_lambda_re'], p['s5_lambda_im'], p['s5_log_step'], p['s5_b_re'],
                                      p['s5_b_im'], p['s5_c_re'], p['s5_c_im'])
    y_perm = _s5(u_perm, kgen, ws.astype(BF16), wc.astype(BF16), a_pow, bsz, seq, tm)
    return _ab_out(h, o_a, y_perm, qkvgu, _row(p['s5_d']), p['s5_w_glu'].astype(BF16), _row(p['s5_b_glu']),
                   p['ab_w_out'].astype(BF16), _row(gains[1]), perm.T, tm)


def _layer_cd(h, gains, p, bsz, seq, tm):
    d = h.shape[1]
    mw = d // 2
    w_in = p['cd_w_in']
    dr = p['rwkv_w2'].shape[0]
    ar = p['rwkv_a2'].shape[0]
    gr = p['rwkv_g2'].shape[0]
    assert dr + ar == 128 and gr == 128
    o = [0, mw, mw + dr, 2 * mw + dr, 3 * mw + dr, 3 * mw + dr + ar, 3 * mw + dr + ar + gr]
    col = lambda i, width: w_in[:, o[i]:o[i] + width]
    wbig = jnp.concatenate([col(0, mw), col(2, mw), col(3, mw), w_in[:, o[6]:]], axis=1).astype(BF16)
    wsm = jnp.concatenate([col(1, dr), col(4, ar), col(5, gr)], axis=1).astype(BF16)
    mu = p['rwkv_mu'].astype(F32)
    mseg = lambda i, width: mu[o[i]:o[i] + width]
    mub = jnp.concatenate([mseg(0, mw), mseg(2, mw), mseg(3, mw)]).reshape(1, -1)
    mus = jnp.concatenate([mseg(1, dr), mseg(4, ar), mseg(5, gr)]).reshape(1, -1)
    w2p = jnp.pad(p['rwkv_w2'], ((0, ar), (0, 0))).astype(BF16)
    a2p = jnp.pad(p['rwkv_a2'], ((dr, 0), (0, 0))).astype(BF16)
    hsum = _head_sum_matrix(mw, RWKV_HEAD)
    vecs = (mub, mus, _row(p['rwkv_w0']), _row(p['rwkv_a0']), _row(p['rwkv_k_k']), _row(p['rwkv_k_a']),
            p['lru_conv_w'].astype(F32), _row(p['lru_conv_b']), _row(p['lru_b_a']), _row(p['lru_b_x']),
            _row(p['lru_lambda']))
    mats = (w2p, a2p, p['rwkv_g2'].astype(BF16), hsum, _block_diag(p['lru_w_a']).astype(BF16),
            _block_diag(p['lru_w_x']).astype(BF16))
    r, lw, k, v, kk, a, g, la, lb, gate = _cd_in(h, _row(gains[0]), wbig, wsm, vecs, mats, bsz, seq, tm, mw)
    o_c = _rwkv(r, lw, k, v, kk, a, g, _row(p['rwkv_r_k']), _row(p['rwkv_ln_gain']), _row(p['rwkv_ln_bias']),
                hsum, bsz, seq)
    o_d = _lru(la, lb, gate, bsz, seq, tm)
    return _cd_out(h, o_c, o_d, p['cd_w_out'].astype(BF16), _row(gains[1]), tm)


def kernel(x, mem, norm_gain, xa_wq, xa_wk, xa_wv, xa_wo, mlp_w1, mlp_w2, ab_w_in, gla_w_decay2, gla_b_decay, gla_norm_gain, s5_lambda_re, s5_lambda_im, s5_log_step, s5_b_re, s5_b_im, s5_c_re, s5_c_im, s5_d, s5_w_glu, s5_b_glu, ab_w_out, cd_w_in, rwkv_mu, rwkv_w0, rwkv_w2, rwkv_a0, rwkv_a2, rwkv_g2, rwkv_k_k, rwkv_k_a, rwkv_r_k, rwkv_ln_gain, rwkv_ln_bias, lru_conv_w, lru_conv_b, lru_w_a, lru_b_a, lru_w_x, lru_b_x, lru_lambda, cd_w_out):
    bsz, seq, d = x.shape
    mem_len = mem.shape[1]
    depth = norm_gain.shape[0]
    tm = min(512, seq)
    ab = dict(ab_w_in=ab_w_in, gla_w_decay2=gla_w_decay2, gla_b_decay=gla_b_decay, gla_norm_gain=gla_norm_gain,
              s5_lambda_re=s5_lambda_re, s5_lambda_im=s5_lambda_im, s5_log_step=s5_log_step, s5_b_re=s5_b_re,
              s5_b_im=s5_b_im, s5_c_re=s5_c_re, s5_c_im=s5_c_im, s5_d=s5_d, s5_w_glu=s5_w_glu, s5_b_glu=s5_b_glu,
              ab_w_out=ab_w_out)
    cd = dict(cd_w_in=cd_w_in, rwkv_mu=rwkv_mu, rwkv_w0=rwkv_w0, rwkv_w2=rwkv_w2, rwkv_a0=rwkv_a0,
              rwkv_a2=rwkv_a2, rwkv_g2=rwkv_g2, rwkv_k_k=rwkv_k_k, rwkv_k_a=rwkv_k_a, rwkv_r_k=rwkv_r_k,
              rwkv_ln_gain=rwkv_ln_gain, rwkv_ln_bias=rwkv_ln_bias, lru_conv_w=lru_conv_w, lru_conv_b=lru_conv_b,
              lru_w_a=lru_w_a, lru_b_a=lru_b_a, lru_w_x=lru_w_x, lru_b_x=lru_b_x, lru_lambda=lru_lambda,
              cd_w_out=cd_w_out)
    h = x.astype(F32).reshape(bsz * seq, d)
    mem2 = mem.astype(F32).reshape(bsz * mem_len, d)
    for layer in range(depth):
        g = norm_gain[layer]
        i = layer // 2
        if layer % 2 == 0:
            h = _layer_ab(h, g, {n: w[i] for n, w in ab.items()}, bsz, seq, tm)
        else:
            h = _layer_cd(h, g, {n: w[i] for n, w in cd.items()}, bsz, seq, tm)
        km, vm = _mem_kv(mem2, _row(g[6]), xa_wk[layer].astype(BF16), xa_wv[layer].astype(BF16), mem_len)
        h = _xattn(h, km, vm, xa_wq[layer].astype(BF16), xa_wo[layer].astype(BF16), _row(g[2]), _row(g[3]),
                   bsz, seq, mem_len, tm)
        h = _mlp(h, _row(g[4]), _row(g[5]), mlp_w1[layer].astype(BF16), mlp_w2[layer].astype(BF16),
                 tm=tm, tf=1024)
    return h.reshape(bsz, seq, d).astype(x.dtype)
```

```python
import functools
import math

import jax
import jax.numpy as jnp
from jax import lax
from jax.experimental import pallas as pl
from jax.experimental.pallas import tpu as pltpu

F32 = jnp.float32
BF16 = jnp.bfloat16
HIGHEST = lax.Precision.HIGHEST

NORM_EPS = 1e-6
GLA_HEADS = 4
GLA_TAU = 16.0
GLA_CHUNK = 64
S5_GROUP = 16
S5_STATE = 64
S5_CHUNK = 16
RWKV_HEAD = 64
RWKV_CHUNK = 64
RWKV_GN_EPS = 64e-5
LRU_CONV = 4
LRU_C = 8.0
XA_HEADS = 4

V7X_SUBLANES = 8
CD_IN_OUT_DTYPES = (BF16, F32, BF16, BF16, BF16, BF16, BF16, F32, F32, F32)
VMEM_LIMIT_BYTES = 48 * 1024 * 1024


def _cparams(*semantics):
    return pltpu.CompilerParams(dimension_semantics=semantics, vmem_limit_bytes=VMEM_LIMIT_BYTES)


def _rms(x, gain):
    return x * lax.rsqrt(jnp.mean(x * x, axis=-1, keepdims=True) + NORM_EPS) * gain


def _bdot(a, b):
    return jnp.dot(a.astype(BF16), b.astype(BF16), preferred_element_type=F32)


def _bdot_nt(a, b):
    return lax.dot_general(a.astype(BF16), b.astype(BF16), (((1,), (1,)), ((), ())),
                           preferred_element_type=F32)


def _bdot_tn(a, b):
    return lax.dot_general(a.astype(BF16), b.astype(BF16), (((0,), (0,)), ((), ())),
                           preferred_element_type=F32)


def _hdot(a, b):
    return jnp.dot(a, b, precision=HIGHEST, preferred_element_type=F32)


def _hdot_nt(a, b):
    return lax.dot_general(a, b, (((1,), (1,)), ((), ())), precision=HIGHEST,
                           preferred_element_type=F32)


def _hdot_tn(a, b):
    return lax.dot_general(a, b, (((0,), (0,)), ((), ())), precision=HIGHEST,
                           preferred_element_type=F32)


def _split_bf16(x):
    hi = x.astype(BF16)
    return hi, (x - hi.astype(F32)).astype(BF16)


def _split_dot(x, w_bf16):
    hi, lo = _split_bf16(x)
    return (jnp.dot(hi, w_bf16, preferred_element_type=F32)
            + jnp.dot(lo, w_bf16, preferred_element_type=F32))


def _split3_dot(w_bf16, x):
    hi = x.astype(BF16)
    r1 = x - hi.astype(F32)
    mid = r1.astype(BF16)
    lo = (r1 - mid.astype(F32)).astype(BF16)
    return (jnp.dot(w_bf16, hi, preferred_element_type=F32) + jnp.dot(w_bf16, mid, preferred_element_type=F32)
            + jnp.dot(w_bf16, lo, preferred_element_type=F32))


def _sigmoid(x):
    return 1.0 / (1.0 + jnp.exp(-x))


def _softplus(x):
    return jnp.maximum(x, 0.0) + jnp.log(1.0 + jnp.exp(-jnp.abs(x)))


def _gelu_tanh(x):
    c = math.sqrt(2.0 / math.pi)
    return x * (0.5 * (1.0 + jnp.tanh(c * (x + 0.044715 * (x * x * x)))))


def _tril_mask(n, strict=False):
    row = lax.broadcasted_iota(jnp.int32, (n, n), 0)
    col = lax.broadcasted_iota(jnp.int32, (n, n), 1)
    return (col < row) if strict else (col <= row)


def _shift_rows(x, shift, carry):
    rolled = pltpu.roll(x, shift, axis=0)
    crolled = pltpu.roll(carry, shift, axis=0)
    rowi = lax.broadcasted_iota(jnp.int32, carry.shape, 0)
    first = jnp.where(rowi < shift, crolled, rolled[0:V7X_SUBLANES])
    return jnp.concatenate([first, rolled[V7X_SUBLANES:]], axis=0)


def _ab_in_kernel(h_ref, g_ref, w_ref, wd_ref, wd2_ref, bd_ref, perm_ref, out_ref, loga_ref, up_ref):
    hn = _rms(h_ref[...], g_ref[...]).astype(BF16)
    out = jnp.dot(hn, w_ref[...], preferred_element_type=F32)
    out_ref[...] = out.astype(BF16)
    dlr = jnp.dot(hn, wd_ref[...], preferred_element_type=F32)
    z = _hdot(dlr, wd2_ref[...]) + bd_ref[...]
    loga_ref[...] = -_softplus(-z) * (1.0 / GLA_TAU)
    mw = up_ref.shape[1]
    u16 = out[:, out.shape[1] - mw:].astype(BF16)
    up_ref[...] = jnp.dot(perm_ref[...], u16, preferred_element_type=F32).astype(BF16)


def _ab_in(h, gain, w_main, w_dlr, w_decay2, b_decay, perm, tm, mw):
    t, d = h.shape
    n_main = w_main.shape[1]
    n_dk = w_decay2.shape[1]
    full = lambda a: pl.BlockSpec(a.shape, lambda i: (0, 0))
    rows = lambda w: pl.BlockSpec((tm, w), lambda i: (i, 0))
    return pl.pallas_call(
        _ab_in_kernel,
        grid=(t // tm,),
        in_specs=[rows(d), full(gain), full(w_main), full(w_dlr), full(w_decay2), full(b_decay), full(perm)],
        out_specs=[rows(n_main), rows(n_dk), rows(mw)],
        out_shape=[jax.ShapeDtypeStruct((t, n_main), BF16), jax.ShapeDtypeStruct((t, n_dk), F32),
                   jax.ShapeDtypeStruct((t, mw), BF16)],
        compiler_params=_cparams("parallel"),
        name="ab_in",
    )(h, gain, w_main, w_dlr, w_decay2, b_decay, perm)


def _chunk_step_perm(tm, tc):
    dst = jnp.arange(tm)
    src = (dst % (tm // tc)) * tc + dst // (tm // tc)
    return (src[:, None] == jnp.arange(tm)[None, :]).astype(BF16)


def _gla_kernel(q_ref, k_ref, v_ref, gate_ref, la_ref, gain_ref, o_ref, state_ref, *, n_chunks, dk, dv):
    c = GLA_CHUNK

    @pl.when(pl.program_id(1) == 0)
    def _():
        state_ref[...] = jnp.zeros_like(state_ref)

    tb = n_chunks * c
    tril = _tril_mask(c)
    scale = dk ** -0.5
    ri = lax.broadcasted_iota(jnp.int32, (tb, tb), 0)
    ci = lax.broadcasted_iota(jnp.int32, (tb, tb), 1)
    blk_tril = jnp.where(((ri // c) == (ci // c)) & (ci <= ri), 1.0, 0.0).astype(BF16)
    b_all = _split3_dot(blk_tril, la_ref[...])
    lane_head = lax.broadcasted_iota(jnp.int32, (c, GLA_HEADS * dk), 1) // dk
    nt_dims = (((1,), (1,)), ((), ()))
    tn_dims = (((0,), (0,)), ((), ()))

    chunks = []
    for i in range(n_chunks):
        rs = slice(i * c, (i + 1) * c)
        b = b_all[rs, :]
        b_last = b[c - 1:c, :]
        k = k_ref[rs, :].astype(F32)
        chunks.append(dict(rs=rs, q_in=(q_ref[rs, :].astype(F32) * jnp.exp(b) * scale).astype(BF16),
                           k_in=(k * jnp.exp(-b)).astype(BF16), k_st=k * jnp.exp(b_last - b),
                           dec=jnp.exp(b_last)))
    for ch in chunks:
        ch['v'] = [v_ref[ch['rs'], h * dv:(h + 1) * dv].astype(BF16) for h in range(GLA_HEADS)]
        zero16 = jnp.zeros((), BF16)
        ch['scores'] = [
            jnp.where(tril, lax.dot_general(jnp.where(lane_head == h, ch['q_in'], zero16), ch['k_in'], nt_dims,
                                            preferred_element_type=F32), 0.0).astype(BF16)
            for h in range(GLA_HEADS)]
    for ch in chunks:
        ch['o'] = [jnp.dot(ch['scores'][h], ch['v'][h], preferred_element_type=F32) for h in range(GLA_HEADS)]
        ch['d_state'] = [
            lax.dot_general(ch['v'][h], jnp.where(lane_head == h, ch['k_st'], 0.0).astype(BF16), tn_dims,
                            preferred_element_type=F32) for h in range(GLA_HEADS)]

    st = [state_ref[h] for h in range(GLA_HEADS)]
    for ch in chunks:
        ch['st'] = [s.astype(BF16) for s in st]
        st = [st[h] * ch['dec'] + ch['d_state'][h] for h in range(GLA_HEADS)]
    for h in range(GLA_HEADS):
        state_ref[h] = st[h]
    for ch in chunks:
        for h in range(GLA_HEADS):
            vs = slice(h * dv, (h + 1) * dv)
            o = ch['o'][h] + lax.dot_general(ch['q_in'], ch['st'][h], nt_dims, preferred_element_type=F32)
            on = o * lax.rsqrt(jnp.mean(o * o, axis=-1, keepdims=True) + NORM_EPS) * gain_ref[:, vs]
            g = gate_ref[ch['rs'], vs].astype(F32)
            o_ref[ch['rs'], vs] = (on * (g * _sigmoid(g))).astype(o_ref.dtype)


def _gla(qkvgu, loga, gain, bsz, seq, tb):
    t = bsz * seq
    dk = loga.shape[1] // GLA_HEADS
    dv = gain.shape[1] // GLA_HEADS
    nq, nv = GLA_HEADS * dk, GLA_HEADS * dv
    nt = seq // tb
    row = lambda b, i: b * nt + i
    kern = functools.partial(_gla_kernel, n_chunks=tb // GLA_CHUNK, dk=dk, dv=dv)
    return pl.pallas_call(
        kern,
        grid=(bsz, nt),
        in_specs=[pl.BlockSpec((tb, nq), lambda b, i: (row(b, i), 0)),
                  pl.BlockSpec((tb, nq), lambda b, i: (row(b, i), 1)),
                  pl.BlockSpec((tb, nv), lambda b, i: (row(b, i), 1)),
                  pl.BlockSpec((tb, nv), lambda b, i: (row(b, i), 2)),
                  pl.BlockSpec((tb, nq), lambda b, i: (row(b, i), 0)),
                  pl.BlockSpec(gain.shape, lambda b, i: (0, 0))],
        out_specs=pl.BlockSpec((tb, nv), lambda b, i: (row(b, i), 0)),
        out_shape=jax.ShapeDtypeStruct((t, nv), BF16),
        scratch_shapes=[pltpu.VMEM((GLA_HEADS, dv, nq), F32)],
        compiler_params=_cparams("parallel", "arbitrary"),
        name="gla",
    )(qkvgu, qkvgu, qkvgu, qkvgu, loga, gain)


def _transpose_lane_chunks(sets, ch):
    n = len(sets[0])
    chunk = lax.broadcasted_iota(jnp.int32, sets[0][0].shape, 1) // ch
    sets = [list(xs) for xs in sets]
    d = n // 2
    while d >= 1:
        low_half = (chunk & d) == 0
        for xs in sets:
            for r in range(n):
                if r & d:
                    continue
                lo, hi = xs[r], xs[r + d]
                xs[r] = jnp.where(low_half, lo, pltpu.roll(hi, d * ch, axis=1))
                xs[r + d] = jnp.where(low_half, pltpu.roll(lo, (n - d) * ch, axis=1), hi)
        d //= 2
    return sets


def _s5_kernel(u_ref, kgen_ref, ws_ref, wc_ref, a_ref, y_ref, uy_scr, s_scr, h_scr, *,
               n_pairs, bsz, seq, tm):
    n2 = 2 * S5_STATE
    tc, ch = S5_CHUNK, S5_GROUP
    lanes = u_ref.shape[1]
    n_grp = lanes // ch
    n_half = tc // n_grp
    kt = tm // tc
    n_tiles = seq // tm
    rows = n_tiles * kt * bsz

    def tile_rows(b, i, step):
        return pl.ds(pl.multiple_of(b * seq + i * tm + step * kt, kt), kt)

    def gather_tile(i, _):
        for b in range(bsz):
            sets = [[u_ref[tile_rows(b, i, half * n_grp + j8), :].astype(F32) for j8 in range(n_grp)]
                    for half in range(n_half)]
            for half, xs in enumerate(_transpose_lane_chunks(sets, ch)):
                for g, x in enumerate(xs):
                    uy_scr[g, half, i, pl.ds(b, kt, stride=bsz), :] = x
        return 0

    lax.fori_loop(0, n_tiles, gather_tile, 0)

    def group_body(g, _):
        _s5_group(g, kgen_ref, ws_ref, wc_ref, a_ref, uy_scr, s_scr, h_scr, n_pairs=n_pairs, bsz=bsz, rows=rows,
                  lanes=lanes, n_half=n_half, n_tiles=n_tiles, kt=kt)
        return 0

    lax.fori_loop(0, n_grp, group_body, 0)

    def scatter_tile(i, _):
        for b in range(bsz):
            sets = [[uy_scr[g, half, i, pl.ds(b, kt, stride=bsz), :] for g in range(n_grp)]
                    for half in range(n_half)]
            for half, ys in enumerate(_transpose_lane_chunks(sets, ch)):
                for j8, y in enumerate(ys):
                    y_ref[tile_rows(b, i, half * n_grp + j8), :] = y
        return 0

    lax.fori_loop(0, n_tiles, scatter_tile, 0)


def _s5_group(g, kgen_ref, ws_ref, wc_ref, a_ref, uy_scr, s_scr, h_scr, *, n_pairs, bsz, rows, lanes, n_half,
              n_tiles, kt):
    n2 = 2 * S5_STATE
    tc, ch = S5_CHUNK, S5_GROUP
    u = jnp.concatenate([uy_scr[g, hf].reshape(rows, lanes) for hf in range(n_half)], axis=1).astype(BF16)
    s_scr[...] = jnp.dot(u, ws_ref[g], preferred_element_type=F32)
    a1 = a_ref[g, 0:1, :]
    a2 = a_ref[g, 1:2, :]

    def body(j, carry):
        h, hs = carry
        r0 = pl.multiple_of(j * (2 * bsz), 2 * bsz)
        blk = s_scr[pl.ds(r0, 2 * bsz), :]
        h1 = a1 * h + a2 * hs + blk[0:bsz, 0:n2]
        hs1 = a1 * hs - a2 * h + blk[0:bsz, n2:2 * n2]
        h2 = a1 * h1 + a2 * hs1 + blk[bsz:2 * bsz, 0:n2]
        hs2 = a1 * hs1 - a2 * h1 + blk[bsz:2 * bsz, n2:2 * n2]
        h_scr[pl.ds(r0, 2 * bsz), :] = jnp.concatenate([h, h1], axis=0)
        return h2, hs2

    zero = jnp.zeros((bsz, n2), F32)
    lax.fori_loop(0, n_pairs, body, (zero, zero))
    kg = kgen_ref[g]
    lane_w = lax.broadcasted_iota(jnp.int32, kg.shape, 1)
    tz = jnp.concatenate([kg] + [jnp.where(lane_w >= ch * j, pltpu.roll(kg, ch * j, axis=1), 0.0)
                                 for j in range(1, tc)], axis=0).astype(BF16)
    yg = (jnp.dot(u, tz, preferred_element_type=F32)
          + jnp.dot(h_scr[...].astype(BF16), wc_ref[g], preferred_element_type=F32))
    for hf in range(n_half):
        uy_scr[g, hf] = yg[:, hf * lanes:(hf + 1) * lanes].reshape(n_tiles, kt * bsz, lanes)


def _s5(u_perm, kgen, ws, wc, a_pow, bsz, seq, tm):
    t, mw = u_perm.shape
    lanes = 128
    width = kgen.shape[2]
    g_per_blk = lanes // S5_GROUP
    assert 2 * bsz == V7X_SUBLANES, "the chunk scan walks two chunks per 8-row tile"
    rows = (seq // S5_CHUNK) * bsz
    n_pairs = rows // (2 * bsz)
    per_blk = lambda a: pl.BlockSpec((g_per_blk,) + a.shape[1:], lambda q: (q, 0, 0))
    blk = pl.BlockSpec((t, lanes), lambda q: (0, q))
    kern = functools.partial(_s5_kernel, n_pairs=n_pairs, bsz=bsz, seq=seq, tm=tm)
    kt = tm // S5_CHUNK
    return pl.pallas_call(
        kern,
        grid=(mw // lanes,),
        in_specs=[blk, per_blk(kgen), per_blk(ws), per_blk(wc), per_blk(a_pow)],
        out_specs=blk,
        out_shape=jax.ShapeDtypeStruct((t, mw), F32),
        scratch_shapes=[pltpu.VMEM((g_per_blk, width // lanes, seq // tm, kt * bsz, lanes), F32),
                        pltpu.VMEM((rows, ws.shape[2]), F32), pltpu.VMEM((rows, wc.shape[1]), F32)],
        compiler_params=_cparams("parallel"),
        name="s5",
    )(u_perm, kgen, ws, wc, a_pow)


def _s5_weights(lam_re, lam_im, log_step, b_re, b_im, c_re, c_im):
    tc = S5_CHUNK
    groups, n = lam_re.shape
    lr = jnp.minimum(lam_re.astype(F32), -1e-4)
    li = lam_im.astype(F32)
    delta = jnp.exp(log_step.astype(F32))[:, None]
    tau = jnp.arange(tc + 1, dtype=F32)
    mag = jnp.exp((lr * delta)[..., None] * tau)
    ang = (li * delta)[..., None] * tau
    pw_re, pw_im = mag * jnp.cos(ang), mag * jnp.sin(ang)
    num_re, num_im = pw_re[..., 1] - 1.0, pw_im[..., 1]
    den = lr * lr + li * li
    f_re = (num_re * lr + num_im * li) / den
    f_im = (num_im * lr - num_re * li) / den
    b_re, b_im = b_re.astype(F32), b_im.astype(F32)
    bb_re = f_re[..., None] * b_re - f_im[..., None] * b_im
    bb_im = f_re[..., None] * b_im + f_im[..., None] * b_re
    ct_re = jnp.transpose(c_re.astype(F32), (0, 2, 1))
    ct_im = jnp.transpose(c_im.astype(F32), (0, 2, 1))
    cp_re = ct_re[:, :, None, :] * pw_re[..., None] - ct_im[:, :, None, :] * pw_im[..., None]
    cp_im = ct_re[:, :, None, :] * pw_im[..., None] + ct_im[:, :, None, :] * pw_re[..., None]
    width = tc * S5_GROUP
    ein = functools.partial(jnp.einsum, precision=HIGHEST)
    kgen = (ein('gnd,gnx->gdx', bb_re, cp_re[:, :, :tc].reshape(groups, n, width))
            - ein('gnd,gnx->gdx', bb_im, cp_im[:, :, :tc].reshape(groups, n, width)))
    wc = jnp.concatenate([cp_re[:, :, 1:].reshape(groups, n, width),
                          -cp_im[:, :, 1:].reshape(groups, n, width)], axis=1)
    rev_re = jnp.transpose(pw_re[..., tc - 1::-1], (0, 2, 1))
    rev_im = jnp.transpose(pw_im[..., tc - 1::-1], (0, 2, 1))
    bt_re = jnp.transpose(bb_re, (0, 2, 1))
    bt_im = jnp.transpose(bb_im, (0, 2, 1))
    s_re = (rev_re[:, :, None, :] * bt_re[:, None] - rev_im[:, :, None, :] * bt_im[:, None]).reshape(groups, width, n)
    s_im = (rev_re[:, :, None, :] * bt_im[:, None] + rev_im[:, :, None, :] * bt_re[:, None]).reshape(groups, width, n)
    ws = jnp.concatenate([s_re, s_im, s_im, s_re], axis=-1)
    a_pow = jnp.stack([jnp.concatenate([pw_re[..., tc], pw_re[..., tc]], axis=-1),
                       jnp.concatenate([-pw_im[..., tc], pw_im[..., tc]], axis=-1)], axis=1)
    return kgen, ws, wc, a_pow


XA_SUB_TILES = 2


def _sub_tiles(tm):
    ts = tm // XA_SUB_TILES
    return [slice(i * ts, (i + 1) * ts) for i in range(XA_SUB_TILES)]


def _xattn_rows(h_subs, subs, k_ref, v_ref, wq_ref, wo_ref, gq_ref, go_ref, o_scr, out_ref):
    d = out_ref.shape[1]
    hd = d // XA_HEADS
    xn = [_rms(h, gq_ref[...]).astype(BF16) for h in h_subs]
    q = [jnp.dot(x, wq_ref[...], preferred_element_type=F32).astype(BF16) for x in xn]
    for hh in range(XA_HEADS):
        cs = slice(hh * hd, (hh + 1) * hd)
        for i, rs in enumerate(subs):
            s = lax.dot_general(q[i][:, cs], k_ref[:, cs], (((1,), (1,)), ((), ())),
                                preferred_element_type=F32) * (hd ** -0.5)
            e = jnp.exp(s - jnp.max(s, axis=-1, keepdims=True))
            p = e / jnp.sum(e, axis=-1, keepdims=True)
            o_scr[rs, cs] = _bdot(p, v_ref[:, cs]).astype(BF16)
    xa = [jnp.dot(o_scr[rs, :], wo_ref[...], preferred_element_type=F32) for rs in subs]
    for i, rs in enumerate(subs):
        out_ref[rs, :] = h_subs[i] + _rms(xa[i], go_ref[...])


def _ab_post_kernel(h_ref, oa_ref, ys_ref, u_ref, d_ref, wglu_ref, bglu_ref, wout_ref, g_ref, perm_ref,
                    k_ref, v_ref, wq_ref, wo_ref, gq_ref, go_ref, out_ref, o_scr):
    tm = h_ref.shape[0]
    mw = oa_ref.shape[1]
    subs = _sub_tiles(tm)
    hi, lo = _split_bf16(ys_ref[...])
    y_ssm = (jnp.dot(perm_ref[...], hi, preferred_element_type=F32)
             + jnp.dot(perm_ref[...], lo, preferred_element_type=F32))
    y = y_ssm + d_ref[...] * u_ref[...].astype(F32)
    ob = (_gelu_tanh(y) * _sigmoid(_bdot(y, wglu_ref[...]) + bglu_ref[...])).astype(BF16)
    mix = [jnp.dot(oa_ref[rs, :], wout_ref[0:mw, :], preferred_element_type=F32)
           + jnp.dot(ob[rs, :], wout_ref[mw:2 * mw, :], preferred_element_type=F32) for rs in subs]
    h1 = [h_ref[rs, :] + _rms(mix[i], g_ref[...]) for i, rs in enumerate(subs)]
    _xattn_rows(h1, subs, k_ref, v_ref, wq_ref, wo_ref, gq_ref, go_ref, o_scr, out_ref)


def _ab_post(h, o_a, y_ssm, qkvgu, d_skip, w_glu, b_glu, w_out, gain, perm_t, xa, bsz, seq, mem_len, tm):
    t, d = h.shape
    mw = o_a.shape[1]
    nt = seq // tm
    k, v, wq, wo, gq, go = xa
    full = lambda a: pl.BlockSpec(a.shape, lambda b, i: (0, 0))
    rows = lambda w, cb=0: pl.BlockSpec((tm, w), lambda b, i: (b * nt + i, cb))
    kv = pl.BlockSpec((mem_len, d), lambda b, i: (b, 0))
    return pl.pallas_call(
        _ab_post_kernel,
        grid=(bsz, nt),
        in_specs=[rows(d), rows(mw), rows(mw), rows(mw, qkvgu.shape[1] // mw - 1), full(d_skip), full(w_glu),
                  full(b_glu), full(w_out), full(gain), full(perm_t), kv, kv, full(wq), full(wo), full(gq), full(go)],
        out_specs=rows(d),
        out_shape=jax.ShapeDtypeStruct((t, d), F32),
        scratch_shapes=[pltpu.VMEM((tm, d), BF16)],
        compiler_params=_cparams("parallel", "parallel"),
        name="ab_post",
    )(h, o_a, y_ssm, qkvgu, d_skip, w_glu, b_glu, w_out, gain, perm_t, k, v, wq, wo, gq, go)


def _mem_kv_kernel(mem_ref, g_ref, wk_ref, wv_ref, k_ref, v_ref):
    mn = _rms(mem_ref[...], g_ref[...]).astype(BF16)
    k_ref[...] = jnp.dot(mn, wk_ref[...], preferred_element_type=F32).astype(BF16)
    v_ref[...] = jnp.dot(mn, wv_ref[...], preferred_element_type=F32).astype(BF16)


def _mem_kv(mem, gain, wk, wv, tm):
    t, d = mem.shape
    full = lambda a: pl.BlockSpec(a.shape, lambda i: (0, 0))
    rows = pl.BlockSpec((tm, d), lambda i: (i, 0))
    return pl.pallas_call(
        _mem_kv_kernel,
        grid=(t // tm,),
        in_specs=[rows, full(gain), full(wk), full(wv)],
        out_specs=[rows, rows],
        out_shape=[jax.ShapeDtypeStruct((t, d), BF16), jax.ShapeDtypeStruct((t, d), BF16)],
        compiler_params=_cparams("parallel"),
        name="mem_kv",
    )(mem, gain, wk, wv)


def _mlp_kernel(h_ref, gi_ref, go_ref, w1_ref, w2_ref, out_ref, *, tf):
    h = h_ref[...]
    xn = _rms(h, gi_ref[...]).astype(BF16)
    dff = w1_ref.shape[1]
    acts = []
    for j in range(dff // tf):
        a = jnp.maximum(jnp.dot(xn, w1_ref[:, j * tf:(j + 1) * tf], preferred_element_type=F32), 0.0)
        acts.append((a * a).astype(BF16))
    ff = jnp.dot(jnp.concatenate(acts, axis=1), w2_ref[...], preferred_element_type=F32)
    out_ref[...] = h + _rms(ff, go_ref[...])


def _mlp(h, gi, go, w1, w2, tm, tf):
    t, d = h.shape
    full = lambda a: pl.BlockSpec(a.shape, lambda i: (0, 0))
    resident = lambda a: pl.BlockSpec(a.shape, lambda i: (0, 0), pipeline_mode=pl.Buffered(1))
    rows = pl.BlockSpec((tm, d), lambda i: (i, 0))
    return pl.pallas_call(
        functools.partial(_mlp_kernel, tf=tf),
        grid=(t // tm,),
        in_specs=[rows, full(gi), full(go), resident(w1), resident(w2)],
        out_specs=rows,
        out_shape=jax.ShapeDtypeStruct((t, d), F32),
        compiler_params=_cparams("parallel"),
        name="mlp",
    )(h, gi, go, w1, w2)


def _cd_in_kernel(h_ref, g_ref, wbig_ref, wsm_ref, mub_ref, mus_ref, w0_ref, w2_ref, a0_ref, a2_ref, g2_ref,
                  kk_ref, ka_ref, hsum_ref, cw_ref, cb_ref, wa_ref, ba_ref, wx_ref, bx_ref, lam_ref,
                  r_o, lw_o, k_o, v_o, kk_o, a_o, g_o, la_o, lb_o, gate_o,
                  carry_b, carry_s, carry_x, *, mw, n_sub):
    tm = h_ref.shape[0]

    @pl.when(pl.program_id(1) == 0)
    def _():
        carry_b[...] = jnp.zeros_like(carry_b)
        carry_s[...] = jnp.zeros_like(carry_s)
        carry_x[...] = jnp.zeros_like(carry_x)

    ts = tm // n_sub
    proj = []
    for sb in range(n_sub):
        hn = _rms(h_ref[sb * ts:(sb + 1) * ts, :], g_ref[...]).astype(BF16)
        proj.append((jnp.dot(hn, wbig_ref[...], preferred_element_type=F32),
                     jnp.dot(hn, wsm_ref[...], preferred_element_type=F32)))

    for sb in range(n_sub):
        rs = slice(sb * ts, (sb + 1) * ts)
        pb, ps = proj[sb]
        p3 = pb[:, 0:3 * mw]
        xb = pb[:, 3 * mw:4 * mw]
        gate_o[rs, :] = pb[:, 4 * mw:5 * mw]

        prev3 = _shift_rows(p3, 1, carry_b[...])
        prevs = _shift_rows(ps, 1, carry_s[...])
        carry_b[...] = p3[ts - V7X_SUBLANES:ts, :]
        carry_s[...] = ps[ts - V7X_SUBLANES:ts, :]
        p3 = p3 + (prev3 - p3) * mub_ref[...]
        ps = ps + (prevs - ps) * mus_ref[...]
        r = p3[:, 0:mw]
        k = p3[:, mw:2 * mw]
        v = p3[:, 2 * mw:3 * mw]
        lora = ps[:, 0:128]
        wlog = -_softplus(-(w0_ref[...] + _bdot(jnp.tanh(lora), w2_ref[...]))) - 0.5
        a = _sigmoid(a0_ref[...] + _bdot(lora, a2_ref[...]))
        kkr = k * kk_ref[...]
        norm = jnp.sqrt(_bdot(kkr * kkr, hsum_ref[...]))
        r_o[rs, :] = r.astype(BF16)
        lw_o[rs, :] = -jnp.exp(wlog)
        k_o[rs, :] = (k * (1.0 + (a - 1.0) * ka_ref[...])).astype(BF16)
        v_o[rs, :] = v.astype(BF16)
        kk_o[rs, :] = (kkr / jnp.maximum(norm, 1e-12)).astype(BF16)
        a_o[rs, :] = a.astype(BF16)
        g_o[rs, :] = _bdot(_sigmoid(ps[:, 128:256]), g2_ref[...]).astype(BF16)

        cx = carry_x[...]
        xc = cb_ref[...] + cw_ref[LRU_CONV - 1:LRU_CONV, :] * xb
        for sh in range(1, LRU_CONV):
            xc = xc + cw_ref[LRU_CONV - 1 - sh:LRU_CONV - sh, :] * _shift_rows(xb, sh, cx)
        carry_x[...] = xb[ts - V7X_SUBLANES:ts, :]
        rg = _sigmoid(_bdot(xc, wa_ref[...]) + ba_ref[...])
        ig = _sigmoid(_bdot(xc, wx_ref[...]) + bx_ref[...])
        log_a = -LRU_C * rg * _softplus(-lam_ref[...])
        la_o[rs, :] = jnp.exp(log_a)
        th = jnp.tanh(log_a)
        lb_o[rs, :] = jnp.sqrt(-2.0 * th / (1.0 - th)) * (ig * xc)


def _cd_in(h, gain, wbig, wsm, vecs, mats, bsz, seq, tm, mw):
    t, d = h.shape
    nt = seq // tm
    full = lambda a: pl.BlockSpec(a.shape, lambda b, i: (0, 0))
    rows = lambda w: pl.BlockSpec((tm, w), lambda b, i: (b * nt + i, 0))
    (mub, mus, w0, a0, kk_w, ka_w, cw, cb, ba, bx, lam) = vecs
    (w2p, a2p, g2, hsum, wa, wx) = mats
    args = (h, gain, wbig, wsm, mub, mus, w0, w2p, a0, a2p, g2, kk_w, ka_w, hsum, cw, cb, wa, ba, wx, bx, lam)
    kern = functools.partial(_cd_in_kernel, mw=mw, n_sub=4)
    return pl.pallas_call(
        kern,
        grid=(bsz, nt),
        in_specs=[rows(d)] + [full(a) for a in args[1:]],
        out_specs=[rows(mw)] * 10,
        out_shape=[jax.ShapeDtypeStruct((t, mw), dt) for dt in CD_IN_OUT_DTYPES],
        scratch_shapes=[pltpu.VMEM((V7X_SUBLANES, 3 * mw), F32), pltpu.VMEM((V7X_SUBLANES, wsm.shape[1]), F32),
                        pltpu.VMEM((V7X_SUBLANES, mw), F32)],
        compiler_params=_cparams("parallel", "arbitrary"),
        name="cd_in",
    )(*args)


def _rwkv_kernel(r_ref, lw_ref, k_ref, v_ref, kk_ref, a_ref, g_ref, rk_ref, lng_ref, lnb_ref, hsum_ref,
                 o_ref, state_ref, y_scr, *, slab_heads, n_chunks):
    c = RWKV_CHUNK
    hd = RWKV_HEAD
    assert c == hd, "one block mask serves both the (head, s) and the (head, d) layouts"
    sw = slab_heads * hd
    n_slabs = r_ref.shape[1] // sw

    @pl.when(pl.program_id(1) == 0)
    def _():
        state_ref[...] = jnp.zeros_like(state_ref)

    tb = n_chunks * c
    tri = lax.broadcasted_iota(jnp.int32, (tb, tb), 0)
    tci = lax.broadcasted_iota(jnp.int32, (tb, tb), 1)
    blk_tril = jnp.where(((tri // c) == (tci // c)) & (tci <= tri), 1.0, 0.0).astype(BF16)
    cum_all = _split3_dot(blk_tril, lw_ref[...])
    blk_m = (lax.broadcasted_iota(jnp.int32, (sw, sw), 0) // c) == (lax.broadcasted_iota(jnp.int32, (sw, sw), 1) // c)
    wide_t = lax.broadcasted_iota(jnp.int32, (c, sw), 0)
    wide_s = lax.broadcasted_iota(jnp.int32, (c, sw), 1) % c
    strict_w = wide_s < wide_t
    incl_w = wide_s <= wide_t
    eye_w = jnp.where(wide_s == wide_t, 1.0, 0.0)

    def bdiag(x):
        x16 = x.astype(BF16)
        return jnp.where(blk_m, jnp.concatenate([x16] * slab_heads, axis=0), jnp.zeros((), BF16))

    def mm(a, w16):
        return jnp.dot(a.astype(BF16), w16, preferred_element_type=F32)

    nt_dims = (((1,), (1,)), ((), ()))
    chains = []
    for chunk in range(n_chunks):
        rs = slice(chunk * c, (chunk + 1) * c)
        lw = lw_ref[rs, :]
        cum = cum_all[rs, :]
        cum_last = cum[c - 1:c, :]
        r = r_ref[rs, :].astype(F32)
        k = k_ref[rs, :].astype(F32)
        v = v_ref[rs, :].astype(F32)
        kk = kk_ref[rs, :].astype(F32)
        bvec = kk * a_ref[rs, :].astype(F32)
        inv_g = jnp.exp(-cum)
        to_end = jnp.exp(cum_last - cum)
        rt = r * jnp.exp(cum)
        kp = kk * jnp.exp(cum - lw)
        be = bvec * inv_g
        kh = k * inv_g
        bb = bvec * to_end
        kb = k * to_end
        g_end = jnp.exp(cum_last)
        for s in range(n_slabs):
            ls = slice(s * sw, (s + 1) * sw)
            chains.append(dict(rs=rs, ls=ls, slab=s, rt=rt[:, ls], kp=kp[:, ls], v=v[:, ls], be=be[:, ls],
                               kh=kh[:, ls], kb=kb[:, ls], bb=bb[:, ls], g_end=g_end[:, ls]))

    for ch in chains:
        lhs2 = jnp.concatenate([ch['kp'], ch['rt']], axis=0).astype(BF16)
        ab = lax.dot_general(lhs2, bdiag(ch['be']), nt_dims, preferred_element_type=F32)
        ak = lax.dot_general(lhs2, bdiag(ch['kh']), nt_dims, preferred_element_type=F32)
        ch['a_kb'] = jnp.where(strict_w, ab[0:c], 0.0)
        ch['a_rb'] = jnp.where(incl_w, ab[c:2 * c], 0.0).astype(BF16)
        ch['a_kr'] = jnp.concatenate([jnp.where(strict_w, ak[0:c], 0.0), jnp.where(incl_w, ak[c:2 * c], 0.0)],
                                     axis=0).astype(BF16)
    for ch in chains:
        x = -ch['a_kb']
        ch['t'] = eye_w + x
        ch['x'] = mm(x, bdiag(x))
    for _ in range(int(math.log2(c)) - 2):
        for ch in chains:
            res = mm(jnp.concatenate([ch['t'], ch['x']], axis=0), bdiag(ch['x']))
            ch['t'] = ch['t'] + res[0:c]
            ch['x'] = res[c:2 * c]
    for ch in chains:
        ch['t'] = ch['t'] + mm(ch['t'], bdiag(ch['x']))
    for ch in chains:
        l_hi, l_lo = _split_bf16(ch['a_kb'])
        t_hi, t_lo = _split_bf16(ch['t'])
        lt = mm(jnp.concatenate([l_hi, l_lo], axis=0), bdiag(t_hi))
        ch['resid'] = eye_w - ch['t'] - (lt[0:c] + lt[c:2 * c] + mm(l_hi, bdiag(t_lo)))
    for ch in chains:
        ch['t16'] = (ch['t'] + mm(ch['t'], bdiag(ch['resid']))).astype(BF16)
    for ch in chains:
        ch['w_tok'] = mm(ch['t16'], bdiag(ch['kp']))
        ch['av'] = mm(ch['a_kr'], bdiag(ch['v']))
    for ch in chains:
        ch['u_tok'] = mm(ch['t16'], bdiag(ch['av'][0:c]))
    tn_dims = (((0,), (0,)), ((), ()))
    for ch in chains:
        ch['q16'] = (ch['rt'] - mm(ch['a_rb'], bdiag(ch['w_tok']))).astype(BF16)
        ch['y0'] = ch['av'][c:2 * c] - mm(ch['a_rb'], bdiag(ch['u_tok']))
    for ch in chains:
        bb16 = ch['bb'].astype(BF16)
        wb = lax.dot_general(ch['w_tok'].astype(BF16), bb16, tn_dims, preferred_element_type=F32)
        ch['wb'] = jnp.where(blk_m, wb, 0.0).astype(BF16)
        d_t = lax.dot_general(jnp.concatenate([ch['v'], -ch['u_tok']], axis=0).astype(BF16),
                              jnp.concatenate([ch['kb'].astype(BF16), bb16], axis=0), tn_dims,
                              preferred_element_type=F32)
        ch['d_t'] = jnp.where(blk_m, d_t, 0.0)

    for ch in chains:
        s = ch['slab']
        p_t = state_ref[s]
        ch['p16'] = p_t.astype(BF16)
        state_ref[s] = p_t * ch['g_end'] - jnp.dot(ch['p16'], ch['wb'], preferred_element_type=F32) + ch['d_t']
    for ch in chains:
        y_scr[ch['rs'], ch['ls']] = ch['y0'] + lax.dot_general(ch['q16'], ch['p16'], nt_dims,
                                                               preferred_element_type=F32)

    hsum = hsum_ref[...]
    y = y_scr[...]
    r = r_ref[...].astype(F32)
    v = v_ref[...].astype(F32)
    mean = _split_dot(y, hsum) * (1.0 / hd)
    yc = y - mean
    var = _bdot(yc * yc, hsum) * (1.0 / hd)
    yn = yc * lax.rsqrt(var + RWKV_GN_EPS) * lng_ref[...] + lnb_ref[...]
    bonus = _bdot(r * k_ref[...].astype(F32) * rk_ref[...], hsum) * v
    o_ref[...] = ((yn + bonus) * g_ref[...].astype(F32)).astype(o_ref.dtype)


def _rwkv(r, lw, k, v, kk, a, g, rk, lng, lnb, hsum, bsz, seq, slab_heads=4, n_chunks=4):
    t, mw = r.shape
    c = RWKV_CHUNK * n_chunks
    nt = seq // c
    sw = slab_heads * RWKV_HEAD
    full = lambda x: pl.BlockSpec(x.shape, lambda b, i: (0, 0))
    rows = pl.BlockSpec((c, mw), lambda b, i: (b * nt + i, 0))
    kern = functools.partial(_rwkv_kernel, slab_heads=slab_heads, n_chunks=n_chunks)
    return pl.pallas_call(
        kern,
        grid=(bsz, nt),
        in_specs=[rows] * 7 + [full(rk), full(lng), full(lnb), full(hsum)],
        out_specs=rows,
        out_shape=jax.ShapeDtypeStruct((t, mw), BF16),
        scratch_shapes=[pltpu.VMEM((mw // sw, sw, sw), F32), pltpu.VMEM((c, mw), F32)],
        compiler_params=_cparams("parallel", "arbitrary"),
        name="rwkv",
    )(r, lw, k, v, kk, a, g, rk, lng, lnb, hsum)


def _cd_post_kernel(h_ref, oc_ref, la_ref, lb_ref, gate_ref, wout_ref, g_ref,
                    k_ref, v_ref, wq_ref, wo_ref, gq_ref, go_ref, out_ref, o_scr, od_scr, carry_ref):
    tm = h_ref.shape[0]
    mw = oc_ref.shape[1]
    sub = V7X_SUBLANES
    subs = _sub_tiles(tm)

    @pl.when(pl.program_id(1) == 0)
    def _():
        carry_ref[...] = jnp.zeros_like(carry_ref)

    rowi = lax.broadcasted_iota(jnp.int32, (sub, mw), 0)

    def scan8(a, b, h):
        sh = 1
        while sh < sub:
            a_sh = jnp.where(rowi >= sh, pltpu.roll(a, sh, axis=0), 1.0)
            b_sh = jnp.where(rowi >= sh, pltpu.roll(b, sh, axis=0), 0.0)
            b = b + a * b_sh
            a = a * a_sh
            sh *= 2
        return b + a * h

    h = carry_ref[0:1, :]
    for j in range(tm // (2 * sub)):
        r0 = j * 2 * sub
        h1 = scan8(la_ref[r0:r0 + sub, :], lb_ref[r0:r0 + sub, :], h)
        h2 = scan8(la_ref[r0 + sub:r0 + 2 * sub, :], lb_ref[r0 + sub:r0 + 2 * sub, :], h1[sub - 1:sub, :])
        h = h2[sub - 1:sub, :]
        gate = gate_ref[r0:r0 + 2 * sub, :]
        od_scr[r0:r0 + 2 * sub, :] = (jnp.concatenate([h1, h2], axis=0) * _gelu_tanh(gate)).astype(BF16)
    carry_ref[0:1, :] = h

    mix = [jnp.dot(oc_ref[rs, :], wout_ref[0:mw, :], preferred_element_type=F32)
           + jnp.dot(od_scr[rs, :], wout_ref[mw:2 * mw, :], preferred_element_type=F32) for rs in subs]
    h1s = [h_ref[rs, :] + _rms(mix[i], g_ref[...]) for i, rs in enumerate(subs)]
    _xattn_rows(h1s, subs, k_ref, v_ref, wq_ref, wo_ref, gq_ref, go_ref, o_scr, out_ref)


def _cd_post(h, o_c, la, lb, gate, w_out, gain, xa, bsz, seq, mem_len, tm):
    t, d = h.shape
    mw = o_c.shape[1]
    nt = seq // tm
    k, v, wq, wo, gq, go = xa
    full = lambda a: pl.BlockSpec(a.shape, lambda b, i: (0, 0))
    rows = lambda w: pl.BlockSpec((tm, w), lambda b, i: (b * nt + i, 0))
    kv = pl.BlockSpec((mem_len, d), lambda b, i: (b, 0))
    return pl.pallas_call(
        _cd_post_kernel,
        grid=(bsz, nt),
        in_specs=[rows(d), rows(mw), rows(mw), rows(mw), rows(mw), full(w_out), full(gain),
                  kv, kv, full(wq), full(wo), full(gq), full(go)],
        out_specs=rows(d),
        out_shape=jax.ShapeDtypeStruct((t, d), F32),
        scratch_shapes=[pltpu.VMEM((tm, d), BF16), pltpu.VMEM((tm, mw), BF16), pltpu.VMEM((V7X_SUBLANES, mw), F32)],
        compiler_params=_cparams("parallel", "arbitrary"),
        name="cd_post",
    )(h, o_c, la, lb, gate, w_out, gain, k, v, wq, wo, gq, go)


def _row(vec):
    return vec.astype(F32).reshape(1, -1)


def _block_diag(blocks):
    nb, bi, bo = blocks.shape
    eye = jnp.eye(nb, dtype=blocks.dtype)
    return (eye[:, None, :, None] * blocks[:, :, None, :]).reshape(nb * bi, nb * bo)


def _head_sum_matrix(width, head):
    idx = jnp.arange(width) // head
    return (idx[:, None] == idx[None, :]).astype(BF16)


def _layer_ab(h, gains, p, xa, bsz, seq, mem_len, tm):
    d = h.shape[1]
    mw = d // 2
    dk = p['gla_b_decay'].shape[0] // GLA_HEADS
    nq = GLA_HEADS * dk
    w_in = p['ab_w_in']
    rank = p['gla_w_decay2'].shape[0]
    o_dlr = 2 * nq + 2 * mw
    w_main = jnp.concatenate([w_in[:, :o_dlr], w_in[:, o_dlr + rank:]], axis=1).astype(BF16)
    w_dlr = jnp.pad(w_in[:, o_dlr:o_dlr + rank], ((0, 0), (0, 128 - rank))).astype(BF16)
    w_decay2 = jnp.pad(p['gla_w_decay2'].astype(F32), ((0, 128 - rank), (0, 0)))
    perm = _chunk_step_perm(tm, S5_CHUNK)
    qkvgu, loga, u_perm = _ab_in(h, _row(gains[0]), w_main, w_dlr, w_decay2, _row(p['gla_b_decay']), perm, tm, mw)
    o_a = _gla(qkvgu, loga, _row(p['gla_norm_gain']), bsz, seq, tb=256)
    kgen, ws, wc, a_pow = _s5_weights(p['s5_lambda_re'], p['s5_lambda_im'], p['s5_log_step'], p['s5_b_re'],
                                      p['s5_b_im'], p['s5_c_re'], p['s5_c_im'])
    y_perm = _s5(u_perm, kgen, ws.astype(BF16), wc.astype(BF16), a_pow, bsz, seq, tm)
    return _ab_post(h, o_a, y_perm, qkvgu, _row(p['s5_d']), p['s5_w_glu'].astype(BF16), _row(p['s5_b_glu']),
                    p['ab_w_out'].astype(BF16), _row(gains[1]), perm.T, xa, bsz, seq, mem_len, tm)


def _layer_cd(h, gains, p, xa, bsz, seq, mem_len, tm):
    d = h.shape[1]
    mw = d // 2
    w_in = p['cd_w_in']
    dr = p['rwkv_w2'].shape[0]
    ar = p['rwkv_a2'].shape[0]
    gr = p['rwkv_g2'].shape[0]
    assert dr + ar == 128 and gr == 128
    o = [0, mw, mw + dr, 2 * mw + dr, 3 * mw + dr, 3 * mw + dr + ar, 3 * mw + dr + ar + gr]
    col = lambda i, width: w_in[:, o[i]:o[i] + width]
    wbig = jnp.concatenate([col(0, mw), col(2, mw), col(3, mw), w_in[:, o[6]:]], axis=1).astype(BF16)
    wsm = jnp.concatenate([col(1, dr), col(4, ar), col(5, gr)], axis=1).astype(BF16)
    mu = p['rwkv_mu'].astype(F32)
    mseg = lambda i, width: mu[o[i]:o[i] + width]
    mub = jnp.concatenate([mseg(0, mw), mseg(2, mw), mseg(3, mw)]).reshape(1, -1)
    mus = jnp.concatenate([mseg(1, dr), mseg(4, ar), mseg(5, gr)]).reshape(1, -1)
    w2p = jnp.pad(p['rwkv_w2'], ((0, ar), (0, 0))).astype(BF16)
    a2p = jnp.pad(p['rwkv_a2'], ((dr, 0), (0, 0))).astype(BF16)
    hsum = _head_sum_matrix(mw, RWKV_HEAD)
    vecs = (mub, mus, _row(p['rwkv_w0']), _row(p['rwkv_a0']), _row(p['rwkv_k_k']), _row(p['rwkv_k_a']),
            p['lru_conv_w'].astype(F32), _row(p['lru_conv_b']), _row(p['lru_b_a']), _row(p['lru_b_x']),
            _row(p['lru_lambda']))
    mats = (w2p, a2p, p['rwkv_g2'].astype(BF16), hsum, _block_diag(p['lru_w_a']).astype(BF16),
            _block_diag(p['lru_w_x']).astype(BF16))
    r, lw, k, v, kk, a, g, la, lb, gate = _cd_in(h, _row(gains[0]), wbig, wsm, vecs, mats, bsz, seq, tm, mw)
    o_c = _rwkv(r, lw, k, v, kk, a, g, _row(p['rwkv_r_k']), _row(p['rwkv_ln_gain']), _row(p['rwkv_ln_bias']),
                hsum, bsz, seq)
    return _cd_post(h, o_c, la, lb, gate, p['cd_w_out'].astype(BF16), _row(gains[1]), xa, bsz, seq, mem_len, tm)


def kernel(x, mem, norm_gain, xa_wq, xa_wk, xa_wv, xa_wo, mlp_w1, mlp_w2, ab_w_in, gla_w_decay2, gla_b_decay, gla_norm_gain, s5_lambda_re, s5_lambda_im, s5_log_step, s5_b_re, s5_b_im, s5_c_re, s5_c_im, s5_d, s5_w_glu, s5_b_glu, ab_w_out, cd_w_in, rwkv_mu, rwkv_w0, rwkv_w2, rwkv_a0, rwkv_a2, rwkv_g2, rwkv_k_k, rwkv_k_a, rwkv_r_k, rwkv_ln_gain, rwkv_ln_bias, lru_conv_w, lru_conv_b, lru_w_a, lru_b_a, lru_w_x, lru_b_x, lru_lambda, cd_w_out):
    bsz, seq, d = x.shape
    mem_len = mem.shape[1]
    depth = norm_gain.shape[0]
    tm = min(512, seq)
    ab = dict(ab_w_in=ab_w_in, gla_w_decay2=gla_w_decay2, gla_b_decay=gla_b_decay, gla_norm_gain=gla_norm_gain,
              s5_lambda_re=s5_lambda_re, s5_lambda_im=s5_lambda_im, s5_log_step=s5_log_step, s5_b_re=s5_b_re,
              s5_b_im=s5_b_im, s5_c_re=s5_c_re, s5_c_im=s5_c_im, s5_d=s5_d, s5_w_glu=s5_w_glu, s5_b_glu=s5_b_glu,
              ab_w_out=ab_w_out)
    cd = dict(cd_w_in=cd_w_in, rwkv_mu=rwkv_mu, rwkv_w0=rwkv_w0, rwkv_w2=rwkv_w2, rwkv_a0=rwkv_a0,
              rwkv_a2=rwkv_a2, rwkv_g2=rwkv_g2, rwkv_k_k=rwkv_k_k, rwkv_k_a=rwkv_k_a, rwkv_r_k=rwkv_r_k,
              rwkv_ln_gain=rwkv_ln_gain, rwkv_ln_bias=rwkv_ln_bias, lru_conv_w=lru_conv_w, lru_conv_b=lru_conv_b,
              lru_w_a=lru_w_a, lru_b_a=lru_b_a, lru_w_x=lru_w_x, lru_b_x=lru_b_x, lru_lambda=lru_lambda,
              cd_w_out=cd_w_out)
    h = x.astype(F32).reshape(bsz * seq, d)
    mem2 = mem.astype(F32).reshape(bsz * mem_len, d)
    for layer in range(depth):
        g = norm_gain[layer]
        i = layer // 2
        km, vm = _mem_kv(mem2, _row(g[6]), xa_wk[layer].astype(BF16), xa_wv[layer].astype(BF16), mem_len)
        xa = (km, vm, xa_wq[layer].astype(BF16), xa_wo[layer].astype(BF16), _row(g[2]), _row(g[3]))
        if layer % 2 == 0:
            h = _layer_ab(h, g, {n: w[i] for n, w in ab.items()}, xa, bsz, seq, mem_len, tm)
        else:
            h = _layer_cd(h, g, {n: w[i] for n, w in cd.items()}, xa, bsz, seq, mem_len, tm)
        h = _mlp(h, _row(g[4]), _row(g[5]), mlp_w1[layer].astype(BF16), mlp_w2[layer].astype(BF16),
                 tm=tm, tf=1024)
    return h.reshape(bsz, seq, d).astype(x.dtype)
```

```python
import functools
import math

import jax
import jax.numpy as jnp
from jax import lax
from jax.experimental import pallas as pl
from jax.experimental.pallas import tpu as pltpu

F32 = jnp.float32
BF16 = jnp.bfloat16
HIGHEST = lax.Precision.HIGHEST

NORM_EPS = 1e-6
GLA_HEADS = 4
GLA_TAU = 16.0
GLA_CHUNK = 64
S5_GROUP = 16
S5_STATE = 64
S5_CHUNK = 16
RWKV_HEAD = 64
RWKV_CHUNK = 64
RWKV_GN_EPS = 64e-5
LRU_CONV = 4
LRU_C = 8.0
XA_HEADS = 4

V7X_SUBLANES = 8
CD_IN_OUT_DTYPES = (BF16, F32, BF16, BF16, BF16, BF16, BF16, BF16, BF16)
VMEM_LIMIT_BYTES = 48 * 1024 * 1024


def _cparams(*semantics):
    return pltpu.CompilerParams(dimension_semantics=semantics, vmem_limit_bytes=VMEM_LIMIT_BYTES)


def _rms(x, gain):
    return x * lax.rsqrt(jnp.mean(x * x, axis=-1, keepdims=True) + NORM_EPS) * gain


def _bdot(a, b):
    return jnp.dot(a.astype(BF16), b.astype(BF16), preferred_element_type=F32)


def _split_bf16(x):
    hi = x.astype(BF16)
    return hi, (x - hi.astype(F32)).astype(BF16)


def _dot3(a, b):
    a_hi, a_lo = _split_bf16(a)
    b_hi, b_lo = _split_bf16(b)
    return (jnp.dot(a_hi, b_hi, preferred_element_type=F32) + jnp.dot(a_lo, b_hi, preferred_element_type=F32)
            + jnp.dot(a_hi, b_lo, preferred_element_type=F32))


def _split_dot(x, w_bf16):
    hi, lo = _split_bf16(x)
    return (jnp.dot(hi, w_bf16, preferred_element_type=F32)
            + jnp.dot(lo, w_bf16, preferred_element_type=F32))


def _split3_dot(w_bf16, x):
    hi = x.astype(BF16)
    r1 = x - hi.astype(F32)
    mid = r1.astype(BF16)
    lo = (r1 - mid.astype(F32)).astype(BF16)
    return (jnp.dot(w_bf16, hi, preferred_element_type=F32) + jnp.dot(w_bf16, mid, preferred_element_type=F32)
            + jnp.dot(w_bf16, lo, preferred_element_type=F32))


def _sigmoid(x):
    return 1.0 / (1.0 + jnp.exp(-x))


def _softplus(x):
    return jnp.maximum(x, 0.0) + jnp.log(1.0 + jnp.exp(-jnp.abs(x)))


def _gelu_tanh(x):
    c = math.sqrt(2.0 / math.pi)
    return x * (0.5 * (1.0 + jnp.tanh(c * (x + 0.044715 * (x * x * x)))))


def _tril_mask(n, strict=False):
    row = lax.broadcasted_iota(jnp.int32, (n, n), 0)
    col = lax.broadcasted_iota(jnp.int32, (n, n), 1)
    return (col < row) if strict else (col <= row)


def _shift_rows(x, shift, carry):
    rolled = pltpu.roll(x, shift, axis=0)
    crolled = pltpu.roll(carry, shift, axis=0)
    rowi = lax.broadcasted_iota(jnp.int32, carry.shape, 0)
    first = jnp.where(rowi < shift, crolled, rolled[0:V7X_SUBLANES])
    return jnp.concatenate([first, rolled[V7X_SUBLANES:]], axis=0)


def _ab_in_kernel(h_ref, g_ref, w_ref, wd_ref, wd2_ref, bd_ref, perm_ref, out_ref, loga_ref, up_ref):
    hn = _rms(h_ref[...], g_ref[...]).astype(BF16)
    out = jnp.dot(hn, w_ref[...], preferred_element_type=F32)
    out_ref[...] = out.astype(BF16)
    dlr = jnp.dot(hn, wd_ref[...], preferred_element_type=F32)
    z = _dot3(dlr, wd2_ref[...]) + bd_ref[...]
    loga_ref[...] = -_softplus(-z) * (1.0 / GLA_TAU)
    mw = up_ref.shape[1]
    u16 = out[:, out.shape[1] - mw:].astype(BF16)
    up_ref[...] = jnp.dot(perm_ref[...], u16, preferred_element_type=F32).astype(BF16)


def _ab_in(h, gain, w_main, w_dlr, w_decay2, b_decay, perm, tm, mw):
    t, d = h.shape
    n_main = w_main.shape[1]
    n_dk = w_decay2.shape[1]
    full = lambda a: pl.BlockSpec(a.shape, lambda i: (0, 0))
    rows = lambda w: pl.BlockSpec((tm, w), lambda i: (i, 0))
    return pl.pallas_call(
        _ab_in_kernel,
        grid=(t // tm,),
        in_specs=[rows(d), full(gain), full(w_main), full(w_dlr), full(w_decay2), full(b_decay), full(perm)],
        out_specs=[rows(n_main), rows(n_dk), rows(mw)],
        out_shape=[jax.ShapeDtypeStruct((t, n_main), BF16), jax.ShapeDtypeStruct((t, n_dk), F32),
                   jax.ShapeDtypeStruct((t, mw), BF16)],
        compiler_params=_cparams("parallel"),
        name="ab_in",
    )(h, gain, w_main, w_dlr, w_decay2, b_decay, perm)


def _chunk_step_perm(tm, tc):
    dst = jnp.arange(tm)
    src = (dst % (tm // tc)) * tc + dst // (tm // tc)
    return (src[:, None] == jnp.arange(tm)[None, :]).astype(BF16)


def _gla_kernel(q_ref, k_ref, v_ref, gate_ref, la_ref, gain_ref, o_ref, state_ref, *, n_chunks, dk, dv):
    c = GLA_CHUNK

    @pl.when(pl.program_id(1) == 0)
    def _():
        state_ref[...] = jnp.zeros_like(state_ref)

    tb = n_chunks * c
    tril = _tril_mask(c)
    scale = dk ** -0.5
    ri = lax.broadcasted_iota(jnp.int32, (tb, tb), 0)
    ci = lax.broadcasted_iota(jnp.int32, (tb, tb), 1)
    blk_tril = jnp.where(((ri // c) == (ci // c)) & (ci <= ri), 1.0, 0.0).astype(BF16)
    b_all = _split3_dot(blk_tril, la_ref[...])
    lane_head = lax.broadcasted_iota(jnp.int32, (c, GLA_HEADS * dk), 1) // dk
    nt_dims = (((1,), (1,)), ((), ()))
    tn_dims = (((0,), (0,)), ((), ()))

    chunks = []
    for i in range(n_chunks):
        rs = slice(i * c, (i + 1) * c)
        b = b_all[rs, :]
        b_last = b[c - 1:c, :]
        k = k_ref[rs, :].astype(F32)
        chunks.append(dict(rs=rs, q_in=(q_ref[rs, :].astype(F32) * jnp.exp(b) * scale).astype(BF16),
                           k_in=(k * jnp.exp(-b)).astype(BF16), k_st=k * jnp.exp(b_last - b),
                           dec=jnp.exp(b_last)))
    for ch in chunks:
        ch['v'] = [v_ref[ch['rs'], h * dv:(h + 1) * dv].astype(BF16) for h in range(GLA_HEADS)]
        zero16 = jnp.zeros((), BF16)
        ch['scores'] = [
            jnp.where(tril, lax.dot_general(jnp.where(lane_head == h, ch['q_in'], zero16), ch['k_in'], nt_dims,
                                            preferred_element_type=F32), 0.0).astype(BF16)
            for h in range(GLA_HEADS)]
    for ch in chunks:
        ch['o'] = [jnp.dot(ch['scores'][h], ch['v'][h], preferred_element_type=F32) for h in range(GLA_HEADS)]
        ch['d_state'] = [
            lax.dot_general(ch['v'][h], jnp.where(lane_head == h, ch['k_st'], 0.0).astype(BF16), tn_dims,
                            preferred_element_type=F32) for h in range(GLA_HEADS)]

    st = [state_ref[h] for h in range(GLA_HEADS)]
    for ch in chunks:
        ch['st'] = [s.astype(BF16) for s in st]
        st = [st[h] * ch['dec'] + ch['d_state'][h] for h in range(GLA_HEADS)]
    for h in range(GLA_HEADS):
        state_ref[h] = st[h]
    for ch in chunks:
        for h in range(GLA_HEADS):
            vs = slice(h * dv, (h + 1) * dv)
            o = ch['o'][h] + lax.dot_general(ch['q_in'], ch['st'][h], nt_dims, preferred_element_type=F32)
            on = o * lax.rsqrt(jnp.mean(o * o, axis=-1, keepdims=True) + NORM_EPS) * gain_ref[:, vs]
            g = gate_ref[ch['rs'], vs].astype(F32)
            o_ref[ch['rs'], vs] = (on * (g * _sigmoid(g))).astype(o_ref.dtype)


def _gla(qkvgu, loga, gain, bsz, seq, tb):
    t = bsz * seq
    dk = loga.shape[1] // GLA_HEADS
    dv = gain.shape[1] // GLA_HEADS
    nq, nv = GLA_HEADS * dk, GLA_HEADS * dv
    nt = seq // tb
    row = lambda b, i: b * nt + i
    kern = functools.partial(_gla_kernel, n_chunks=tb // GLA_CHUNK, dk=dk, dv=dv)
    return pl.pallas_call(
        kern,
        grid=(bsz, nt),
        in_specs=[pl.BlockSpec((tb, nq), lambda b, i: (row(b, i), 0)),
                  pl.BlockSpec((tb, nq), lambda b, i: (row(b, i), 1)),
                  pl.BlockSpec((tb, nv), lambda b, i: (row(b, i), 1)),
                  pl.BlockSpec((tb, nv), lambda b, i: (row(b, i), 2)),
                  pl.BlockSpec((tb, nq), lambda b, i: (row(b, i), 0)),
                  pl.BlockSpec(gain.shape, lambda b, i: (0, 0))],
        out_specs=pl.BlockSpec((tb, nv), lambda b, i: (row(b, i), 0)),
        out_shape=jax.ShapeDtypeStruct((t, nv), BF16),
        scratch_shapes=[pltpu.VMEM((GLA_HEADS, dv, nq), F32)],
        compiler_params=_cparams("parallel", "arbitrary"),
        name="gla",
    )(qkvgu, qkvgu, qkvgu, qkvgu, loga, gain)


def _transpose_lane_chunks(sets, ch):
    n = len(sets[0])
    chunk = lax.broadcasted_iota(jnp.int32, sets[0][0].shape, 1) // ch
    sets = [list(xs) for xs in sets]
    d = n // 2
    while d >= 1:
        low_half = (chunk & d) == 0
        for xs in sets:
            for r in range(n):
                if r & d:
                    continue
                lo, hi = xs[r], xs[r + d]
                xs[r] = jnp.where(low_half, lo, pltpu.roll(hi, d * ch, axis=1))
                xs[r + d] = jnp.where(low_half, pltpu.roll(lo, (n - d) * ch, axis=1), hi)
        d //= 2
    return sets


def _s5_kernel(u_ref, kgen_ref, ws_ref, wc_ref, a_ref, y_ref, uy_scr, s_scr, h_scr, *,
               n_pairs, bsz, seq, tm):
    n2 = 2 * S5_STATE
    tc, ch = S5_CHUNK, S5_GROUP
    lanes = u_ref.shape[1]
    n_grp = lanes // ch
    n_half = tc // n_grp
    kt = tm // tc
    n_tiles = seq // tm
    rows = n_tiles * kt * bsz

    def tile_rows(b, i, step):
        return pl.ds(pl.multiple_of(b * seq + i * tm + step * kt, kt), kt)

    def gather_tile(i, _):
        for b in range(bsz):
            sets = [[u_ref[tile_rows(b, i, half * n_grp + j8), :].astype(F32) for j8 in range(n_grp)]
                    for half in range(n_half)]
            for half, xs in enumerate(_transpose_lane_chunks(sets, ch)):
                for g, x in enumerate(xs):
                    uy_scr[g, half, i, pl.ds(b, kt, stride=bsz), :] = x
        return 0

    lax.fori_loop(0, n_tiles, gather_tile, 0)

    def group_body(g, _):
        _s5_group(g, kgen_ref, ws_ref, wc_ref, a_ref, uy_scr, s_scr, h_scr, n_pairs=n_pairs, bsz=bsz, rows=rows,
                  lanes=lanes, n_half=n_half, n_tiles=n_tiles, kt=kt)
        return 0

    lax.fori_loop(0, n_grp, group_body, 0)

    def scatter_tile(i, _):
        for b in range(bsz):
            sets = [[uy_scr[g, half, i, pl.ds(b, kt, stride=bsz), :] for g in range(n_grp)]
                    for half in range(n_half)]
            for half, ys in enumerate(_transpose_lane_chunks(sets, ch)):
                for j8, y in enumerate(ys):
                    y_ref[tile_rows(b, i, half * n_grp + j8), :] = y
        return 0

    lax.fori_loop(0, n_tiles, scatter_tile, 0)


def _s5_group(g, kgen_ref, ws_ref, wc_ref, a_ref, uy_scr, s_scr, h_scr, *, n_pairs, bsz, rows, lanes, n_half,
              n_tiles, kt):
    n2 = 2 * S5_STATE
    tc, ch = S5_CHUNK, S5_GROUP
    u = jnp.concatenate([uy_scr[g, hf].reshape(rows, lanes) for hf in range(n_half)], axis=1).astype(BF16)
    s_scr[...] = jnp.dot(u, ws_ref[g], preferred_element_type=F32)
    a1 = a_ref[g, 0:1, :]
    a2 = a_ref[g, 1:2, :]

    def body(j, carry):
        h, hs = carry
        r0 = pl.multiple_of(j * (2 * bsz), 2 * bsz)
        blk = s_scr[pl.ds(r0, 2 * bsz), :]
        h1 = a1 * h + a2 * hs + blk[0:bsz, 0:n2]
        hs1 = a1 * hs - a2 * h + blk[0:bsz, n2:2 * n2]
        h2 = a1 * h1 + a2 * hs1 + blk[bsz:2 * bsz, 0:n2]
        hs2 = a1 * hs1 - a2 * h1 + blk[bsz:2 * bsz, n2:2 * n2]
        h_scr[pl.ds(r0, 2 * bsz), :] = jnp.concatenate([h, h1], axis=0)
        return h2, hs2

    zero = jnp.zeros((bsz, n2), F32)
    lax.fori_loop(0, n_pairs, body, (zero, zero))
    kg = kgen_ref[g]
    lane_w = lax.broadcasted_iota(jnp.int32, kg.shape, 1)
    tz = jnp.concatenate([kg] + [jnp.where(lane_w >= ch * j, pltpu.roll(kg, ch * j, axis=1), 0.0)
                                 for j in range(1, tc)], axis=0).astype(BF16)
    yg = (jnp.dot(u, tz, preferred_element_type=F32)
          + jnp.dot(h_scr[...].astype(BF16), wc_ref[g], preferred_element_type=F32))
    for hf in range(n_half):
        uy_scr[g, hf] = yg[:, hf * lanes:(hf + 1) * lanes].reshape(n_tiles, kt * bsz, lanes)


def _s5(u_perm, kgen, ws, wc, a_pow, bsz, seq, tm):
    t, mw = u_perm.shape
    lanes = 128
    width = kgen.shape[2]
    g_per_blk = lanes // S5_GROUP
    assert 2 * bsz == V7X_SUBLANES, "the chunk scan walks two chunks per 8-row tile"
    rows = (seq // S5_CHUNK) * bsz
    n_pairs = rows // (2 * bsz)
    per_blk = lambda a: pl.BlockSpec((g_per_blk,) + a.shape[1:], lambda q: (q, 0, 0))
    blk = pl.BlockSpec((t, lanes), lambda q: (0, q))
    kern = functools.partial(_s5_kernel, n_pairs=n_pairs, bsz=bsz, seq=seq, tm=tm)
    kt = tm // S5_CHUNK
    return pl.pallas_call(
        kern,
        grid=(mw // lanes,),
        in_specs=[blk, per_blk(kgen), per_blk(ws), per_blk(wc), per_blk(a_pow)],
        out_specs=blk,
        out_shape=jax.ShapeDtypeStruct((t, mw), F32),
        scratch_shapes=[pltpu.VMEM((g_per_blk, width // lanes, seq // tm, kt * bsz, lanes), F32),
                        pltpu.VMEM((rows, ws.shape[2]), F32), pltpu.VMEM((rows, wc.shape[1]), F32)],
        compiler_params=_cparams("parallel"),
        name="s5",
    )(u_perm, kgen, ws, wc, a_pow)


def _s5_weights(lam_re, lam_im, log_step, b_re, b_im, c_re, c_im):
    tc = S5_CHUNK
    groups, n = lam_re.shape
    lr = jnp.minimum(lam_re.astype(F32), -1e-4)
    li = lam_im.astype(F32)
    delta = jnp.exp(log_step.astype(F32))[:, None]
    tau = jnp.arange(tc + 1, dtype=F32)
    mag = jnp.exp((lr * delta)[..., None] * tau)
    ang = (li * delta)[..., None] * tau
    pw_re, pw_im = mag * jnp.cos(ang), mag * jnp.sin(ang)
    num_re, num_im = pw_re[..., 1] - 1.0, pw_im[..., 1]
    den = lr * lr + li * li
    f_re = (num_re * lr + num_im * li) / den
    f_im = (num_im * lr - num_re * li) / den
    b_re, b_im = b_re.astype(F32), b_im.astype(F32)
    bb_re = f_re[..., None] * b_re - f_im[..., None] * b_im
    bb_im = f_re[..., None] * b_im + f_im[..., None] * b_re
    ct_re = jnp.transpose(c_re.astype(F32), (0, 2, 1))
    ct_im = jnp.transpose(c_im.astype(F32), (0, 2, 1))
    cp_re = ct_re[:, :, None, :] * pw_re[..., None] - ct_im[:, :, None, :] * pw_im[..., None]
    cp_im = ct_re[:, :, None, :] * pw_im[..., None] + ct_im[:, :, None, :] * pw_re[..., None]
    width = tc * S5_GROUP
    ein = functools.partial(jnp.einsum, precision=HIGHEST)
    kgen = (ein('gnd,gnx->gdx', bb_re, cp_re[:, :, :tc].reshape(groups, n, width))
            - ein('gnd,gnx->gdx', bb_im, cp_im[:, :, :tc].reshape(groups, n, width)))
    wc = jnp.concatenate([cp_re[:, :, 1:].reshape(groups, n, width),
                          -cp_im[:, :, 1:].reshape(groups, n, width)], axis=1)
    rev_re = jnp.transpose(pw_re[..., tc - 1::-1], (0, 2, 1))
    rev_im = jnp.transpose(pw_im[..., tc - 1::-1], (0, 2, 1))
    bt_re = jnp.transpose(bb_re, (0, 2, 1))
    bt_im = jnp.transpose(bb_im, (0, 2, 1))
    s_re = (rev_re[:, :, None, :] * bt_re[:, None] - rev_im[:, :, None, :] * bt_im[:, None]).reshape(groups, width, n)
    s_im = (rev_re[:, :, None, :] * bt_im[:, None] + rev_im[:, :, None, :] * bt_re[:, None]).reshape(groups, width, n)
    ws = jnp.concatenate([s_re, s_im, s_im, s_re], axis=-1)
    a_pow = jnp.stack([jnp.concatenate([pw_re[..., tc], pw_re[..., tc]], axis=-1),
                       jnp.concatenate([-pw_im[..., tc], pw_im[..., tc]], axis=-1)], axis=1)
    return kgen, ws, wc, a_pow


XA_SUB_TILES = 2


def _sub_tiles(tm):
    ts = tm // XA_SUB_TILES
    return [slice(i * ts, (i + 1) * ts) for i in range(XA_SUB_TILES)]


def _xattn_rows(h_subs, subs, k_ref, v_ref, wq_ref, wo_ref, gq_ref, go_ref, o_scr, out_ref):
    d = out_ref.shape[1]
    hd = d // XA_HEADS
    xn = [_rms(h, gq_ref[...]).astype(BF16) for h in h_subs]
    q = [jnp.dot(x, wq_ref[...], preferred_element_type=F32).astype(BF16) for x in xn]
    for hh in range(XA_HEADS):
        cs = slice(hh * hd, (hh + 1) * hd)
        for i, rs in enumerate(subs):
            s = lax.dot_general(q[i][:, cs], k_ref[:, cs], (((1,), (1,)), ((), ())),
                                preferred_element_type=F32) * (hd ** -0.5)
            e = jnp.exp(s - jnp.max(s, axis=-1, keepdims=True))
            p = e / jnp.sum(e, axis=-1, keepdims=True)
            o_scr[rs, cs] = _bdot(p, v_ref[:, cs]).astype(BF16)
    xa = [jnp.dot(o_scr[rs, :], wo_ref[...], preferred_element_type=F32) for rs in subs]
    for i, rs in enumerate(subs):
        out_ref[rs, :] = h_subs[i] + _rms(xa[i], go_ref[...])


def _ab_post_kernel(h_ref, oa_ref, ys_ref, u_ref, d_ref, wglu_ref, bglu_ref, wout_ref, g_ref, perm_ref,
                    k_ref, v_ref, wq_ref, wo_ref, gq_ref, go_ref, out_ref, o_scr):
    tm = h_ref.shape[0]
    mw = oa_ref.shape[1]
    subs = _sub_tiles(tm)
    hi, lo = _split_bf16(ys_ref[...])
    y_ssm = (jnp.dot(perm_ref[...], hi, preferred_element_type=F32)
             + jnp.dot(perm_ref[...], lo, preferred_element_type=F32))
    y = y_ssm + d_ref[...] * u_ref[...].astype(F32)
    ob = (_gelu_tanh(y) * _sigmoid(_bdot(y, wglu_ref[...]) + bglu_ref[...])).astype(BF16)
    mix = [jnp.dot(oa_ref[rs, :], wout_ref[0:mw, :], preferred_element_type=F32)
           + jnp.dot(ob[rs, :], wout_ref[mw:2 * mw, :], preferred_element_type=F32) for rs in subs]
    h1 = [h_ref[rs, :] + _rms(mix[i], g_ref[...]) for i, rs in enumerate(subs)]
    _xattn_rows(h1, subs, k_ref, v_ref, wq_ref, wo_ref, gq_ref, go_ref, o_scr, out_ref)


def _ab_post(h, o_a, y_ssm, qkvgu, d_skip, w_glu, b_glu, w_out, gain, perm_t, xa, bsz, seq, mem_len, tm):
    t, d = h.shape
    mw = o_a.shape[1]
    nt = seq // tm
    k, v, wq, wo, gq, go = xa
    full = lambda a: pl.BlockSpec(a.shape, lambda b, i: (0, 0))
    rows = lambda w, cb=0: pl.BlockSpec((tm, w), lambda b, i: (b * nt + i, cb))
    kv = pl.BlockSpec((mem_len, d), lambda b, i: (b, 0))
    return pl.pallas_call(
        _ab_post_kernel,
        grid=(bsz, nt),
        in_specs=[rows(d), rows(mw), rows(mw), rows(mw, qkvgu.shape[1] // mw - 1), full(d_skip), full(w_glu),
                  full(b_glu), full(w_out), full(gain), full(perm_t), kv, kv, full(wq), full(wo), full(gq), full(go)],
        out_specs=rows(d),
        out_shape=jax.ShapeDtypeStruct((t, d), F32),
        scratch_shapes=[pltpu.VMEM((tm, d), BF16)],
        compiler_params=_cparams("parallel", "parallel"),
        name="ab_post",
    )(h, o_a, y_ssm, qkvgu, d_skip, w_glu, b_glu, w_out, gain, perm_t, k, v, wq, wo, gq, go)


def _mem_kv_kernel(mem_ref, g_ref, wk_ref, wv_ref, k_ref, v_ref):
    mn = _rms(mem_ref[...], g_ref[...]).astype(BF16)
    k_ref[...] = jnp.dot(mn, wk_ref[...], preferred_element_type=F32).astype(BF16)
    v_ref[...] = jnp.dot(mn, wv_ref[...], preferred_element_type=F32).astype(BF16)


def _mem_kv(mem, gain, wk, wv, tm):
    t, d = mem.shape
    full = lambda a: pl.BlockSpec(a.shape, lambda i: (0, 0))
    rows = pl.BlockSpec((tm, d), lambda i: (i, 0))
    return pl.pallas_call(
        _mem_kv_kernel,
        grid=(t // tm,),
        in_specs=[rows, full(gain), full(wk), full(wv)],
        out_specs=[rows, rows],
        out_shape=[jax.ShapeDtypeStruct((t, d), BF16), jax.ShapeDtypeStruct((t, d), BF16)],
        compiler_params=_cparams("parallel"),
        name="mem_kv",
    )(mem, gain, wk, wv)


def _mlp_kernel(h_ref, gi_ref, go_ref, w1_ref, w2_ref, out_ref, *, tf):
    h = h_ref[...]
    xn = _rms(h, gi_ref[...]).astype(BF16)
    dff = w1_ref.shape[1]
    acts = []
    for j in range(dff // tf):
        a = jnp.maximum(jnp.dot(xn, w1_ref[:, j * tf:(j + 1) * tf], preferred_element_type=F32), 0.0)
        acts.append((a * a).astype(BF16))
    ff = jnp.dot(jnp.concatenate(acts, axis=1), w2_ref[...], preferred_element_type=F32)
    out_ref[...] = h + _rms(ff, go_ref[...])


def _mlp(h, gi, go, w1, w2, tm, tf):
    t, d = h.shape
    full = lambda a: pl.BlockSpec(a.shape, lambda i: (0, 0))
    resident = lambda a: pl.BlockSpec(a.shape, lambda i: (0, 0), pipeline_mode=pl.Buffered(1))
    rows = pl.BlockSpec((tm, d), lambda i: (i, 0))
    return pl.pallas_call(
        functools.partial(_mlp_kernel, tf=tf),
        grid=(t // tm,),
        in_specs=[rows, full(gi), full(go), resident(w1), resident(w2)],
        out_specs=rows,
        out_shape=jax.ShapeDtypeStruct((t, d), F32),
        compiler_params=_cparams("parallel"),
        name="mlp",
    )(h, gi, go, w1, w2)


def _cd_in_kernel(h_ref, g_ref, wbig_ref, wsm_ref, mub_ref, mus_ref, w0_ref, w2_ref, a0_ref, a2_ref, g2_ref,
                  kk_ref, ka_ref, hsum_ref,
                  r_o, lw_o, k_o, v_o, kk_o, a_o, g_o, xb_o, gate_o,
                  carry_b, carry_s, *, mw, n_sub):
    tm = h_ref.shape[0]

    @pl.when(pl.program_id(1) == 0)
    def _():
        carry_b[...] = jnp.zeros_like(carry_b)
        carry_s[...] = jnp.zeros_like(carry_s)

    ts = tm // n_sub
    proj = []
    for sb in range(n_sub):
        hn = _rms(h_ref[sb * ts:(sb + 1) * ts, :], g_ref[...]).astype(BF16)
        proj.append((jnp.dot(hn, wbig_ref[...], preferred_element_type=F32),
                     jnp.dot(hn, wsm_ref[...], preferred_element_type=F32)))

    for sb in range(n_sub):
        rs = slice(sb * ts, (sb + 1) * ts)
        pb, ps = proj[sb]
        p3 = pb[:, 0:3 * mw]
        xb_o[rs, :] = pb[:, 3 * mw:4 * mw].astype(BF16)
        gate_o[rs, :] = pb[:, 4 * mw:5 * mw].astype(BF16)

        prev3 = _shift_rows(p3, 1, carry_b[...])
        prevs = _shift_rows(ps, 1, carry_s[...])
        carry_b[...] = p3[ts - V7X_SUBLANES:ts, :]
        carry_s[...] = ps[ts - V7X_SUBLANES:ts, :]
        p3 = p3 + (prev3 - p3) * mub_ref[...]
        ps = ps + (prevs - ps) * mus_ref[...]
        r = p3[:, 0:mw]
        k = p3[:, mw:2 * mw]
        v = p3[:, 2 * mw:3 * mw]
        lora = ps[:, 0:128]
        wlog = -_softplus(-(w0_ref[...] + _bdot(jnp.tanh(lora), w2_ref[...]))) - 0.5
        a = _sigmoid(a0_ref[...] + _bdot(lora, a2_ref[...]))
        kkr = k * kk_ref[...]
        norm = jnp.sqrt(_bdot(kkr * kkr, hsum_ref[...]))
        r_o[rs, :] = r.astype(BF16)
        lw_o[rs, :] = -jnp.exp(wlog)
        k_o[rs, :] = (k * (1.0 + (a - 1.0) * ka_ref[...])).astype(BF16)
        v_o[rs, :] = v.astype(BF16)
        kk_o[rs, :] = (kkr / jnp.maximum(norm, 1e-12)).astype(BF16)
        a_o[rs, :] = a.astype(BF16)
        g_o[rs, :] = _bdot(_sigmoid(ps[:, 128:256]), g2_ref[...]).astype(BF16)


def _cd_in(h, gain, wbig, wsm, vecs, mats, bsz, seq, tm, mw):
    t, d = h.shape
    nt = seq // tm
    full = lambda a: pl.BlockSpec(a.shape, lambda b, i: (0, 0))
    rows = lambda w: pl.BlockSpec((tm, w), lambda b, i: (b * nt + i, 0))
    (mub, mus, w0, a0, kk_w, ka_w) = vecs
    (w2p, a2p, g2, hsum) = mats
    args = (h, gain, wbig, wsm, mub, mus, w0, w2p, a0, a2p, g2, kk_w, ka_w, hsum)
    kern = functools.partial(_cd_in_kernel, mw=mw, n_sub=4)
    return pl.pallas_call(
        kern,
        grid=(bsz, nt),
        in_specs=[rows(d)] + [full(a) for a in args[1:]],
        out_specs=[rows(mw)] * len(CD_IN_OUT_DTYPES),
        out_shape=[jax.ShapeDtypeStruct((t, mw), dt) for dt in CD_IN_OUT_DTYPES],
        scratch_shapes=[pltpu.VMEM((V7X_SUBLANES, 3 * mw), F32), pltpu.VMEM((V7X_SUBLANES, wsm.shape[1]), F32)],
        compiler_params=_cparams("parallel", "arbitrary"),
        name="cd_in",
    )(*args)


def _rwkv_kernel(r_ref, lw_ref, k_ref, v_ref, kk_ref, a_ref, g_ref, rk_ref, lng_ref, lnb_ref, hsum_ref,
                 o_ref, state_ref, y_scr, *, slab_heads, n_chunks):
    c = RWKV_CHUNK
    hd = RWKV_HEAD
    assert c == hd, "one block mask serves both the (head, s) and the (head, d) layouts"
    sw = slab_heads * hd
    n_slabs = r_ref.shape[1] // sw

    @pl.when(pl.program_id(1) == 0)
    def _():
        state_ref[...] = jnp.zeros_like(state_ref)

    tb = n_chunks * c
    tri = lax.broadcasted_iota(jnp.int32, (tb, tb), 0)
    tci = lax.broadcasted_iota(jnp.int32, (tb, tb), 1)
    blk_tril = jnp.where(((tri // c) == (tci // c)) & (tci <= tri), 1.0, 0.0).astype(BF16)
    cum_all = _split3_dot(blk_tril, lw_ref[...])
    blk_m = (lax.broadcasted_iota(jnp.int32, (sw, sw), 0) // c) == (lax.broadcasted_iota(jnp.int32, (sw, sw), 1) // c)
    wide_t = lax.broadcasted_iota(jnp.int32, (c, sw), 0)
    wide_s = lax.broadcasted_iota(jnp.int32, (c, sw), 1) % c
    strict_w = wide_s < wide_t
    incl_w = wide_s <= wide_t
    eye_w = jnp.where(wide_s == wide_t, 1.0, 0.0)

    def bdiag(x):
        x16 = x.astype(BF16)
        return jnp.where(blk_m, jnp.concatenate([x16] * slab_heads, axis=0), jnp.zeros((), BF16))

    def mm(a, w16):
        return jnp.dot(a.astype(BF16), w16, preferred_element_type=F32)

    nt_dims = (((1,), (1,)), ((), ()))
    chains = []
    for chunk in range(n_chunks):
        rs = slice(chunk * c, (chunk + 1) * c)
        lw = lw_ref[rs, :]
        cum = cum_all[rs, :]
        cum_last = cum[c - 1:c, :]
        r = r_ref[rs, :].astype(F32)
        k = k_ref[rs, :].astype(F32)
        v = v_ref[rs, :].astype(F32)
        kk = kk_ref[rs, :].astype(F32)
        bvec = kk * a_ref[rs, :].astype(F32)
        inv_g = jnp.exp(-cum)
        to_end = jnp.exp(cum_last - cum)
        rt = r * jnp.exp(cum)
        kp = kk * jnp.exp(cum - lw)
        be = bvec * inv_g
        kh = k * inv_g
        bb = bvec * to_end
        kb = k * to_end
        g_end = jnp.exp(cum_last)
        for s in range(n_slabs):
            ls = slice(s * sw, (s + 1) * sw)
            chains.append(dict(rs=rs, ls=ls, slab=s, rt=rt[:, ls], kp=kp[:, ls], v=v[:, ls], be=be[:, ls],
                               kh=kh[:, ls], kb=kb[:, ls], bb=bb[:, ls], g_end=g_end[:, ls]))

    for ch in chains:
        lhs2 = jnp.concatenate([ch['kp'], ch['rt']], axis=0).astype(BF16)
        ab = lax.dot_general(lhs2, bdiag(ch['be']), nt_dims, preferred_element_type=F32)
        ak = lax.dot_general(lhs2, bdiag(ch['kh']), nt_dims, preferred_element_type=F32)
        ch['a_kb'] = jnp.where(strict_w, ab[0:c], 0.0)
        ch['a_rb'] = jnp.where(incl_w, ab[c:2 * c], 0.0).astype(BF16)
        ch['a_kr'] = jnp.concatenate([jnp.where(strict_w, ak[0:c], 0.0), jnp.where(incl_w, ak[c:2 * c], 0.0)],
                                     axis=0).astype(BF16)
    for ch in chains:
        x = -ch['a_kb']
        ch['t'] = eye_w + x
        ch['x'] = mm(x, bdiag(x))
    for _ in range(int(math.log2(c)) - 2):
        for ch in chains:
            res = mm(jnp.concatenate([ch['t'], ch['x']], axis=0), bdiag(ch['x']))
            ch['t'] = ch['t'] + res[0:c]
            ch['x'] = res[c:2 * c]
    for ch in chains:
        ch['t'] = ch['t'] + mm(ch['t'], bdiag(ch['x']))
    for ch in chains:
        l_hi, l_lo = _split_bf16(ch['a_kb'])
        t_hi, t_lo = _split_bf16(ch['t'])
        lt = mm(jnp.concatenate([l_hi, l_lo], axis=0), bdiag(t_hi))
        ch['resid'] = eye_w - ch['t'] - (lt[0:c] + lt[c:2 * c] + mm(l_hi, bdiag(t_lo)))
    for ch in chains:
        ch['t16'] = (ch['t'] + mm(ch['t'], bdiag(ch['resid']))).astype(BF16)
    for ch in chains:
        ch['w_tok'] = mm(ch['t16'], bdiag(ch['kp']))
        ch['av'] = mm(ch['a_kr'], bdiag(ch['v']))
    for ch in chains:
        ch['u_tok'] = mm(ch['t16'], bdiag(ch['av'][0:c]))
    tn_dims = (((0,), (0,)), ((), ()))
    for ch in chains:
        ch['q16'] = (ch['rt'] - mm(ch['a_rb'], bdiag(ch['w_tok']))).astype(BF16)
        ch['y0'] = ch['av'][c:2 * c] - mm(ch['a_rb'], bdiag(ch['u_tok']))
    for ch in chains:
        bb16 = ch['bb'].astype(BF16)
        wb = lax.dot_general(ch['w_tok'].astype(BF16), bb16, tn_dims, preferred_element_type=F32)
        ch['wb'] = jnp.where(blk_m, wb, 0.0).astype(BF16)
        d_t = lax.dot_general(jnp.concatenate([ch['v'], -ch['u_tok']], axis=0).astype(BF16),
                              jnp.concatenate([ch['kb'].astype(BF16), bb16], axis=0), tn_dims,
                              preferred_element_type=F32)
        ch['d_t'] = jnp.where(blk_m, d_t, 0.0)

    for ch in chains:
        s = ch['slab']
        p_t = state_ref[s]
        ch['p16'] = p_t.astype(BF16)
        state_ref[s] = p_t * ch['g_end'] - jnp.dot(ch['p16'], ch['wb'], preferred_element_type=F32) + ch['d_t']
    for ch in chains:
        y_scr[ch['rs'], ch['ls']] = ch['y0'] + lax.dot_general(ch['q16'], ch['p16'], nt_dims,
                                                               preferred_element_type=F32)

    hsum = hsum_ref[...]
    y = y_scr[...]
    r = r_ref[...].astype(F32)
    v = v_ref[...].astype(F32)
    mean = _split_dot(y, hsum) * (1.0 / hd)
    yc = y - mean
    var = _bdot(yc * yc, hsum) * (1.0 / hd)
    yn = yc * lax.rsqrt(var + RWKV_GN_EPS) * lng_ref[...] + lnb_ref[...]
    bonus = _bdot(r * k_ref[...].astype(F32) * rk_ref[...], hsum) * v
    o_ref[...] = ((yn + bonus) * g_ref[...].astype(F32)).astype(o_ref.dtype)


def _rwkv(r, lw, k, v, kk, a, g, rk, lng, lnb, hsum, bsz, seq, slab_heads=4, n_chunks=4):
    t, mw = r.shape
    c = RWKV_CHUNK * n_chunks
    nt = seq // c
    sw = slab_heads * RWKV_HEAD
    full = lambda x: pl.BlockSpec(x.shape, lambda b, i: (0, 0))
    rows = pl.BlockSpec((c, mw), lambda b, i: (b * nt + i, 0))
    kern = functools.partial(_rwkv_kernel, slab_heads=slab_heads, n_chunks=n_chunks)
    return pl.pallas_call(
        kern,
        grid=(bsz, nt),
        in_specs=[rows] * 7 + [full(rk), full(lng), full(lnb), full(hsum)],
        out_specs=rows,
        out_shape=jax.ShapeDtypeStruct((t, mw), BF16),
        scratch_shapes=[pltpu.VMEM((mw // sw, sw, sw), F32), pltpu.VMEM((c, mw), F32)],
        compiler_params=_cparams("parallel", "arbitrary"),
        name="rwkv",
    )(r, lw, k, v, kk, a, g, rk, lng, lnb, hsum)


def _cd_post_kernel(h_ref, oc_ref, xb_ref, gate_ref, cw_ref, cb_ref, wa_ref, ba_ref, wx_ref, bx_ref, lam_ref,
                    wout_ref, g_ref, k_ref, v_ref, wq_ref, wo_ref, gq_ref, go_ref, out_ref,
                    o_scr, od_scr, la_ref, lb_ref, carry_ref, carry_x):
    tm = h_ref.shape[0]
    mw = oc_ref.shape[1]
    sub = V7X_SUBLANES
    subs = _sub_tiles(tm)

    @pl.when(pl.program_id(1) == 0)
    def _():
        carry_ref[...] = jnp.zeros_like(carry_ref)
        carry_x[...] = jnp.zeros_like(carry_x)

    rowi = lax.broadcasted_iota(jnp.int32, (sub, mw), 0)

    def lru_coeffs(rs, cx):
        xb = xb_ref[rs, :].astype(F32)
        xc = cb_ref[...] + cw_ref[LRU_CONV - 1:LRU_CONV, :] * xb
        for sh in range(1, LRU_CONV):
            xc = xc + cw_ref[LRU_CONV - 1 - sh:LRU_CONV - sh, :] * _shift_rows(xb, sh, cx)
        rg = _sigmoid(_bdot(xc, wa_ref[...]) + ba_ref[...])
        ig = _sigmoid(_bdot(xc, wx_ref[...]) + bx_ref[...])
        log_a = -LRU_C * rg * _softplus(-lam_ref[...])
        la_ref[rs, :] = jnp.exp(log_a)
        th = jnp.tanh(log_a)
        lb_ref[rs, :] = jnp.sqrt(-2.0 * th / (1.0 - th)) * (ig * xc)
        return xb[xb.shape[0] - sub:, :]

    def scan8(a, b, h):
        sh = 1
        while sh < sub:
            a_sh = jnp.where(rowi >= sh, pltpu.roll(a, sh, axis=0), 1.0)
            b_sh = jnp.where(rowi >= sh, pltpu.roll(b, sh, axis=0), 0.0)
            b = b + a * b_sh
            a = a * a_sh
            sh *= 2
        return b + a * h

    h = carry_ref[0:1, :]
    cx = carry_x[...]
    for rs in subs:
        cx = lru_coeffs(rs, cx)
        for r0 in range(rs.start, rs.stop, 2 * sub):
            h1 = scan8(la_ref[r0:r0 + sub, :], lb_ref[r0:r0 + sub, :], h)
            h2 = scan8(la_ref[r0 + sub:r0 + 2 * sub, :], lb_ref[r0 + sub:r0 + 2 * sub, :], h1[sub - 1:sub, :])
            h = h2[sub - 1:sub, :]
            gate = gate_ref[r0:r0 + 2 * sub, :].astype(F32)
            od_scr[r0:r0 + 2 * sub, :] = (jnp.concatenate([h1, h2], axis=0) * _gelu_tanh(gate)).astype(BF16)
    carry_ref[0:1, :] = h
    carry_x[...] = cx

    mix = [jnp.dot(oc_ref[rs, :], wout_ref[0:mw, :], preferred_element_type=F32)
           + jnp.dot(od_scr[rs, :], wout_ref[mw:2 * mw, :], preferred_element_type=F32) for rs in subs]
    h1s = [h_ref[rs, :] + _rms(mix[i], g_ref[...]) for i, rs in enumerate(subs)]
    _xattn_rows(h1s, subs, k_ref, v_ref, wq_ref, wo_ref, gq_ref, go_ref, o_scr, out_ref)


def _cd_post(h, o_c, xb, gate, lru, w_out, gain, xa, bsz, seq, mem_len, tm):
    t, d = h.shape
    mw = o_c.shape[1]
    nt = seq // tm
    k, v, wq, wo, gq, go = xa
    full = lambda a: pl.BlockSpec(a.shape, lambda b, i: (0, 0))
    rows = lambda w: pl.BlockSpec((tm, w), lambda b, i: (b * nt + i, 0))
    kv = pl.BlockSpec((mem_len, d), lambda b, i: (b, 0))
    return pl.pallas_call(
        _cd_post_kernel,
        grid=(bsz, nt),
        in_specs=[rows(d), rows(mw), rows(mw), rows(mw)] + [full(a) for a in lru] + [full(w_out), full(gain),
                  kv, kv, full(wq), full(wo), full(gq), full(go)],
        out_specs=rows(d),
        out_shape=jax.ShapeDtypeStruct((t, d), F32),
        scratch_shapes=[pltpu.VMEM((tm, d), BF16), pltpu.VMEM((tm, mw), BF16), pltpu.VMEM((tm, mw), F32),
                        pltpu.VMEM((tm, mw), F32), pltpu.VMEM((V7X_SUBLANES, mw), F32),
                        pltpu.VMEM((V7X_SUBLANES, mw), F32)],
        compiler_params=_cparams("parallel", "arbitrary"),
        name="cd_post",
    )(h, o_c, xb, gate, *lru, w_out, gain, k, v, wq, wo, gq, go)


def _row(vec):
    return vec.astype(F32).reshape(1, -1)


def _block_diag(blocks):
    nb, bi, bo = blocks.shape
    eye = jnp.eye(nb, dtype=blocks.dtype)
    return (eye[:, None, :, None] * blocks[:, :, None, :]).reshape(nb * bi, nb * bo)


def _head_sum_matrix(width, head):
    idx = jnp.arange(width) // head
    return (idx[:, None] == idx[None, :]).astype(BF16)


def _layer_ab(h, gains, p, xa, bsz, seq, mem_len, tm):
    d = h.shape[1]
    mw = d // 2
    dk = p['gla_b_decay'].shape[0] // GLA_HEADS
    nq = GLA_HEADS * dk
    w_in = p['ab_w_in']
    rank = p['gla_w_decay2'].shape[0]
    o_dlr = 2 * nq + 2 * mw
    w_main = jnp.concatenate([w_in[:, :o_dlr], w_in[:, o_dlr + rank:]], axis=1).astype(BF16)
    w_dlr = jnp.pad(w_in[:, o_dlr:o_dlr + rank], ((0, 0), (0, 128 - rank))).astype(BF16)
    w_decay2 = jnp.pad(p['gla_w_decay2'].astype(F32), ((0, 128 - rank), (0, 0)))
    perm = _chunk_step_perm(tm, S5_CHUNK)
    qkvgu, loga, u_perm = _ab_in(h, _row(gains[0]), w_main, w_dlr, w_decay2, _row(p['gla_b_decay']), perm, tm, mw)
    o_a = _gla(qkvgu, loga, _row(p['gla_norm_gain']), bsz, seq, tb=256)
    kgen, ws, wc, a_pow = _s5_weights(p['s5_lambda_re'], p['s5_lambda_im'], p['s5_log_step'], p['s5_b_re'],
                                      p['s5_b_im'], p['s5_c_re'], p['s5_c_im'])
    y_perm = _s5(u_perm, kgen, ws.astype(BF16), wc.astype(BF16), a_pow, bsz, seq, tm)
    return _ab_post(h, o_a, y_perm, qkvgu, _row(p['s5_d']), p['s5_w_glu'].astype(BF16), _row(p['s5_b_glu']),
                    p['ab_w_out'].astype(BF16), _row(gains[1]), perm.T, xa, bsz, seq, mem_len, tm)


def _layer_cd(h, gains, p, xa, bsz, seq, mem_len, tm):
    d = h.shape[1]
    mw = d // 2
    w_in = p['cd_w_in']
    dr = p['rwkv_w2'].shape[0]
    ar = p['rwkv_a2'].shape[0]
    gr = p['rwkv_g2'].shape[0]
    assert dr + ar == 128 and gr == 128
    o = [0, mw, mw + dr, 2 * mw + dr, 3 * mw + dr, 3 * mw + dr + ar, 3 * mw + dr + ar + gr]
    col = lambda i, width: w_in[:, o[i]:o[i] + width]
    wbig = jnp.concatenate([col(0, mw), col(2, mw), col(3, mw), w_in[:, o[6]:]], axis=1).astype(BF16)
    wsm = jnp.concatenate([col(1, dr), col(4, ar), col(5, gr)], axis=1).astype(BF16)
    mu = p['rwkv_mu'].astype(F32)
    mseg = lambda i, width: mu[o[i]:o[i] + width]
    mub = jnp.concatenate([mseg(0, mw), mseg(2, mw), mseg(3, mw)]).reshape(1, -1)
    mus = jnp.concatenate([mseg(1, dr), mseg(4, ar), mseg(5, gr)]).reshape(1, -1)
    w2p = jnp.pad(p['rwkv_w2'], ((0, ar), (0, 0))).astype(BF16)
    a2p = jnp.pad(p['rwkv_a2'], ((dr, 0), (0, 0))).astype(BF16)
    hsum = _head_sum_matrix(mw, RWKV_HEAD)
    vecs = (mub, mus, _row(p['rwkv_w0']), _row(p['rwkv_a0']), _row(p['rwkv_k_k']), _row(p['rwkv_k_a']))
    mats = (w2p, a2p, p['rwkv_g2'].astype(BF16), hsum)
    r, lw, k, v, kk, a, g, xb, gate = _cd_in(h, _row(gains[0]), wbig, wsm, vecs, mats, bsz, seq, tm, mw)
    o_c = _rwkv(r, lw, k, v, kk, a, g, _row(p['rwkv_r_k']), _row(p['rwkv_ln_gain']), _row(p['rwkv_ln_bias']),
                hsum, bsz, seq)
    lru = (p['lru_conv_w'].astype(F32), _row(p['lru_conv_b']), _block_diag(p['lru_w_a']).astype(BF16),
           _row(p['lru_b_a']), _block_diag(p['lru_w_x']).astype(BF16), _row(p['lru_b_x']), _row(p['lru_lambda']))
    return _cd_post(h, o_c, xb, gate, lru, p['cd_w_out'].astype(BF16), _row(gains[1]), xa, bsz, seq, mem_len, tm)


def kernel(x, mem, norm_gain, xa_wq, xa_wk, xa_wv, xa_wo, mlp_w1, mlp_w2, ab_w_in, gla_w_decay2, gla_b_decay, gla_norm_gain, s5_lambda_re, s5_lambda_im, s5_log_step, s5_b_re, s5_b_im, s5_c_re, s5_c_im, s5_d, s5_w_glu, s5_b_glu, ab_w_out, cd_w_in, rwkv_mu, rwkv_w0, rwkv_w2, rwkv_a0, rwkv_a2, rwkv_g2, rwkv_k_k, rwkv_k_a, rwkv_r_k, rwkv_ln_gain, rwkv_ln_bias, lru_conv_w, lru_conv_b, lru_w_a, lru_b_a, lru_w_x, lru_b_x, lru_lambda, cd_w_out):
    bsz, seq, d = x.shape
    mem_len = mem.shape[1]
    depth = norm_gain.shape[0]
    tm = min(512, seq)
    ab = dict(ab_w_in=ab_w_in, gla_w_decay2=gla_w_decay2, gla_b_decay=gla_b_decay, gla_norm_gain=gla_norm_gain,
              s5_lambda_re=s5_lambda_re, s5_lambda_im=s5_lambda_im, s5_log_step=s5_log_step, s5_b_re=s5_b_re,
              s5_b_im=s5_b_im, s5_c_re=s5_c_re, s5_c_im=s5_c_im, s5_d=s5_d, s5_w_glu=s5_w_glu, s5_b_glu=s5_b_glu,
              ab_w_out=ab_w_out)
    cd = dict(cd_w_in=cd_w_in, rwkv_mu=rwkv_mu, rwkv_w0=rwkv_w0, rwkv_w2=rwkv_w2, rwkv_a0=rwkv_a0,
              rwkv_a2=rwkv_a2, rwkv_g2=rwkv_g2, rwkv_k_k=rwkv_k_k, rwkv_k_a=rwkv_k_a, rwkv_r_k=rwkv_r_k,
              rwkv_ln_gain=rwkv_ln_gain, rwkv_ln_bias=rwkv_ln_bias, lru_conv_w=lru_conv_w, lru_conv_b=lru_conv_b,
              lru_w_a=lru_w_a, lru_b_a=lru_b_a, lru_w_x=lru_w_x, lru_b_x=lru_b_x, lru_lambda=lru_lambda,
              cd_w_out=cd_w_out)
    h = x.astype(F32).reshape(bsz * seq, d)
    mem2 = mem.astype(F32).reshape(bsz * mem_len, d)
    for layer in range(depth):
        g = norm_gain[layer]
        i = layer // 2
        km, vm = _mem_kv(mem2, _row(g[6]), xa_wk[layer].astype(BF16), xa_wv[layer].astype(BF16), mem_len)
        xa = (km, vm, xa_wq[layer].astype(BF16), xa_wo[layer].astype(BF16), _row(g[2]), _row(g[3]))
        if layer % 2 == 0:
            h = _layer_ab(h, g, {n: w[i] for n, w in ab.items()}, xa, bsz, seq, mem_len, tm)
        else:
            h = _layer_cd(h, g, {n: w[i] for n, w in cd.items()}, xa, bsz, seq, mem_len, tm)
        h = _mlp(h, _row(g[4]), _row(g[5]), mlp_w1[layer].astype(BF16), mlp_w2[layer].astype(BF16),
                 tm=tm, tf=1024)
    return h.reshape(bsz, seq, d).astype(x.dtype)
```

```python
import functools
import math

import jax
import jax.numpy as jnp
from jax import lax
from jax.experimental import pallas as pl
from jax.experimental.pallas import tpu as pltpu

F32 = jnp.float32
BF16 = jnp.bfloat16
HIGHEST = lax.Precision.HIGHEST

NORM_EPS = 1e-6
GLA_HEADS = 4
GLA_TAU = 16.0
GLA_CHUNK = 64
S5_GROUP = 16
S5_STATE = 64
S5_CHUNK = 16
RWKV_HEAD = 64
RWKV_CHUNK = 64
RWKV_GN_EPS = 64e-5
LRU_CONV = 4
LRU_C = 8.0
XA_HEADS = 4

V7X_SUBLANES = 8
CD_IN_OUT_DTYPES = (BF16, F32, BF16, BF16, BF16, BF16, BF16, BF16, BF16)
VMEM_LIMIT_BYTES = 48 * 1024 * 1024


def _cparams(*semantics):
    return pltpu.CompilerParams(dimension_semantics=semantics, vmem_limit_bytes=VMEM_LIMIT_BYTES)


def _rms(x, gain):
    return x * lax.rsqrt(jnp.mean(x * x, axis=-1, keepdims=True) + NORM_EPS) * gain


def _bdot(a, b):
    return jnp.dot(a.astype(BF16), b.astype(BF16), preferred_element_type=F32)


def _split_bf16(x):
    hi = x.astype(BF16)
    return hi, (x - hi.astype(F32)).astype(BF16)


def _dot3(a, b):
    a_hi, a_lo = _split_bf16(a)
    b_hi, b_lo = _split_bf16(b)
    return (jnp.dot(a_hi, b_hi, preferred_element_type=F32) + jnp.dot(a_lo, b_hi, preferred_element_type=F32)
            + jnp.dot(a_hi, b_lo, preferred_element_type=F32))


def _split_dot(x, w_bf16):
    hi, lo = _split_bf16(x)
    return (jnp.dot(hi, w_bf16, preferred_element_type=F32)
            + jnp.dot(lo, w_bf16, preferred_element_type=F32))


def _split3_dot(w_bf16, x):
    hi = x.astype(BF16)
    r1 = x - hi.astype(F32)
    mid = r1.astype(BF16)
    lo = (r1 - mid.astype(F32)).astype(BF16)
    return (jnp.dot(w_bf16, hi, preferred_element_type=F32) + jnp.dot(w_bf16, mid, preferred_element_type=F32)
            + jnp.dot(w_bf16, lo, preferred_element_type=F32))


def _sigmoid(x):
    return 1.0 / (1.0 + jnp.exp(-x))


def _softplus(x):
    return jnp.maximum(x, 0.0) + jnp.log(1.0 + jnp.exp(-jnp.abs(x)))


def _gelu_tanh(x):
    c = math.sqrt(2.0 / math.pi)
    return x * (0.5 * (1.0 + jnp.tanh(c * (x + 0.044715 * (x * x * x)))))


def _tril_mask(n, strict=False):
    row = lax.broadcasted_iota(jnp.int32, (n, n), 0)
    col = lax.broadcasted_iota(jnp.int32, (n, n), 1)
    return (col < row) if strict else (col <= row)


def _shift_rows(x, shift, carry):
    rolled = pltpu.roll(x, shift, axis=0)
    crolled = pltpu.roll(carry, shift, axis=0)
    rowi = lax.broadcasted_iota(jnp.int32, carry.shape, 0)
    first = jnp.where(rowi < shift, crolled, rolled[0:V7X_SUBLANES])
    return jnp.concatenate([first, rolled[V7X_SUBLANES:]], axis=0)


def _ab_in_kernel(h_ref, g_ref, w_ref, wd_ref, wd2_ref, bd_ref, perm_ref, out_ref, loga_ref, up_ref):
    hn = _rms(h_ref[...], g_ref[...]).astype(BF16)
    out = jnp.dot(hn, w_ref[...], preferred_element_type=F32)
    out_ref[...] = out.astype(BF16)
    dlr = jnp.dot(hn, wd_ref[...], preferred_element_type=F32)
    z = _dot3(dlr, wd2_ref[...]) + bd_ref[...]
    loga_ref[...] = -_softplus(-z) * (1.0 / GLA_TAU)
    mw = up_ref.shape[1]
    u16 = out[:, out.shape[1] - mw:].astype(BF16)
    up_ref[...] = jnp.dot(perm_ref[...], u16, preferred_element_type=F32).astype(BF16)


def _ab_in(h, gain, w_main, w_dlr, w_decay2, b_decay, perm, tm, mw):
    t, d = h.shape
    n_main = w_main.shape[1]
    n_dk = w_decay2.shape[1]
    full = lambda a: pl.BlockSpec(a.shape, lambda i: (0, 0))
    rows = lambda w: pl.BlockSpec((tm, w), lambda i: (i, 0))
    return pl.pallas_call(
        _ab_in_kernel,
        grid=(t // tm,),
        in_specs=[rows(d), full(gain), full(w_main), full(w_dlr), full(w_decay2), full(b_decay), full(perm)],
        out_specs=[rows(n_main), rows(n_dk), rows(mw)],
        out_shape=[jax.ShapeDtypeStruct((t, n_main), BF16), jax.ShapeDtypeStruct((t, n_dk), F32),
                   jax.ShapeDtypeStruct((t, mw), BF16)],
        compiler_params=_cparams("parallel"),
        name="ab_in",
    )(h, gain, w_main, w_dlr, w_decay2, b_decay, perm)


def _chunk_step_perm(tm, tc):
    dst = jnp.arange(tm)
    src = (dst % (tm // tc)) * tc + dst // (tm // tc)
    return (src[:, None] == jnp.arange(tm)[None, :]).astype(BF16)


def _gla_kernel(q_ref, k_ref, v_ref, gate_ref, la_ref, gain_ref, o_ref, state_ref, *, n_chunks, dk, dv):
    c = GLA_CHUNK

    @pl.when(pl.program_id(1) == 0)
    def _():
        state_ref[...] = jnp.zeros_like(state_ref)

    tb = n_chunks * c
    tril = _tril_mask(c)
    scale = dk ** -0.5
    ri = lax.broadcasted_iota(jnp.int32, (tb, tb), 0)
    ci = lax.broadcasted_iota(jnp.int32, (tb, tb), 1)
    blk_tril = jnp.where(((ri // c) == (ci // c)) & (ci <= ri), 1.0, 0.0).astype(BF16)
    b_all = _split3_dot(blk_tril, la_ref[...])
    lane_head = lax.broadcasted_iota(jnp.int32, (c, GLA_HEADS * dk), 1) // dk
    nt_dims = (((1,), (1,)), ((), ()))
    tn_dims = (((0,), (0,)), ((), ()))

    chunks = []
    for i in range(n_chunks):
        rs = slice(i * c, (i + 1) * c)
        b = b_all[rs, :]
        b_last = b[c - 1:c, :]
        k = k_ref[rs, :].astype(F32)
        chunks.append(dict(rs=rs, q_in=(q_ref[rs, :].astype(F32) * jnp.exp(b) * scale).astype(BF16),
                           k_in=(k * jnp.exp(-b)).astype(BF16), k_st=k * jnp.exp(b_last - b),
                           dec=jnp.exp(b_last)))
    for ch in chunks:
        ch['v'] = [v_ref[ch['rs'], h * dv:(h + 1) * dv].astype(BF16) for h in range(GLA_HEADS)]
        zero16 = jnp.zeros((), BF16)
        ch['scores'] = [
            jnp.where(tril, lax.dot_general(jnp.where(lane_head == h, ch['q_in'], zero16), ch['k_in'], nt_dims,
                                            preferred_element_type=F32), 0.0).astype(BF16)
            for h in range(GLA_HEADS)]
    for ch in chunks:
        ch['o'] = [jnp.dot(ch['scores'][h], ch['v'][h], preferred_element_type=F32) for h in range(GLA_HEADS)]
        ch['d_state'] = [
            lax.dot_general(ch['v'][h], jnp.where(lane_head == h, ch['k_st'], 0.0).astype(BF16), tn_dims,
                            preferred_element_type=F32) for h in range(GLA_HEADS)]

    st = [state_ref[h] for h in range(GLA_HEADS)]
    for ch in chunks:
        ch['st'] = [s.astype(BF16) for s in st]
        st = [st[h] * ch['dec'] + ch['d_state'][h] for h in range(GLA_HEADS)]
    for h in range(GLA_HEADS):
        state_ref[h] = st[h]
    for ch in chunks:
        for h in range(GLA_HEADS):
            vs = slice(h * dv, (h + 1) * dv)
            o = ch['o'][h] + lax.dot_general(ch['q_in'], ch['st'][h], nt_dims, preferred_element_type=F32)
            on = o * lax.rsqrt(jnp.mean(o * o, axis=-1, keepdims=True) + NORM_EPS) * gain_ref[:, vs]
            g = gate_ref[ch['rs'], vs].astype(F32)
            o_ref[ch['rs'], vs] = (on * (g * _sigmoid(g))).astype(o_ref.dtype)


def _gla(qkvgu, loga, gain, bsz, seq, tb):
    t = bsz * seq
    dk = loga.shape[1] // GLA_HEADS
    dv = gain.shape[1] // GLA_HEADS
    nq, nv = GLA_HEADS * dk, GLA_HEADS * dv
    nt = seq // tb
    row = lambda b, i: b * nt + i
    kern = functools.partial(_gla_kernel, n_chunks=tb // GLA_CHUNK, dk=dk, dv=dv)
    return pl.pallas_call(
        kern,
        grid=(bsz, nt),
        in_specs=[pl.BlockSpec((tb, nq), lambda b, i: (row(b, i), 0)),
                  pl.BlockSpec((tb, nq), lambda b, i: (row(b, i), 1)),
                  pl.BlockSpec((tb, nv), lambda b, i: (row(b, i), 1)),
                  pl.BlockSpec((tb, nv), lambda b, i: (row(b, i), 2)),
                  pl.BlockSpec((tb, nq), lambda b, i: (row(b, i), 0)),
                  pl.BlockSpec(gain.shape, lambda b, i: (0, 0))],
        out_specs=pl.BlockSpec((tb, nv), lambda b, i: (row(b, i), 0)),
        out_shape=jax.ShapeDtypeStruct((t, nv), BF16),
        scratch_shapes=[pltpu.VMEM((GLA_HEADS, dv, nq), F32)],
        compiler_params=_cparams("parallel", "arbitrary"),
        name="gla",
    )(qkvgu, qkvgu, qkvgu, qkvgu, loga, gain)


def _transpose_lane_chunks(sets, ch):
    n = len(sets[0])
    chunk = lax.broadcasted_iota(jnp.int32, sets[0][0].shape, 1) // ch
    sets = [list(xs) for xs in sets]
    d = n // 2
    while d >= 1:
        low_half = (chunk & d) == 0
        for xs in sets:
            for r in range(n):
                if r & d:
                    continue
                lo, hi = xs[r], xs[r + d]
                xs[r] = jnp.where(low_half, lo, pltpu.roll(hi, d * ch, axis=1))
                xs[r + d] = jnp.where(low_half, pltpu.roll(lo, (n - d) * ch, axis=1), hi)
        d //= 2
    return sets


def _s5_kernel(u_ref, kgen_ref, ws_ref, wc_ref, a_ref, y_ref, uy_scr, s_scr, h_scr, *,
               n_pairs, bsz, seq, tm):
    n2 = 2 * S5_STATE
    tc, ch = S5_CHUNK, S5_GROUP
    lanes = u_ref.shape[1]
    n_grp = lanes // ch
    n_half = tc // n_grp
    kt = tm // tc
    n_tiles = seq // tm
    rows = n_tiles * kt * bsz

    def tile_rows(b, i, step):
        return pl.ds(pl.multiple_of(b * seq + i * tm + step * kt, kt), kt)

    def gather_tile(i, _):
        for b in range(bsz):
            sets = [[u_ref[tile_rows(b, i, half * n_grp + j8), :].astype(F32) for j8 in range(n_grp)]
                    for half in range(n_half)]
            for half, xs in enumerate(_transpose_lane_chunks(sets, ch)):
                for g, x in enumerate(xs):
                    uy_scr[g, half, i, pl.ds(b, kt, stride=bsz), :] = x
        return 0

    lax.fori_loop(0, n_tiles, gather_tile, 0)

    def group_body(g, _):
        _s5_group(g, kgen_ref, ws_ref, wc_ref, a_ref, uy_scr, s_scr, h_scr, n_pairs=n_pairs, bsz=bsz, rows=rows,
                  lanes=lanes, n_half=n_half, n_tiles=n_tiles, kt=kt)
        return 0

    lax.fori_loop(0, n_grp, group_body, 0)

    def scatter_tile(i, _):
        for b in range(bsz):
            sets = [[uy_scr[g, half, i, pl.ds(b, kt, stride=bsz), :] for g in range(n_grp)]
                    for half in range(n_half)]
            for half, ys in enumerate(_transpose_lane_chunks(sets, ch)):
                for j8, y in enumerate(ys):
                    y_ref[tile_rows(b, i, half * n_grp + j8), :] = y.astype(y_ref.dtype)
        return 0

    lax.fori_loop(0, n_tiles, scatter_tile, 0)


def _s5_group(g, kgen_ref, ws_ref, wc_ref, a_ref, uy_scr, s_scr, h_scr, *, n_pairs, bsz, rows, lanes, n_half,
              n_tiles, kt):
    n2 = 2 * S5_STATE
    tc, ch = S5_CHUNK, S5_GROUP
    u = jnp.concatenate([uy_scr[g, hf].reshape(rows, lanes) for hf in range(n_half)], axis=1).astype(BF16)
    s_scr[...] = jnp.dot(u, ws_ref[g], preferred_element_type=F32)
    a1 = a_ref[g, 0:1, :]
    a2 = a_ref[g, 1:2, :]

    def body(j, carry):
        h, hs = carry
        r0 = pl.multiple_of(j * (2 * bsz), 2 * bsz)
        blk = s_scr[pl.ds(r0, 2 * bsz), :]
        h1 = a1 * h + a2 * hs + blk[0:bsz, 0:n2]
        hs1 = a1 * hs - a2 * h + blk[0:bsz, n2:2 * n2]
        h2 = a1 * h1 + a2 * hs1 + blk[bsz:2 * bsz, 0:n2]
        hs2 = a1 * hs1 - a2 * h1 + blk[bsz:2 * bsz, n2:2 * n2]
        h_scr[pl.ds(r0, 2 * bsz), :] = jnp.concatenate([h, h1], axis=0)
        return h2, hs2

    zero = jnp.zeros((bsz, n2), F32)
    lax.fori_loop(0, n_pairs, body, (zero, zero))
    kg = kgen_ref[g]
    lane_w = lax.broadcasted_iota(jnp.int32, kg.shape, 1)
    tz = jnp.concatenate([kg] + [jnp.where(lane_w >= ch * j, pltpu.roll(kg, ch * j, axis=1), 0.0)
                                 for j in range(1, tc)], axis=0).astype(BF16)
    yg = (jnp.dot(u, tz, preferred_element_type=F32)
          + jnp.dot(h_scr[...].astype(BF16), wc_ref[g], preferred_element_type=F32))
    for hf in range(n_half):
        uy_scr[g, hf] = yg[:, hf * lanes:(hf + 1) * lanes].reshape(n_tiles, kt * bsz, lanes)


def _s5(u_perm, kgen, ws, wc, a_pow, bsz, seq, tm):
    t, mw = u_perm.shape
    lanes = 128
    width = kgen.shape[2]
    g_per_blk = lanes // S5_GROUP
    assert 2 * bsz == V7X_SUBLANES, "the chunk scan walks two chunks per 8-row tile"
    rows = (seq // S5_CHUNK) * bsz
    n_pairs = rows // (2 * bsz)
    per_blk = lambda a: pl.BlockSpec((g_per_blk,) + a.shape[1:], lambda q: (q, 0, 0))
    blk = pl.BlockSpec((t, lanes), lambda q: (0, q))
    kern = functools.partial(_s5_kernel, n_pairs=n_pairs, bsz=bsz, seq=seq, tm=tm)
    kt = tm // S5_CHUNK
    return pl.pallas_call(
        kern,
        grid=(mw // lanes,),
        in_specs=[blk, per_blk(kgen), per_blk(ws), per_blk(wc), per_blk(a_pow)],
        out_specs=blk,
        out_shape=jax.ShapeDtypeStruct((t, mw), BF16),
        scratch_shapes=[pltpu.VMEM((g_per_blk, width // lanes, seq // tm, kt * bsz, lanes), F32),
                        pltpu.VMEM((rows, ws.shape[2]), F32), pltpu.VMEM((rows, wc.shape[1]), F32)],
        compiler_params=_cparams("parallel"),
        name="s5",
    )(u_perm, kgen, ws, wc, a_pow)


def _s5_weights(lam_re, lam_im, log_step, b_re, b_im, c_re, c_im):
    tc = S5_CHUNK
    groups, n = lam_re.shape
    lr = jnp.minimum(lam_re.astype(F32), -1e-4)
    li = lam_im.astype(F32)
    delta = jnp.exp(log_step.astype(F32))[:, None]
    tau = jnp.arange(tc + 1, dtype=F32)
    mag = jnp.exp((lr * delta)[..., None] * tau)
    ang = (li * delta)[..., None] * tau
    pw_re, pw_im = mag * jnp.cos(ang), mag * jnp.sin(ang)
    num_re, num_im = pw_re[..., 1] - 1.0, pw_im[..., 1]
    den = lr * lr + li * li
    f_re = (num_re * lr + num_im * li) / den
    f_im = (num_im * lr - num_re * li) / den
    b_re, b_im = b_re.astype(F32), b_im.astype(F32)
    bb_re = f_re[..., None] * b_re - f_im[..., None] * b_im
    bb_im = f_re[..., None] * b_im + f_im[..., None] * b_re
    ct_re = jnp.transpose(c_re.astype(F32), (0, 2, 1))
    ct_im = jnp.transpose(c_im.astype(F32), (0, 2, 1))
    cp_re = ct_re[:, :, None, :] * pw_re[..., None] - ct_im[:, :, None, :] * pw_im[..., None]
    cp_im = ct_re[:, :, None, :] * pw_im[..., None] + ct_im[:, :, None, :] * pw_re[..., None]
    width = tc * S5_GROUP
    ein = functools.partial(jnp.einsum, precision=HIGHEST)
    kgen = (ein('gnd,gnx->gdx', bb_re, cp_re[:, :, :tc].reshape(groups, n, width))
            - ein('gnd,gnx->gdx', bb_im, cp_im[:, :, :tc].reshape(groups, n, width)))
    wc = jnp.concatenate([cp_re[:, :, 1:].reshape(groups, n, width),
                          -cp_im[:, :, 1:].reshape(groups, n, width)], axis=1)
    rev_re = jnp.transpose(pw_re[..., tc - 1::-1], (0, 2, 1))
    rev_im = jnp.transpose(pw_im[..., tc - 1::-1], (0, 2, 1))
    bt_re = jnp.transpose(bb_re, (0, 2, 1))
    bt_im = jnp.transpose(bb_im, (0, 2, 1))
    s_re = (rev_re[:, :, None, :] * bt_re[:, None] - rev_im[:, :, None, :] * bt_im[:, None]).reshape(groups, width, n)
    s_im = (rev_re[:, :, None, :] * bt_im[:, None] + rev_im[:, :, None, :] * bt_re[:, None]).reshape(groups, width, n)
    ws = jnp.concatenate([s_re, s_im, s_im, s_re], axis=-1)
    a_pow = jnp.stack([jnp.concatenate([pw_re[..., tc], pw_re[..., tc]], axis=-1),
                       jnp.concatenate([-pw_im[..., tc], pw_im[..., tc]], axis=-1)], axis=1)
    return kgen, ws, wc, a_pow


XA_SUB_TILES = 2


def _sub_tiles(tm):
    ts = tm // XA_SUB_TILES
    return [slice(i * ts, (i + 1) * ts) for i in range(XA_SUB_TILES)]


def _xattn_stages(h_subs, subs, k_ref, v_ref, wq_ref, wo_ref, gq_ref, go_ref, o_scr, out_ref):
    d = out_ref.shape[1]
    hd = d // XA_HEADS
    xn = [_rms(h, gq_ref[...]).astype(BF16) for h in h_subs]
    yield
    q = [jnp.dot(x, wq_ref[...], preferred_element_type=F32).astype(BF16) for x in xn]
    yield
    for hh in range(XA_HEADS):
        cs = slice(hh * hd, (hh + 1) * hd)
        for i, rs in enumerate(subs):
            s = lax.dot_general(q[i][:, cs], k_ref[:, cs], (((1,), (1,)), ((), ())),
                                preferred_element_type=F32) * (hd ** -0.5)
            e = jnp.exp(s - jnp.max(s, axis=-1, keepdims=True))
            p = e / jnp.sum(e, axis=-1, keepdims=True)
            o_scr[rs, cs] = _bdot(p, v_ref[:, cs]).astype(BF16)
            yield
    xa = [jnp.dot(o_scr[rs, :], wo_ref[...], preferred_element_type=F32) for rs in subs]
    yield
    for i, rs in enumerate(subs):
        out_ref[rs, :] = h_subs[i] + _rms(xa[i], go_ref[...])
        yield


def _ab_post_kernel(h_ref, oa_ref, ys_ref, u_ref, d_ref, wglu_ref, bglu_ref, wout_ref, g_ref, perm_ref,
                    k_ref, v_ref, wq_ref, wo_ref, gq_ref, go_ref, out_ref, o_scr):
    tm = h_ref.shape[0]
    mw = oa_ref.shape[1]
    subs = _sub_tiles(tm)
    y_ssm = jnp.dot(perm_ref[...], ys_ref[...], preferred_element_type=F32)
    y = y_ssm + d_ref[...] * u_ref[...].astype(F32)
    ob = (_gelu_tanh(y) * _sigmoid(_bdot(y, wglu_ref[...]) + bglu_ref[...])).astype(BF16)
    mix = [jnp.dot(oa_ref[rs, :], wout_ref[0:mw, :], preferred_element_type=F32)
           + jnp.dot(ob[rs, :], wout_ref[mw:2 * mw, :], preferred_element_type=F32) for rs in subs]
    h1 = [h_ref[rs, :] + _rms(mix[i], g_ref[...]) for i, rs in enumerate(subs)]
    for _ in _xattn_stages(h1, subs, k_ref, v_ref, wq_ref, wo_ref, gq_ref, go_ref, o_scr, out_ref):
        pass


def _ab_post(h, o_a, y_ssm, qkvgu, d_skip, w_glu, b_glu, w_out, gain, perm_t, xa, bsz, seq, mem_len, tm):
    t, d = h.shape
    mw = o_a.shape[1]
    nt = seq // tm
    k, v, wq, wo, gq, go = xa
    full = lambda a: pl.BlockSpec(a.shape, lambda b, i: (0, 0))
    rows = lambda w, cb=0: pl.BlockSpec((tm, w), lambda b, i: (b * nt + i, cb))
    kv = pl.BlockSpec((mem_len, d), lambda b, i: (b, 0))
    return pl.pallas_call(
        _ab_post_kernel,
        grid=(bsz, nt),
        in_specs=[rows(d), rows(mw), rows(mw), rows(mw, qkvgu.shape[1] // mw - 1), full(d_skip), full(w_glu),
                  full(b_glu), full(w_out), full(gain), full(perm_t), kv, kv, full(wq), full(wo), full(gq), full(go)],
        out_specs=rows(d),
        out_shape=jax.ShapeDtypeStruct((t, d), F32),
        scratch_shapes=[pltpu.VMEM((tm, d), BF16)],
        compiler_params=_cparams("parallel", "parallel"),
        name="ab_post",
    )(h, o_a, y_ssm, qkvgu, d_skip, w_glu, b_glu, w_out, gain, perm_t, k, v, wq, wo, gq, go)


def _mem_kv_kernel(mem_ref, g_ref, wk_ref, wv_ref, k_ref, v_ref):
    mn = _rms(mem_ref[...], g_ref[...]).astype(BF16)
    k_ref[...] = jnp.dot(mn, wk_ref[...], preferred_element_type=F32).astype(BF16)
    v_ref[...] = jnp.dot(mn, wv_ref[...], preferred_element_type=F32).astype(BF16)


def _mem_kv(mem, gain, wk, wv, tm):
    t, d = mem.shape
    full = lambda a: pl.BlockSpec(a.shape, lambda i: (0, 0))
    rows = pl.BlockSpec((tm, d), lambda i: (i, 0))
    return pl.pallas_call(
        _mem_kv_kernel,
        grid=(t // tm,),
        in_specs=[rows, full(gain), full(wk), full(wv)],
        out_specs=[rows, rows],
        out_shape=[jax.ShapeDtypeStruct((t, d), BF16), jax.ShapeDtypeStruct((t, d), BF16)],
        compiler_params=_cparams("parallel"),
        name="mem_kv",
    )(mem, gain, wk, wv)


def _mlp_kernel(h_ref, gi_ref, go_ref, w1_ref, w2_ref, out_ref, *, tf):
    h = h_ref[...]
    xn = _rms(h, gi_ref[...]).astype(BF16)
    dff = w1_ref.shape[1]
    acts = []
    for j in range(dff // tf):
        a = jnp.maximum(jnp.dot(xn, w1_ref[:, j * tf:(j + 1) * tf], preferred_element_type=F32), 0.0)
        acts.append((a * a).astype(BF16))
    ff = jnp.dot(jnp.concatenate(acts, axis=1), w2_ref[...], preferred_element_type=F32)
    out_ref[...] = h + _rms(ff, go_ref[...])


def _mlp(h, gi, go, w1, w2, tm, tf):
    t, d = h.shape
    full = lambda a: pl.BlockSpec(a.shape, lambda i: (0, 0))
    resident = lambda a: pl.BlockSpec(a.shape, lambda i: (0, 0), pipeline_mode=pl.Buffered(1))
    rows = pl.BlockSpec((tm, d), lambda i: (i, 0))
    return pl.pallas_call(
        functools.partial(_mlp_kernel, tf=tf),
        grid=(t // tm,),
        in_specs=[rows, full(gi), full(go), resident(w1), resident(w2)],
        out_specs=rows,
        out_shape=jax.ShapeDtypeStruct((t, d), F32),
        compiler_params=_cparams("parallel"),
        name="mlp",
    )(h, gi, go, w1, w2)


def _cd_in_kernel(h_ref, g_ref, wbig_ref, wsm_ref, mub_ref, mus_ref, w0_ref, w2_ref, a0_ref, a2_ref, g2_ref,
                  kk_ref, ka_ref, hsum_ref,
                  r_o, lw_o, k_o, v_o, kk_o, a_o, g_o, xb_o, gate_o,
                  carry_b, carry_s, *, mw, n_sub):
    tm = h_ref.shape[0]

    @pl.when(pl.program_id(1) == 0)
    def _():
        carry_b[...] = jnp.zeros_like(carry_b)
        carry_s[...] = jnp.zeros_like(carry_s)

    ts = tm // n_sub
    proj = []
    for sb in range(n_sub):
        hn = _rms(h_ref[sb * ts:(sb + 1) * ts, :], g_ref[...]).astype(BF16)
        proj.append((jnp.dot(hn, wbig_ref[...], preferred_element_type=F32),
                     jnp.dot(hn, wsm_ref[...], preferred_element_type=F32)))

    for sb in range(n_sub):
        rs = slice(sb * ts, (sb + 1) * ts)
        pb, ps = proj[sb]
        p3 = pb[:, 0:3 * mw]
        xb_o[rs, :] = pb[:, 3 * mw:4 * mw].astype(BF16)
        gate_o[rs, :] = pb[:, 4 * mw:5 * mw].astype(BF16)

        prev3 = _shift_rows(p3, 1, carry_b[...])
        prevs = _shift_rows(ps, 1, carry_s[...])
        carry_b[...] = p3[ts - V7X_SUBLANES:ts, :]
        carry_s[...] = ps[ts - V7X_SUBLANES:ts, :]
        p3 = p3 + (prev3 - p3) * mub_ref[...]
        ps = ps + (prevs - ps) * mus_ref[...]
        r = p3[:, 0:mw]
        k = p3[:, mw:2 * mw]
        v = p3[:, 2 * mw:3 * mw]
        lora = ps[:, 0:128]
        wlog = -_softplus(-(w0_ref[...] + _bdot(jnp.tanh(lora), w2_ref[...]))) - 0.5
        a = _sigmoid(a0_ref[...] + _bdot(lora, a2_ref[...]))
        kkr = k * kk_ref[...]
        norm = jnp.sqrt(_bdot(kkr * kkr, hsum_ref[...]))
        r_o[rs, :] = r.astype(BF16)
        lw_o[rs, :] = -jnp.exp(wlog)
        k_o[rs, :] = (k * (1.0 + (a - 1.0) * ka_ref[...])).astype(BF16)
        v_o[rs, :] = v.astype(BF16)
        kk_o[rs, :] = (kkr / jnp.maximum(norm, 1e-12)).astype(BF16)
        a_o[rs, :] = a.astype(BF16)
        g_o[rs, :] = _bdot(_sigmoid(ps[:, 128:256]), g2_ref[...]).astype(BF16)


def _cd_in(h, gain, wbig, wsm, vecs, mats, bsz, seq, tm, mw):
    t, d = h.shape
    nt = seq // tm
    full = lambda a: pl.BlockSpec(a.shape, lambda b, i: (0, 0))
    rows = lambda w: pl.BlockSpec((tm, w), lambda b, i: (b * nt + i, 0))
    (mub, mus, w0, a0, kk_w, ka_w) = vecs
    (w2p, a2p, g2, hsum) = mats
    args = (h, gain, wbig, wsm, mub, mus, w0, w2p, a0, a2p, g2, kk_w, ka_w, hsum)
    kern = functools.partial(_cd_in_kernel, mw=mw, n_sub=4)
    return pl.pallas_call(
        kern,
        grid=(bsz, nt),
        in_specs=[rows(d)] + [full(a) for a in args[1:]],
        out_specs=[rows(mw)] * len(CD_IN_OUT_DTYPES),
        out_shape=[jax.ShapeDtypeStruct((t, mw), dt) for dt in CD_IN_OUT_DTYPES],
        scratch_shapes=[pltpu.VMEM((V7X_SUBLANES, 3 * mw), F32), pltpu.VMEM((V7X_SUBLANES, wsm.shape[1]), F32)],
        compiler_params=_cparams("parallel", "arbitrary"),
        name="cd_in",
    )(*args)


def _rwkv_kernel(r_ref, lw_ref, k_ref, v_ref, kk_ref, a_ref, g_ref, rk_ref, lng_ref, lnb_ref, hsum_ref,
                 o_ref, state_ref, y_scr, *, slab_heads, n_chunks):
    c = RWKV_CHUNK
    hd = RWKV_HEAD
    assert c == hd, "one block mask serves both the (head, s) and the (head, d) layouts"
    sw = slab_heads * hd
    n_slabs = r_ref.shape[1] // sw

    @pl.when(pl.program_id(1) == 0)
    def _():
        state_ref[...] = jnp.zeros_like(state_ref)

    tb = n_chunks * c
    tri = lax.broadcasted_iota(jnp.int32, (tb, tb), 0)
    tci = lax.broadcasted_iota(jnp.int32, (tb, tb), 1)
    blk_tril = jnp.where(((tri // c) == (tci // c)) & (tci <= tri), 1.0, 0.0).astype(BF16)
    cum_all = _split3_dot(blk_tril, lw_ref[...])
    blk_m = (lax.broadcasted_iota(jnp.int32, (sw, sw), 0) // c) == (lax.broadcasted_iota(jnp.int32, (sw, sw), 1) // c)
    wide_t = lax.broadcasted_iota(jnp.int32, (c, sw), 0)
    wide_s = lax.broadcasted_iota(jnp.int32, (c, sw), 1) % c
    strict_w = wide_s < wide_t
    incl_w = wide_s <= wide_t
    eye_w = jnp.where(wide_s == wide_t, 1.0, 0.0)

    def bdiag(x):
        x16 = x.astype(BF16)
        return jnp.where(blk_m, jnp.concatenate([x16] * slab_heads, axis=0), jnp.zeros((), BF16))

    def mm(a, w16):
        return jnp.dot(a.astype(BF16), w16, preferred_element_type=F32)

    nt_dims = (((1,), (1,)), ((), ()))
    chains = []
    for chunk in range(n_chunks):
        rs = slice(chunk * c, (chunk + 1) * c)
        lw = lw_ref[rs, :]
        cum = cum_all[rs, :]
        cum_last = cum[c - 1:c, :]
        r = r_ref[rs, :].astype(F32)
        k = k_ref[rs, :].astype(F32)
        v = v_ref[rs, :].astype(F32)
        kk = kk_ref[rs, :].astype(F32)
        bvec = kk * a_ref[rs, :].astype(F32)
        inv_g = jnp.exp(-cum)
        to_end = jnp.exp(cum_last - cum)
        rt = r * jnp.exp(cum)
        kp = kk * jnp.exp(cum - lw)
        be = bvec * inv_g
        kh = k * inv_g
        bb = bvec * to_end
        kb = k * to_end
        g_end = jnp.exp(cum_last)
        for s in range(n_slabs):
            ls = slice(s * sw, (s + 1) * sw)
            chains.append(dict(rs=rs, ls=ls, slab=s, rt=rt[:, ls], kp=kp[:, ls], v=v[:, ls], be=be[:, ls],
                               kh=kh[:, ls], kb=kb[:, ls], bb=bb[:, ls], g_end=g_end[:, ls]))

    for ch in chains:
        lhs2 = jnp.concatenate([ch['kp'], ch['rt']], axis=0).astype(BF16)
        ab = lax.dot_general(lhs2, bdiag(ch['be']), nt_dims, preferred_element_type=F32)
        ak = lax.dot_general(lhs2, bdiag(ch['kh']), nt_dims, preferred_element_type=F32)
        ch['a_kb'] = jnp.where(strict_w, ab[0:c], 0.0)
        ch['a_rb'] = jnp.where(incl_w, ab[c:2 * c], 0.0).astype(BF16)
        ch['a_kr'] = jnp.concatenate([jnp.where(strict_w, ak[0:c], 0.0), jnp.where(incl_w, ak[c:2 * c], 0.0)],
                                     axis=0).astype(BF16)
    for ch in chains:
        x = -ch['a_kb']
        ch['t'] = eye_w + x
        ch['x'] = mm(x, bdiag(x))
    for _ in range(int(math.log2(c)) - 2):
        for ch in chains:
            res = mm(jnp.concatenate([ch['t'], ch['x']], axis=0), bdiag(ch['x']))
            ch['t'] = ch['t'] + res[0:c]
            ch['x'] = res[c:2 * c]
    for ch in chains:
        ch['t'] = ch['t'] + mm(ch['t'], bdiag(ch['x']))
    for ch in chains:
        l_hi, l_lo = _split_bf16(ch['a_kb'])
        t_hi, t_lo = _split_bf16(ch['t'])
        lt = mm(jnp.concatenate([l_hi, l_lo], axis=0), bdiag(t_hi))
        ch['resid'] = eye_w - ch['t'] - (lt[0:c] + lt[c:2 * c] + mm(l_hi, bdiag(t_lo)))
    for ch in chains:
        ch['t16'] = (ch['t'] + mm(ch['t'], bdiag(ch['resid']))).astype(BF16)
    for ch in chains:
        ch['w_tok'] = mm(ch['t16'], bdiag(ch['kp']))
        ch['av'] = mm(ch['a_kr'], bdiag(ch['v']))
    for ch in chains:
        ch['u_tok'] = mm(ch['t16'], bdiag(ch['av'][0:c]))
    tn_dims = (((0,), (0,)), ((), ()))
    for ch in chains:
        ch['q16'] = (ch['rt'] - mm(ch['a_rb'], bdiag(ch['w_tok']))).astype(BF16)
        ch['y0'] = ch['av'][c:2 * c] - mm(ch['a_rb'], bdiag(ch['u_tok']))
    for ch in chains:
        bb16 = ch['bb'].astype(BF16)
        wb = lax.dot_general(ch['w_tok'].astype(BF16), bb16, tn_dims, preferred_element_type=F32)
        ch['wb'] = jnp.where(blk_m, wb, 0.0).astype(BF16)
        d_t = lax.dot_general(jnp.concatenate([ch['v'], -ch['u_tok']], axis=0).astype(BF16),
                              jnp.concatenate([ch['kb'].astype(BF16), bb16], axis=0), tn_dims,
                              preferred_element_type=F32)
        ch['d_t'] = jnp.where(blk_m, d_t, 0.0)

    for ch in chains:
        s = ch['slab']
        p_t = state_ref[s]
        ch['p16'] = p_t.astype(BF16)
        state_ref[s] = p_t * ch['g_end'] - jnp.dot(ch['p16'], ch['wb'], preferred_element_type=F32) + ch['d_t']
    for ch in chains:
        y_scr[ch['rs'], ch['ls']] = ch['y0'] + lax.dot_general(ch['q16'], ch['p16'], nt_dims,
                                                               preferred_element_type=F32)

    hsum = hsum_ref[...]
    y = y_scr[...]
    r = r_ref[...].astype(F32)
    v = v_ref[...].astype(F32)
    mean = _split_dot(y, hsum) * (1.0 / hd)
    yc = y - mean
    var = _bdot(yc * yc, hsum) * (1.0 / hd)
    yn = yc * lax.rsqrt(var + RWKV_GN_EPS) * lng_ref[...] + lnb_ref[...]
    bonus = _bdot(r * k_ref[...].astype(F32) * rk_ref[...], hsum) * v
    o_ref[...] = ((yn + bonus) * g_ref[...].astype(F32)).astype(o_ref.dtype)


def _rwkv(r, lw, k, v, kk, a, g, rk, lng, lnb, hsum, bsz, seq, slab_heads=4, n_chunks=8):
    t, mw = r.shape
    c = RWKV_CHUNK * n_chunks
    nt = seq // c
    sw = slab_heads * RWKV_HEAD
    full = lambda x: pl.BlockSpec(x.shape, lambda b, i: (0, 0))
    rows = pl.BlockSpec((c, mw), lambda b, i: (b * nt + i, 0))
    kern = functools.partial(_rwkv_kernel, slab_heads=slab_heads, n_chunks=n_chunks)
    return pl.pallas_call(
        kern,
        grid=(bsz, nt),
        in_specs=[rows] * 7 + [full(rk), full(lng), full(lnb), full(hsum)],
        out_specs=rows,
        out_shape=jax.ShapeDtypeStruct((t, mw), BF16),
        scratch_shapes=[pltpu.VMEM((mw // sw, sw, sw), F32), pltpu.VMEM((c, mw), F32)],
        compiler_params=_cparams("parallel", "arbitrary"),
        name="rwkv",
    )(r, lw, k, v, kk, a, g, rk, lng, lnb, hsum)


def _cd_post_kernel(h_ref, oc_ref, xb_ref, gate_ref, cw_ref, cb_ref, wa_ref, ba_ref, wx_ref, bx_ref, lam_ref,
                    wout_ref, g_ref, k_ref, v_ref, wq_ref, wo_ref, gq_ref, go_ref, out_ref,
                    o_scr, od_scr, la_ref, lb_ref, carry_ref, carry_x):
    tm = h_ref.shape[0]
    mw = oc_ref.shape[1]
    sub = V7X_SUBLANES
    subs = _sub_tiles(tm)
    rowi = lax.broadcasted_iota(jnp.int32, (sub, mw), 0)

    def lru_coeffs(rs, cx):
        xb = xb_ref[rs, :].astype(F32)
        xc = cb_ref[...] + cw_ref[LRU_CONV - 1:LRU_CONV, :] * xb
        for sh in range(1, LRU_CONV):
            xc = xc + cw_ref[LRU_CONV - 1 - sh:LRU_CONV - sh, :] * _shift_rows(xb, sh, cx)
        rg = _sigmoid(_bdot(xc, wa_ref[...]) + ba_ref[...])
        ig = _sigmoid(_bdot(xc, wx_ref[...]) + bx_ref[...])
        log_a = -LRU_C * rg * _softplus(-lam_ref[...])
        la_ref[rs, :] = jnp.exp(log_a)
        th = jnp.tanh(log_a)
        lb_ref[rs, :] = jnp.sqrt(-2.0 * th / (1.0 - th)) * (ig * xc)
        return xb[xb.shape[0] - sub:, :]

    def scan8(a, b, h):
        sh = 1
        while sh < sub:
            a_sh = jnp.where(rowi >= sh, pltpu.roll(a, sh, axis=0), 1.0)
            b_sh = jnp.where(rowi >= sh, pltpu.roll(b, sh, axis=0), 0.0)
            b = b + a * b_sh
            a = a * a_sh
            sh *= 2
        return b + a * h

    @pl.when(pl.program_id(1) == 0)
    def _():
        carry_ref[...] = jnp.zeros_like(carry_ref)
        carry_x[...] = jnp.zeros_like(carry_x)

    h = carry_ref[0:1, :]
    cx = carry_x[...]
    for rs in subs:
        cx = lru_coeffs(rs, cx)
        for r0 in range(rs.start, rs.stop, 2 * sub):
            h1 = scan8(la_ref[r0:r0 + sub, :], lb_ref[r0:r0 + sub, :], h)
            h2 = scan8(la_ref[r0 + sub:r0 + 2 * sub, :], lb_ref[r0 + sub:r0 + 2 * sub, :], h1[sub - 1:sub, :])
            h = h2[sub - 1:sub, :]
            gate = gate_ref[r0:r0 + 2 * sub, :].astype(F32)
            od_scr[r0:r0 + 2 * sub, :] = (jnp.concatenate([h1, h2], axis=0) * _gelu_tanh(gate)).astype(BF16)
    carry_ref[0:1, :] = h
    carry_x[...] = cx

    mix = [jnp.dot(oc_ref[rs, :], wout_ref[0:mw, :], preferred_element_type=F32)
           + jnp.dot(od_scr[rs, :], wout_ref[mw:2 * mw, :], preferred_element_type=F32) for rs in subs]
    h1s = [h_ref[rs, :] + _rms(mix[i], g_ref[...]) for i, rs in enumerate(subs)]
    for _ in _xattn_stages(h1s, subs, k_ref, v_ref, wq_ref, wo_ref, gq_ref, go_ref, o_scr, out_ref):
        pass


def _cd_post(h, o_c, xb, gate, lru, w_out, gain, xa, bsz, seq, mem_len, tm):
    t, d = h.shape
    mw = o_c.shape[1]
    nt = seq // tm
    k, v, wq, wo, gq, go = xa
    full = lambda a: pl.BlockSpec(a.shape, lambda b, i: (0, 0))
    rows = lambda w: pl.BlockSpec((tm, w), lambda b, i: (b * nt + i, 0))
    kv = pl.BlockSpec((mem_len, d), lambda b, i: (b, 0))
    return pl.pallas_call(
        _cd_post_kernel,
        grid=(bsz, nt),
        in_specs=[rows(d), rows(mw), rows(mw), rows(mw)] + [full(a) for a in lru] + [full(w_out), full(gain),
                  kv, kv, full(wq), full(wo), full(gq), full(go)],
        out_specs=rows(d),
        out_shape=jax.ShapeDtypeStruct((t, d), F32),
        scratch_shapes=[pltpu.VMEM((tm, d), BF16), pltpu.VMEM((tm, mw), BF16), pltpu.VMEM((tm, mw), F32),
                        pltpu.VMEM((tm, mw), F32), pltpu.VMEM((V7X_SUBLANES, mw), F32),
                        pltpu.VMEM((V7X_SUBLANES, mw), F32)],
        compiler_params=_cparams("parallel", "arbitrary"),
        name="cd_post",
    )(h, o_c, xb, gate, *lru, w_out, gain, k, v, wq, wo, gq, go)


def _row(vec):
    return vec.astype(F32).reshape(1, -1)


def _block_diag(blocks):
    nb, bi, bo = blocks.shape
    eye = jnp.eye(nb, dtype=blocks.dtype)
    return (eye[:, None, :, None] * blocks[:, :, None, :]).reshape(nb * bi, nb * bo)


def _head_sum_matrix(width, head):
    idx = jnp.arange(width) // head
    return (idx[:, None] == idx[None, :]).astype(BF16)


def _layer_ab(h, gains, p, xa, bsz, seq, mem_len, tm):
    d = h.shape[1]
    mw = d // 2
    dk = p['gla_b_decay'].shape[0] // GLA_HEADS
    nq = GLA_HEADS * dk
    w_in = p['ab_w_in']
    rank = p['gla_w_decay2'].shape[0]
    o_dlr = 2 * nq + 2 * mw
    w_main = jnp.concatenate([w_in[:, :o_dlr], w_in[:, o_dlr + rank:]], axis=1).astype(BF16)
    w_dlr = jnp.pad(w_in[:, o_dlr:o_dlr + rank], ((0, 0), (0, 128 - rank))).astype(BF16)
    w_decay2 = jnp.pad(p['gla_w_decay2'].astype(F32), ((0, 128 - rank), (0, 0)))
    perm = _chunk_step_perm(tm, S5_CHUNK)
    qkvgu, loga, u_perm = _ab_in(h, _row(gains[0]), w_main, w_dlr, w_decay2, _row(p['gla_b_decay']), perm, tm, mw)
    o_a = _gla(qkvgu, loga, _row(p['gla_norm_gain']), bsz, seq, tb=256)
    kgen, ws, wc, a_pow = _s5_weights(p['s5_lambda_re'], p['s5_lambda_im'], p['s5_log_step'], p['s5_b_re'],
                                      p['s5_b_im'], p['s5_c_re'], p['s5_c_im'])
    y_perm = _s5(u_perm, kgen, ws.astype(BF16), wc.astype(BF16), a_pow, bsz, seq, tm)
    return _ab_post(h, o_a, y_perm, qkvgu, _row(p['s5_d']), p['s5_w_glu'].astype(BF16), _row(p['s5_b_glu']),
                    p['ab_w_out'].astype(BF16), _row(gains[1]), perm.T, xa, bsz, seq, mem_len, tm)


def _layer_cd(h, gains, p, xa, bsz, seq, mem_len, tm):
    d = h.shape[1]
    mw = d // 2
    w_in = p['cd_w_in']
    dr = p['rwkv_w2'].shape[0]
    ar = p['rwkv_a2'].shape[0]
    gr = p['rwkv_g2'].shape[0]
    assert dr + ar == 128 and gr == 128
    o = [0, mw, mw + dr, 2 * mw + dr, 3 * mw + dr, 3 * mw + dr + ar, 3 * mw + dr + ar + gr]
    col = lambda i, width: w_in[:, o[i]:o[i] + width]
    wbig = jnp.concatenate([col(0, mw), col(2, mw), col(3, mw), w_in[:, o[6]:]], axis=1).astype(BF16)
    wsm = jnp.concatenate([col(1, dr), col(4, ar), col(5, gr)], axis=1).astype(BF16)
    mu = p['rwkv_mu'].astype(F32)
    mseg = lambda i, width: mu[o[i]:o[i] + width]
    mub = jnp.concatenate([mseg(0, mw), mseg(2, mw), mseg(3, mw)]).reshape(1, -1)
    mus = jnp.concatenate([mseg(1, dr), mseg(4, ar), mseg(5, gr)]).reshape(1, -1)
    w2p = jnp.pad(p['rwkv_w2'], ((0, ar), (0, 0))).astype(BF16)
    a2p = jnp.pad(p['rwkv_a2'], ((dr, 0), (0, 0))).astype(BF16)
    hsum = _head_sum_matrix(mw, RWKV_HEAD)
    vecs = (mub, mus, _row(p['rwkv_w0']), _row(p['rwkv_a0']), _row(p['rwkv_k_k']), _row(p['rwkv_k_a']))
    mats = (w2p, a2p, p['rwkv_g2'].astype(BF16), hsum)
    r, lw, k, v, kk, a, g, xb, gate = _cd_in(h, _row(gains[0]), wbig, wsm, vecs, mats, bsz, seq, tm, mw)
    o_c = _rwkv(r, lw, k, v, kk, a, g, _row(p['rwkv_r_k']), _row(p['rwkv_ln_gain']), _row(p['rwkv_ln_bias']),
                hsum, bsz, seq)
    lru = (p['lru_conv_w'].astype(F32), _row(p['lru_conv_b']), _block_diag(p['lru_w_a']).astype(BF16),
           _row(p['lru_b_a']), _block_diag(p['lru_w_x']).astype(BF16), _row(p['lru_b_x']), _row(p['lru_lambda']))
    return _cd_post(h, o_c, xb, gate, lru, p['cd_w_out'].astype(BF16), _row(gains[1]), xa, bsz, seq, mem_len, tm)


def kernel(x, mem, norm_gain, xa_wq, xa_wk, xa_wv, xa_wo, mlp_w1, mlp_w2, ab_w_in, gla_w_decay2, gla_b_decay, gla_norm_gain, s5_lambda_re, s5_lambda_im, s5_log_step, s5_b_re, s5_b_im, s5_c_re, s5_c_im, s5_d, s5_w_glu, s5_b_glu, ab_w_out, cd_w_in, rwkv_mu, rwkv_w0, rwkv_w2, rwkv_a0, rwkv_a2, rwkv_g2, rwkv_k_k, rwkv_k_a, rwkv_r_k, rwkv_ln_gain, rwkv_ln_bias, lru_conv_w, lru_conv_b, lru_w_a, lru_b_a, lru_w_x, lru_b_x, lru_lambda, cd_w_out):
    bsz, seq, d = x.shape
    mem_len = mem.shape[1]
    depth = norm_gain.shape[0]
    tm = min(512, seq)
    ab = dict(ab_w_in=ab_w_in, gla_w_decay2=gla_w_decay2, gla_b_decay=gla_b_decay, gla_norm_gain=gla_norm_gain,
              s5_lambda_re=s5_lambda_re, s5_lambda_im=s5_lambda_im, s5_log_step=s5_log_step, s5_b_re=s5_b_re,
              s5_b_im=s5_b_im, s5_c_re=s5_c_re, s5_c_im=s5_c_im, s5_d=s5_d, s5_w_glu=s5_w_glu, s5_b_glu=s5_b_glu,
              ab_w_out=ab_w_out)
    cd = dict(cd_w_in=cd_w_in, rwkv_mu=rwkv_mu, rwkv_w0=rwkv_w0, rwkv_w2=rwkv_w2, rwkv_a0=rwkv_a0,
              rwkv_a2=rwkv_a2, rwkv_g2=rwkv_g2, rwkv_k_k=rwkv_k_k, rwkv_k_a=rwkv_k_a, rwkv_r_k=rwkv_r_k,
              rwkv_ln_gain=rwkv_ln_gain, rwkv_ln_bias=rwkv_ln_bias, lru_conv_w=lru_conv_w, lru_conv_b=lru_conv_b,
              lru_w_a=lru_w_a, lru_b_a=lru_b_a, lru_w_x=lru_w_x, lru_b_x=lru_b_x, lru_lambda=lru_lambda,
              cd_w_out=cd_w_out)
    h = x.astype(F32).reshape(bsz * seq, d)
    mem2 = mem.astype(F32).reshape(bsz * mem_len, d)
    for layer in range(depth):
        g = norm_gain[layer]
        i = layer // 2
        km, vm = _mem_kv(mem2, _row(g[6]), xa_wk[layer].astype(BF16), xa_wv[layer].astype(BF16), mem_len)
        xa = (km, vm, xa_wq[layer].astype(BF16), xa_wo[layer].astype(BF16), _row(g[2]), _row(g[3]))
        if layer % 2 == 0:
            h = _layer_ab(h, g, {n: w[i] for n, w in ab.items()}, xa, bsz, seq, mem_len, tm)
        else:
            h = _layer_cd(h, g, {n: w[i] for n, w in cd.items()}, xa, bsz, seq, mem_len, tm)
        h = _mlp(h, _row(g[4]), _row(g[5]), mlp_w1[layer].astype(BF16), mlp_w2[layer].astype(BF16),
                 tm=tm, tf=1024)
    return h.reshape(bsz, seq, d).astype(x.dtype)
```

```python
import functools
import math

import jax
import jax.numpy as jnp
from jax import lax
from jax.experimental import pallas as pl
from jax.experimental.pallas import tpu as pltpu

F32 = jnp.float32
BF16 = jnp.bfloat16
HIGHEST = lax.Precision.HIGHEST

NORM_EPS = 1e-6
GLA_HEADS = 4
GLA_TAU = 16.0
GLA_CHUNK = 64
S5_GROUP = 16
S5_STATE = 64
S5_CHUNK = 16
RWKV_HEAD = 64
RWKV_CHUNK = 64
RWKV_GN_EPS = 64e-5
LRU_CONV = 4
LRU_C = 8.0
XA_HEADS = 4

V7X_SUBLANES = 8
V7X_LANES = 128
V7X_MXU_DIM = 256
V7X_VMEM_BYTES = 64 * 1024 * 1024
VMEM_LIMIT_BYTES = (3 * V7X_VMEM_BYTES) // 4

ROW_TILE = 512
CD_IN_SUB_TILES = 2
GLA_ROWS_PER_STEP = 256
RWKV_CHUNKS_PER_STEP = 8
RWKV_SLAB_HEADS = V7X_MXU_DIM // RWKV_HEAD
MLP_FF_TILE = 1024
ROW_SUB_TILES = 2

CD_IN_OUT_DTYPES = (BF16, F32, F32, BF16, BF16, BF16, BF16, BF16, BF16, BF16)


def _cparams(*semantics):
    return pltpu.CompilerParams(dimension_semantics=semantics, vmem_limit_bytes=VMEM_LIMIT_BYTES)


def _rms(x, gain):
    return x * lax.rsqrt(jnp.mean(x * x, axis=-1, keepdims=True) + NORM_EPS) * gain


def _bdot(a, b):
    return jnp.dot(a.astype(BF16), b.astype(BF16), preferred_element_type=F32)


def _split_bf16(x):
    hi = x.astype(BF16)
    return hi, (x - hi.astype(F32)).astype(BF16)


def _dot3(a, b):
    a_hi, a_lo = _split_bf16(a)
    b_hi, b_lo = _split_bf16(b)
    return (jnp.dot(a_hi, b_hi, preferred_element_type=F32) + jnp.dot(a_lo, b_hi, preferred_element_type=F32)
            + jnp.dot(a_hi, b_lo, preferred_element_type=F32))


def _split_dot(x, w_bf16):
    hi, lo = _split_bf16(x)
    return (jnp.dot(hi, w_bf16, preferred_element_type=F32)
            + jnp.dot(lo, w_bf16, preferred_element_type=F32))


def _split3_dot(w_bf16, x):
    hi = x.astype(BF16)
    r1 = x - hi.astype(F32)
    mid = r1.astype(BF16)
    lo = (r1 - mid.astype(F32)).astype(BF16)
    return (jnp.dot(w_bf16, hi, preferred_element_type=F32) + jnp.dot(w_bf16, mid, preferred_element_type=F32)
            + jnp.dot(w_bf16, lo, preferred_element_type=F32))


def _sigmoid(x):
    return 1.0 / (1.0 + jnp.exp(-x))


def _softplus(x):
    return jnp.maximum(x, 0.0) + jnp.log(1.0 + jnp.exp(-jnp.abs(x)))


def _gelu_tanh(x):
    c = math.sqrt(2.0 / math.pi)
    return x * (0.5 * (1.0 + jnp.tanh(c * (x + 0.044715 * (x * x * x)))))


def _sub_tiles(tm):
    ts = tm // ROW_SUB_TILES
    return [slice(i * ts, (i + 1) * ts) for i in range(ROW_SUB_TILES)]


def _tril_mask(n, strict=False):
    row = lax.broadcasted_iota(jnp.int32, (n, n), 0)
    col = lax.broadcasted_iota(jnp.int32, (n, n), 1)
    return (col < row) if strict else (col <= row)


def _shift_rows(x, shift, carry):
    rolled = pltpu.roll(x, shift, axis=0)
    crolled = pltpu.roll(carry, shift, axis=0)
    rowi = lax.broadcasted_iota(jnp.int32, carry.shape, 0)
    first = jnp.where(rowi < shift, crolled, rolled[0:V7X_SUBLANES])
    return jnp.concatenate([first, rolled[V7X_SUBLANES:]], axis=0)


def _ab_in_kernel(h_ref, g_ref, w_ref, wd_ref, wd2_ref, bd_ref, perm_ref, tril_ref, out_ref, loga_ref, up_ref):
    subs = _sub_tiles(h_ref.shape[0])
    hn = [_rms(h_ref[rs, :], g_ref[...]).astype(BF16) for rs in subs]
    out16 = [jnp.dot(x, w_ref[...], preferred_element_type=F32).astype(BF16) for x in hn]
    dlr = [jnp.dot(x, wd_ref[...], preferred_element_type=F32) for x in hn]
    for i, rs in enumerate(subs):
        out_ref[rs, :] = out16[i]
        z = _dot3(dlr[i], wd2_ref[...]) + bd_ref[...]
        loga_ref[rs, :] = _split3_dot(tril_ref[...], -_softplus(-z) * (1.0 / GLA_TAU))
    mw = up_ref.shape[1]
    u16 = jnp.concatenate([o[:, o.shape[1] - mw:] for o in out16], axis=0)
    up_ref[...] = jnp.dot(perm_ref[...], u16, preferred_element_type=F32).astype(BF16)


def _ab_in(h, gain, w_main, w_dlr, w_decay2, b_decay, perm, tm, mw):
    t, d = h.shape
    n_main = w_main.shape[1]
    n_dk = w_decay2.shape[1]
    full = lambda a: pl.BlockSpec(a.shape, lambda i: (0, 0))
    rows = lambda w: pl.BlockSpec((tm, w), lambda i: (i, 0))
    tril = _chunk_tril(tm // ROW_SUB_TILES, GLA_CHUNK)
    return pl.pallas_call(
        _ab_in_kernel,
        grid=(t // tm,),
        in_specs=[rows(d), full(gain), full(w_main), full(w_dlr), full(w_decay2), full(b_decay), full(perm),
                  full(tril)],
        out_specs=[rows(n_main), rows(n_dk), rows(mw)],
        out_shape=[jax.ShapeDtypeStruct((t, n_main), BF16), jax.ShapeDtypeStruct((t, n_dk), F32),
                   jax.ShapeDtypeStruct((t, mw), BF16)],
        compiler_params=_cparams("parallel"),
        name="ab_in",
    )(h, gain, w_main, w_dlr, w_decay2, b_decay, perm, tril)


def _chunk_tril(n, c):
    idx = jnp.arange(n)
    return (((idx[:, None] // c) == (idx[None, :] // c)) & (idx[None, :] <= idx[:, None])).astype(BF16)


def _chunk_step_perm(tm, tc):
    dst = jnp.arange(tm)
    src = (dst % (tm // tc)) * tc + dst // (tm // tc)
    return (src[:, None] == jnp.arange(tm)[None, :]).astype(BF16)


def _gla_kernel(q_ref, k_ref, v_ref, gate_ref, la_ref, gain_ref, o_ref, state_ref, *, n_chunks, dk, dv):
    c = GLA_CHUNK

    @pl.when(pl.program_id(1) == 0)
    def _():
        state_ref[...] = jnp.zeros_like(state_ref)

    tril = _tril_mask(c)
    scale = dk ** -0.5
    lane_head = lax.broadcasted_iota(jnp.int32, (c, GLA_HEADS * dk), 1) // dk
    nt_dims = (((1,), (1,)), ((), ()))
    tn_dims = (((0,), (0,)), ((), ()))

    chunks = []
    for i in range(n_chunks):
        rs = slice(i * c, (i + 1) * c)
        b = la_ref[rs, :]
        b_last = b[c - 1:c, :]
        k = k_ref[rs, :].astype(F32)
        chunks.append(dict(rs=rs, q_in=(q_ref[rs, :].astype(F32) * jnp.exp(b) * scale).astype(BF16),
                           k_in=(k * jnp.exp(-b)).astype(BF16), k_st=k * jnp.exp(b_last - b),
                           dec=jnp.exp(b_last)))
    for ch in chunks:
        ch['v'] = [v_ref[ch['rs'], h * dv:(h + 1) * dv].astype(BF16) for h in range(GLA_HEADS)]
        zero16 = jnp.zeros((), BF16)
        ch['scores'] = [
            jnp.where(tril, lax.dot_general(jnp.where(lane_head == h, ch['q_in'], zero16), ch['k_in'], nt_dims,
                                            preferred_element_type=F32), 0.0).astype(BF16)
            for h in range(GLA_HEADS)]
    for ch in chunks:
        ch['o'] = [jnp.dot(ch['scores'][h], ch['v'][h], preferred_element_type=F32) for h in range(GLA_HEADS)]
        ch['d_state'] = [
            lax.dot_general(ch['v'][h], jnp.where(lane_head == h, ch['k_st'], 0.0).astype(BF16), tn_dims,
                            preferred_element_type=F32) for h in range(GLA_HEADS)]

    st = [state_ref[h] for h in range(GLA_HEADS)]
    for ch in chunks:
        ch['st'] = [s.astype(BF16) for s in st]
        st = [st[h] * ch['dec'] + ch['d_state'][h] for h in range(GLA_HEADS)]
    for h in range(GLA_HEADS):
        state_ref[h] = st[h]
    for ch in chunks:
        for h in range(GLA_HEADS):
            vs = slice(h * dv, (h + 1) * dv)
            o = ch['o'][h] + lax.dot_general(ch['q_in'], ch['st'][h], nt_dims, preferred_element_type=F32)
            on = o * lax.rsqrt(jnp.mean(o * o, axis=-1, keepdims=True) + NORM_EPS) * gain_ref[:, vs]
            g = gate_ref[ch['rs'], vs].astype(F32)
            o_ref[ch['rs'], vs] = (on * (g * _sigmoid(g))).astype(o_ref.dtype)


def _gla(qkvgu, loga, gain, bsz, seq, tb):
    t = bsz * seq
    dk = loga.shape[1] // GLA_HEADS
    dv = gain.shape[1] // GLA_HEADS
    nq, nv = GLA_HEADS * dk, GLA_HEADS * dv
    nt = seq // tb
    row = lambda b, i: b * nt + i
    kern = functools.partial(_gla_kernel, n_chunks=tb // GLA_CHUNK, dk=dk, dv=dv)
    return pl.pallas_call(
        kern,
        grid=(bsz, nt),
        in_specs=[pl.BlockSpec((tb, nq), lambda b, i: (row(b, i), 0)),
                  pl.BlockSpec((tb, nq), lambda b, i: (row(b, i), 1)),
                  pl.BlockSpec((tb, nv), lambda b, i: (row(b, i), 1)),
                  pl.BlockSpec((tb, nv), lambda b, i: (row(b, i), 2)),
                  pl.BlockSpec((tb, nq), lambda b, i: (row(b, i), 0)),
                  pl.BlockSpec(gain.shape, lambda b, i: (0, 0))],
        out_specs=pl.BlockSpec((tb, nv), lambda b, i: (row(b, i), 0)),
        out_shape=jax.ShapeDtypeStruct((t, nv), BF16),
        scratch_shapes=[pltpu.VMEM((GLA_HEADS, dv, nq), F32)],
        compiler_params=_cparams("parallel", "arbitrary"),
        name="gla",
    )(qkvgu, qkvgu, qkvgu, qkvgu, loga, gain)


def _transpose_lane_chunks(sets, ch):
    n = len(sets[0])
    chunk = lax.broadcasted_iota(jnp.int32, sets[0][0].shape, 1) // ch
    sets = [list(xs) for xs in sets]
    d = n // 2
    while d >= 1:
        low_half = (chunk & d) == 0
        for xs in sets:
            for r in range(n):
                if r & d:
                    continue
                lo, hi = xs[r], xs[r + d]
                xs[r] = jnp.where(low_half, lo, pltpu.roll(hi, d * ch, axis=1))
                xs[r + d] = jnp.where(low_half, pltpu.roll(lo, (n - d) * ch, axis=1), hi)
        d //= 2
    return sets


def _s5_kernel(u_ref, kgen_ref, ws_ref, wc_ref, a_ref, y_ref, uy_scr, s_scr, h_scr, *,
               n_pairs, bsz, seq, tm):
    n2 = 2 * S5_STATE
    tc, ch = S5_CHUNK, S5_GROUP
    lanes = u_ref.shape[1]
    n_grp = lanes // ch
    n_half = tc // n_grp
    kt = tm // tc
    n_tiles = seq // tm
    rows = n_tiles * kt * bsz

    def tile_rows(b, i, step):
        return pl.ds(pl.multiple_of(b * seq + i * tm + step * kt, kt), kt)

    def gather_tile(i, _):
        for b in range(bsz):
            sets = [[u_ref[tile_rows(b, i, half * n_grp + j8), :].astype(F32) for j8 in range(n_grp)]
                    for half in range(n_half)]
            for half, xs in enumerate(_transpose_lane_chunks(sets, ch)):
                for g, x in enumerate(xs):
                    uy_scr[g, half, i, pl.ds(b, kt, stride=bsz), :] = x
        return 0

    lax.fori_loop(0, n_tiles, gather_tile, 0)

    def group_body(g, _):
        _s5_group(g, kgen_ref, ws_ref, wc_ref, a_ref, uy_scr, s_scr, h_scr, n_pairs=n_pairs, bsz=bsz, rows=rows,
                  lanes=lanes, n_half=n_half, n_tiles=n_tiles, kt=kt)
        return 0

    lax.fori_loop(0, n_grp // 2, group_body, 0)

    def scatter_tile(i, _):
        for b in range(bsz):
            sets = [[uy_scr[g, half, i, pl.ds(b, kt, stride=bsz), :] for g in range(n_grp)]
                    for half in range(n_half)]
            for half, ys in enumerate(_transpose_lane_chunks(sets, ch)):
                for j8, y in enumerate(ys):
                    y_ref[tile_rows(b, i, half * n_grp + j8), :] = y.astype(y_ref.dtype)
        return 0

    lax.fori_loop(0, n_tiles, scatter_tile, 0)


def _s5_group(g, kgen_ref, ws_ref, wc_ref, a_ref, uy_scr, s_scr, h_scr, *, n_pairs, bsz, rows, lanes, n_half,
              n_tiles, kt):
    n2 = 2 * S5_STATE
    tc, ch = S5_CHUNK, S5_GROUP
    pair = [2 * g, 2 * g + 1]
    us = [jnp.concatenate([uy_scr[gi, hf].reshape(rows, lanes) for hf in range(n_half)], axis=1).astype(BF16)
          for gi in pair]
    for p, gi in enumerate(pair):
        s_scr[p] = jnp.dot(us[p], ws_ref[gi], preferred_element_type=F32)
    a1 = [a_ref[gi, 0:1, :] for gi in pair]
    a2 = [a_ref[gi, 1:2, :] for gi in pair]

    def body(j, carry):
        r0 = pl.multiple_of(j * (2 * bsz), 2 * bsz)
        out = []
        for p in range(len(pair)):
            h, hs = carry[2 * p], carry[2 * p + 1]
            blk = s_scr[p, pl.ds(r0, 2 * bsz), :]
            h1 = a1[p] * h + a2[p] * hs + blk[0:bsz, 0:n2]
            hs1 = a1[p] * hs - a2[p] * h + blk[0:bsz, n2:2 * n2]
            h2 = a1[p] * h1 + a2[p] * hs1 + blk[bsz:2 * bsz, 0:n2]
            hs2 = a1[p] * hs1 - a2[p] * h1 + blk[bsz:2 * bsz, n2:2 * n2]
            h_scr[p, pl.ds(r0, 2 * bsz), :] = jnp.concatenate([h, h1], axis=0)
            out += [h2, hs2]
        return tuple(out)

    zero = jnp.zeros((bsz, n2), F32)
    lax.fori_loop(0, n_pairs, body, (zero,) * (2 * len(pair)))
    for p, gi in enumerate(pair):
        kg = kgen_ref[gi]
        lane_w = lax.broadcasted_iota(jnp.int32, kg.shape, 1)
        tz = jnp.concatenate([kg] + [jnp.where(lane_w >= ch * j, pltpu.roll(kg, ch * j, axis=1), 0.0)
                                     for j in range(1, tc)], axis=0).astype(BF16)
        yg = (jnp.dot(us[p], tz, preferred_element_type=F32)
              + jnp.dot(h_scr[p].astype(BF16), wc_ref[gi], preferred_element_type=F32))
        for hf in range(n_half):
            uy_scr[gi, hf] = yg[:, hf * lanes:(hf + 1) * lanes].reshape(n_tiles, kt * bsz, lanes)


def _s5(u_perm, kgen, ws, wc, a_pow, bsz, seq, tm):
    t, mw = u_perm.shape
    lanes = V7X_LANES
    width = kgen.shape[2]
    g_per_blk = lanes // S5_GROUP
    assert 2 * bsz == V7X_SUBLANES, "the chunk scan walks two chunks per 8-row tile"
    rows = (seq // S5_CHUNK) * bsz
    n_pairs = rows // (2 * bsz)
    per_blk = lambda a: pl.BlockSpec((g_per_blk,) + a.shape[1:], lambda q: (q, 0, 0))
    blk = pl.BlockSpec((t, lanes), lambda q: (0, q))
    kern = functools.partial(_s5_kernel, n_pairs=n_pairs, bsz=bsz, seq=seq, tm=tm)
    kt = tm // S5_CHUNK
    return pl.pallas_call(
        kern,
        grid=(mw // lanes,),
        in_specs=[blk, per_blk(kgen), per_blk(ws), per_blk(wc), per_blk(a_pow)],
        out_specs=blk,
        out_shape=jax.ShapeDtypeStruct((t, mw), BF16),
        scratch_shapes=[pltpu.VMEM((g_per_blk, width // lanes, seq // tm, kt * bsz, lanes), F32),
                        pltpu.VMEM((2, rows, ws.shape[2]), F32), pltpu.VMEM((2, rows, wc.shape[1]), F32)],
        compiler_params=_cparams("parallel"),
        name="s5",
    )(u_perm, kgen, ws, wc, a_pow)


def _s5_weights(lam_re, lam_im, log_step, b_re, b_im, c_re, c_im):
    tc = S5_CHUNK
    groups, n = lam_re.shape
    lr = jnp.minimum(lam_re.astype(F32), -1e-4)
    li = lam_im.astype(F32)
    delta = jnp.exp(log_step.astype(F32))[:, None]
    tau = jnp.arange(tc + 1, dtype=F32)
    mag = jnp.exp((lr * delta)[..., None] * tau)
    ang = (li * delta)[..., None] * tau
    pw_re, pw_im = mag * jnp.cos(ang), mag * jnp.sin(ang)
    num_re, num_im = pw_re[..., 1] - 1.0, pw_im[..., 1]
    den = lr * lr + li * li
    f_re = (num_re * lr + num_im * li) / den
    f_im = (num_im * lr - num_re * li) / den
    b_re, b_im = b_re.astype(F32), b_im.astype(F32)
    bb_re = f_re[..., None] * b_re - f_im[..., None] * b_im
    bb_im = f_re[..., None] * b_im + f_im[..., None] * b_re
    ct_re = jnp.transpose(c_re.astype(F32), (0, 2, 1))
    ct_im = jnp.transpose(c_im.astype(F32), (0, 2, 1))
    cp_re = ct_re[:, :, None, :] * pw_re[..., None] - ct_im[:, :, None, :] * pw_im[..., None]
    cp_im = ct_re[:, :, None, :] * pw_im[..., None] + ct_im[:, :, None, :] * pw_re[..., None]
    width = tc * S5_GROUP
    ein = functools.partial(jnp.einsum, precision=HIGHEST)
    kgen = (ein('gnd,gnx->gdx', bb_re, cp_re[:, :, :tc].reshape(groups, n, width))
            - ein('gnd,gnx->gdx', bb_im, cp_im[:, :, :tc].reshape(groups, n, width)))
    wc = jnp.concatenate([cp_re[:, :, 1:].reshape(groups, n, width),
                          -cp_im[:, :, 1:].reshape(groups, n, width)], axis=1)
    rev_re = jnp.transpose(pw_re[..., tc - 1::-1], (0, 2, 1))
    rev_im = jnp.transpose(pw_im[..., tc - 1::-1], (0, 2, 1))
    bt_re = jnp.transpose(bb_re, (0, 2, 1))
    bt_im = jnp.transpose(bb_im, (0, 2, 1))
    s_re = (rev_re[:, :, None, :] * bt_re[:, None] - rev_im[:, :, None, :] * bt_im[:, None]).reshape(groups, width, n)
    s_im = (rev_re[:, :, None, :] * bt_im[:, None] + rev_im[:, :, None, :] * bt_re[:, None]).reshape(groups, width, n)
    ws = jnp.concatenate([s_re, s_im, s_im, s_re], axis=-1)
    a_pow = jnp.stack([jnp.concatenate([pw_re[..., tc], pw_re[..., tc]], axis=-1),
                       jnp.concatenate([-pw_im[..., tc], pw_im[..., tc]], axis=-1)], axis=1)
    return kgen, ws, wc, a_pow


def _xattn_stages(h_subs, subs, k_ref, v_ref, wq_ref, wo_ref, gq_ref, go_ref, o_scr, out_ref):
    d = out_ref.shape[1]
    hd = d // XA_HEADS
    xn = [_rms(h, gq_ref[...]).astype(BF16) for h in h_subs]
    yield
    q = [jnp.dot(x, wq_ref[...], preferred_element_type=F32).astype(BF16) for x in xn]
    yield
    for hh in range(XA_HEADS):
        cs = slice(hh * hd, (hh + 1) * hd)
        for i, rs in enumerate(subs):
            s = lax.dot_general(q[i][:, cs], k_ref[:, cs], (((1,), (1,)), ((), ())),
                                preferred_element_type=F32) * (hd ** -0.5)
            e = jnp.exp(s - jnp.max(s, axis=-1, keepdims=True))
            p = e / jnp.sum(e, axis=-1, keepdims=True)
            o_scr[rs, cs] = _bdot(p, v_ref[:, cs]).astype(BF16)
            yield
    xa = [jnp.dot(o_scr[rs, :], wo_ref[...], preferred_element_type=F32) for rs in subs]
    yield
    for i, rs in enumerate(subs):
        out_ref[rs, :] = h_subs[i] + _rms(xa[i], go_ref[...])
        yield


def _ab_post_kernel(h_ref, oa_ref, ys_ref, u_ref, d_ref, wglu_ref, bglu_ref, wout_ref, g_ref, perm_ref,
                    k_ref, v_ref, wq_ref, wo_ref, gq_ref, go_ref, out_ref, o_scr):
    tm = h_ref.shape[0]
    mw = oa_ref.shape[1]
    subs = _sub_tiles(tm)
    y_ssm = jnp.dot(perm_ref[...], ys_ref[...], preferred_element_type=F32)
    y = y_ssm + d_ref[...] * u_ref[...].astype(F32)
    ob = (_gelu_tanh(y) * _sigmoid(_bdot(y, wglu_ref[...]) + bglu_ref[...])).astype(BF16)
    mix = [jnp.dot(oa_ref[rs, :], wout_ref[0:mw, :], preferred_element_type=F32)
           + jnp.dot(ob[rs, :], wout_ref[mw:2 * mw, :], preferred_element_type=F32) for rs in subs]
    h1 = [h_ref[rs, :] + _rms(mix[i], g_ref[...]) for i, rs in enumerate(subs)]
    for _ in _xattn_stages(h1, subs, k_ref, v_ref, wq_ref, wo_ref, gq_ref, go_ref, o_scr, out_ref):
        pass


def _ab_post(h, o_a, y_ssm, qkvgu, d_skip, w_glu, b_glu, w_out, gain, perm_t, xa, bsz, seq, mem_len, tm):
    t, d = h.shape
    mw = o_a.shape[1]
    nt = seq // tm
    k, v, wq, wo, gq, go = xa
    full = lambda a: pl.BlockSpec(a.shape, lambda b, i: (0, 0))
    rows = lambda w, cb=0: pl.BlockSpec((tm, w), lambda b, i: (b * nt + i, cb))
    kv = pl.BlockSpec((mem_len, d), lambda b, i: (b, 0))
    return pl.pallas_call(
        _ab_post_kernel,
        grid=(bsz, nt),
        in_specs=[rows(d), rows(mw), rows(mw), rows(mw, qkvgu.shape[1] // mw - 1), full(d_skip), full(w_glu),
                  full(b_glu), full(w_out), full(gain), full(perm_t), kv, kv, full(wq), full(wo), full(gq), full(go)],
        out_specs=rows(d),
        out_shape=jax.ShapeDtypeStruct((t, d), F32),
        scratch_shapes=[pltpu.VMEM((tm, d), BF16)],
        compiler_params=_cparams("parallel", "parallel"),
        name="ab_post",
    )(h, o_a, y_ssm, qkvgu, d_skip, w_glu, b_glu, w_out, gain, perm_t, k, v, wq, wo, gq, go)


def _mem_kv_kernel(mem_ref, g_ref, wk_ref, wv_ref, k_ref, v_ref):
    mn = _rms(mem_ref[...], g_ref[...]).astype(BF16)
    k_ref[...] = jnp.dot(mn, wk_ref[...], preferred_element_type=F32).astype(BF16)
    v_ref[...] = jnp.dot(mn, wv_ref[...], preferred_element_type=F32).astype(BF16)


def _mem_kv(mem, gain, wk, wv, tm):
    t, d = mem.shape
    full = lambda a: pl.BlockSpec(a.shape, lambda i: (0, 0))
    rows = pl.BlockSpec((tm, d), lambda i: (i, 0))
    return pl.pallas_call(
        _mem_kv_kernel,
        grid=(t // tm,),
        in_specs=[rows, full(gain), full(wk), full(wv)],
        out_specs=[rows, rows],
        out_shape=[jax.ShapeDtypeStruct((t, d), BF16), jax.ShapeDtypeStruct((t, d), BF16)],
        compiler_params=_cparams("parallel"),
        name="mem_kv",
    )(mem, gain, wk, wv)


def _mlp_kernel(h_ref, gi_ref, go_ref, w1_ref, w2_ref, out_ref, *, tf):
    subs = _sub_tiles(h_ref.shape[0])
    xn = [_rms(h_ref[rs, :], gi_ref[...]).astype(BF16) for rs in subs]
    dff = w1_ref.shape[1]
    acts = [[] for _ in subs]
    for j in range(dff // tf):
        for i in range(len(subs)):
            a = jnp.maximum(jnp.dot(xn[i], w1_ref[:, j * tf:(j + 1) * tf], preferred_element_type=F32), 0.0)
            acts[i].append((a * a).astype(BF16))
    ff = [jnp.dot(jnp.concatenate(acts[i], axis=1), w2_ref[...], preferred_element_type=F32)
          for i in range(len(subs))]
    for i, rs in enumerate(subs):
        out_ref[rs, :] = h_ref[rs, :] + _rms(ff[i], go_ref[...])


def _mlp(h, gi, go, w1, w2, tm, tf):
    t, d = h.shape
    full = lambda a: pl.BlockSpec(a.shape, lambda i: (0, 0))
    resident = lambda a: pl.BlockSpec(a.shape, lambda i: (0, 0), pipeline_mode=pl.Buffered(1))
    rows = pl.BlockSpec((tm, d), lambda i: (i, 0))
    return pl.pallas_call(
        functools.partial(_mlp_kernel, tf=tf),
        grid=(t // tm,),
        in_specs=[rows, full(gi), full(go), resident(w1), resident(w2)],
        out_specs=rows,
        out_shape=jax.ShapeDtypeStruct((t, d), F32),
        compiler_params=_cparams("parallel"),
        name="mlp",
    )(h, gi, go, w1, w2)


def _cd_in_kernel(h_ref, g_ref, wbig_ref, wsm_ref, mub_ref, mus_ref, w0_ref, w2_ref, a0_ref, a2_ref, g2_ref,
                  kk_ref, ka_ref, hsum_ref, tril_ref,
                  r_o, cum_o, cump_o, k_o, v_o, kk_o, a_o, g_o, xb_o, gate_o,
                  carry_b, carry_s, *, mw, n_sub):
    tm = h_ref.shape[0]

    @pl.when(pl.program_id(1) == 0)
    def _():
        carry_b[...] = jnp.zeros_like(carry_b)
        carry_s[...] = jnp.zeros_like(carry_s)

    ts = tm // n_sub
    proj = []
    for sb in range(n_sub):
        hn = _rms(h_ref[sb * ts:(sb + 1) * ts, :], g_ref[...]).astype(BF16)
        proj.append((jnp.dot(hn, wbig_ref[...], preferred_element_type=F32),
                     jnp.dot(hn, wsm_ref[...], preferred_element_type=F32)))

    for sb in range(n_sub):
        rs = slice(sb * ts, (sb + 1) * ts)
        pb, ps = proj[sb]
        p3 = pb[:, 0:3 * mw]
        xb_o[rs, :] = pb[:, 3 * mw:4 * mw].astype(BF16)
        gate_o[rs, :] = pb[:, 4 * mw:5 * mw].astype(BF16)

        prev3 = _shift_rows(p3, 1, carry_b[...])
        prevs = _shift_rows(ps, 1, carry_s[...])
        carry_b[...] = p3[ts - V7X_SUBLANES:ts, :]
        carry_s[...] = ps[ts - V7X_SUBLANES:ts, :]
        p3 = p3 + (prev3 - p3) * mub_ref[...]
        ps = ps + (prevs - ps) * mus_ref[...]
        r = p3[:, 0:mw]
        k = p3[:, mw:2 * mw]
        v = p3[:, 2 * mw:3 * mw]
        lora = ps[:, 0:V7X_LANES]
        wlog = -_softplus(-(w0_ref[...] + _bdot(jnp.tanh(lora), w2_ref[...]))) - 0.5
        a = _sigmoid(a0_ref[...] + _bdot(lora, a2_ref[...]))
        kkr = k * kk_ref[...]
        norm = jnp.sqrt(_bdot(kkr * kkr, hsum_ref[...]))
        r_o[rs, :] = r.astype(BF16)
        lw = -jnp.exp(wlog)
        cum = _split3_dot(tril_ref[...], lw)
        cum_o[rs, :] = cum
        cump_o[rs, :] = cum - lw
        k_o[rs, :] = (k * (1.0 + (a - 1.0) * ka_ref[...])).astype(BF16)
        v_o[rs, :] = v.astype(BF16)
        kk_o[rs, :] = (kkr / jnp.maximum(norm, 1e-12)).astype(BF16)
        a_o[rs, :] = a.astype(BF16)
        g_o[rs, :] = _bdot(_sigmoid(ps[:, V7X_LANES:2 * V7X_LANES]), g2_ref[...]).astype(BF16)


def _cd_in(h, gain, wbig, wsm, vecs, mats, bsz, seq, tm, mw):
    t, d = h.shape
    nt = seq // tm
    full = lambda a: pl.BlockSpec(a.shape, lambda b, i: (0, 0))
    rows = lambda w: pl.BlockSpec((tm, w), lambda b, i: (b * nt + i, 0))
    (mub, mus, w0, a0, kk_w, ka_w) = vecs
    (w2p, a2p, g2, hsum) = mats
    tril = _chunk_tril(tm // CD_IN_SUB_TILES, RWKV_CHUNK)
    args = (h, gain, wbig, wsm, mub, mus, w0, w2p, a0, a2p, g2, kk_w, ka_w, hsum, tril)
    kern = functools.partial(_cd_in_kernel, mw=mw, n_sub=CD_IN_SUB_TILES)
    return pl.pallas_call(
        kern,
        grid=(bsz, nt),
        in_specs=[rows(d)] + [full(a) for a in args[1:]],
        out_specs=[rows(mw)] * len(CD_IN_OUT_DTYPES),
        out_shape=[jax.ShapeDtypeStruct((t, mw), dt) for dt in CD_IN_OUT_DTYPES],
        scratch_shapes=[pltpu.VMEM((V7X_SUBLANES, 3 * mw), F32), pltpu.VMEM((V7X_SUBLANES, wsm.shape[1]), F32)],
        compiler_params=_cparams("parallel", "arbitrary"),
        name="cd_in",
    )(*args)


def _rwkv_kernel(r_ref, cum_ref, cump_ref, k_ref, v_ref, kk_ref, a_ref, g_ref, rk_ref, lng_ref, lnb_ref, hsum_ref,
                 o_ref, state_ref, y_scr, *, slab_heads, n_chunks):
    c = RWKV_CHUNK
    hd = RWKV_HEAD
    assert c == hd, "one block mask serves both the (head, s) and the (head, d) layouts"
    sw = slab_heads * hd
    n_slabs = r_ref.shape[1] // sw

    @pl.when(pl.program_id(1) == 0)
    def _():
        state_ref[...] = jnp.zeros_like(state_ref)

    blk_m = (lax.broadcasted_iota(jnp.int32, (sw, sw), 0) // c) == (lax.broadcasted_iota(jnp.int32, (sw, sw), 1) // c)
    wide_t = lax.broadcasted_iota(jnp.int32, (c, sw), 0)
    wide_s = lax.broadcasted_iota(jnp.int32, (c, sw), 1) % c
    strict_w = wide_s < wide_t
    incl_w = wide_s <= wide_t
    eye_w = jnp.where(wide_s == wide_t, 1.0, 0.0)

    def bdiag(x):
        x16 = x.astype(BF16)
        return jnp.where(blk_m, jnp.concatenate([x16] * slab_heads, axis=0), jnp.zeros((), BF16))

    def mm(a, w16):
        return jnp.dot(a.astype(BF16), w16, preferred_element_type=F32)

    nt_dims = (((1,), (1,)), ((), ()))
    chains = []
    for chunk in range(n_chunks):
        rs = slice(chunk * c, (chunk + 1) * c)
        cum = cum_ref[rs, :]
        cum_last = cum[c - 1:c, :]
        r = r_ref[rs, :].astype(F32)
        k = k_ref[rs, :].astype(F32)
        v = v_ref[rs, :].astype(F32)
        kk = kk_ref[rs, :].astype(F32)
        bvec = kk * a_ref[rs, :].astype(F32)
        inv_g = jnp.exp(-cum)
        to_end = jnp.exp(cum_last - cum)
        rt = r * jnp.exp(cum)
        kp = kk * jnp.exp(cump_ref[rs, :])
        be = bvec * inv_g
        kh = k * inv_g
        bb = bvec * to_end
        kb = k * to_end
        g_end = jnp.exp(cum_last)
        for s in range(n_slabs):
            ls = slice(s * sw, (s + 1) * sw)
            chains.append(dict(rs=rs, ls=ls, slab=s, rt=rt[:, ls], kp=kp[:, ls], v=v[:, ls], be=be[:, ls],
                               kh=kh[:, ls], kb=kb[:, ls], bb=bb[:, ls], g_end=g_end[:, ls]))

    for ch in chains:
        lhs2 = jnp.concatenate([ch['kp'], ch['rt']], axis=0).astype(BF16)
        ab = lax.dot_general(lhs2, bdiag(ch['be']), nt_dims, preferred_element_type=F32)
        ak = lax.dot_general(lhs2, bdiag(ch['kh']), nt_dims, preferred_element_type=F32)
        ch['a_kb'] = jnp.where(strict_w, ab[0:c], 0.0)
        ch['a_rb'] = jnp.where(incl_w, ab[c:2 * c], 0.0).astype(BF16)
        ch['a_kr'] = jnp.concatenate([jnp.where(strict_w, ak[0:c], 0.0), jnp.where(incl_w, ak[c:2 * c], 0.0)],
                                     axis=0).astype(BF16)
    for ch in chains:
        x = -ch['a_kb']
        ch['t'] = eye_w + x
        ch['x'] = mm(x, bdiag(x))
    for _ in range(int(math.log2(c)) - 2):
        for ch in chains:
            res = mm(jnp.concatenate([ch['t'], ch['x']], axis=0), bdiag(ch['x']))
            ch['t'] = ch['t'] + res[0:c]
            ch['x'] = res[c:2 * c]
    for ch in chains:
        ch['t'] = ch['t'] + mm(ch['t'], bdiag(ch['x']))
    for ch in chains:
        l_hi, l_lo = _split_bf16(ch['a_kb'])
        t_hi, t_lo = _split_bf16(ch['t'])
        lt = mm(jnp.concatenate([l_hi, l_lo], axis=0), bdiag(t_hi))
        ch['resid'] = eye_w - ch['t'] - (lt[0:c] + lt[c:2 * c] + mm(l_hi, bdiag(t_lo)))
    for ch in chains:
        ch['t16'] = (ch['t'] + mm(ch['t'], bdiag(ch['resid']))).astype(BF16)
    for ch in chains:
        ch['w_tok'] = mm(ch['t16'], bdiag(ch['kp']))
        ch['av'] = mm(ch['a_kr'], bdiag(ch['v']))
    for ch in chains:
        ch['u_tok'] = mm(ch['t16'], bdiag(ch['av'][0:c]))
    tn_dims = (((0,), (0,)), ((), ()))
    for ch in chains:
        ch['q16'] = (ch['rt'] - mm(ch['a_rb'], bdiag(ch['w_tok']))).astype(BF16)
        ch['y0'] = ch['av'][c:2 * c] - mm(ch['a_rb'], bdiag(ch['u_tok']))
    for ch in chains:
        bb16 = ch['bb'].astype(BF16)
        wb = lax.dot_general(ch['w_tok'].astype(BF16), bb16, tn_dims, preferred_element_type=F32)
        ch['wb'] = jnp.where(blk_m, wb, 0.0).astype(BF16)
        d_t = lax.dot_general(jnp.concatenate([ch['v'], -ch['u_tok']], axis=0).astype(BF16),
                              jnp.concatenate([ch['kb'].astype(BF16), bb16], axis=0), tn_dims,
                              preferred_element_type=F32)
        ch['d_t'] = jnp.where(blk_m, d_t, 0.0)

    for ch in chains:
        s = ch['slab']
        p_t = state_ref[s]
        ch['p16'] = p_t.astype(BF16)
        state_ref[s] = p_t * ch['g_end'] - jnp.dot(ch['p16'], ch['wb'], preferred_element_type=F32) + ch['d_t']
    for ch in chains:
        y_scr[ch['rs'], ch['ls']] = ch['y0'] + lax.dot_general(ch['q16'], ch['p16'], nt_dims,
                                                               preferred_element_type=F32)

    hsum = hsum_ref[...]
    y = y_scr[...]
    r = r_ref[...].astype(F32)
    v = v_ref[...].astype(F32)
    mean = _split_dot(y, hsum) * (1.0 / hd)
    yc = y - mean
    var = _bdot(yc * yc, hsum) * (1.0 / hd)
    yn = yc * lax.rsqrt(var + RWKV_GN_EPS) * lng_ref[...] + lnb_ref[...]
    bonus = _bdot(r * k_ref[...].astype(F32) * rk_ref[...], hsum) * v
    o_ref[...] = ((yn + bonus) * g_ref[...].astype(F32)).astype(o_ref.dtype)


def _rwkv(r, cum, cump, k, v, kk, a, g, rk, lng, lnb, hsum, bsz, seq, slab_heads=RWKV_SLAB_HEADS,
          n_chunks=RWKV_CHUNKS_PER_STEP):
    t, mw = r.shape
    c = RWKV_CHUNK * n_chunks
    nt = seq // c
    sw = slab_heads * RWKV_HEAD
    full = lambda x: pl.BlockSpec(x.shape, lambda b, i: (0, 0))
    rows = pl.BlockSpec((c, mw), lambda b, i: (b * nt + i, 0))
    kern = functools.partial(_rwkv_kernel, slab_heads=slab_heads, n_chunks=n_chunks)
    return pl.pallas_call(
        kern,
        grid=(bsz, nt),
        in_specs=[rows] * 8 + [full(rk), full(lng), full(lnb), full(hsum)],
        out_specs=rows,
        out_shape=jax.ShapeDtypeStruct((t, mw), BF16),
        scratch_shapes=[pltpu.VMEM((mw // sw, sw, sw), F32), pltpu.VMEM((c, mw), F32)],
        compiler_params=_cparams("parallel", "arbitrary"),
        name="rwkv",
    )(r, cum, cump, k, v, kk, a, g, rk, lng, lnb, hsum)


def _cd_post_kernel(h_ref, oc_ref, xb_ref, gate_ref, cw_ref, cb_ref, wa_ref, ba_ref, wx_ref, bx_ref, lam_ref,
                    wout_ref, g_ref, k_ref, v_ref, wq_ref, wo_ref, gq_ref, go_ref, out_ref,
                    o_scr, od_scr, la_ref, lb_ref, carry_ref, carry_x):
    tm = h_ref.shape[0]
    mw = oc_ref.shape[1]
    sub = V7X_SUBLANES
    subs = _sub_tiles(tm)
    rowi = lax.broadcasted_iota(jnp.int32, (sub, mw), 0)

    def lru_coeffs(rs, cx):
        xb = xb_ref[rs, :].astype(F32)
        xc = cb_ref[...] + cw_ref[LRU_CONV - 1:LRU_CONV, :] * xb
        for sh in range(1, LRU_CONV):
            xc = xc + cw_ref[LRU_CONV - 1 - sh:LRU_CONV - sh, :] * _shift_rows(xb, sh, cx)
        rg = _sigmoid(_bdot(xc, wa_ref[...]) + ba_ref[...])
        ig = _sigmoid(_bdot(xc, wx_ref[...]) + bx_ref[...])
        log_a = -LRU_C * rg * _softplus(-lam_ref[...])
        la_ref[rs, :] = jnp.exp(log_a)
        th = jnp.tanh(log_a)
        lb_ref[rs, :] = jnp.sqrt(-2.0 * th / (1.0 - th)) * (ig * xc)
        return xb[xb.shape[0] - sub:, :]

    def scan8(a, b, h):
        sh = 1
        while sh < sub:
            a_sh = jnp.where(rowi >= sh, pltpu.roll(a, sh, axis=0), 1.0)
            b_sh = jnp.where(rowi >= sh, pltpu.roll(b, sh, axis=0), 0.0)
            b = b + a * b_sh
            a = a * a_sh
            sh *= 2
        return b + a * h

    @pl.when(pl.program_id(1) == 0)
    def _():
        carry_ref[...] = jnp.zeros_like(carry_ref)
        carry_x[...] = jnp.zeros_like(carry_x)

    h = carry_ref[0:1, :]
    cx = carry_x[...]
    for rs in subs:
        cx = lru_coeffs(rs, cx)
        for r0 in range(rs.start, rs.stop, 2 * sub):
            h1 = scan8(la_ref[r0:r0 + sub, :], lb_ref[r0:r0 + sub, :], h)
            h2 = scan8(la_ref[r0 + sub:r0 + 2 * sub, :], lb_ref[r0 + sub:r0 + 2 * sub, :], h1[sub - 1:sub, :])
            h = h2[sub - 1:sub, :]
            gate = gate_ref[r0:r0 + 2 * sub, :].astype(F32)
            od_scr[r0:r0 + 2 * sub, :] = (jnp.concatenate([h1, h2], axis=0) * _gelu_tanh(gate)).astype(BF16)
    carry_ref[0:1, :] = h
    carry_x[...] = cx

    mix = [jnp.dot(oc_ref[rs, :], wout_ref[0:mw, :], preferred_element_type=F32)
           + jnp.dot(od_scr[rs, :], wout_ref[mw:2 * mw, :], preferred_element_type=F32) for rs in subs]
    h1s = [h_ref[rs, :] + _rms(mix[i], g_ref[...]) for i, rs in enumerate(subs)]
    for _ in _xattn_stages(h1s, subs, k_ref, v_ref, wq_ref, wo_ref, gq_ref, go_ref, o_scr, out_ref):
        pass


def _cd_post(h, o_c, xb, gate, lru, w_out, gain, xa, bsz, seq, mem_len, tm):
    t, d = h.shape
    mw = o_c.shape[1]
    nt = seq // tm
    k, v, wq, wo, gq, go = xa
    full = lambda a: pl.BlockSpec(a.shape, lambda b, i: (0, 0))
    rows = lambda w: pl.BlockSpec((tm, w), lambda b, i: (b * nt + i, 0))
    kv = pl.BlockSpec((mem_len, d), lambda b, i: (b, 0))
    return pl.pallas_call(
        _cd_post_kernel,
        grid=(bsz, nt),
        in_specs=[rows(d), rows(mw), rows(mw), rows(mw)] + [full(a) for a in lru] + [full(w_out), full(gain),
                  kv, kv, full(wq), full(wo), full(gq), full(go)],
        out_specs=rows(d),
        out_shape=jax.ShapeDtypeStruct((t, d), F32),
        scratch_shapes=[pltpu.VMEM((tm, d), BF16), pltpu.VMEM((tm, mw), BF16), pltpu.VMEM((tm, mw), F32),
                        pltpu.VMEM((tm, mw), F32), pltpu.VMEM((V7X_SUBLANES, mw), F32),
                        pltpu.VMEM((V7X_SUBLANES, mw), F32)],
        compiler_params=_cparams("parallel", "arbitrary"),
        name="cd_post",
    )(h, o_c, xb, gate, *lru, w_out, gain, k, v, wq, wo, gq, go)


def _row(vec):
    return vec.astype(F32).reshape(1, -1)


def _block_diag(blocks):
    nb, bi, bo = blocks.shape
    eye = jnp.eye(nb, dtype=blocks.dtype)
    return (eye[:, None, :, None] * blocks[:, :, None, :]).reshape(nb * bi, nb * bo)


def _head_sum_matrix(width, head):
    idx = jnp.arange(width) // head
    return (idx[:, None] == idx[None, :]).astype(BF16)


def _layer_ab(h, gains, p, xa, bsz, seq, mem_len, tm):
    d = h.shape[1]
    mw = d // 2
    dk = p['gla_b_decay'].shape[0] // GLA_HEADS
    nq = GLA_HEADS * dk
    w_in = p['ab_w_in']
    rank = p['gla_w_decay2'].shape[0]
    o_dlr = 2 * nq + 2 * mw
    w_main = jnp.concatenate([w_in[:, :o_dlr], w_in[:, o_dlr + rank:]], axis=1).astype(BF16)
    w_dlr = jnp.pad(w_in[:, o_dlr:o_dlr + rank], ((0, 0), (0, V7X_LANES - rank))).astype(BF16)
    w_decay2 = jnp.pad(p['gla_w_decay2'].astype(F32), ((0, V7X_LANES - rank), (0, 0)))
    perm = _chunk_step_perm(tm, S5_CHUNK)
    qkvgu, loga, u_perm = _ab_in(h, _row(gains[0]), w_main, w_dlr, w_decay2, _row(p['gla_b_decay']), perm, tm, mw)
    o_a = _gla(qkvgu, loga, _row(p['gla_norm_gain']), bsz, seq, tb=min(GLA_ROWS_PER_STEP, seq))
    kgen, ws, wc, a_pow = _s5_weights(p['s5_lambda_re'], p['s5_lambda_im'], p['s5_log_step'], p['s5_b_re'],
                                      p['s5_b_im'], p['s5_c_re'], p['s5_c_im'])
    y_perm = _s5(u_perm, kgen, ws.astype(BF16), wc.astype(BF16), a_pow, bsz, seq, tm)
    return _ab_post(h, o_a, y_perm, qkvgu, _row(p['s5_d']), p['s5_w_glu'].astype(BF16), _row(p['s5_b_glu']),
                    p['ab_w_out'].astype(BF16), _row(gains[1]), perm.T, xa, bsz, seq, mem_len, tm)


def _layer_cd(h, gains, p, xa, bsz, seq, mem_len, tm):
    d = h.shape[1]
    mw = d // 2
    w_in = p['cd_w_in']
    dr = p['rwkv_w2'].shape[0]
    ar = p['rwkv_a2'].shape[0]
    gr = p['rwkv_g2'].shape[0]
    assert dr + ar == V7X_LANES and gr == V7X_LANES, "the three low-rank projections fill two lane tiles"
    o = [0, mw, mw + dr, 2 * mw + dr, 3 * mw + dr, 3 * mw + dr + ar, 3 * mw + dr + ar + gr]
    col = lambda i, width: w_in[:, o[i]:o[i] + width]
    wbig = jnp.concatenate([col(0, mw), col(2, mw), col(3, mw), w_in[:, o[6]:]], axis=1).astype(BF16)
    wsm = jnp.concatenate([col(1, dr), col(4, ar), col(5, gr)], axis=1).astype(BF16)
    mu = p['rwkv_mu'].astype(F32)
    mseg = lambda i, width: mu[o[i]:o[i] + width]
    mub = jnp.concatenate([mseg(0, mw), mseg(2, mw), mseg(3, mw)]).reshape(1, -1)
    mus = jnp.concatenate([mseg(1, dr), mseg(4, ar), mseg(5, gr)]).reshape(1, -1)
    w2p = jnp.pad(p['rwkv_w2'], ((0, ar), (0, 0))).astype(BF16)
    a2p = jnp.pad(p['rwkv_a2'], ((dr, 0), (0, 0))).astype(BF16)
    hsum = _head_sum_matrix(mw, RWKV_HEAD)
    vecs = (mub, mus, _row(p['rwkv_w0']), _row(p['rwkv_a0']), _row(p['rwkv_k_k']), _row(p['rwkv_k_a']))
    mats = (w2p, a2p, p['rwkv_g2'].astype(BF16), hsum)
    r, cum, cump, k, v, kk, a, g, xb, gate = _cd_in(h, _row(gains[0]), wbig, wsm, vecs, mats, bsz, seq, tm, mw)
    o_c = _rwkv(r, cum, cump, k, v, kk, a, g, _row(p['rwkv_r_k']), _row(p['rwkv_ln_gain']),
                _row(p['rwkv_ln_bias']), hsum, bsz, seq)
    lru = (p['lru_conv_w'].astype(F32), _row(p['lru_conv_b']), _block_diag(p['lru_w_a']).astype(BF16),
           _row(p['lru_b_a']), _block_diag(p['lru_w_x']).astype(BF16), _row(p['lru_b_x']), _row(p['lru_lambda']))
    return _cd_post(h, o_c, xb, gate, lru, p['cd_w_out'].astype(BF16), _row(gains[1]), xa, bsz, seq, mem_len, tm)


def kernel(x, mem, norm_gain, xa_wq, xa_wk, xa_wv, xa_wo, mlp_w1, mlp_w2, ab_w_in, gla_w_decay2, gla_b_decay, gla_norm_gain, s5_lambda_re, s5_lambda_im, s5_log_step, s5_b_re, s5_b_im, s5_c_re, s5_c_im, s5_d, s5_w_glu, s5_b_glu, ab_w_out, cd_w_in, rwkv_mu, rwkv_w0, rwkv_w2, rwkv_a0, rwkv_a2, rwkv_g2, rwkv_k_k, rwkv_k_a, rwkv_r_k, rwkv_ln_gain, rwkv_ln_bias, lru_conv_w, lru_conv_b, lru_w_a, lru_b_a, lru_w_x, lru_b_x, lru_lambda, cd_w_out):
    bsz, seq, d = x.shape
    mem_len = mem.shape[1]
    depth = norm_gain.shape[0]
    tm = min(ROW_TILE, seq)
    ab = dict(ab_w_in=ab_w_in, gla_w_decay2=gla_w_decay2, gla_b_decay=gla_b_decay, gla_norm_gain=gla_norm_gain,
              s5_lambda_re=s5_lambda_re, s5_lambda_im=s5_lambda_im, s5_log_step=s5_log_step, s5_b_re=s5_b_re,
              s5_b_im=s5_b_im, s5_c_re=s5_c_re, s5_c_im=s5_c_im, s5_d=s5_d, s5_w_glu=s5_w_glu, s5_b_glu=s5_b_glu,
              ab_w_out=ab_w_out)
    cd = dict(cd_w_in=cd_w_in, rwkv_mu=rwkv_mu, rwkv_w0=rwkv_w0, rwkv_w2=rwkv_w2, rwkv_a0=rwkv_a0,
              rwkv_a2=rwkv_a2, rwkv_g2=rwkv_g2, rwkv_k_k=rwkv_k_k, rwkv_k_a=rwkv_k_a, rwkv_r_k=rwkv_r_k,
              rwkv_ln_gain=rwkv_ln_gain, rwkv_ln_bias=rwkv_ln_bias, lru_conv_w=lru_conv_w, lru_conv_b=lru_conv_b,
              lru_w_a=lru_w_a, lru_b_a=lru_b_a, lru_w_x=lru_w_x, lru_b_x=lru_b_x, lru_lambda=lru_lambda,
              cd_w_out=cd_w_out)
    h = x.astype(F32).reshape(bsz * seq, d)
    mem2 = mem.astype(F32).reshape(bsz * mem_len, d)
    for layer in range(depth):
        g = norm_gain[layer]
        i = layer // 2
        km, vm = _mem_kv(mem2, _row(g[6]), xa_wk[layer].astype(BF16), xa_wv[layer].astype(BF16), mem_len)
        xa = (km, vm, xa_wq[layer].astype(BF16), xa_wo[layer].astype(BF16), _row(g[2]), _row(g[3]))
        if layer % 2 == 0:
            h = _layer_ab(h, g, {n: w[i] for n, w in ab.items()}, xa, bsz, seq, mem_len, tm)
        else:
            h = _layer_cd(h, g, {n: w[i] for n, w in cd.items()}, xa, bsz, seq, mem_len, tm)
        h = _mlp(h, _row(g[4]), _row(g[5]), mlp_w1[layer].astype(BF16), mlp_w2[layer].astype(BF16),
                 tm=tm, tf=MLP_FF_TILE)
    return h.reshape(bsz, seq, d).astype(x.dtype)
```

```python
import functools
import math

import jax
import jax.numpy as jnp
from jax import lax
from jax.experimental import pallas as pl
from jax.experimental.pallas import tpu as pltpu

F32 = jnp.float32
BF16 = jnp.bfloat16
HIGHEST = lax.Precision.HIGHEST

NORM_EPS = 1e-6
GLA_HEADS = 4
GLA_TAU = 16.0
GLA_CHUNK = 64
S5_GROUP = 16
S5_STATE = 64
S5_CHUNK = 16
RWKV_HEAD = 64
RWKV_CHUNK = 64
RWKV_GN_EPS = 64e-5
LRU_CONV = 4
LRU_C = 8.0
XA_HEADS = 4

V7X_SUBLANES = 8
V7X_LANES = 128
V7X_MXU_DIM = 256
V7X_VMEM_BYTES = 64 * 1024 * 1024
VMEM_LIMIT_BYTES = (3 * V7X_VMEM_BYTES) // 4

ROW_TILE = 512
CD_IN_SUB_TILES = 2
GLA_ROWS_PER_STEP = 256
RWKV_CHUNKS_PER_STEP = 8
RWKV_SLAB_HEADS = V7X_MXU_DIM // RWKV_HEAD
MLP_FF_TILE = 1024
ROW_SUB_TILES = 2

CD_IN_OUT_DTYPES = (BF16, F32, F32, BF16, BF16, BF16, BF16, BF16, BF16, BF16)


def _cparams(*semantics):
    return pltpu.CompilerParams(dimension_semantics=semantics, vmem_limit_bytes=VMEM_LIMIT_BYTES)


def _rms(x, gain):
    return x * lax.rsqrt(jnp.mean(x * x, axis=-1, keepdims=True) + NORM_EPS) * gain


def _bdot(a, b):
    return jnp.dot(a.astype(BF16), b.astype(BF16), preferred_element_type=F32)


def _split_bf16(x):
    hi = x.astype(BF16)
    return hi, (x - hi.astype(F32)).astype(BF16)


def _dot3(a, b):
    a_hi, a_lo = _split_bf16(a)
    b_hi, b_lo = _split_bf16(b)
    return (jnp.dot(a_hi, b_hi, preferred_element_type=F32) + jnp.dot(a_lo, b_hi, preferred_element_type=F32)
            + jnp.dot(a_hi, b_lo, preferred_element_type=F32))


def _split_dot(x, w_bf16):
    hi, lo = _split_bf16(x)
    return (jnp.dot(hi, w_bf16, preferred_element_type=F32)
            + jnp.dot(lo, w_bf16, preferred_element_type=F32))


def _split3_dot(w_bf16, x):
    hi = x.astype(BF16)
    r1 = x - hi.astype(F32)
    mid = r1.astype(BF16)
    lo = (r1 - mid.astype(F32)).astype(BF16)
    return (jnp.dot(w_bf16, hi, preferred_element_type=F32) + jnp.dot(w_bf16, mid, preferred_element_type=F32)
            + jnp.dot(w_bf16, lo, preferred_element_type=F32))


def _sigmoid(x):
    return 1.0 / (1.0 + jnp.exp(-x))


def _softplus(x):
    return jnp.maximum(x, 0.0) + jnp.log(1.0 + jnp.exp(-jnp.abs(x)))


def _gelu_tanh(x):
    c = math.sqrt(2.0 / math.pi)
    return x * (0.5 * (1.0 + jnp.tanh(c * (x + 0.044715 * (x * x * x)))))


def _sub_tiles(tm):
    ts = tm // ROW_SUB_TILES
    return [slice(i * ts, (i + 1) * ts) for i in range(ROW_SUB_TILES)]


def _tril_mask(n, strict=False):
    row = lax.broadcasted_iota(jnp.int32, (n, n), 0)
    col = lax.broadcasted_iota(jnp.int32, (n, n), 1)
    return (col < row) if strict else (col <= row)


def _shift_rows(x, shift, carry):
    rolled = pltpu.roll(x, shift, axis=0)
    crolled = pltpu.roll(carry, shift, axis=0)
    rowi = lax.broadcasted_iota(jnp.int32, carry.shape, 0)
    first = jnp.where(rowi < shift, crolled, rolled[0:V7X_SUBLANES])
    return jnp.concatenate([first, rolled[V7X_SUBLANES:]], axis=0)


def _ab_in_kernel(h_ref, g_ref, w_ref, wd_ref, wd2_ref, bd_ref, perm_ref, tril_ref, out_ref, loga_ref, up_ref):
    subs = _sub_tiles(h_ref.shape[0])
    hn = [_rms(h_ref[rs, :], g_ref[...]).astype(BF16) for rs in subs]
    out16 = [jnp.dot(x, w_ref[...], preferred_element_type=F32).astype(BF16) for x in hn]
    dlr = [jnp.dot(x, wd_ref[...], preferred_element_type=F32) for x in hn]
    for i, rs in enumerate(subs):
        out_ref[rs, :] = out16[i]
        z = _dot3(dlr[i], wd2_ref[...]) + bd_ref[...]
        loga_ref[rs, :] = _split3_dot(tril_ref[...], -_softplus(-z) * (1.0 / GLA_TAU))
    mw = up_ref.shape[1]
    u16 = jnp.concatenate([o[:, o.shape[1] - mw:] for o in out16], axis=0)
    up_ref[...] = jnp.dot(perm_ref[...], u16, preferred_element_type=F32).astype(BF16)


def _ab_in(h, gain, w_main, w_dlr, w_decay2, b_decay, perm, tm, mw):
    t, d = h.shape
    n_main = w_main.shape[1]
    n_dk = w_decay2.shape[1]
    full = lambda a: pl.BlockSpec(a.shape, lambda i: (0, 0))
    rows = lambda w: pl.BlockSpec((tm, w), lambda i: (i, 0))
    tril = _chunk_tril(tm // ROW_SUB_TILES, GLA_CHUNK)
    return pl.pallas_call(
        _ab_in_kernel,
        grid=(t // tm,),
        in_specs=[rows(d), full(gain), full(w_main), full(w_dlr), full(w_decay2), full(b_decay), full(perm),
                  full(tril)],
        out_specs=[rows(n_main), rows(n_dk), rows(mw)],
        out_shape=[jax.ShapeDtypeStruct((t, n_main), BF16), jax.ShapeDtypeStruct((t, n_dk), F32),
                   jax.ShapeDtypeStruct((t, mw), BF16)],
        compiler_params=_cparams("parallel"),
        name="ab_in",
    )(h, gain, w_main, w_dlr, w_decay2, b_decay, perm, tril)


def _chunk_tril(n, c):
    idx = jnp.arange(n)
    return (((idx[:, None] // c) == (idx[None, :] // c)) & (idx[None, :] <= idx[:, None])).astype(BF16)


def _chunk_step_perm(tm, tc):
    dst = jnp.arange(tm)
    src = (dst % (tm // tc)) * tc + dst // (tm // tc)
    return (src[:, None] == jnp.arange(tm)[None, :]).astype(BF16)


def _gla_kernel(q_ref, k_ref, v_ref, gate_ref, la_ref, gain_ref, o_ref, state_ref, *, n_chunks, dk, dv):
    c = GLA_CHUNK

    @pl.when(pl.program_id(1) == 0)
    def _():
        state_ref[...] = jnp.zeros_like(state_ref)

    tril = _tril_mask(c)
    scale = dk ** -0.5
    lane_head = lax.broadcasted_iota(jnp.int32, (c, GLA_HEADS * dk), 1) // dk
    nt_dims = (((1,), (1,)), ((), ()))
    tn_dims = (((0,), (0,)), ((), ()))

    chunks = []
    for i in range(n_chunks):
        rs = slice(i * c, (i + 1) * c)
        b = la_ref[rs, :]
        b_last = b[c - 1:c, :]
        k = k_ref[rs, :].astype(F32)
        chunks.append(dict(rs=rs, q_in=(q_ref[rs, :].astype(F32) * jnp.exp(b) * scale).astype(BF16),
                           k_in=(k * jnp.exp(-b)).astype(BF16), k_st=k * jnp.exp(b_last - b),
                           dec=jnp.exp(b_last)))
    for ch in chunks:
        ch['v'] = [v_ref[ch['rs'], h * dv:(h + 1) * dv].astype(BF16) for h in range(GLA_HEADS)]
        zero16 = jnp.zeros((), BF16)
        ch['scores'] = [
            jnp.where(tril, lax.dot_general(jnp.where(lane_head == h, ch['q_in'], zero16), ch['k_in'], nt_dims,
                                            preferred_element_type=F32), 0.0).astype(BF16)
            for h in range(GLA_HEADS)]
    for ch in chunks:
        ch['o'] = [jnp.dot(ch['scores'][h], ch['v'][h], preferred_element_type=F32) for h in range(GLA_HEADS)]
        ch['d_state'] = [
            lax.dot_general(ch['v'][h], jnp.where(lane_head == h, ch['k_st'], 0.0).astype(BF16), tn_dims,
                            preferred_element_type=F32) for h in range(GLA_HEADS)]

    st = [state_ref[h] for h in range(GLA_HEADS)]
    for ch in chunks:
        ch['st'] = [s.astype(BF16) for s in st]
        st = [st[h] * ch['dec'] + ch['d_state'][h] for h in range(GLA_HEADS)]
    for h in range(GLA_HEADS):
        state_ref[h] = st[h]
    for ch in chunks:
        for h in range(GLA_HEADS):
            vs = slice(h * dv, (h + 1) * dv)
            o = ch['o'][h] + lax.dot_general(ch['q_in'], ch['st'][h], nt_dims, preferred_element_type=F32)
            on = o * lax.rsqrt(jnp.mean(o * o, axis=-1, keepdims=True) + NORM_EPS) * gain_ref[:, vs]
            g = gate_ref[ch['rs'], vs].astype(F32)
            o_ref[ch['rs'], vs] = (on * (g * _sigmoid(g))).astype(o_ref.dtype)


def _gla(qkvgu, loga, gain, bsz, seq, tb):
    t = bsz * seq
    dk = loga.shape[1] // GLA_HEADS
    dv = gain.shape[1] // GLA_HEADS
    nq, nv = GLA_HEADS * dk, GLA_HEADS * dv
    nt = seq // tb
    row = lambda b, i: b * nt + i
    kern = functools.partial(_gla_kernel, n_chunks=tb // GLA_CHUNK, dk=dk, dv=dv)
    return pl.pallas_call(
        kern,
        grid=(bsz, nt),
        in_specs=[pl.BlockSpec((tb, nq), lambda b, i: (row(b, i), 0)),
                  pl.BlockSpec((tb, nq), lambda b, i: (row(b, i), 1)),
                  pl.BlockSpec((tb, nv), lambda b, i: (row(b, i), 1)),
                  pl.BlockSpec((tb, nv), lambda b, i: (row(b, i), 2)),
                  pl.BlockSpec((tb, nq), lambda b, i: (row(b, i), 0)),
                  pl.BlockSpec(gain.shape, lambda b, i: (0, 0))],
        out_specs=pl.BlockSpec((tb, nv), lambda b, i: (row(b, i), 0)),
        out_shape=jax.ShapeDtypeStruct((t, nv), BF16),
        scratch_shapes=[pltpu.VMEM((GLA_HEADS, dv, nq), F32)],
        compiler_params=_cparams("parallel", "arbitrary"),
        name="gla",
    )(qkvgu, qkvgu, qkvgu, qkvgu, loga, gain)


def _transpose_lane_chunks(sets, ch):
    n = len(sets[0])
    chunk = lax.broadcasted_iota(jnp.int32, sets[0][0].shape, 1) // ch
    sets = [list(xs) for xs in sets]
    d = n // 2
    while d >= 1:
        low_half = (chunk & d) == 0
        for xs in sets:
            for r in range(n):
                if r & d:
                    continue
                lo, hi = xs[r], xs[r + d]
                xs[r] = jnp.where(low_half, lo, pltpu.roll(hi, d * ch, axis=1))
                xs[r + d] = jnp.where(low_half, pltpu.roll(lo, (n - d) * ch, axis=1), hi)
        d //= 2
    return sets


def _s5_kernel(u_ref, kgen_ref, ws_ref, wc_ref, a_ref, y_ref, uy_scr, s_scr, h_scr, *,
               n_pairs, bsz, seq, tm):
    n2 = 2 * S5_STATE
    tc, ch = S5_CHUNK, S5_GROUP
    lanes = u_ref.shape[1]
    n_grp = lanes // ch
    n_half = tc // n_grp
    kt = tm // tc
    n_tiles = seq // tm
    rows = n_tiles * kt * bsz

    def tile_rows(b, i, step):
        return pl.ds(pl.multiple_of(b * seq + i * tm + step * kt, kt), kt)

    def gather_tile(i, _):
        for b in range(bsz):
            sets = [[u_ref[tile_rows(b, i, half * n_grp + j8), :].astype(F32) for j8 in range(n_grp)]
                    for half in range(n_half)]
            for half, xs in enumerate(_transpose_lane_chunks(sets, ch)):
                for g, x in enumerate(xs):
                    uy_scr[g, half, i, pl.ds(b, kt, stride=bsz), :] = x
        return 0

    lax.fori_loop(0, n_tiles, gather_tile, 0)

    def group_body(g, _):
        _s5_group(g, kgen_ref, ws_ref, wc_ref, a_ref, uy_scr, s_scr, h_scr, n_pairs=n_pairs, bsz=bsz, rows=rows,
                  lanes=lanes, n_half=n_half, n_tiles=n_tiles, kt=kt)
        return 0

    lax.fori_loop(0, n_grp // 2, group_body, 0)

    def scatter_tile(i, _):
        for b in range(bsz):
            sets = [[uy_scr[g, half, i, pl.ds(b, kt, stride=bsz), :] for g in range(n_grp)]
                    for half in range(n_half)]
            for half, ys in enumerate(_transpose_lane_chunks(sets, ch)):
                for j8, y in enumerate(ys):
                    y_ref[tile_rows(b, i, half * n_grp + j8), :] = y.astype(y_ref.dtype)
        return 0

    lax.fori_loop(0, n_tiles, scatter_tile, 0)


def _s5_group(g, kgen_ref, ws_ref, wc_ref, a_ref, uy_scr, s_scr, h_scr, *, n_pairs, bsz, rows, lanes, n_half,
              n_tiles, kt):
    n2 = 2 * S5_STATE
    tc, ch = S5_CHUNK, S5_GROUP
    pair = [2 * g, 2 * g + 1]
    us = [jnp.concatenate([uy_scr[gi, hf].reshape(rows, lanes) for hf in range(n_half)], axis=1).astype(BF16)
          for gi in pair]
    for p, gi in enumerate(pair):
        s_scr[p] = jnp.dot(us[p], ws_ref[gi], preferred_element_type=F32)
    a1 = [a_ref[gi, 0:1, :] for gi in pair]
    a2 = [a_ref[gi, 1:2, :] for gi in pair]

    def body(j, carry):
        r0 = pl.multiple_of(j * (2 * bsz), 2 * bsz)
        out = []
        for p in range(len(pair)):
            h, hs = carry[2 * p], carry[2 * p + 1]
            blk = s_scr[p, pl.ds(r0, 2 * bsz), :]
            h1 = a1[p] * h + a2[p] * hs + blk[0:bsz, 0:n2]
            hs1 = a1[p] * hs - a2[p] * h + blk[0:bsz, n2:2 * n2]
            h2 = a1[p] * h1 + a2[p] * hs1 + blk[bsz:2 * bsz, 0:n2]
            hs2 = a1[p] * hs1 - a2[p] * h1 + blk[bsz:2 * bsz, n2:2 * n2]
            h_scr[p, pl.ds(r0, 2 * bsz), :] = jnp.concatenate([h, h1], axis=0)
            out += [h2, hs2]
        return tuple(out)

    zero = jnp.zeros((bsz, n2), F32)
    lax.fori_loop(0, n_pairs, body, (zero,) * (2 * len(pair)))
    for p, gi in enumerate(pair):
        kg = kgen_ref[gi]
        lane_w = lax.broadcasted_iota(jnp.int32, kg.shape, 1)
        tz = jnp.concatenate([kg] + [jnp.where(lane_w >= ch * j, pltpu.roll(kg, ch * j, axis=1), 0.0)
                                     for j in range(1, tc)], axis=0).astype(BF16)
        yg = (jnp.dot(us[p], tz, preferred_element_type=F32)
              + jnp.dot(h_scr[p].astype(BF16), wc_ref[gi], preferred_element_type=F32))
        for hf in range(n_half):
            uy_scr[gi, hf] = yg[:, hf * lanes:(hf + 1) * lanes].reshape(n_tiles, kt * bsz, lanes)


def _s5(u_perm, kgen, ws, wc, a_pow, bsz, seq, tm):
    t, mw = u_perm.shape
    lanes = V7X_LANES
    width = kgen.shape[2]
    g_per_blk = lanes // S5_GROUP
    assert 2 * bsz == V7X_SUBLANES, "the chunk scan walks two chunks per 8-row tile"
    rows = (seq // S5_CHUNK) * bsz
    n_pairs = rows // (2 * bsz)
    per_blk = lambda a: pl.BlockSpec((g_per_blk,) + a.shape[1:], lambda q: (q, 0, 0))
    blk = pl.BlockSpec((t, lanes), lambda q: (0, q))
    kern = functools.partial(_s5_kernel, n_pairs=n_pairs, bsz=bsz, seq=seq, tm=tm)
    kt = tm // S5_CHUNK
    return pl.pallas_call(
        kern,
        grid=(mw // lanes,),
        in_specs=[blk, per_blk(kgen), per_blk(ws), per_blk(wc), per_blk(a_pow)],
        out_specs=blk,
        out_shape=jax.ShapeDtypeStruct((t, mw), BF16),
        scratch_shapes=[pltpu.VMEM((g_per_blk, width // lanes, seq // tm, kt * bsz, lanes), F32),
                        pltpu.VMEM((2, rows, ws.shape[2]), F32), pltpu.VMEM((2, rows, wc.shape[1]), F32)],
        compiler_params=_cparams("parallel"),
        name="s5",
    )(u_perm, kgen, ws, wc, a_pow)


def _s5_weights(lam_re, lam_im, log_step, b_re, b_im, c_re, c_im):
    tc = S5_CHUNK
    groups, n = lam_re.shape
    lr = jnp.minimum(lam_re.astype(F32), -1e-4)
    li = lam_im.astype(F32)
    delta = jnp.exp(log_step.astype(F32))[:, None]
    tau = jnp.arange(tc + 1, dtype=F32)
    mag = jnp.exp((lr * delta)[..., None] * tau)
    ang = (li * delta)[..., None] * tau
    pw_re, pw_im = mag * jnp.cos(ang), mag * jnp.sin(ang)
    num_re, num_im = pw_re[..., 1] - 1.0, pw_im[..., 1]
    den = lr * lr + li * li
    f_re = (num_re * lr + num_im * li) / den
    f_im = (num_im * lr - num_re * li) / den
    b_re, b_im = b_re.astype(F32), b_im.astype(F32)
    bb_re = f_re[..., None] * b_re - f_im[..., None] * b_im
    bb_im = f_re[..., None] * b_im + f_im[..., None] * b_re
    ct_re = jnp.transpose(c_re.astype(F32), (0, 2, 1))
    ct_im = jnp.transpose(c_im.astype(F32), (0, 2, 1))
    cp_re = ct_re[:, :, None, :] * pw_re[..., None] - ct_im[:, :, None, :] * pw_im[..., None]
    cp_im = ct_re[:, :, None, :] * pw_im[..., None] + ct_im[:, :, None, :] * pw_re[..., None]
    width = tc * S5_GROUP
    ein = functools.partial(jnp.einsum, precision=HIGHEST)
    kgen = (ein('gnd,gnx->gdx', bb_re, cp_re[:, :, :tc].reshape(groups, n, width))
            - ein('gnd,gnx->gdx', bb_im, cp_im[:, :, :tc].reshape(groups, n, width)))
    wc = jnp.concatenate([cp_re[:, :, 1:].reshape(groups, n, width),
                          -cp_im[:, :, 1:].reshape(groups, n, width)], axis=1)
    rev_re = jnp.transpose(pw_re[..., tc - 1::-1], (0, 2, 1))
    rev_im = jnp.transpose(pw_im[..., tc - 1::-1], (0, 2, 1))
    bt_re = jnp.transpose(bb_re, (0, 2, 1))
    bt_im = jnp.transpose(bb_im, (0, 2, 1))
    s_re = (rev_re[:, :, None, :] * bt_re[:, None] - rev_im[:, :, None, :] * bt_im[:, None]).reshape(groups, width, n)
    s_im = (rev_re[:, :, None, :] * bt_im[:, None] + rev_im[:, :, None, :] * bt_re[:, None]).reshape(groups, width, n)
    ws = jnp.concatenate([s_re, s_im, s_im, s_re], axis=-1)
    a_pow = jnp.stack([jnp.concatenate([pw_re[..., tc], pw_re[..., tc]], axis=-1),
                       jnp.concatenate([-pw_im[..., tc], pw_im[..., tc]], axis=-1)], axis=1)
    return kgen, ws, wc, a_pow


def _xattn_stages(h_subs, subs, k_ref, v_ref, wq_ref, wo_ref, gq_ref, go_ref, o_scr, out_ref):
    d = out_ref.shape[1]
    hd = d // XA_HEADS
    xn = [_rms(h, gq_ref[...]).astype(BF16) for h in h_subs]
    yield
    q = [jnp.dot(x, wq_ref[...], preferred_element_type=F32).astype(BF16) for x in xn]
    yield
    for hh in range(XA_HEADS):
        cs = slice(hh * hd, (hh + 1) * hd)
        for i, rs in enumerate(subs):
            s = lax.dot_general(q[i][:, cs], k_ref[:, cs], (((1,), (1,)), ((), ())),
                                preferred_element_type=F32) * (hd ** -0.5)
            e = jnp.exp(s - jnp.max(s, axis=-1, keepdims=True))
            p = e / jnp.sum(e, axis=-1, keepdims=True)
            o_scr[rs, cs] = _bdot(p, v_ref[:, cs]).astype(BF16)
            yield
    xa = [jnp.dot(o_scr[rs, :], wo_ref[...], preferred_element_type=F32) for rs in subs]
    yield
    for i, rs in enumerate(subs):
        out_ref[rs, :] = h_subs[i] + _rms(xa[i], go_ref[...])
        yield


def _ab_post_kernel(h_ref, oa_ref, ys_ref, u_ref, d_ref, wglu_ref, bglu_ref, wout_ref, g_ref, perm_ref,
                    k_ref, v_ref, wq_ref, wo_ref, gq_ref, go_ref, out_ref, o_scr):
    tm = h_ref.shape[0]
    mw = oa_ref.shape[1]
    subs = _sub_tiles(tm)
    y_ssm = jnp.dot(perm_ref[...], ys_ref[...], preferred_element_type=F32)
    y = y_ssm + d_ref[...] * u_ref[...].astype(F32)
    ob = (_gelu_tanh(y) * _sigmoid(_bdot(y, wglu_ref[...]) + bglu_ref[...])).astype(BF16)
    mix = [jnp.dot(oa_ref[rs, :], wout_ref[0:mw, :], preferred_element_type=F32)
           + jnp.dot(ob[rs, :], wout_ref[mw:2 * mw, :], preferred_element_type=F32) for rs in subs]
    h1 = [h_ref[rs, :] + _rms(mix[i], g_ref[...]) for i, rs in enumerate(subs)]
    for _ in _xattn_stages(h1, subs, k_ref, v_ref, wq_ref, wo_ref, gq_ref, go_ref, o_scr, out_ref):
        pass


def _ab_post(h, o_a, y_ssm, qkvgu, d_skip, w_glu, b_glu, w_out, gain, perm_t, xa, bsz, seq, mem_len, tm):
    t, d = h.shape
    mw = o_a.shape[1]
    nt = seq // tm
    k, v, wq, wo, gq, go = xa
    full = lambda a: pl.BlockSpec(a.shape, lambda b, i: (0, 0))
    rows = lambda w, cb=0: pl.BlockSpec((tm, w), lambda b, i: (b * nt + i, cb))
    kv = pl.BlockSpec((mem_len, d), lambda b, i: (b, 0))
    return pl.pallas_call(
        _ab_post_kernel,
        grid=(bsz, nt),
        in_specs=[rows(d), rows(mw), rows(mw), rows(mw, qkvgu.shape[1] // mw - 1), full(d_skip), full(w_glu),
                  full(b_glu), full(w_out), full(gain), full(perm_t), kv, kv, full(wq), full(wo), full(gq), full(go)],
        out_specs=rows(d),
        out_shape=jax.ShapeDtypeStruct((t, d), F32),
        scratch_shapes=[pltpu.VMEM((tm, d), BF16)],
        compiler_params=_cparams("parallel", "parallel"),
        name="ab_post",
    )(h, o_a, y_ssm, qkvgu, d_skip, w_glu, b_glu, w_out, gain, perm_t, k, v, wq, wo, gq, go)


def _mem_kv_kernel(mem_ref, g_ref, wk_ref, wv_ref, k_ref, v_ref):
    mn = _rms(mem_ref[...], g_ref[...]).astype(BF16)
    k_ref[...] = jnp.dot(mn, wk_ref[...], preferred_element_type=F32).astype(BF16)
    v_ref[...] = jnp.dot(mn, wv_ref[...], preferred_element_type=F32).astype(BF16)


def _mem_kv(mem, gain, wk, wv, tm):
    t, d = mem.shape
    full = lambda a: pl.BlockSpec(a.shape, lambda i: (0, 0))
    rows = pl.BlockSpec((tm, d), lambda i: (i, 0))
    return pl.pallas_call(
        _mem_kv_kernel,
        grid=(t // tm,),
        in_specs=[rows, full(gain), full(wk), full(wv)],
        out_specs=[rows, rows],
        out_shape=[jax.ShapeDtypeStruct((t, d), BF16), jax.ShapeDtypeStruct((t, d), BF16)],
        compiler_params=_cparams("parallel"),
        name="mem_kv",
    )(mem, gain, wk, wv)


def _mlp_kernel(h_ref, gi_ref, go_ref, w1_ref, w2_ref, out_ref, *, tf):
    subs = _sub_tiles(h_ref.shape[0])
    xn = [_rms(h_ref[rs, :], gi_ref[...]).astype(BF16) for rs in subs]
    dff = w1_ref.shape[1]
    acts = [[] for _ in subs]
    for j in range(dff // tf):
        w1c = w1_ref[:, j * tf:(j + 1) * tf].astype(BF16)
        for i in range(len(subs)):
            a = jnp.maximum(jnp.dot(xn[i], w1c, preferred_element_type=F32), 0.0)
            acts[i].append((a * a).astype(BF16))
    ff = [None] * len(subs)
    for j in range(dff // tf):
        w2c = w2_ref[j * tf:(j + 1) * tf, :].astype(BF16)
        for i in range(len(subs)):
            part = jnp.dot(acts[i][j], w2c, preferred_element_type=F32)
            ff[i] = part if ff[i] is None else part + ff[i]
    for i, rs in enumerate(subs):
        out_ref[rs, :] = h_ref[rs, :] + _rms(ff[i], go_ref[...])


def _mlp(h, gi, go, w1, w2, tm, tf):
    t, d = h.shape
    full = lambda a: pl.BlockSpec(a.shape, lambda i: (0, 0))
    resident = lambda a: pl.BlockSpec(a.shape, lambda i: (0, 0), pipeline_mode=pl.Buffered(1))
    rows = pl.BlockSpec((tm, d), lambda i: (i, 0))
    return pl.pallas_call(
        functools.partial(_mlp_kernel, tf=tf),
        grid=(t // tm,),
        in_specs=[rows, full(gi), full(go), resident(w1), resident(w2)],
        out_specs=rows,
        out_shape=jax.ShapeDtypeStruct((t, d), F32),
        compiler_params=_cparams("parallel"),
        name="mlp",
    )(h, gi, go, w1, w2)


def _cd_in_kernel(h_ref, g_ref, wbig_ref, wsm_ref, mub_ref, mus_ref, w0_ref, w2_ref, a0_ref, a2_ref, g2_ref,
                  kk_ref, ka_ref, hsum_ref, tril_ref,
                  r_o, cum_o, cump_o, k_o, v_o, kk_o, a_o, g_o, xb_o, gate_o,
                  carry_b, carry_s, *, mw, n_sub):
    tm = h_ref.shape[0]

    @pl.when(pl.program_id(1) == 0)
    def _():
        carry_b[...] = jnp.zeros_like(carry_b)
        carry_s[...] = jnp.zeros_like(carry_s)

    ts = tm // n_sub
    proj = []
    for sb in range(n_sub):
        hn = _rms(h_ref[sb * ts:(sb + 1) * ts, :], g_ref[...]).astype(BF16)
        proj.append((jnp.dot(hn, wbig_ref[...], preferred_element_type=F32),
                     jnp.dot(hn, wsm_ref[...], preferred_element_type=F32)))

    for sb in range(n_sub):
        rs = slice(sb * ts, (sb + 1) * ts)
        pb, ps = proj[sb]
        p3 = pb[:, 0:3 * mw]
        xb_o[rs, :] = pb[:, 3 * mw:4 * mw].astype(BF16)
        gate_o[rs, :] = pb[:, 4 * mw:5 * mw].astype(BF16)

        prev3 = _shift_rows(p3, 1, carry_b[...])
        prevs = _shift_rows(ps, 1, carry_s[...])
        carry_b[...] = p3[ts - V7X_SUBLANES:ts, :]
        carry_s[...] = ps[ts - V7X_SUBLANES:ts, :]
        p3 = p3 + (prev3 - p3) * mub_ref[...]
        ps = ps + (prevs - ps) * mus_ref[...]
        r = p3[:, 0:mw]
        k = p3[:, mw:2 * mw]
        v = p3[:, 2 * mw:3 * mw]
        lora = ps[:, 0:V7X_LANES]
        wlog = -_softplus(-(w0_ref[...] + _bdot(jnp.tanh(lora), w2_ref[...]))) - 0.5
        a = _sigmoid(a0_ref[...] + _bdot(lora, a2_ref[...]))
        kkr = k * kk_ref[...]
        norm = jnp.sqrt(_bdot(kkr * kkr, hsum_ref[...]))
        r_o[rs, :] = r.astype(BF16)
        lw = -jnp.exp(wlog)
        cum = _split3_dot(tril_ref[...], lw)
        cum_o[rs, :] = cum
        cump_o[rs, :] = cum - lw
        k_o[rs, :] = (k * (1.0 + (a - 1.0) * ka_ref[...])).astype(BF16)
        v_o[rs, :] = v.astype(BF16)
        kk_o[rs, :] = (kkr / jnp.maximum(norm, 1e-12)).astype(BF16)
        a_o[rs, :] = a.astype(BF16)
        g_o[rs, :] = _bdot(_sigmoid(ps[:, V7X_LANES:2 * V7X_LANES]), g2_ref[...]).astype(BF16)


def _cd_in(h, gain, wbig, wsm, vecs, mats, bsz, seq, tm, mw):
    t, d = h.shape
    nt = seq // tm
    full = lambda a: pl.BlockSpec(a.shape, lambda b, i: (0, 0))
    rows = lambda w: pl.BlockSpec((tm, w), lambda b, i: (b * nt + i, 0))
    (mub, mus, w0, a0, kk_w, ka_w) = vecs
    (w2p, a2p, g2, hsum) = mats
    tril = _chunk_tril(tm // CD_IN_SUB_TILES, RWKV_CHUNK)
    args = (h, gain, wbig, wsm, mub, mus, w0, w2p, a0, a2p, g2, kk_w, ka_w, hsum, tril)
    kern = functools.partial(_cd_in_kernel, mw=mw, n_sub=CD_IN_SUB_TILES)
    return pl.pallas_call(
        kern,
        grid=(bsz, nt),
        in_specs=[rows(d)] + [full(a) for a in args[1:]],
        out_specs=[rows(mw)] * len(CD_IN_OUT_DTYPES),
        out_shape=[jax.ShapeDtypeStruct((t, mw), dt) for dt in CD_IN_OUT_DTYPES],
        scratch_shapes=[pltpu.VMEM((V7X_SUBLANES, 3 * mw), F32), pltpu.VMEM((V7X_SUBLANES, wsm.shape[1]), F32)],
        compiler_params=_cparams("parallel", "arbitrary"),
        name="cd_in",
    )(*args)


def _rwkv_kernel(r_ref, cum_ref, cump_ref, k_ref, v_ref, kk_ref, a_ref, g_ref, rk_ref, lng_ref, lnb_ref, hsum_ref,
                 o_ref, state_ref, y_scr, *, slab_heads, n_chunks):
    c = RWKV_CHUNK
    hd = RWKV_HEAD
    assert c == hd, "one block mask serves both the (head, s) and the (head, d) layouts"
    sw = slab_heads * hd
    n_slabs = r_ref.shape[1] // sw

    @pl.when(pl.program_id(1) == 0)
    def _():
        state_ref[...] = jnp.zeros_like(state_ref)

    blk_m = (lax.broadcasted_iota(jnp.int32, (sw, sw), 0) // c) == (lax.broadcasted_iota(jnp.int32, (sw, sw), 1) // c)
    wide_t = lax.broadcasted_iota(jnp.int32, (c, sw), 0)
    wide_s = lax.broadcasted_iota(jnp.int32, (c, sw), 1) % c
    strict_w = wide_s < wide_t
    incl_w = wide_s <= wide_t
    eye_w = jnp.where(wide_s == wide_t, 1.0, 0.0)

    def bdiag(x):
        x16 = x.astype(BF16)
        return jnp.where(blk_m, jnp.concatenate([x16] * slab_heads, axis=0), jnp.zeros((), BF16))

    def mm(a, w16):
        return jnp.dot(a.astype(BF16), w16, preferred_element_type=F32)

    nt_dims = (((1,), (1,)), ((), ()))
    chains = []
    for chunk in range(n_chunks):
        rs = slice(chunk * c, (chunk + 1) * c)
        cum = cum_ref[rs, :]
        cum_last = cum[c - 1:c, :]
        r = r_ref[rs, :].astype(F32)
        k = k_ref[rs, :].astype(F32)
        v = v_ref[rs, :].astype(F32)
        kk = kk_ref[rs, :].astype(F32)
        bvec = kk * a_ref[rs, :].astype(F32)
        inv_g = jnp.exp(-cum)
        to_end = jnp.exp(cum_last - cum)
        rt = r * jnp.exp(cum)
        kp = kk * jnp.exp(cump_ref[rs, :])
        be = bvec * inv_g
        kh = k * inv_g
        bb = bvec * to_end
        kb = k * to_end
        g_end = jnp.exp(cum_last)
        for s in range(n_slabs):
            ls = slice(s * sw, (s + 1) * sw)
            chains.append(dict(rs=rs, ls=ls, slab=s, rt=rt[:, ls], kp=kp[:, ls], v=v[:, ls], be=be[:, ls],
                               kh=kh[:, ls], kb=kb[:, ls], bb=bb[:, ls], g_end=g_end[:, ls]))

    for ch in chains:
        lhs2 = jnp.concatenate([ch['kp'], ch['rt']], axis=0).astype(BF16)
        ab = lax.dot_general(lhs2, bdiag(ch['be']), nt_dims, preferred_element_type=F32)
        ak = lax.dot_general(lhs2, bdiag(ch['kh']), nt_dims, preferred_element_type=F32)
        ch['a_kb'] = jnp.where(strict_w, ab[0:c], 0.0)
        ch['a_rb'] = jnp.where(incl_w, ab[c:2 * c], 0.0).astype(BF16)
        ch['a_kr'] = jnp.concatenate([jnp.where(strict_w, ak[0:c], 0.0), jnp.where(incl_w, ak[c:2 * c], 0.0)],
                                     axis=0).astype(BF16)
    for ch in chains:
        x = -ch['a_kb']
        ch['t'] = eye_w + x
        ch['x'] = mm(x, bdiag(x))
    for _ in range(int(math.log2(c)) - 2):
        for ch in chains:
            res = mm(jnp.concatenate([ch['t'], ch['x']], axis=0), bdiag(ch['x']))
            ch['t'] = ch['t'] + res[0:c]
            ch['x'] = res[c:2 * c]
    for ch in chains:
        ch['t'] = ch['t'] + mm(ch['t'], bdiag(ch['x']))
    for ch in chains:
        l_hi, l_lo = _split_bf16(ch['a_kb'])
        t_hi, t_lo = _split_bf16(ch['t'])
        lt = mm(jnp.concatenate([l_hi, l_lo], axis=0), bdiag(t_hi))
        ch['resid'] = eye_w - ch['t'] - (lt[0:c] + lt[c:2 * c] + mm(l_hi, bdiag(t_lo)))
    for ch in chains:
        ch['t16'] = (ch['t'] + mm(ch['t'], bdiag(ch['resid']))).astype(BF16)
    for ch in chains:
        ch['w_tok'] = mm(ch['t16'], bdiag(ch['kp']))
        ch['av'] = mm(ch['a_kr'], bdiag(ch['v']))
    for ch in chains:
        ch['u_tok'] = mm(ch['t16'], bdiag(ch['av'][0:c]))
    tn_dims = (((0,), (0,)), ((), ()))
    for ch in chains:
        ch['q16'] = (ch['rt'] - mm(ch['a_rb'], bdiag(ch['w_tok']))).astype(BF16)
        ch['y0'] = ch['av'][c:2 * c] - mm(ch['a_rb'], bdiag(ch['u_tok']))
    for ch in chains:
        bb16 = ch['bb'].astype(BF16)
        wb = lax.dot_general(ch['w_tok'].astype(BF16), bb16, tn_dims, preferred_element_type=F32)
        ch['wb'] = jnp.where(blk_m, wb, 0.0).astype(BF16)
        d_t = lax.dot_general(jnp.concatenate([ch['v'], -ch['u_tok']], axis=0).astype(BF16),
                              jnp.concatenate([ch['kb'].astype(BF16), bb16], axis=0), tn_dims,
                              preferred_element_type=F32)
        ch['d_t'] = jnp.where(blk_m, d_t, 0.0)

    for ch in chains:
        s = ch['slab']
        p_t = state_ref[s]
        ch['p16'] = p_t.astype(BF16)
        state_ref[s] = p_t * ch['g_end'] - jnp.dot(ch['p16'], ch['wb'], preferred_element_type=F32) + ch['d_t']
    for ch in chains:
        y_scr[ch['rs'], ch['ls']] = ch['y0'] + lax.dot_general(ch['q16'], ch['p16'], nt_dims,
                                                               preferred_element_type=F32)

    hsum = hsum_ref[...]
    y = y_scr[...]
    r = r_ref[...].astype(F32)
    v = v_ref[...].astype(F32)
    mean = _split_dot(y, hsum) * (1.0 / hd)
    yc = y - mean
    var = _bdot(yc * yc, hsum) * (1.0 / hd)
    yn = yc * lax.rsqrt(var + RWKV_GN_EPS) * lng_ref[...] + lnb_ref[...]
    bonus = _bdot(r * k_ref[...].astype(F32) * rk_ref[...], hsum) * v
    o_ref[...] = ((yn + bonus) * g_ref[...].astype(F32)).astype(o_ref.dtype)


def _rwkv(r, cum, cump, k, v, kk, a, g, rk, lng, lnb, hsum, bsz, seq, slab_heads=RWKV_SLAB_HEADS,
          n_chunks=RWKV_CHUNKS_PER_STEP):
    t, mw = r.shape
    c = RWKV_CHUNK * n_chunks
    nt = seq // c
    sw = slab_heads * RWKV_HEAD
    full = lambda x: pl.BlockSpec(x.shape, lambda b, i: (0, 0))
    rows = pl.BlockSpec((c, mw), lambda b, i: (b * nt + i, 0))
    kern = functools.partial(_rwkv_kernel, slab_heads=slab_heads, n_chunks=n_chunks)
    return pl.pallas_call(
        kern,
        grid=(bsz, nt),
        in_specs=[rows] * 8 + [full(rk), full(lng), full(lnb), full(hsum)],
        out_specs=rows,
        out_shape=jax.ShapeDtypeStruct((t, mw), BF16),
        scratch_shapes=[pltpu.VMEM((mw // sw, sw, sw), F32), pltpu.VMEM((c, mw), F32)],
        compiler_params=_cparams("parallel", "arbitrary"),
        name="rwkv",
    )(r, cum, cump, k, v, kk, a, g, rk, lng, lnb, hsum)


def _cd_post_kernel(h_ref, oc_ref, xb_ref, gate_ref, cw_ref, cb_ref, wa_ref, ba_ref, wx_ref, bx_ref, lam_ref,
                    wout_ref, g_ref, k_ref, v_ref, wq_ref, wo_ref, gq_ref, go_ref, out_ref,
                    o_scr, od_scr, la_ref, lb_ref, carry_ref, carry_x):
    tm = h_ref.shape[0]
    mw = oc_ref.shape[1]
    sub = V7X_SUBLANES
    subs = _sub_tiles(tm)
    rowi = lax.broadcasted_iota(jnp.int32, (sub, mw), 0)

    def lru_coeffs(rs, cx):
        xb = xb_ref[rs, :].astype(F32)
        xc = cb_ref[...] + cw_ref[LRU_CONV - 1:LRU_CONV, :] * xb
        for sh in range(1, LRU_CONV):
            xc = xc + cw_ref[LRU_CONV - 1 - sh:LRU_CONV - sh, :] * _shift_rows(xb, sh, cx)
        rg = _sigmoid(_bdot(xc, wa_ref[...]) + ba_ref[...])
        ig = _sigmoid(_bdot(xc, wx_ref[...]) + bx_ref[...])
        log_a = -LRU_C * rg * _softplus(-lam_ref[...])
        la_ref[rs, :] = jnp.exp(log_a)
        th = jnp.tanh(log_a)
        lb_ref[rs, :] = jnp.sqrt(-2.0 * th / (1.0 - th)) * (ig * xc)
        return xb[xb.shape[0] - sub:, :]

    def scan8(a, b, h):
        sh = 1
        while sh < sub:
            a_sh = jnp.where(rowi >= sh, pltpu.roll(a, sh, axis=0), 1.0)
            b_sh = jnp.where(rowi >= sh, pltpu.roll(b, sh, axis=0), 0.0)
            b = b + a * b_sh
            a = a * a_sh
            sh *= 2
        return b + a * h

    @pl.when(pl.program_id(1) == 0)
    def _():
        carry_ref[...] = jnp.zeros_like(carry_ref)
        carry_x[...] = jnp.zeros_like(carry_x)

    h = carry_ref[0:1, :]
    cx = carry_x[...]
    for rs in subs:
        cx = lru_coeffs(rs, cx)
        for r0 in range(rs.start, rs.stop, 2 * sub):
            h1 = scan8(la_ref[r0:r0 + sub, :], lb_ref[r0:r0 + sub, :], h)
            h2 = scan8(la_ref[r0 + sub:r0 + 2 * sub, :], lb_ref[r0 + sub:r0 + 2 * sub, :], h1[sub - 1:sub, :])
            h = h2[sub - 1:sub, :]
            gate = gate_ref[r0:r0 + 2 * sub, :].astype(F32)
            od_scr[r0:r0 + 2 * sub, :] = (jnp.concatenate([h1, h2], axis=0) * _gelu_tanh(gate)).astype(BF16)
    carry_ref[0:1, :] = h
    carry_x[...] = cx

    mix = [jnp.dot(oc_ref[rs, :], wout_ref[0:mw, :], preferred_element_type=F32)
           + jnp.dot(od_scr[rs, :], wout_ref[mw:2 * mw, :], preferred_element_type=F32) for rs in subs]
    h1s = [h_ref[rs, :] + _rms(mix[i], g_ref[...]) for i, rs in enumerate(subs)]
    for _ in _xattn_stages(h1s, subs, k_ref, v_ref, wq_ref, wo_ref, gq_ref, go_ref, o_scr, out_ref):
        pass


def _cd_post(h, o_c, xb, gate, lru, w_out, gain, xa, bsz, seq, mem_len, tm):
    t, d = h.shape
    mw = o_c.shape[1]
    nt = seq // tm
    k, v, wq, wo, gq, go = xa
    full = lambda a: pl.BlockSpec(a.shape, lambda b, i: (0, 0))
    rows = lambda w: pl.BlockSpec((tm, w), lambda b, i: (b * nt + i, 0))
    kv = pl.BlockSpec((mem_len, d), lambda b, i: (b, 0))
    return pl.pallas_call(
        _cd_post_kernel,
        grid=(bsz, nt),
        in_specs=[rows(d), rows(mw), rows(mw), rows(mw)] + [full(a) for a in lru] + [full(w_out), full(gain),
                  kv, kv, full(wq), full(wo), full(gq), full(go)],
        out_specs=rows(d),
        out_shape=jax.ShapeDtypeStruct((t, d), F32),
        scratch_shapes=[pltpu.VMEM((tm, d), BF16), pltpu.VMEM((tm, mw), BF16), pltpu.VMEM((tm, mw), F32),
                        pltpu.VMEM((tm, mw), F32), pltpu.VMEM((V7X_SUBLANES, mw), F32),
                        pltpu.VMEM((V7X_SUBLANES, mw), F32)],
        compiler_params=_cparams("parallel", "arbitrary"),
        name="cd_post",
    )(h, o_c, xb, gate, *lru, w_out, gain, k, v, wq, wo, gq, go)


def _row(vec):
    return vec.astype(F32).reshape(1, -1)


def _block_diag(blocks):
    nb, bi, bo = blocks.shape
    eye = jnp.eye(nb, dtype=blocks.dtype)
    return (eye[:, None, :, None] * blocks[:, :, None, :]).reshape(nb * bi, nb * bo)


def _head_sum_matrix(width, head):
    idx = jnp.arange(width) // head
    return (idx[:, None] == idx[None, :]).astype(BF16)


def _layer_ab(h, gains, p, xa, bsz, seq, mem_len, tm):
    d = h.shape[1]
    mw = d // 2
    dk = p['gla_b_decay'].shape[0] // GLA_HEADS
    nq = GLA_HEADS * dk
    w_in = p['ab_w_in']
    rank = p['gla_w_decay2'].shape[0]
    o_dlr = 2 * nq + 2 * mw
    w_main = jnp.concatenate([w_in[:, :o_dlr], w_in[:, o_dlr + rank:]], axis=1).astype(BF16)
    w_dlr = jnp.pad(w_in[:, o_dlr:o_dlr + rank], ((0, 0), (0, V7X_LANES - rank))).astype(BF16)
    w_decay2 = jnp.pad(p['gla_w_decay2'].astype(F32), ((0, V7X_LANES - rank), (0, 0)))
    perm = _chunk_step_perm(tm, S5_CHUNK)
    qkvgu, loga, u_perm = _ab_in(h, _row(gains[0]), w_main, w_dlr, w_decay2, _row(p['gla_b_decay']), perm, tm, mw)
    o_a = _gla(qkvgu, loga, _row(p['gla_norm_gain']), bsz, seq, tb=min(GLA_ROWS_PER_STEP, seq))
    kgen, ws, wc, a_pow = _s5_weights(p['s5_lambda_re'], p['s5_lambda_im'], p['s5_log_step'], p['s5_b_re'],
                                      p['s5_b_im'], p['s5_c_re'], p['s5_c_im'])
    y_perm = _s5(u_perm, kgen, ws.astype(BF16), wc.astype(BF16), a_pow, bsz, seq, tm)
    return _ab_post(h, o_a, y_perm, qkvgu, _row(p['s5_d']), p['s5_w_glu'].astype(BF16), _row(p['s5_b_glu']),
                    p['ab_w_out'].astype(BF16), _row(gains[1]), perm.T, xa, bsz, seq, mem_len, tm)


def _layer_cd(h, gains, p, xa, bsz, seq, mem_len, tm):
    d = h.shape[1]
    mw = d // 2
    w_in = p['cd_w_in']
    dr = p['rwkv_w2'].shape[0]
    ar = p['rwkv_a2'].shape[0]
    gr = p['rwkv_g2'].shape[0]
    assert dr + ar == V7X_LANES and gr == V7X_LANES, "the three low-rank projections fill two lane tiles"
    o = [0, mw, mw + dr, 2 * mw + dr, 3 * mw + dr, 3 * mw + dr + ar, 3 * mw + dr + ar + gr]
    col = lambda i, width: w_in[:, o[i]:o[i] + width]
    wbig = jnp.concatenate([col(0, mw), col(2, mw), col(3, mw), w_in[:, o[6]:]], axis=1).astype(BF16)
    wsm = jnp.concatenate([col(1, dr), col(4, ar), col(5, gr)], axis=1).astype(BF16)
    mu = p['rwkv_mu'].astype(F32)
    mseg = lambda i, width: mu[o[i]:o[i] + width]
    mub = jnp.concatenate([mseg(0, mw), mseg(2, mw), mseg(3, mw)]).reshape(1, -1)
    mus = jnp.concatenate([mseg(1, dr), mseg(4, ar), mseg(5, gr)]).reshape(1, -1)
    w2p = jnp.pad(p['rwkv_w2'], ((0, ar), (0, 0))).astype(BF16)
    a2p = jnp.pad(p['rwkv_a2'], ((dr, 0), (0, 0))).astype(BF16)
    hsum = _head_sum_matrix(mw, RWKV_HEAD)
    vecs = (mub, mus, _row(p['rwkv_w0']), _row(p['rwkv_a0']), _row(p['rwkv_k_k']), _row(p['rwkv_k_a']))
    mats = (w2p, a2p, p['rwkv_g2'].astype(BF16), hsum)
    r, cum, cump, k, v, kk, a, g, xb, gate = _cd_in(h, _row(gains[0]), wbig, wsm, vecs, mats, bsz, seq, tm, mw)
    o_c = _rwkv(r, cum, cump, k, v, kk, a, g, _row(p['rwkv_r_k']), _row(p['rwkv_ln_gain']),
                _row(p['rwkv_ln_bias']), hsum, bsz, seq)
    lru = (p['lru_conv_w'].astype(F32), _row(p['lru_conv_b']), _block_diag(p['lru_w_a']).astype(BF16),
           _row(p['lru_b_a']), _block_diag(p['lru_w_x']).astype(BF16), _row(p['lru_b_x']), _row(p['lru_lambda']))
    return _cd_post(h, o_c, xb, gate, lru, p['cd_w_out'].astype(BF16), _row(gains[1]), xa, bsz, seq, mem_len, tm)


def kernel(x, mem, norm_gain, xa_wq, xa_wk, xa_wv, xa_wo, mlp_w1, mlp_w2, ab_w_in, gla_w_decay2, gla_b_decay, gla_norm_gain, s5_lambda_re, s5_lambda_im, s5_log_step, s5_b_re, s5_b_im, s5_c_re, s5_c_im, s5_d, s5_w_glu, s5_b_glu, ab_w_out, cd_w_in, rwkv_mu, rwkv_w0, rwkv_w2, rwkv_a0, rwkv_a2, rwkv_g2, rwkv_k_k, rwkv_k_a, rwkv_r_k, rwkv_ln_gain, rwkv_ln_bias, lru_conv_w, lru_conv_b, lru_w_a, lru_b_a, lru_w_x, lru_b_x, lru_lambda, cd_w_out):
    bsz, seq, d = x.shape
    mem_len = mem.shape[1]
    depth = norm_gain.shape[0]
    tm = min(ROW_TILE, seq)
    ab = dict(ab_w_in=ab_w_in, gla_w_decay2=gla_w_decay2, gla_b_decay=gla_b_decay, gla_norm_gain=gla_norm_gain,
              s5_lambda_re=s5_lambda_re, s5_lambda_im=s5_lambda_im, s5_log_step=s5_log_step, s5_b_re=s5_b_re,
              s5_b_im=s5_b_im, s5_c_re=s5_c_re, s5_c_im=s5_c_im, s5_d=s5_d, s5_w_glu=s5_w_glu, s5_b_glu=s5_b_glu,
              ab_w_out=ab_w_out)
    cd = dict(cd_w_in=cd_w_in, rwkv_mu=rwkv_mu, rwkv_w0=rwkv_w0, rwkv_w2=rwkv_w2, rwkv_a0=rwkv_a0,
              rwkv_a2=rwkv_a2, rwkv_g2=rwkv_g2, rwkv_k_k=rwkv_k_k, rwkv_k_a=rwkv_k_a, rwkv_r_k=rwkv_r_k,
              rwkv_ln_gain=rwkv_ln_gain, rwkv_ln_bias=rwkv_ln_bias, lru_conv_w=lru_conv_w, lru_conv_b=lru_conv_b,
              lru_w_a=lru_w_a, lru_b_a=lru_b_a, lru_w_x=lru_w_x, lru_b_x=lru_b_x, lru_lambda=lru_lambda,
              cd_w_out=cd_w_out)
    h = x.astype(F32).reshape(bsz * seq, d)
    mem2 = mem.astype(F32).reshape(bsz * mem_len, d)
    for layer in range(depth):
        g = norm_gain[layer]
        i = layer // 2
        km, vm = _mem_kv(mem2, _row(g[6]), xa_wk[layer].astype(BF16), xa_wv[layer].astype(BF16), mem_len)
        xa = (km, vm, xa_wq[layer].astype(BF16), xa_wo[layer].astype(BF16), _row(g[2]), _row(g[3]))
        if layer % 2 == 0:
            h = _layer_ab(h, g, {n: w[i] for n, w in ab.items()}, xa, bsz, seq, mem_len, tm)
        else:
            h = _layer_cd(h, g, {n: w[i] for n, w in cd.items()}, xa, bsz, seq, mem_len, tm)
        h = _mlp(h, _row(g[4]), _row(g[5]), mlp_w1[layer].astype(F32), mlp_w2[layer].astype(F32),
                 tm=tm, tf=MLP_FF_TILE)
    return h.reshape(bsz, seq, d).astype(x.dtype)
```

```python
import functools
import math

import jax
import jax.numpy as jnp
from jax import lax
from jax.experimental import pallas as pl
from jax.experimental.pallas import tpu as pltpu

F32 = jnp.float32
BF16 = jnp.bfloat16
HIGHEST = lax.Precision.HIGHEST

NORM_EPS = 1e-6
GLA_HEADS = 4
GLA_TAU = 16.0
GLA_CHUNK = 64
S5_GROUP = 16
S5_STATE = 64
S5_CHUNK = 16
RWKV_HEAD = 64
RWKV_CHUNK = 64
RWKV_GN_EPS = 64e-5
LRU_CONV = 4
LRU_C = 8.0
XA_HEADS = 4

V7X_SUBLANES = 8
V7X_LANES = 128
V7X_MXU_DIM = 256
V7X_VMEM_BYTES = 64 * 1024 * 1024
VMEM_LIMIT_BYTES = (3 * V7X_VMEM_BYTES) // 4

ROW_TILE = 1024
CD_IN_SUB_TILES = 4
GLA_ROWS_PER_STEP = 256
RWKV_CHUNKS_PER_STEP = 8
RWKV_SLAB_HEADS = V7X_MXU_DIM // RWKV_HEAD
MLP_FF_TILE = 1024
ROW_SUB_TILES = 4

CD_IN_OUT_DTYPES = (BF16, F32, F32, BF16, BF16, BF16, BF16, BF16, BF16, BF16)


def _cparams(*semantics):
    return pltpu.CompilerParams(dimension_semantics=semantics, vmem_limit_bytes=VMEM_LIMIT_BYTES)


def _rms(x, gain):
    return x * lax.rsqrt(jnp.mean(x * x, axis=-1, keepdims=True) + NORM_EPS) * gain


def _bdot(a, b):
    return jnp.dot(a.astype(BF16), b.astype(BF16), preferred_element_type=F32)


def _split_bf16(x):
    hi = x.astype(BF16)
    return hi, (x - hi.astype(F32)).astype(BF16)


def _dot3(a, b):
    a_hi, a_lo = _split_bf16(a)
    b_hi, b_lo = _split_bf16(b)
    return (jnp.dot(a_hi, b_hi, preferred_element_type=F32) + jnp.dot(a_lo, b_hi, preferred_element_type=F32)
            + jnp.dot(a_hi, b_lo, preferred_element_type=F32))


def _split_dot(x, w_bf16):
    hi, lo = _split_bf16(x)
    return (jnp.dot(hi, w_bf16, preferred_element_type=F32)
            + jnp.dot(lo, w_bf16, preferred_element_type=F32))


def _split3_dot(w_bf16, x):
    hi = x.astype(BF16)
    r1 = x - hi.astype(F32)
    mid = r1.astype(BF16)
    lo = (r1 - mid.astype(F32)).astype(BF16)
    return (jnp.dot(w_bf16, hi, preferred_element_type=F32) + jnp.dot(w_bf16, mid, preferred_element_type=F32)
            + jnp.dot(w_bf16, lo, preferred_element_type=F32))


def _sigmoid(x):
    return 1.0 / (1.0 + jnp.exp(-x))


def _softplus(x):
    return jnp.maximum(x, 0.0) + jnp.log(1.0 + jnp.exp(-jnp.abs(x)))


def _gelu_tanh(x):
    c = math.sqrt(2.0 / math.pi)
    return x * (0.5 * (1.0 + jnp.tanh(c * (x + 0.044715 * (x * x * x)))))


def _sub_tiles(tm):
    ts = tm // ROW_SUB_TILES
    return [slice(i * ts, (i + 1) * ts) for i in range(ROW_SUB_TILES)]


def _tril_mask(n, strict=False):
    row = lax.broadcasted_iota(jnp.int32, (n, n), 0)
    col = lax.broadcasted_iota(jnp.int32, (n, n), 1)
    return (col < row) if strict else (col <= row)


def _shift_rows(x, shift, carry):
    rolled = pltpu.roll(x, shift, axis=0)
    crolled = pltpu.roll(carry, shift, axis=0)
    rowi = lax.broadcasted_iota(jnp.int32, carry.shape, 0)
    first = jnp.where(rowi < shift, crolled, rolled[0:V7X_SUBLANES])
    return jnp.concatenate([first, rolled[V7X_SUBLANES:]], axis=0)


def _ab_in_kernel(h_ref, g_ref, w_ref, wd_ref, wd2_ref, bd_ref, perm_ref, tril_ref, out_ref, loga_ref, up_ref):
    subs = _sub_tiles(h_ref.shape[0])
    hn = [_rms(h_ref[rs, :], g_ref[...]).astype(BF16) for rs in subs]
    out16 = [jnp.dot(x, w_ref[...], preferred_element_type=F32).astype(BF16) for x in hn]
    dlr = [jnp.dot(x, wd_ref[...], preferred_element_type=F32) for x in hn]
    for i, rs in enumerate(subs):
        out_ref[rs, :] = out16[i]
        z = _dot3(dlr[i], wd2_ref[...]) + bd_ref[...]
        loga_ref[rs, :] = _split3_dot(tril_ref[...], -_softplus(-z) * (1.0 / GLA_TAU))
    mw = up_ref.shape[1]
    u16 = jnp.concatenate([o[:, o.shape[1] - mw:] for o in out16], axis=0)
    up_ref[...] = jnp.dot(perm_ref[...], u16, preferred_element_type=F32).astype(BF16)


def _ab_in(h, gain, w_main, w_dlr, w_decay2, b_decay, perm, tm, mw):
    t, d = h.shape
    n_main = w_main.shape[1]
    n_dk = w_decay2.shape[1]
    full = lambda a: pl.BlockSpec(a.shape, lambda i: (0, 0))
    rows = lambda w: pl.BlockSpec((tm, w), lambda i: (i, 0))
    tril = _chunk_tril(tm // ROW_SUB_TILES, GLA_CHUNK)
    return pl.pallas_call(
        _ab_in_kernel,
        grid=(t // tm,),
        in_specs=[rows(d), full(gain), full(w_main), full(w_dlr), full(w_decay2), full(b_decay), full(perm),
                  full(tril)],
        out_specs=[rows(n_main), rows(n_dk), rows(mw)],
        out_shape=[jax.ShapeDtypeStruct((t, n_main), BF16), jax.ShapeDtypeStruct((t, n_dk), F32),
                   jax.ShapeDtypeStruct((t, mw), BF16)],
        compiler_params=_cparams("parallel"),
        name="ab_in",
    )(h, gain, w_main, w_dlr, w_decay2, b_decay, perm, tril)


def _chunk_tril(n, c):
    idx = jnp.arange(n)
    return (((idx[:, None] // c) == (idx[None, :] // c)) & (idx[None, :] <= idx[:, None])).astype(BF16)


def _chunk_step_perm(tm, tc):
    dst = jnp.arange(tm)
    src = (dst % (tm // tc)) * tc + dst // (tm // tc)
    return (src[:, None] == jnp.arange(tm)[None, :]).astype(BF16)


def _gla_kernel(q_ref, k_ref, v_ref, gate_ref, la_ref, gain_ref, o_ref, state_ref, *, n_chunks, dk, dv):
    c = GLA_CHUNK

    @pl.when(pl.program_id(1) == 0)
    def _():
        state_ref[...] = jnp.zeros_like(state_ref)

    tril = _tril_mask(c)
    scale = dk ** -0.5
    lane_head = lax.broadcasted_iota(jnp.int32, (c, GLA_HEADS * dk), 1) // dk
    nt_dims = (((1,), (1,)), ((), ()))
    tn_dims = (((0,), (0,)), ((), ()))

    chunks = []
    for i in range(n_chunks):
        rs = slice(i * c, (i + 1) * c)
        b = la_ref[rs, :]
        b_last = b[c - 1:c, :]
        k = k_ref[rs, :].astype(F32)
        chunks.append(dict(rs=rs, q_in=(q_ref[rs, :].astype(F32) * jnp.exp(b) * scale).astype(BF16),
                           k_in=(k * jnp.exp(-b)).astype(BF16), k_st=k * jnp.exp(b_last - b),
                           dec=jnp.exp(b_last)))
    for ch in chunks:
        ch['v'] = [v_ref[ch['rs'], h * dv:(h + 1) * dv].astype(BF16) for h in range(GLA_HEADS)]
        zero16 = jnp.zeros((), BF16)
        ch['scores'] = [
            jnp.where(tril, lax.dot_general(jnp.where(lane_head == h, ch['q_in'], zero16), ch['k_in'], nt_dims,
                                            preferred_element_type=F32), 0.0).astype(BF16)
            for h in range(GLA_HEADS)]
    for ch in chunks:
        ch['o'] = [jnp.dot(ch['scores'][h], ch['v'][h], preferred_element_type=F32) for h in range(GLA_HEADS)]
        ch['d_state'] = [
            lax.dot_general(ch['v'][h], jnp.where(lane_head == h, ch['k_st'], 0.0).astype(BF16), tn_dims,
                            preferred_element_type=F32) for h in range(GLA_HEADS)]

    st = [state_ref[h] for h in range(GLA_HEADS)]
    for ch in chunks:
        ch['st'] = [s.astype(BF16) for s in st]
        st = [st[h] * ch['dec'] + ch['d_state'][h] for h in range(GLA_HEADS)]
    for h in range(GLA_HEADS):
        state_ref[h] = st[h]
    for ch in chunks:
        for h in range(GLA_HEADS):
            vs = slice(h * dv, (h + 1) * dv)
            o = ch['o'][h] + lax.dot_general(ch['q_in'], ch['st'][h], nt_dims, preferred_element_type=F32)
            on = o * lax.rsqrt(jnp.mean(o * o, axis=-1, keepdims=True) + NORM_EPS) * gain_ref[:, vs]
            g = gate_ref[ch['rs'], vs].astype(F32)
            o_ref[ch['rs'], vs] = (on * (g * _sigmoid(g))).astype(o_ref.dtype)


def _gla(qkvgu, loga, gain, bsz, seq, tb):
    t = bsz * seq
    dk = loga.shape[1] // GLA_HEADS
    dv = gain.shape[1] // GLA_HEADS
    nq, nv = GLA_HEADS * dk, GLA_HEADS * dv
    nt = seq // tb
    row = lambda b, i: b * nt + i
    kern = functools.partial(_gla_kernel, n_chunks=tb // GLA_CHUNK, dk=dk, dv=dv)
    return pl.pallas_call(
        kern,
        grid=(bsz, nt),
        in_specs=[pl.BlockSpec((tb, nq), lambda b, i: (row(b, i), 0)),
                  pl.BlockSpec((tb, nq), lambda b, i: (row(b, i), 1)),
                  pl.BlockSpec((tb, nv), lambda b, i: (row(b, i), 1)),
                  pl.BlockSpec((tb, nv), lambda b, i: (row(b, i), 2)),
                  pl.BlockSpec((tb, nq), lambda b, i: (row(b, i), 0)),
                  pl.BlockSpec(gain.shape, lambda b, i: (0, 0))],
        out_specs=pl.BlockSpec((tb, nv), lambda b, i: (row(b, i), 0)),
        out_shape=jax.ShapeDtypeStruct((t, nv), BF16),
        scratch_shapes=[pltpu.VMEM((GLA_HEADS, dv, nq), F32)],
        compiler_params=_cparams("parallel", "arbitrary"),
        name="gla",
    )(qkvgu, qkvgu, qkvgu, qkvgu, loga, gain)


def _transpose_lane_chunks(sets, ch):
    n = len(sets[0])
    chunk = lax.broadcasted_iota(jnp.int32, sets[0][0].shape, 1) // ch
    sets = [list(xs) for xs in sets]
    d = n // 2
    while d >= 1:
        low_half = (chunk & d) == 0
        for xs in sets:
            for r in range(n):
                if r & d:
                    continue
                lo, hi = xs[r], xs[r + d]
                xs[r] = jnp.where(low_half, lo, pltpu.roll(hi, d * ch, axis=1))
                xs[r + d] = jnp.where(low_half, pltpu.roll(lo, (n - d) * ch, axis=1), hi)
        d //= 2
    return sets


def _s5_kernel(u_ref, kgen_ref, ws_ref, wc_ref, a_ref, y_ref, uy_scr, s_scr, h_scr, *,
               n_pairs, bsz, seq, tm):
    n2 = 2 * S5_STATE
    tc, ch = S5_CHUNK, S5_GROUP
    lanes = u_ref.shape[1]
    n_grp = lanes // ch
    n_half = tc // n_grp
    kt = tm // tc
    n_tiles = seq // tm
    rows = n_tiles * kt * bsz

    def tile_rows(b, i, step):
        return pl.ds(pl.multiple_of(b * seq + i * tm + step * kt, kt), kt)

    def gather_tile(i, _):
        for b in range(bsz):
            sets = [[u_ref[tile_rows(b, i, half * n_grp + j8), :].astype(F32) for j8 in range(n_grp)]
                    for half in range(n_half)]
            for half, xs in enumerate(_transpose_lane_chunks(sets, ch)):
                for g, x in enumerate(xs):
                    uy_scr[g, half, i, pl.ds(b, kt, stride=bsz), :] = x
        return 0

    lax.fori_loop(0, n_tiles, gather_tile, 0)

    def group_body(g, _):
        _s5_group(g, kgen_ref, ws_ref, wc_ref, a_ref, uy_scr, s_scr, h_scr, n_pairs=n_pairs, bsz=bsz, rows=rows,
                  lanes=lanes, n_half=n_half, n_tiles=n_tiles, kt=kt)
        return 0

    lax.fori_loop(0, n_grp // 2, group_body, 0)

    def scatter_tile(i, _):
        for b in range(bsz):
            sets = [[uy_scr[g, half, i, pl.ds(b, kt, stride=bsz), :] for g in range(n_grp)]
                    for half in range(n_half)]
            for half, ys in enumerate(_transpose_lane_chunks(sets, ch)):
                for j8, y in enumerate(ys):
                    y_ref[tile_rows(b, i, half * n_grp + j8), :] = y.astype(y_ref.dtype)
        return 0

    lax.fori_loop(0, n_tiles, scatter_tile, 0)


def _s5_group(g, kgen_ref, ws_ref, wc_ref, a_ref, uy_scr, s_scr, h_scr, *, n_pairs, bsz, rows, lanes, n_half,
              n_tiles, kt):
    n2 = 2 * S5_STATE
    tc, ch = S5_CHUNK, S5_GROUP
    pair = [2 * g, 2 * g + 1]
    us = [jnp.concatenate([uy_scr[gi, hf].reshape(rows, lanes) for hf in range(n_half)], axis=1).astype(BF16)
          for gi in pair]
    for p, gi in enumerate(pair):
        s_scr[p] = jnp.dot(us[p], ws_ref[gi], preferred_element_type=F32)
    a1 = [a_ref[gi, 0:1, :] for gi in pair]
    a2 = [a_ref[gi, 1:2, :] for gi in pair]

    def body(j, carry):
        r0 = pl.multiple_of(j * (2 * bsz), 2 * bsz)
        out = []
        for p in range(len(pair)):
            h, hs = carry[2 * p], carry[2 * p + 1]
            blk = s_scr[p, pl.ds(r0, 2 * bsz), :]
            h1 = a1[p] * h + a2[p] * hs + blk[0:bsz, 0:n2]
            hs1 = a1[p] * hs - a2[p] * h + blk[0:bsz, n2:2 * n2]
            h2 = a1[p] * h1 + a2[p] * hs1 + blk[bsz:2 * bsz, 0:n2]
            hs2 = a1[p] * hs1 - a2[p] * h1 + blk[bsz:2 * bsz, n2:2 * n2]
            h_scr[p, pl.ds(r0, 2 * bsz), :] = jnp.concatenate([h, h1], axis=0)
            out += [h2, hs2]
        return tuple(out)

    zero = jnp.zeros((bsz, n2), F32)
    lax.fori_loop(0, n_pairs, body, (zero,) * (2 * len(pair)))
    for p, gi in enumerate(pair):
        kg = kgen_ref[gi]
        lane_w = lax.broadcasted_iota(jnp.int32, kg.shape, 1)
        tz = jnp.concatenate([kg] + [jnp.where(lane_w >= ch * j, pltpu.roll(kg, ch * j, axis=1), 0.0)
                                     for j in range(1, tc)], axis=0).astype(BF16)
        yg = (jnp.dot(us[p], tz, preferred_element_type=F32)
              + jnp.dot(h_scr[p].astype(BF16), wc_ref[gi], preferred_element_type=F32))
        for hf in range(n_half):
            uy_scr[gi, hf] = yg[:, hf * lanes:(hf + 1) * lanes].reshape(n_tiles, kt * bsz, lanes)


def _s5(u_perm, kgen, ws, wc, a_pow, bsz, seq, tm):
    t, mw = u_perm.shape
    lanes = V7X_LANES
    width = kgen.shape[2]
    g_per_blk = lanes // S5_GROUP
    assert 2 * bsz == V7X_SUBLANES, "the chunk scan walks two chunks per 8-row tile"
    rows = (seq // S5_CHUNK) * bsz
    n_pairs = rows // (2 * bsz)
    per_blk = lambda a: pl.BlockSpec((g_per_blk,) + a.shape[1:], lambda q: (q, 0, 0))
    blk = pl.BlockSpec((t, lanes), lambda q: (0, q))
    kern = functools.partial(_s5_kernel, n_pairs=n_pairs, bsz=bsz, seq=seq, tm=tm)
    kt = tm // S5_CHUNK
    return pl.pallas_call(
        kern,
        grid=(mw // lanes,),
        in_specs=[blk, per_blk(kgen), per_blk(ws), per_blk(wc), per_blk(a_pow)],
        out_specs=blk,
        out_shape=jax.ShapeDtypeStruct((t, mw), BF16),
        scratch_shapes=[pltpu.VMEM((g_per_blk, width // lanes, seq // tm, kt * bsz, lanes), F32),
                        pltpu.VMEM((2, rows, ws.shape[2]), F32), pltpu.VMEM((2, rows, wc.shape[1]), F32)],
        compiler_params=_cparams("parallel"),
        name="s5",
    )(u_perm, kgen, ws, wc, a_pow)


def _s5_weights(lam_re, lam_im, log_step, b_re, b_im, c_re, c_im):
    tc = S5_CHUNK
    groups, n = lam_re.shape
    lr = jnp.minimum(lam_re.astype(F32), -1e-4)
    li = lam_im.astype(F32)
    delta = jnp.exp(log_step.astype(F32))[:, None]
    tau = jnp.arange(tc + 1, dtype=F32)
    mag = jnp.exp((lr * delta)[..., None] * tau)
    ang = (li * delta)[..., None] * tau
    pw_re, pw_im = mag * jnp.cos(ang), mag * jnp.sin(ang)
    num_re, num_im = pw_re[..., 1] - 1.0, pw_im[..., 1]
    den = lr * lr + li * li
    f_re = (num_re * lr + num_im * li) / den
    f_im = (num_im * lr - num_re * li) / den
    b_re, b_im = b_re.astype(F32), b_im.astype(F32)
    bb_re = f_re[..., None] * b_re - f_im[..., None] * b_im
    bb_im = f_re[..., None] * b_im + f_im[..., None] * b_re
    ct_re = jnp.transpose(c_re.astype(F32), (0, 2, 1))
    ct_im = jnp.transpose(c_im.astype(F32), (0, 2, 1))
    cp_re = ct_re[:, :, None, :] * pw_re[..., None] - ct_im[:, :, None, :] * pw_im[..., None]
    cp_im = ct_re[:, :, None, :] * pw_im[..., None] + ct_im[:, :, None, :] * pw_re[..., None]
    width = tc * S5_GROUP
    ein = functools.partial(jnp.einsum, precision=HIGHEST)
    kgen = (ein('gnd,gnx->gdx', bb_re, cp_re[:, :, :tc].reshape(groups, n, width))
            - ein('gnd,gnx->gdx', bb_im, cp_im[:, :, :tc].reshape(groups, n, width)))
    wc = jnp.concatenate([cp_re[:, :, 1:].reshape(groups, n, width),
                          -cp_im[:, :, 1:].reshape(groups, n, width)], axis=1)
    rev_re = jnp.transpose(pw_re[..., tc - 1::-1], (0, 2, 1))
    rev_im = jnp.transpose(pw_im[..., tc - 1::-1], (0, 2, 1))
    bt_re = jnp.transpose(bb_re, (0, 2, 1))
    bt_im = jnp.transpose(bb_im, (0, 2, 1))
    s_re = (rev_re[:, :, None, :] * bt_re[:, None] - rev_im[:, :, None, :] * bt_im[:, None]).reshape(groups, width, n)
    s_im = (rev_re[:, :, None, :] * bt_im[:, None] + rev_im[:, :, None, :] * bt_re[:, None]).reshape(groups, width, n)
    ws = jnp.concatenate([s_re, s_im, s_im, s_re], axis=-1)
    a_pow = jnp.stack([jnp.concatenate([pw_re[..., tc], pw_re[..., tc]], axis=-1),
                       jnp.concatenate([-pw_im[..., tc], pw_im[..., tc]], axis=-1)], axis=1)
    return kgen, ws, wc, a_pow


def _xattn_stages(h_subs, subs, k_ref, v_ref, wq_ref, wo_ref, gq_ref, go_ref, o_scr, out_ref):
    d = out_ref.shape[1]
    hd = d // XA_HEADS
    xn = [_rms(h, gq_ref[...]).astype(BF16) for h in h_subs]
    yield
    q = [jnp.dot(x, wq_ref[...], preferred_element_type=F32).astype(BF16) for x in xn]
    yield
    for hh in range(XA_HEADS):
        cs = slice(hh * hd, (hh + 1) * hd)
        for i, rs in enumerate(subs):
            s = lax.dot_general(q[i][:, cs], k_ref[:, cs], (((1,), (1,)), ((), ())),
                                preferred_element_type=F32) * (hd ** -0.5)
            e = jnp.exp(s - jnp.max(s, axis=-1, keepdims=True))
            p = e / jnp.sum(e, axis=-1, keepdims=True)
            o_scr[rs, cs] = _bdot(p, v_ref[:, cs]).astype(BF16)
            yield
    xa = [jnp.dot(o_scr[rs, :], wo_ref[...], preferred_element_type=F32) for rs in subs]
    yield
    for i, rs in enumerate(subs):
        out_ref[rs, :] = h_subs[i] + _rms(xa[i], go_ref[...])
        yield


def _ab_post_kernel(h_ref, oa_ref, ys_ref, u_ref, d_ref, wglu_ref, bglu_ref, wout_ref, g_ref, perm_ref,
                    k_ref, v_ref, wq_ref, wo_ref, gq_ref, go_ref, out_ref, o_scr):
    tm = h_ref.shape[0]
    mw = oa_ref.shape[1]
    subs = _sub_tiles(tm)
    y_ssm = jnp.dot(perm_ref[...], ys_ref[...], preferred_element_type=F32)
    y = y_ssm + d_ref[...] * u_ref[...].astype(F32)
    ob = (_gelu_tanh(y) * _sigmoid(_bdot(y, wglu_ref[...]) + bglu_ref[...])).astype(BF16)
    mix = [jnp.dot(oa_ref[rs, :], wout_ref[0:mw, :], preferred_element_type=F32)
           + jnp.dot(ob[rs, :], wout_ref[mw:2 * mw, :], preferred_element_type=F32) for rs in subs]
    h1 = [h_ref[rs, :] + _rms(mix[i], g_ref[...]) for i, rs in enumerate(subs)]
    for _ in _xattn_stages(h1, subs, k_ref, v_ref, wq_ref, wo_ref, gq_ref, go_ref, o_scr, out_ref):
        pass


def _ab_post(h, o_a, y_ssm, qkvgu, d_skip, w_glu, b_glu, w_out, gain, perm_t, xa, bsz, seq, mem_len, tm):
    t, d = h.shape
    mw = o_a.shape[1]
    nt = seq // tm
    k, v, wq, wo, gq, go = xa
    full = lambda a: pl.BlockSpec(a.shape, lambda b, i: (0, 0))
    rows = lambda w, cb=0: pl.BlockSpec((tm, w), lambda b, i: (b * nt + i, cb))
    kv = pl.BlockSpec((mem_len, d), lambda b, i: (b, 0))
    return pl.pallas_call(
        _ab_post_kernel,
        grid=(bsz, nt),
        in_specs=[rows(d), rows(mw), rows(mw), rows(mw, qkvgu.shape[1] // mw - 1), full(d_skip), full(w_glu),
                  full(b_glu), full(w_out), full(gain), full(perm_t), kv, kv, full(wq), full(wo), full(gq), full(go)],
        out_specs=rows(d),
        out_shape=jax.ShapeDtypeStruct((t, d), F32),
        scratch_shapes=[pltpu.VMEM((tm, d), BF16)],
        compiler_params=_cparams("parallel", "parallel"),
        name="ab_post",
    )(h, o_a, y_ssm, qkvgu, d_skip, w_glu, b_glu, w_out, gain, perm_t, k, v, wq, wo, gq, go)


def _mem_kv_kernel(mem_ref, g_ref, wk_ref, wv_ref, k_ref, v_ref):
    mn = _rms(mem_ref[...], g_ref[...]).astype(BF16)
    k_ref[...] = jnp.dot(mn, wk_ref[...], preferred_element_type=F32).astype(BF16)
    v_ref[...] = jnp.dot(mn, wv_ref[...], preferred_element_type=F32).astype(BF16)


def _mem_kv(mem, gain, wk, wv, tm):
    t, d = mem.shape
    full = lambda a: pl.BlockSpec(a.shape, lambda i: (0, 0))
    rows = pl.BlockSpec((tm, d), lambda i: (i, 0))
    return pl.pallas_call(
        _mem_kv_kernel,
        grid=(t // tm,),
        in_specs=[rows, full(gain), full(wk), full(wv)],
        out_specs=[rows, rows],
        out_shape=[jax.ShapeDtypeStruct((t, d), BF16), jax.ShapeDtypeStruct((t, d), BF16)],
        compiler_params=_cparams("parallel"),
        name="mem_kv",
    )(mem, gain, wk, wv)


def _mlp_kernel(h_ref, gi_ref, go_ref, w1_ref, w2_ref, out_ref, *, tf):
    subs = _sub_tiles(h_ref.shape[0])
    xn = [_rms(h_ref[rs, :], gi_ref[...]).astype(BF16) for rs in subs]
    dff = w1_ref.shape[1]
    acts = [[] for _ in subs]
    for j in range(dff // tf):
        for i in range(len(subs)):
            a = jnp.maximum(jnp.dot(xn[i], w1_ref[:, j * tf:(j + 1) * tf], preferred_element_type=F32), 0.0)
            acts[i].append((a * a).astype(BF16))
    ff = [jnp.dot(jnp.concatenate(acts[i], axis=1), w2_ref[...], preferred_element_type=F32)
          for i in range(len(subs))]
    for i, rs in enumerate(subs):
        out_ref[rs, :] = h_ref[rs, :] + _rms(ff[i], go_ref[...])


def _mlp(h, gi, go, w1, w2, tm, tf):
    t, d = h.shape
    full = lambda a: pl.BlockSpec(a.shape, lambda i: (0, 0))
    resident = lambda a: pl.BlockSpec(a.shape, lambda i: (0, 0), pipeline_mode=pl.Buffered(1))
    rows = pl.BlockSpec((tm, d), lambda i: (i, 0))
    return pl.pallas_call(
        functools.partial(_mlp_kernel, tf=tf),
        grid=(t // tm,),
        in_specs=[rows, full(gi), full(go), resident(w1), resident(w2)],
        out_specs=rows,
        out_shape=jax.ShapeDtypeStruct((t, d), F32),
        compiler_params=_cparams("parallel"),
        name="mlp",
    )(h, gi, go, w1, w2)


def _cd_in_kernel(h_ref, g_ref, wbig_ref, wsm_ref, mub_ref, mus_ref, w0_ref, w2_ref, a0_ref, a2_ref, g2_ref,
                  kk_ref, ka_ref, hsum_ref, tril_ref,
                  r_o, cum_o, cump_o, k_o, v_o, kk_o, a_o, g_o, xb_o, gate_o,
                  carry_b, carry_s, *, mw, n_sub):
    tm = h_ref.shape[0]

    @pl.when(pl.program_id(1) == 0)
    def _():
        carry_b[...] = jnp.zeros_like(carry_b)
        carry_s[...] = jnp.zeros_like(carry_s)

    ts = tm // n_sub
    proj = []
    for sb in range(n_sub):
        hn = _rms(h_ref[sb * ts:(sb + 1) * ts, :], g_ref[...]).astype(BF16)
        proj.append((jnp.dot(hn, wbig_ref[...], preferred_element_type=F32),
                     jnp.dot(hn, wsm_ref[...], preferred_element_type=F32)))

    for sb in range(n_sub):
        rs = slice(sb * ts, (sb + 1) * ts)
        pb, ps = proj[sb]
        p3 = pb[:, 0:3 * mw]
        xb_o[rs, :] = pb[:, 3 * mw:4 * mw].astype(BF16)
        gate_o[rs, :] = pb[:, 4 * mw:5 * mw].astype(BF16)

        prev3 = _shift_rows(p3, 1, carry_b[...])
        prevs = _shift_rows(ps, 1, carry_s[...])
        carry_b[...] = p3[ts - V7X_SUBLANES:ts, :]
        carry_s[...] = ps[ts - V7X_SUBLANES:ts, :]
        p3 = p3 + (prev3 - p3) * mub_ref[...]
        ps = ps + (prevs - ps) * mus_ref[...]
        r = p3[:, 0:mw]
        k = p3[:, mw:2 * mw]
        v = p3[:, 2 * mw:3 * mw]
        lora = ps[:, 0:V7X_LANES]
        wlog = -_softplus(-(w0_ref[...] + _bdot(jnp.tanh(lora), w2_ref[...]))) - 0.5
        a = _sigmoid(a0_ref[...] + _bdot(lora, a2_ref[...]))
        kkr = k * kk_ref[...]
        norm = jnp.sqrt(_bdot(kkr * kkr, hsum_ref[...]))
        r_o[rs, :] = r.astype(BF16)
        lw = -jnp.exp(wlog)
        cum = _split3_dot(tril_ref[...], lw)
        cum_o[rs, :] = cum
        cump_o[rs, :] = cum - lw
        k_o[rs, :] = (k * (1.0 + (a - 1.0) * ka_ref[...])).astype(BF16)
        v_o[rs, :] = v.astype(BF16)
        kk_o[rs, :] = (kkr / jnp.maximum(norm, 1e-12)).astype(BF16)
        a_o[rs, :] = a.astype(BF16)
        g_o[rs, :] = _bdot(_sigmoid(ps[:, V7X_LANES:2 * V7X_LANES]), g2_ref[...]).astype(BF16)


def _cd_in(h, gain, wbig, wsm, vecs, mats, bsz, seq, tm, mw):
    t, d = h.shape
    nt = seq // tm
    full = lambda a: pl.BlockSpec(a.shape, lambda b, i: (0, 0))
    rows = lambda w: pl.BlockSpec((tm, w), lambda b, i: (b * nt + i, 0))
    (mub, mus, w0, a0, kk_w, ka_w) = vecs
    (w2p, a2p, g2, hsum) = mats
    tril = _chunk_tril(tm // CD_IN_SUB_TILES, RWKV_CHUNK)
    args = (h, gain, wbig, wsm, mub, mus, w0, w2p, a0, a2p, g2, kk_w, ka_w, hsum, tril)
    kern = functools.partial(_cd_in_kernel, mw=mw, n_sub=CD_IN_SUB_TILES)
    return pl.pallas_call(
        kern,
        grid=(bsz, nt),
        in_specs=[rows(d)] + [full(a) for a in args[1:]],
        out_specs=[rows(mw)] * len(CD_IN_OUT_DTYPES),
        out_shape=[jax.ShapeDtypeStruct((t, mw), dt) for dt in CD_IN_OUT_DTYPES],
        scratch_shapes=[pltpu.VMEM((V7X_SUBLANES, 3 * mw), F32), pltpu.VMEM((V7X_SUBLANES, wsm.shape[1]), F32)],
        compiler_params=_cparams("parallel", "arbitrary"),
        name="cd_in",
    )(*args)


def _rwkv_kernel(r_ref, cum_ref, cump_ref, k_ref, v_ref, kk_ref, a_ref, g_ref, rk_ref, lng_ref, lnb_ref, hsum_ref,
                 o_ref, state_ref, y_scr, *, slab_heads, n_chunks):
    c = RWKV_CHUNK
    hd = RWKV_HEAD
    assert c == hd, "one block mask serves both the (head, s) and the (head, d) layouts"
    sw = slab_heads * hd
    n_slabs = r_ref.shape[1] // sw

    @pl.when(pl.program_id(1) == 0)
    def _():
        state_ref[...] = jnp.zeros_like(state_ref)

    blk_m = (lax.broadcasted_iota(jnp.int32, (sw, sw), 0) // c) == (lax.broadcasted_iota(jnp.int32, (sw, sw), 1) // c)
    wide_t = lax.broadcasted_iota(jnp.int32, (c, sw), 0)
    wide_s = lax.broadcasted_iota(jnp.int32, (c, sw), 1) % c
    strict_w = wide_s < wide_t
    incl_w = wide_s <= wide_t
    eye_w = jnp.where(wide_s == wide_t, 1.0, 0.0)

    def bdiag(x):
        x16 = x.astype(BF16)
        return jnp.where(blk_m, jnp.concatenate([x16] * slab_heads, axis=0), jnp.zeros((), BF16))

    def mm(a, w16):
        return jnp.dot(a.astype(BF16), w16, preferred_element_type=F32)

    nt_dims = (((1,), (1,)), ((), ()))
    chains = []
    for chunk in range(n_chunks):
        rs = slice(chunk * c, (chunk + 1) * c)
        cum = cum_ref[rs, :]
        cum_last = cum[c - 1:c, :]
        r = r_ref[rs, :].astype(F32)
        k = k_ref[rs, :].astype(F32)
        v = v_ref[rs, :].astype(F32)
        kk = kk_ref[rs, :].astype(F32)
        bvec = kk * a_ref[rs, :].astype(F32)
        inv_g = jnp.exp(-cum)
        to_end = jnp.exp(cum_last - cum)
        rt = r * jnp.exp(cum)
        kp = kk * jnp.exp(cump_ref[rs, :])
        be = bvec * inv_g
        kh = k * inv_g
        bb = bvec * to_end
        kb = k * to_end
        g_end = jnp.exp(cum_last)
        for s in range(n_slabs):
            ls = slice(s * sw, (s + 1) * sw)
            chains.append(dict(rs=rs, ls=ls, slab=s, rt=rt[:, ls], kp=kp[:, ls], v=v[:, ls], be=be[:, ls],
                               kh=kh[:, ls], kb=kb[:, ls], bb=bb[:, ls], g_end=g_end[:, ls]))

    for ch in chains:
        lhs2 = jnp.concatenate([ch['kp'], ch['rt']], axis=0).astype(BF16)
        ab = lax.dot_general(lhs2, bdiag(ch['be']), nt_dims, preferred_element_type=F32)
        ak = lax.dot_general(lhs2, bdiag(ch['kh']), nt_dims, preferred_element_type=F32)
        ch['a_kb'] = jnp.where(strict_w, ab[0:c], 0.0)
        ch['a_rb'] = jnp.where(incl_w, ab[c:2 * c], 0.0).astype(BF16)
        ch['a_kr'] = jnp.concatenate([jnp.where(strict_w, ak[0:c], 0.0), jnp.where(incl_w, ak[c:2 * c], 0.0)],
                                     axis=0).astype(BF16)
    for ch in chains:
        x = -ch['a_kb']
        ch['t'] = eye_w + x
        ch['x'] = mm(x, bdiag(x))
    for _ in range(int(math.log2(c)) - 2):
        for ch in chains:
            res = mm(jnp.concatenate([ch['t'], ch['x']], axis=0), bdiag(ch['x']))
            ch['t'] = ch['t'] + res[0:c]
            ch['x'] = res[c:2 * c]
    for ch in chains:
        ch['t'] = ch['t'] + mm(ch['t'], bdiag(ch['x']))
    for ch in chains:
        l_hi, l_lo = _split_bf16(ch['a_kb'])
        t_hi, t_lo = _split_bf16(ch['t'])
        lt = mm(jnp.concatenate([l_hi, l_lo], axis=0), bdiag(t_hi))
        ch['resid'] = eye_w - ch['t'] - (lt[0:c] + lt[c:2 * c] + mm(l_hi, bdiag(t_lo)))
    for ch in chains:
        ch['t16'] = (ch['t'] + mm(ch['t'], bdiag(ch['resid']))).astype(BF16)
    for ch in chains:
        ch['w_tok'] = mm(ch['t16'], bdiag(ch['kp']))
        ch['av'] = mm(ch['a_kr'], bdiag(ch['v']))
    for ch in chains:
        ch['u_tok'] = mm(ch['t16'], bdiag(ch['av'][0:c]))
    tn_dims = (((0,), (0,)), ((), ()))
    for ch in chains:
        ch['q16'] = (ch['rt'] - mm(ch['a_rb'], bdiag(ch['w_tok']))).astype(BF16)
        ch['y0'] = ch['av'][c:2 * c] - mm(ch['a_rb'], bdiag(ch['u_tok']))
    for ch in chains:
        bb16 = ch['bb'].astype(BF16)
        wb = lax.dot_general(ch['w_tok'].astype(BF16), bb16, tn_dims, preferred_element_type=F32)
        ch['wb'] = jnp.where(blk_m, wb, 0.0).astype(BF16)
        d_t = lax.dot_general(jnp.concatenate([ch['v'], -ch['u_tok']], axis=0).astype(BF16),
                              jnp.concatenate([ch['kb'].astype(BF16), bb16], axis=0), tn_dims,
                              preferred_element_type=F32)
        ch['d_t'] = jnp.where(blk_m, d_t, 0.0)

    for ch in chains:
        s = ch['slab']
        p_t = state_ref[s]
        ch['p16'] = p_t.astype(BF16)
        state_ref[s] = p_t * ch['g_end'] - jnp.dot(ch['p16'], ch['wb'], preferred_element_type=F32) + ch['d_t']
    for ch in chains:
        y_scr[ch['rs'], ch['ls']] = ch['y0'] + lax.dot_general(ch['q16'], ch['p16'], nt_dims,
                                                               preferred_element_type=F32)

    hsum = hsum_ref[...]
    y = y_scr[...]
    r = r_ref[...].astype(F32)
    v = v_ref[...].astype(F32)
    mean = _split_dot(y, hsum) * (1.0 / hd)
    yc = y - mean
    var = _bdot(yc * yc, hsum) * (1.0 / hd)
    yn = yc * lax.rsqrt(var + RWKV_GN_EPS) * lng_ref[...] + lnb_ref[...]
    bonus = _bdot(r * k_ref[...].astype(F32) * rk_ref[...], hsum) * v
    o_ref[...] = ((yn + bonus) * g_ref[...].astype(F32)).astype(o_ref.dtype)


def _rwkv(r, cum, cump, k, v, kk, a, g, rk, lng, lnb, hsum, bsz, seq, slab_heads=RWKV_SLAB_HEADS,
          n_chunks=RWKV_CHUNKS_PER_STEP):
    t, mw = r.shape
    c = RWKV_CHUNK * n_chunks
    nt = seq // c
    sw = slab_heads * RWKV_HEAD
    full = lambda x: pl.BlockSpec(x.shape, lambda b, i: (0, 0))
    rows = pl.BlockSpec((c, mw), lambda b, i: (b * nt + i, 0))
    kern = functools.partial(_rwkv_kernel, slab_heads=slab_heads, n_chunks=n_chunks)
    return pl.pallas_call(
        kern,
        grid=(bsz, nt),
        in_specs=[rows] * 8 + [full(rk), full(lng), full(lnb), full(hsum)],
        out_specs=rows,
        out_shape=jax.ShapeDtypeStruct((t, mw), BF16),
        scratch_shapes=[pltpu.VMEM((mw // sw, sw, sw), F32), pltpu.VMEM((c, mw), F32)],
        compiler_params=_cparams("parallel", "arbitrary"),
        name="rwkv",
    )(r, cum, cump, k, v, kk, a, g, rk, lng, lnb, hsum)


def _cd_post_kernel(h_ref, oc_ref, xb_ref, gate_ref, cw_ref, cb_ref, wa_ref, ba_ref, wx_ref, bx_ref, lam_ref,
                    wout_ref, g_ref, k_ref, v_ref, wq_ref, wo_ref, gq_ref, go_ref, out_ref,
                    o_scr, od_scr, la_ref, lb_ref, carry_ref, carry_x):
    tm = h_ref.shape[0]
    mw = oc_ref.shape[1]
    sub = V7X_SUBLANES
    subs = _sub_tiles(tm)
    rowi = lax.broadcasted_iota(jnp.int32, (sub, mw), 0)

    def lru_coeffs(rs, cx):
        xb = xb_ref[rs, :].astype(F32)
        xc = cb_ref[...] + cw_ref[LRU_CONV - 1:LRU_CONV, :] * xb
        for sh in range(1, LRU_CONV):
            xc = xc + cw_ref[LRU_CONV - 1 - sh:LRU_CONV - sh, :] * _shift_rows(xb, sh, cx)
        rg = _sigmoid(_bdot(xc, wa_ref[...]) + ba_ref[...])
        ig = _sigmoid(_bdot(xc, wx_ref[...]) + bx_ref[...])
        log_a = -LRU_C * rg * _softplus(-lam_ref[...])
        la_ref[rs, :] = jnp.exp(log_a)
        th = jnp.tanh(log_a)
        lb_ref[rs, :] = jnp.sqrt(-2.0 * th / (1.0 - th)) * (ig * xc)
        return xb[xb.shape[0] - sub:, :]

    def scan8(a, b, h):
        sh = 1
        while sh < sub:
            a_sh = jnp.where(rowi >= sh, pltpu.roll(a, sh, axis=0), 1.0)
            b_sh = jnp.where(rowi >= sh, pltpu.roll(b, sh, axis=0), 0.0)
            b = b + a * b_sh
            a = a * a_sh
            sh *= 2
        return b + a * h

    @pl.when(pl.program_id(1) == 0)
    def _():
        carry_ref[...] = jnp.zeros_like(carry_ref)
        carry_x[...] = jnp.zeros_like(carry_x)

    h = carry_ref[0:1, :]
    cx = carry_x[...]
    for rs in subs:
        cx = lru_coeffs(rs, cx)
        for r0 in range(rs.start, rs.stop, 2 * sub):
            h1 = scan8(la_ref[r0:r0 + sub, :], lb_ref[r0:r0 + sub, :], h)
            h2 = scan8(la_ref[r0 + sub:r0 + 2 * sub, :], lb_ref[r0 + sub:r0 + 2 * sub, :], h1[sub - 1:sub, :])
            h = h2[sub - 1:sub, :]
            gate = gate_ref[r0:r0 + 2 * sub, :].astype(F32)
            od_scr[r0:r0 + 2 * sub, :] = (jnp.concatenate([h1, h2], axis=0) * _gelu_tanh(gate)).astype(BF16)
    carry_ref[0:1, :] = h
    carry_x[...] = cx

    mix = [jnp.dot(oc_ref[rs, :], wout_ref[0:mw, :], preferred_element_type=F32)
           + jnp.dot(od_scr[rs, :], wout_ref[mw:2 * mw, :], preferred_element_type=F32) for rs in subs]
    h1s = [h_ref[rs, :] + _rms(mix[i], g_ref[...]) for i, rs in enumerate(subs)]
    for _ in _xattn_stages(h1s, subs, k_ref, v_ref, wq_ref, wo_ref, gq_ref, go_ref, o_scr, out_ref):
        pass


def _cd_post(h, o_c, xb, gate, lru, w_out, gain, xa, bsz, seq, mem_len, tm):
    t, d = h.shape
    mw = o_c.shape[1]
    nt = seq // tm
    k, v, wq, wo, gq, go = xa
    full = lambda a: pl.BlockSpec(a.shape, lambda b, i: (0, 0))
    rows = lambda w: pl.BlockSpec((tm, w), lambda b, i: (b * nt + i, 0))
    kv = pl.BlockSpec((mem_len, d), lambda b, i: (b, 0))
    return pl.pallas_call(
        _cd_post_kernel,
        grid=(bsz, nt),
        in_specs=[rows(d), rows(mw), rows(mw), rows(mw)] + [full(a) for a in lru] + [full(w_out), full(gain),
                  kv, kv, full(wq), full(wo), full(gq), full(go)],
        out_specs=rows(d),
        out_shape=jax.ShapeDtypeStruct((t, d), F32),
        scratch_shapes=[pltpu.VMEM((tm, d), BF16), pltpu.VMEM((tm, mw), BF16), pltpu.VMEM((tm, mw), F32),
                        pltpu.VMEM((tm, mw), F32), pltpu.VMEM((V7X_SUBLANES, mw), F32),
                        pltpu.VMEM((V7X_SUBLANES, mw), F32)],
        compiler_params=_cparams("parallel", "arbitrary"),
        name="cd_post",
    )(h, o_c, xb, gate, *lru, w_out, gain, k, v, wq, wo, gq, go)


def _row(vec):
    return vec.astype(F32).reshape(1, -1)


def _block_diag(blocks):
    nb, bi, bo = blocks.shape
    eye = jnp.eye(nb, dtype=blocks.dtype)
    return (eye[:, None, :, None] * blocks[:, :, None, :]).reshape(nb * bi, nb * bo)


def _head_sum_matrix(width, head):
    idx = jnp.arange(width) // head
    return (idx[:, None] == idx[None, :]).astype(BF16)


def _layer_ab(h, gains, p, xa, bsz, seq, mem_len, tm):
    d = h.shape[1]
    mw = d // 2
    dk = p['gla_b_decay'].shape[0] // GLA_HEADS
    nq = GLA_HEADS * dk
    w_in = p['ab_w_in']
    rank = p['gla_w_decay2'].shape[0]
    o_dlr = 2 * nq + 2 * mw
    w_main = jnp.concatenate([w_in[:, :o_dlr], w_in[:, o_dlr + rank:]], axis=1).astype(BF16)
    w_dlr = jnp.pad(w_in[:, o_dlr:o_dlr + rank], ((0, 0), (0, V7X_LANES - rank))).astype(BF16)
    w_decay2 = jnp.pad(p['gla_w_decay2'].astype(F32), ((0, V7X_LANES - rank), (0, 0)))
    perm = _chunk_step_perm(tm, S5_CHUNK)
    qkvgu, loga, u_perm = _ab_in(h, _row(gains[0]), w_main, w_dlr, w_decay2, _row(p['gla_b_decay']), perm, tm, mw)
    o_a = _gla(qkvgu, loga, _row(p['gla_norm_gain']), bsz, seq, tb=min(GLA_ROWS_PER_STEP, seq))
    kgen, ws, wc, a_pow = _s5_weights(p['s5_lambda_re'], p['s5_lambda_im'], p['s5_log_step'], p['s5_b_re'],
                                      p['s5_b_im'], p['s5_c_re'], p['s5_c_im'])
    y_perm = _s5(u_perm, kgen, ws.astype(BF16), wc.astype(BF16), a_pow, bsz, seq, tm)
    return _ab_post(h, o_a, y_perm, qkvgu, _row(p['s5_d']), p['s5_w_glu'].astype(BF16), _row(p['s5_b_glu']),
                    p['ab_w_out'].astype(BF16), _row(gains[1]), perm.T, xa, bsz, seq, mem_len, tm)


def _layer_cd(h, gains, p, xa, bsz, seq, mem_len, tm):
    d = h.shape[1]
    mw = d // 2
    w_in = p['cd_w_in']
    dr = p['rwkv_w2'].shape[0]
    ar = p['rwkv_a2'].shape[0]
    gr = p['rwkv_g2'].shape[0]
    assert dr + ar == V7X_LANES and gr == V7X_LANES, "the three low-rank projections fill two lane tiles"
    o = [0, mw, mw + dr, 2 * mw + dr, 3 * mw + dr, 3 * mw + dr + ar, 3 * mw + dr + ar + gr]
    col = lambda i, width: w_in[:, o[i]:o[i] + width]
    wbig = jnp.concatenate([col(0, mw), col(2, mw), col(3, mw), w_in[:, o[6]:]], axis=1).astype(BF16)
    wsm = jnp.concatenate([col(1, dr), col(4, ar), col(5, gr)], axis=1).astype(BF16)
    mu = p['rwkv_mu'].astype(F32)
    mseg = lambda i, width: mu[o[i]:o[i] + width]
    mub = jnp.concatenate([mseg(0, mw), mseg(2, mw), mseg(3, mw)]).reshape(1, -1)
    mus = jnp.concatenate([mseg(1, dr), mseg(4, ar), mseg(5, gr)]).reshape(1, -1)
    w2p = jnp.pad(p['rwkv_w2'], ((0, ar), (0, 0))).astype(BF16)
    a2p = jnp.pad(p['rwkv_a2'], ((dr, 0), (0, 0))).astype(BF16)
    hsum = _head_sum_matrix(mw, RWKV_HEAD)
    vecs = (mub, mus, _row(p['rwkv_w0']), _row(p['rwkv_a0']), _row(p['rwkv_k_k']), _row(p['rwkv_k_a']))
    mats = (w2p, a2p, p['rwkv_g2'].astype(BF16), hsum)
    r, cum, cump, k, v, kk, a, g, xb, gate = _cd_in(h, _row(gains[0]), wbig, wsm, vecs, mats, bsz, seq, tm, mw)
    o_c = _rwkv(r, cum, cump, k, v, kk, a, g, _row(p['rwkv_r_k']), _row(p['rwkv_ln_gain']),
                _row(p['rwkv_ln_bias']), hsum, bsz, seq)
    lru = (p['lru_conv_w'].astype(F32), _row(p['lru_conv_b']), _block_diag(p['lru_w_a']).astype(BF16),
           _row(p['lru_b_a']), _block_diag(p['lru_w_x']).astype(BF16), _row(p['lru_b_x']), _row(p['lru_lambda']))
    return _cd_post(h, o_c, xb, gate, lru, p['cd_w_out'].astype(BF16), _row(gains[1]), xa, bsz, seq, mem_len, tm)


def kernel(x, mem, norm_gain, xa_wq, xa_wk, xa_wv, xa_wo, mlp_w1, mlp_w2, ab_w_in, gla_w_decay2, gla_b_decay, gla_norm_gain, s5_lambda_re, s5_lambda_im, s5_log_step, s5_b_re, s5_b_im, s5_c_re, s5_c_im, s5_d, s5_w_glu, s5_b_glu, ab_w_out, cd_w_in, rwkv_mu, rwkv_w0, rwkv_w2, rwkv_a0, rwkv_a2, rwkv_g2, rwkv_k_k, rwkv_k_a, rwkv_r_k, rwkv_ln_gain, rwkv_ln_bias, lru_conv_w, lru_conv_b, lru_w_a, lru_b_a, lru_w_x, lru_b_x, lru_lambda, cd_w_out):
    bsz, seq, d = x.shape
    mem_len = mem.shape[1]
    depth = norm_gain.shape[0]
    tm = min(ROW_TILE, seq)
    ab = dict(ab_w_in=ab_w_in, gla_w_decay2=gla_w_decay2, gla_b_decay=gla_b_decay, gla_norm_gain=gla_norm_gain,
              s5_lambda_re=s5_lambda_re, s5_lambda_im=s5_lambda_im, s5_log_step=s5_log_step, s5_b_re=s5_b_re,
              s5_b_im=s5_b_im, s5_c_re=s5_c_re, s5_c_im=s5_c_im, s5_d=s5_d, s5_w_glu=s5_w_glu, s5_b_glu=s5_b_glu,
              ab_w_out=ab_w_out)
    cd = dict(cd_w_in=cd_w_in, rwkv_mu=rwkv_mu, rwkv_w0=rwkv_w0, rwkv_w2=rwkv_w2, rwkv_a0=rwkv_a0,
              rwkv_a2=rwkv_a2, rwkv_g2=rwkv_g2, rwkv_k_k=rwkv_k_k, rwkv_k_a=rwkv_k_a, rwkv_r_k=rwkv_r_k,
              rwkv_ln_gain=rwkv_ln_gain, rwkv_ln_bias=rwkv_ln_bias, lru_conv_w=lru_conv_w, lru_conv_b=lru_conv_b,
              lru_w_a=lru_w_a, lru_b_a=lru_b_a, lru_w_x=lru_w_x, lru_b_x=lru_b_x, lru_lambda=lru_lambda,
              cd_w_out=cd_w_out)
    h = x.astype(F32).reshape(bsz * seq, d)
    mem2 = mem.astype(F32).reshape(bsz * mem_len, d)
    for layer in range(depth):
        g = norm_gain[layer]
        i = layer // 2
        km, vm = _mem_kv(mem2, _row(g[6]), xa_wk[layer].astype(BF16), xa_wv[layer].astype(BF16), mem_len)
        xa = (km, vm, xa_wq[layer].astype(BF16), xa_wo[layer].astype(BF16), _row(g[2]), _row(g[3]))
        if layer % 2 == 0:
            h = _layer_ab(h, g, {n: w[i] for n, w in ab.items()}, xa, bsz, seq, mem_len, tm)
        else:
            h = _layer_cd(h, g, {n: w[i] for n, w in cd.items()}, xa, bsz, seq, mem_len, tm)
        h = _mlp(h, _row(g[4]), _row(g[5]), mlp_w1[layer].astype(BF16), mlp_w2[layer].astype(BF16),
                 tm=tm, tf=MLP_FF_TILE)
    return h.reshape(bsz, seq, d).astype(x.dtype)
```
